```python
import math
import jax, jax.numpy as jnp
from jax import lax
import numpy as np

D_MODEL = 2048
BATCH = 4
SEQ = 2048
DEPTH = 1

N_META = 16
ATTN_WIDTH = D_MODEL // 2
POOL_WIDTH = D_MODEL - ATTN_WIDTH
N_HEADS = 8
HEAD_DIM = ATTN_WIDTH // N_HEADS
KV_RANK = D_MODEL // 8
IDX_HEADS = 16
IDX_DIM = 64
TOPK_MAX = 256
POOL_WINDOWS = (2, 4, 8, 16)
POOL_GROUP = POOL_WIDTH // len(POOL_WINDOWS)
D_FF = ((8 * D_MODEL // 3 + 255) // 256) * 256
CONV_WIDTH = 3
REL_BUCKETS = 32
REL_MAX_DIST = 128
Q_BLOCK = 128
ALPHA = (2.0 * DEPTH) ** 0.25
BETA = (8.0 * DEPTH) ** -0.25
LN_EPS = 1e-5
NEG_INF = -1e30
PROJ_SIZES = (ATTN_WIDTH, KV_RANK, IDX_HEADS * IDX_DIM, IDX_DIM, IDX_HEADS, POOL_WIDTH)
PROJ_COLS = sum(PROJ_SIZES)

kernel_name = "hybrid_dsa_pool_deepnorm_layer"


def layer_norm(x, g, b):
    xf = x.astype(jnp.float32)
    mu = jnp.mean(xf, axis=-1, keepdims=True)
    var = jnp.mean(jnp.square(xf - mu), axis=-1, keepdims=True)
    return ((xf - mu) * lax.rsqrt(var + LN_EPS) * g.astype(jnp.float32) + b.astype(jnp.float32)).astype(x.dtype)


def rms_norm(x, g):
    xf = x.astype(jnp.float32)
    return (xf * lax.rsqrt(jnp.mean(jnp.square(xf), axis=-1, keepdims=True) + LN_EPS) * g.astype(jnp.float32)).astype(x.dtype)


def split_columns(p):
    offs = np.cumsum(PROJ_SIZES)[:-1].tolist()
    return jnp.split(p, offs, axis=-1)


def t5_bucket(dist):
    max_exact = REL_BUCKETS // 2
    d_f = jnp.maximum(dist, 1).astype(jnp.float32)
    large = max_exact + (jnp.log(d_f / max_exact) / math.log(REL_MAX_DIST / max_exact)
                         * (REL_BUCKETS - max_exact)).astype(jnp.int32)
    large = jnp.minimum(large, REL_BUCKETS - 1)
    return jnp.where(dist < max_exact, dist, large)


def dsa_attention(q_abs, c_kv, q_idx, k_idx, w_idx, rel_bias, top_k):
    B, T = c_kv.shape[0], c_kv.shape[1]
    n_blocks = -(-T // Q_BLOCK)
    t_pad = n_blocks * Q_BLOCK - T

    def to_blocks(a):
        a = jnp.pad(a, [(0, 0), (0, t_pad)] + [(0, 0)] * (a.ndim - 2))
        return jnp.moveaxis(a.reshape((B, n_blocks, Q_BLOCK) + a.shape[2:]), 1, 0)

    key_pos = jnp.arange(T, dtype=jnp.int32)

    def block(args):
        qa, qi, wi, blk = args
        t = blk * Q_BLOCK + jnp.arange(Q_BLOCK, dtype=jnp.int32)
        dots = jnp.einsum('bqhd,bsd->bqhs', qi, k_idx)
        score = jnp.einsum('bqhs,bqh->bqs', jax.nn.relu(dots), wi).astype(jnp.float32)
        causal = key_pos[None, :] <= t[:, None]
        score = jnp.where(causal[None], score, NEG_INF)
        _, sel = lax.top_k(score, top_k)
        c_sel = jax.vmap(lambda c, i: c[i])(c_kv, sel)
        logits = jnp.einsum('bqhc,bqkc->bhqk', qa, c_sel).astype(jnp.float32) * (HEAD_DIM ** -0.5)
        dist = t[None, :, None] - sel
        bias = rel_bias[t5_bucket(jnp.maximum(dist, 0))].astype(jnp.float32)
        logits = logits + jnp.moveaxis(bias, -1, 1)
        logits = jnp.where((dist >= 0)[:, None], logits, NEG_INF)
        p = jax.nn.softmax(logits, axis=-1).astype(c_sel.dtype)
        return jnp.einsum('bhqk,bqkc->bqhc', p, c_sel)

    out = lax.map(block, (to_blocks(q_abs), to_blocks(q_idx), to_blocks(w_idx),
                          jnp.arange(n_blocks, dtype=jnp.int32)))
    out = jnp.moveaxis(out, 0, 1).reshape((B, n_blocks * Q_BLOCK) + out.shape[3:])
    return out[:, :T]


def multiscale_pool(u, w_pool, scale):
    B, T, C = u.shape
    cs = jnp.cumsum(u.astype(jnp.float32), axis=1)
    cp = jnp.concatenate([jnp.zeros((B, 1, C), jnp.float32), cs], axis=1)
    pos = jnp.arange(T, dtype=jnp.int32)
    outs = []
    for g, w in enumerate(POOL_WINDOWS):
        lo = g * POOL_GROUP
        cg = cp[..., lo:lo + POOL_GROUP]
        start = jnp.maximum(pos + 1 - w, 0)
        win_sum = cg[:, 1:] - jnp.take(cg, start, axis=1)
        count = jnp.minimum(pos + 1, w).astype(jnp.float32)[None, :, None]
        diff = (win_sum / count).astype(u.dtype) - u[..., lo:lo + POOL_GROUP]
        outs.append(diff @ w_pool[g])
    return jnp.concatenate(outs, axis=-1) * scale


def conv_gated_ffn(h, w_up, conv_w, conv_b, w_down):
    T = h.shape[1]
    z = h @ w_up
    zp = jnp.pad(z, ((0, 0), (CONV_WIDTH - 1, 0), (0, 0)))
    z = sum(zp[:, j:j + T] * conv_w[j] for j in range(CONV_WIDTH)) + conv_b
    a, g = jnp.split(z, 2, axis=-1)
    return (jax.nn.gelu(a) * g) @ w_down


def setup_inputs(seed: int = 0) -> dict:
    key = jax.random.key(seed)
    ks = jax.random.split(key, 20)
    f32 = jnp.float32
    nrm = lambda k, s, sc: jax.random.normal(k, s, f32) * sc
    L = DEPTH
    return {
        "x": nrm(ks[0], (BATCH, SEQ, D_MODEL), 1.0),
        "meta": nrm(ks[1], (N_META, D_MODEL), 1.0),
        "rel_bias": nrm(ks[2], (REL_BUCKETS, N_HEADS), 0.5),
        "w_in": nrm(ks[3], (L, D_MODEL, PROJ_COLS), D_MODEL ** -0.5),
        "kv_norm_g": 1.0 + nrm(ks[4], (L, KV_RANK), 0.02),
        "w_uk": nrm(ks[5], (L, KV_RANK, N_HEADS, HEAD_DIM), KV_RANK ** -0.5),
        "w_uv": nrm(ks[6], (L, KV_RANK, N_HEADS, HEAD_DIM), BETA * KV_RANK ** -0.5),
        "w_pool": nrm(ks[7], (L, len(POOL_WINDOWS), POOL_GROUP, POOL_GROUP), BETA * POOL_GROUP ** -0.5),
        "pool_scale": 1.0 + nrm(ks[8], (L, POOL_WIDTH), 0.02),
        "w_o": nrm(ks[9], (L, D_MODEL, D_MODEL), BETA * D_MODEL ** -0.5),
        "ln1_g": 1.0 + nrm(ks[10], (L, D_MODEL), 0.02),
        "ln1_b": nrm(ks[11], (L, D_MODEL), 0.02),
        "w_up": nrm(ks[12], (L, D_MODEL, 2 * D_FF), D_MODEL ** -0.5),
        "conv_w": nrm(ks[13], (L, CONV_WIDTH, 2 * D_FF), CONV_WIDTH ** -0.5),
        "conv_b": nrm(ks[14], (L, 2 * D_FF), 0.02),
        "w_down": nrm(ks[15], (L, D_FF, D_MODEL), BETA * D_FF ** -0.5),
        "ln2_g": 1.0 + nrm(ks[16], (L, D_MODEL), 0.02),
        "ln2_b": nrm(ks[17], (L, D_MODEL), 0.02),
    }


def reference(x, meta, rel_bias, w_in, kv_norm_g, w_uk, w_uv, w_pool, pool_scale, w_o,
              ln1_g, ln1_b, w_up, conv_w, conv_b, w_down, ln2_g, ln2_b):
    B, S, D = x.shape
    top_k = min(TOPK_MAX, S // 4)
    h = jnp.concatenate([jnp.broadcast_to(meta[None].astype(x.dtype), (B, N_META, D)), x], axis=1)
    T = h.shape[1]
    for l in range(DEPTH):
        q, c_kv, q_idx, k_idx, w_idx, u = split_columns(h @ w_in[l])
        q = q.reshape(B, T, N_HEADS, HEAD_DIM)
        c_kv = rms_norm(c_kv, kv_norm_g[l])
        q_abs = jnp.einsum('bthd,chd->bthc', q, w_uk[l])
        q_idx = q_idx.reshape(B, T, IDX_HEADS, IDX_DIM) * (IDX_DIM ** -0.5)
        w_idx = w_idx * (IDX_HEADS ** -0.5)
        o_lat = dsa_attention(q_abs, c_kv, q_idx, k_idx, w_idx, rel_bias, top_k)
        attn_out = jnp.einsum('bthc,chd->bthd', o_lat, w_uv[l]).reshape(B, T, ATTN_WIDTH)
        pool_out = multiscale_pool(u, w_pool[l], pool_scale[l])
        mix = jnp.concatenate([attn_out, pool_out], axis=-1) @ w_o[l]
        h = layer_norm(ALPHA * h + mix, ln1_g[l], ln1_b[l])
        f = conv_gated_ffn(h, w_up[l], conv_w[l], conv_b[l], w_down[l])
        h = layer_norm(ALPHA * h + f, ln2_g[l], ln2_b[l])
    return h[:, N_META:]
```

```python
import functools
import math

import numpy as np
import jax
import jax.numpy as jnp
from jax import lax
from jax.experimental import pallas as pl
from jax.experimental.pallas import tpu as pltpu

F32 = jnp.float32
BF16 = jnp.bfloat16

N_META = 16
N_HEADS = 8
HEAD_DIM = 128
KV_RANK = 256
IDX_HEADS = 16
IDX_DIM = 64
TOPK_MAX = 256
POOL_WINDOWS = (2, 4, 8, 16)
POOL_GROUP = 256
CONV_WIDTH = 3
REL_BUCKETS = 32
REL_MAX_DIST = 128
DEPTH = 1
ALPHA = (2.0 * DEPTH) ** 0.25
LN_EPS = 1e-5
NEG_INF = -1e30

VMEM_LIMIT_BYTES = 56 * 1024 * 1024
ROW_ALIGN = 768
PROJ_ROWS = 256
Q_TILE = 128
K_CHUNK = 256
MIX_ROWS = 384
FFN_ROWS = 512
FFN_COLS = 512
HALO = 16
INT_MIN = -(2 ** 31)


def _dot(a, b):
    return jnp.dot(a, b, preferred_element_type=F32)


def _dot_nt(a, b):
    return lax.dot_general(a, b, (((1,), (1,)), ((), ())), preferred_element_type=F32)


def _layer_norm(y, g, b):
    mu = jnp.mean(y, axis=-1, keepdims=True)
    yc = y - mu
    var = jnp.mean(yc * yc, axis=-1, keepdims=True)
    return yc * lax.rsqrt(var + LN_EPS) * g + b


def _proj_rows(a_ref, w_ref, store):
    n_steps = a_ref.shape[0] // PROJ_ROWS

    def body(r, carry):
        rows = pl.ds(pl.multiple_of(r * PROJ_ROWS, PROJ_ROWS), PROJ_ROWS)
        store(rows, _dot(a_ref[rows, :], w_ref[...]))
        return carry

    lax.fori_loop(0, n_steps, body, 0)


def _proj_cast_kernel(a_ref, w_ref, o_ref):
    def store(rows, acc):
        o_ref[rows, :] = acc.astype(o_ref.dtype)

    _proj_rows(a_ref, w_ref, store)


def _proj_pool_kernel(a_ref, w_ref, o_ref, u_scr):
    def store(rows, acc):
        u_scr[rows, :] = acc

    _proj_rows(a_ref, w_ref, store)
    group = pl.program_id(1)
    tp = u_scr.shape[0]
    pos = lax.broadcasted_iota(jnp.int32, (tp, 1), 0)
    for g, window in enumerate(POOL_WINDOWS):

        @pl.when(group == g)
        def _():
            u = u_scr[...]
            s = u
            shift = 1
            while shift < window:
                s = s + pltpu.roll(s, shift, axis=0)
                shift *= 2
            count = jnp.minimum(pos + 1, window).astype(F32)
            o_ref[...] = (s / count - u).astype(o_ref.dtype)


def _proj_small_kernel(a_ref, w_ref, g_ref, c_ref, kk_ref, wi_ref):
    def store(rows, acc):
        c = acc[:, :KV_RANK]
        ms = jnp.mean(c * c, axis=-1, keepdims=True)
        c_ref[rows, :] = (c * lax.rsqrt(ms + LN_EPS) * g_ref[...]).astype(c_ref.dtype)
        kk_ref[rows, :] = acc[:, KV_RANK:KV_RANK + 256].astype(kk_ref.dtype)
        wi_ref[rows, :] = acc[:, KV_RANK + 256:]

    _proj_rows(a_ref, w_ref, store)


def _batch_block(tp, n):
    return pl.BlockSpec((None, tp, n), lambda b, j: (b, 0, 0))


def _compiler_params(semantics):
    return pltpu.CompilerParams(dimension_semantics=semantics, vmem_limit_bytes=VMEM_LIMIT_BYTES)


def _proj_qq(hb, w):
    B, tp, d = hb.shape
    n = w.shape[1]
    tn = 512
    return pl.pallas_call(
        _proj_cast_kernel,
        grid=(B, n // tn),
        in_specs=[_batch_block(tp, d), pl.BlockSpec((d, tn), lambda b, j: (0, j))],
        out_specs=pl.BlockSpec((None, tp, tn), lambda b, j: (b, 0, j)),
        out_shape=jax.ShapeDtypeStruct((B, tp, n), BF16),
        compiler_params=_compiler_params(("parallel", "arbitrary")),
        name="proj_qq",
    )(hb, w)


def _proj_pool(hb, w):
    B, tp, d = hb.shape
    n = w.shape[1]
    return pl.pallas_call(
        _proj_pool_kernel,
        grid=(B, n // POOL_GROUP),
        in_specs=[_batch_block(tp, d), pl.BlockSpec((d, POOL_GROUP), lambda b, j: (0, j))],
        out_specs=pl.BlockSpec((None, tp, POOL_GROUP), lambda b, j: (b, 0, j)),
        out_shape=jax.ShapeDtypeStruct((B, tp, n), BF16),
        scratch_shapes=[pltpu.VMEM((tp, POOL_GROUP), F32)],
        compiler_params=_compiler_params(("parallel", "arbitrary")),
        name="proj_pool",
    )(hb, w)


def _proj_small(hb, w, kv_g):
    B, tp, d = hb.shape
    n = w.shape[1]
    return pl.pallas_call(
        _proj_small_kernel,
        grid=(B, 1),
        in_specs=[_batch_block(tp, d), pl.BlockSpec((d, n), lambda b, j: (0, 0)),
                  pl.BlockSpec((1, KV_RANK), lambda b, j: (0, 0))],
        out_specs=[_batch_block(tp, KV_RANK), _batch_block(tp, 256), _batch_block(tp, 128)],
        out_shape=[jax.ShapeDtypeStruct((B, tp, KV_RANK), BF16),
                   jax.ShapeDtypeStruct((B, tp, 256), BF16),
                   jax.ShapeDtypeStruct((B, tp, 128), F32)],
        compiler_params=_compiler_params(("parallel", "arbitrary")),
        name="proj_small",
    )(hb, w, kv_g)


def _attn_kernel(qq_ref, wi_ref, kk_ref, c_ref, wuk_ref, nb_ref, o_ref, key_scr, mb_scr, *, top_k):
    i = pl.program_id(1)
    n_chunks = i // (K_CHUNK // Q_TILE) + 1
    attn_w = N_HEADS * HEAD_DIM
    scale = HEAD_DIM ** -0.5
    rows_h = N_HEADS * Q_TILE

    t_pos = i * Q_TILE + lax.broadcasted_iota(jnp.int32, (Q_TILE, 1), 0)
    lane = lax.broadcasted_iota(jnp.int32, (1, K_CHUNK), 1)

    q_abs = jnp.concatenate(
        [_dot(qq_ref[:, h * HEAD_DIM:(h + 1) * HEAD_DIM], wuk_ref[h]) for h in range(N_HEADS)], axis=0).astype(BF16)
    n_pairs = IDX_HEADS // 2
    q_idx = jnp.concatenate(
        [qq_ref[:, attn_w + p * 128:attn_w + (p + 1) * 128] for p in range(n_pairs)], axis=0)
    w_idx = wi_ref[...]

    def idx_chunk(j, carry):
        ks = kk_ref[pl.ds(pl.multiple_of(j * K_CHUNK, K_CHUNK), K_CHUNK), :]
        d_even = _dot_nt(q_idx, ks[:, :128])
        d_odd = _dot_nt(q_idx, ks[:, 128:])
        score = jnp.zeros((Q_TILE, K_CHUNK), F32)
        for p in range(n_pairs):
            rows = slice(p * Q_TILE, (p + 1) * Q_TILE)
            score = score + jnp.maximum(d_even[rows], 0.0) * w_idx[:, 2 * p:2 * p + 1]
            score = score + jnp.maximum(d_odd[rows], 0.0) * w_idx[:, 2 * p + 1:2 * p + 2]
        causal = (j * K_CHUNK + lane) <= t_pos
        score = jnp.where(causal, score, NEG_INF)
        bits = lax.bitcast_convert_type(score, jnp.int32)
        key_scr[j] = jnp.where(bits < 0, bits ^ jnp.int32(0x7FFFFFFF), bits)
        return carry

    lax.fori_loop(0, n_chunks, idx_chunk, 0)

    def count_where(pred):
        def body(j, acc):
            return acc + jnp.where(pred(j, key_scr[j]), 1.0, 0.0)

        acc = lax.fori_loop(0, n_chunks, body, jnp.zeros((Q_TILE, K_CHUNK), F32))
        return jnp.sum(acc, axis=-1, keepdims=True)

    k_f = float(top_k)

    def bit_step(_, carry):
        thr, bit = carry
        cand = thr + bit
        cnt = count_where(lambda j, key: key >= cand)
        return jnp.where(cnt >= k_f, cand, thr), lax.shift_right_logical(bit, jnp.int32(1))

    thr, _ = lax.fori_loop(0, 32, bit_step,
                           (jnp.full((Q_TILE, 1), INT_MIN, jnp.int32), jnp.int32(INT_MIN)))

    n_gt = count_where(lambda j, key: key > thr)
    n_ge = count_where(lambda j, key: key >= thr)
    need = k_f - n_gt
    neg_key = jnp.int32(np.array(NEG_INF, np.float32).view(np.int32) ^ 0x7FFFFFFF)
    has_tie = jnp.max(jnp.where((n_ge > k_f) & (thr > neg_key), 1.0, 0.0)) > 0.0
    idx_bits = int(math.ceil(math.log2(key_scr.shape[0] * K_CHUNK)))

    def tie_cut():
        def step(_, carry):
            cut, bit = carry
            cand = cut + bit
            cnt = count_where(lambda j, key: (key == thr) & ((j * K_CHUNK + lane) < cand))
            return jnp.where(cnt < need, cand, cut), lax.shift_right_logical(bit, jnp.int32(1))

        cut, _ = lax.fori_loop(0, idx_bits, step,
                               (jnp.zeros((Q_TILE, 1), jnp.int32), jnp.int32(2 ** (idx_bits - 1))))
        return cut

    cut = lax.cond(has_tie, tie_cut, lambda: jnp.full((Q_TILE, 1), 2 ** 30, jnp.int32))

    def mask_chunk(j, carry):
        key = key_scr[j]
        s_pos = j * K_CHUNK + lane
        keep = ((key > thr) | ((key == thr) & (s_pos <= cut))) & (s_pos <= t_pos)
        mb_scr[j] = jnp.where(keep, 0.0, NEG_INF)
        return carry

    lax.fori_loop(0, n_chunks, mask_chunk, 0)

    n_near = nb_ref.shape[0]

    def att_chunk(j, carry):
        m, l, acc = carry
        cs = c_ref[pl.ds(pl.multiple_of(j * K_CHUNK, K_CHUNK), K_CHUNK), :]
        s = _dot_nt(q_abs, cs) * scale
        near = jnp.minimum(i - (K_CHUNK // Q_TILE) * j, n_near - 1)
        s = s + nb_ref[near]
        mb = mb_scr[j]
        s = jnp.concatenate([s[h * Q_TILE:(h + 1) * Q_TILE] + mb for h in range(N_HEADS)], axis=0)
        m_new = jnp.maximum(m, jnp.max(s, axis=-1, keepdims=True))
        alpha = jnp.exp(m - m_new)
        p = jnp.exp(s - m_new)
        l = alpha * l + jnp.sum(p, axis=-1, keepdims=True)
        acc = alpha * acc + _dot(p.astype(BF16), cs)
        return m_new, l, acc

    m0 = jnp.full((rows_h, 1), NEG_INF, F32)
    l0 = jnp.zeros((rows_h, 1), F32)
    acc0 = jnp.zeros((rows_h, KV_RANK), F32)
    _, l, acc = lax.fori_loop(0, n_chunks, att_chunk, (m0, l0, acc0))
    out = acc / l
    for h in range(N_HEADS):
        o_ref[:, h * KV_RANK:(h + 1) * KV_RANK] = out[h * Q_TILE:(h + 1) * Q_TILE].astype(o_ref.dtype)


def _dsa_attention(qq, wi, kk, c_kv, wuk, near_bias, top_k):
    B, tp, _ = qq.shape
    n_q = tp // Q_TILE
    n_chunks_max = tp // K_CHUNK
    return pl.pallas_call(
        functools.partial(_attn_kernel, top_k=top_k),
        grid=(B, n_q),
        in_specs=[
            pl.BlockSpec((None, Q_TILE, qq.shape[2]), lambda b, i: (b, i, 0)),
            pl.BlockSpec((None, Q_TILE, 128), lambda b, i: (b, i, 0)),
            pl.BlockSpec((None, tp, 256), lambda b, i: (b, 0, 0)),
            pl.BlockSpec((None, tp, KV_RANK), lambda b, i: (b, 0, 0)),
            pl.BlockSpec(wuk.shape, lambda b, i: (0, 0, 0)),
            pl.BlockSpec(near_bias.shape, lambda b, i: (0, 0, 0)),
        ],
        out_specs=pl.BlockSpec((None, Q_TILE, N_HEADS * KV_RANK), lambda b, i: (b, i, 0)),
        out_shape=jax.ShapeDtypeStruct((B, tp, N_HEADS * KV_RANK), BF16),
        scratch_shapes=[pltpu.VMEM((n_chunks_max, Q_TILE, K_CHUNK), jnp.int32),
                        pltpu.VMEM((n_chunks_max, Q_TILE, K_CHUNK), F32)],
        compiler_params=_compiler_params(("parallel", "arbitrary")),
        name="dsa_attention",
    )(qq, wi, kk, c_kv, wuk, near_bias)


def _mix_kernel(ol_ref, pd_ref, h_ref, wuv_ref, wp_ref, ps_ref, wo_ref, g_ref, b_ref, h1_ref, h1b_ref):
    attn = [_dot(ol_ref[:, h * KV_RANK:(h + 1) * KV_RANK], wuv_ref[h]) for h in range(N_HEADS)]
    pool = [_dot(pd_ref[:, g * POOL_GROUP:(g + 1) * POOL_GROUP], wp_ref[g]) for g in range(len(POOL_WINDOWS))]
    pool = jnp.concatenate(pool, axis=-1) * ps_ref[...]
    cat = jnp.concatenate(attn + [pool], axis=-1).astype(BF16)
    y = ALPHA * h_ref[...] + _dot(cat, wo_ref[...])
    h1 = _layer_norm(y, g_ref[...], b_ref[...])
    h1_ref[...] = h1
    h1b_ref[...] = h1.astype(h1b_ref.dtype)


def _mix_ln1(o_lat, pool_diff, hp, wuv, wpool, pool_scale, wo, g, b):
    B, tp, d = hp.shape
    row = lambda n: pl.BlockSpec((None, MIX_ROWS, n), lambda bi, r: (bi, r, 0))
    full = lambda a: pl.BlockSpec(a.shape, lambda bi, r: (0,) * a.ndim)
    return pl.pallas_call(
        _mix_kernel,
        grid=(B, tp // MIX_ROWS),
        in_specs=[row(o_lat.shape[2]), row(pool_diff.shape[2]), row(d),
                  full(wuv), full(wpool), full(pool_scale), full(wo), full(g), full(b)],
        out_specs=[row(d), row(d)],
        out_shape=[jax.ShapeDtypeStruct((B, tp, d), F32), jax.ShapeDtypeStruct((B, tp, d), BF16)],
        compiler_params=_compiler_params(("parallel", "arbitrary")),
        name="mix_ln1",
    )(o_lat, pool_diff, hp, wuv, wpool, pool_scale, wo, g, b)


def _gelu_tanh(x):
    return 0.5 * x * (1.0 + jnp.tanh(math.sqrt(2.0 / math.pi) * (x + 0.044715 * (x * x * x))))


def _ffn_kernel(hw_ref, hres_ref, wa_ref, wg_ref, cwa_ref, cwg_ref, cba_ref, cbg_ref, wd_ref, g_ref, b_ref, o_ref):
    c = pl.program_id(2)
    x = hw_ref[...]

    def conv(z, cw_ref, cb_ref):
        cw = cw_ref[...]
        n = z.shape[0]
        out = z[HALO - 2:n - 2] * cw[0:1] + z[HALO - 1:n - 1] * cw[1:2] + z[HALO:] * cw[2:3]
        return out + cb_ref[...]

    a = conv(_dot(x, wa_ref[...]), cwa_ref, cba_ref)
    gate = conv(_dot(x, wg_ref[...]), cwg_ref, cbg_ref)
    act = (_gelu_tanh(a) * gate).astype(BF16)
    part = _dot(act, wd_ref[...])

    @pl.when(c == 0)
    def _():
        o_ref[...] = part

    @pl.when(c > 0)
    def _():
        o_ref[...] += part

    @pl.when(c == pl.num_programs(2) - 1)
    def _():
        o_ref[...] = _layer_norm(ALPHA * hres_ref[...] + o_ref[...], g_ref[...], b_ref[...])


def _ffn_ln2(h1, h1b, w_up, conv_w, conv_b, w_down, g, b, seq):
    B, tp, d = h1.shape
    d_ff = w_down.shape[0]
    n_c = d_ff // FFN_COLS
    vec = lambda off: pl.BlockSpec((1, FFN_COLS), lambda bi, r, c: (0, c + off))
    return pl.pallas_call(
        _ffn_kernel,
        grid=(B, seq // FFN_ROWS, n_c),
        in_specs=[
            pl.BlockSpec((None, pl.Element(FFN_ROWS + HALO), pl.Element(d)),
                         lambda bi, r, c: (bi, r * FFN_ROWS + N_META - HALO, 0)),
            pl.BlockSpec((None, pl.Element(FFN_ROWS), pl.Element(d)),
                         lambda bi, r, c: (bi, pl.multiple_of(r * FFN_ROWS + N_META, N_META), 0)),
            pl.BlockSpec((d, FFN_COLS), lambda bi, r, c: (0, c)),
            pl.BlockSpec((d, FFN_COLS), lambda bi, r, c: (0, c + n_c)),
            pl.BlockSpec((CONV_WIDTH, FFN_COLS), lambda bi, r, c: (0, c)),
            pl.BlockSpec((CONV_WIDTH, FFN_COLS), lambda bi, r, c: (0, c + n_c)),
            vec(0), vec(n_c),
            pl.BlockSpec((FFN_COLS, d), lambda bi, r, c: (c, 0)),
            pl.BlockSpec((1, d), lambda bi, r, c: (0, 0)),
            pl.BlockSpec((1, d), lambda bi, r, c: (0, 0)),
        ],
        out_specs=pl.BlockSpec((None, FFN_ROWS, d), lambda bi, r, c: (bi, r, 0)),
        out_shape=jax.ShapeDtypeStruct((B, seq, d), F32),
        compiler_params=_compiler_params(("parallel", "parallel", "arbitrary")),
        name="ffn_ln2",
    )(h1b, h1, w_up, w_up, conv_w, conv_w, conv_b, conv_b, w_down, g, b)


def _t5_bucket_table(n):
    dist = np.arange(n, dtype=np.int32)
    max_exact = REL_BUCKETS // 2
    d_f = np.maximum(dist, 1).astype(np.float32)
    large = max_exact + (np.log(d_f / np.float32(max_exact)) / np.float32(math.log(REL_MAX_DIST / max_exact))
                         * np.float32(REL_BUCKETS - max_exact)).astype(np.int32)
    return np.where(dist < max_exact, dist, np.minimum(large, REL_BUCKETS - 1))


def _near_bias_tiles(rel_bias):
    n_cls = 2 * (K_CHUNK // Q_TILE)
    buckets = _t5_bucket_table(n_cls * Q_TILE + K_CHUNK)
    first_far = int(np.argmax(buckets == REL_BUCKETS - 1))
    assert np.all(buckets[first_far:] == REL_BUCKETS - 1) and (n_cls - 1) * Q_TILE - (K_CHUNK - 1) >= first_far
    r = np.arange(Q_TILE)[:, None]
    c = np.arange(K_CHUNK)[None, :]
    idx = np.stack([buckets[np.maximum(k * Q_TILE + r - c, 0)] for k in range(n_cls)])
    rel = rel_bias.astype(F32) - rel_bias[REL_BUCKETS - 1:].astype(F32)
    tiles = jnp.transpose(rel[idx], (0, 3, 1, 2))
    return tiles.reshape(n_cls, N_HEADS * Q_TILE, K_CHUNK)


def kernel(x, meta, rel_bias, w_in, kv_norm_g, w_uk, w_uv, w_pool, pool_scale, w_o, ln1_g, ln1_b, w_up, conv_w,
           conv_b, w_down, ln2_g, ln2_b):
    B, S, D = x.shape
    assert w_in.shape[0] == DEPTH and S % FFN_ROWS == 0
    T = S + N_META
    tp = -(-T // ROW_ALIGN) * ROW_ALIGN
    assert tp - T >= max(POOL_WINDOWS)
    top_k = min(TOPK_MAX, S // 4)
    assert top_k <= K_CHUNK

    h = jnp.concatenate([jnp.broadcast_to(meta[None].astype(x.dtype), (B, N_META, D)), x], axis=1)
    hp = jnp.pad(h, ((0, 0), (0, tp - T), (0, 0)))
    hb = hp.astype(BF16)

    attn_w = N_HEADS * HEAD_DIM
    idx_w = IDX_HEADS * IDX_DIM
    o_c, o_qi = attn_w, attn_w + KV_RANK
    o_ki = o_qi + idx_w
    o_wi = o_ki + IDX_DIM
    o_u = o_wi + IDX_HEADS
    w = w_in[0]
    w_qq = jnp.concatenate([w[:, :o_c], w[:, o_qi:o_ki] * (IDX_DIM ** -0.5)], axis=1).astype(BF16)
    w_u = w[:, o_u:].astype(BF16)
    z64 = jnp.zeros((D, IDX_DIM), w.dtype)
    w_small = jnp.concatenate([w[:, o_c:o_qi], w[:, o_ki:o_wi], z64, z64, w[:, o_ki:o_wi],
                               w[:, o_wi:o_u] * (IDX_HEADS ** -0.5), jnp.zeros((D, 128 - IDX_HEADS), w.dtype)],
                              axis=1).astype(BF16)

    qq = _proj_qq(hb, w_qq)
    pool_diff = _proj_pool(hb, w_u)
    c_kv, kk, wi = _proj_small(hb, w_small, kv_norm_g[0].reshape(1, KV_RANK))

    wuk = jnp.transpose(w_uk[0], (1, 2, 0)).astype(BF16)
    o_lat = _dsa_attention(qq, wi, kk, c_kv, wuk, _near_bias_tiles(rel_bias), top_k)

    wuv = jnp.transpose(w_uv[0], (1, 0, 2)).astype(BF16)
    h1, h1b = _mix_ln1(o_lat, pool_diff, hp, wuv, w_pool[0].astype(BF16), pool_scale[0].reshape(1, -1),
                       w_o[0].astype(BF16), ln1_g[0].reshape(1, D), ln1_b[0].reshape(1, D))

    return _ffn_ln2(h1, h1b, w_up[0].astype(BF16), conv_w[0], conv_b[0].reshape(1, -1), w_down[0].astype(BF16),
                    ln2_g[0].reshape(1, D), ln2_b[0].reshape(1, D), S)
```

```python
import functools
import math

import numpy as np
import jax
import jax.numpy as jnp
from jax import lax
from jax.experimental import pallas as pl
from jax.experimental.pallas import tpu as pltpu

F32 = jnp.float32
BF16 = jnp.bfloat16

N_META = 16
N_HEADS = 8
HEAD_DIM = 128
KV_RANK = 256
IDX_HEADS = 16
IDX_DIM = 64
TOPK_MAX = 256
POOL_WINDOWS = (2, 4, 8, 16)
POOL_GROUP = 256
CONV_WIDTH = 3
REL_BUCKETS = 32
REL_MAX_DIST = 128
DEPTH = 1
ALPHA = (2.0 * DEPTH) ** 0.25
LN_EPS = 1e-5
NEG_INF = -1e30

VMEM_LIMIT_BYTES = 56 * 1024 * 1024
SUBLANES = 8
LANES = 128
ROW_ALIGN = 768
PROJ_ROWS = 256
Q_TILE = 128
K_CHUNK = 256
MIX_ROWS = 384
FFN_ROWS = 512
FFN_COLS = 512
HALO = 16
INT_MIN = -(2 ** 31)


def _dot(a, b):
    return jnp.dot(a, b, preferred_element_type=F32)


def _dot_nt(a, b):
    return lax.dot_general(a, b, (((1,), (1,)), ((), ())), preferred_element_type=F32)


def _layer_norm(y, g, b):
    mu = jnp.mean(y, axis=-1, keepdims=True)
    yc = y - mu
    var = jnp.mean(yc * yc, axis=-1, keepdims=True)
    return yc * lax.rsqrt(var + LN_EPS) * g + b


def _proj_rows(a_ref, w_ref, store):
    n_steps = a_ref.shape[0] // PROJ_ROWS

    def body(r, carry):
        rows = pl.ds(pl.multiple_of(r * PROJ_ROWS, PROJ_ROWS), PROJ_ROWS)
        store(rows, _dot(a_ref[rows, :], w_ref[...]))
        return carry

    lax.fori_loop(0, n_steps, body, 0)


def _proj_cast_kernel(a_ref, w_ref, o_ref):
    def store(rows, acc):
        o_ref[rows, :] = acc.astype(o_ref.dtype)

    _proj_rows(a_ref, w_ref, store)


def _proj_pool_kernel(a_ref, w_ref, o_ref, u_scr):
    def store(rows, acc):
        u_scr[rows, :] = acc

    _proj_rows(a_ref, w_ref, store)
    group = pl.program_id(1)
    tp = u_scr.shape[0]
    pos = lax.broadcasted_iota(jnp.int32, (tp, 1), 0)
    for g, window in enumerate(POOL_WINDOWS):

        @pl.when(group == g)
        def _():
            u = u_scr[...]
            s = u
            shift = 1
            while shift < window:
                s = s + pltpu.roll(s, shift, axis=0)
                shift *= 2
            count = jnp.minimum(pos + 1, window).astype(F32)
            o_ref[...] = (s / count - u).astype(o_ref.dtype)


def _proj_small_kernel(a_ref, w_ref, g_ref, c_ref, kk_ref, wi_ref):
    def store(rows, acc):
        c = acc[:, :KV_RANK]
        ms = jnp.mean(c * c, axis=-1, keepdims=True)
        c_ref[rows, :] = (c * lax.rsqrt(ms + LN_EPS) * g_ref[...]).astype(c_ref.dtype)
        kk_ref[rows, :] = acc[:, KV_RANK:KV_RANK + 256].astype(kk_ref.dtype)
        wi_ref[rows, :] = acc[:, KV_RANK + 256:]

    _proj_rows(a_ref, w_ref, store)


def _batch_block(tp, n):
    return pl.BlockSpec((None, tp, n), lambda b, j: (b, 0, 0))


def _compiler_params(semantics):
    return pltpu.CompilerParams(dimension_semantics=semantics, vmem_limit_bytes=VMEM_LIMIT_BYTES)


def _proj_qq(hb, w):
    B, tp, d = hb.shape
    n = w.shape[1]
    tn = 512
    return pl.pallas_call(
        _proj_cast_kernel,
        grid=(B, n // tn),
        in_specs=[_batch_block(tp, d), pl.BlockSpec((d, tn), lambda b, j: (0, j))],
        out_specs=pl.BlockSpec((None, tp, tn), lambda b, j: (b, 0, j)),
        out_shape=jax.ShapeDtypeStruct((B, tp, n), BF16),
        compiler_params=_compiler_params(("parallel", "arbitrary")),
        name="proj_qq",
    )(hb, w)


def _proj_pool(hb, w):
    B, tp, d = hb.shape
    n = w.shape[1]
    return pl.pallas_call(
        _proj_pool_kernel,
        grid=(B, n // POOL_GROUP),
        in_specs=[_batch_block(tp, d), pl.BlockSpec((d, POOL_GROUP), lambda b, j: (0, j))],
        out_specs=pl.BlockSpec((None, tp, POOL_GROUP), lambda b, j: (b, 0, j)),
        out_shape=jax.ShapeDtypeStruct((B, tp, n), BF16),
        scratch_shapes=[pltpu.VMEM((tp, POOL_GROUP), F32)],
        compiler_params=_compiler_params(("parallel", "arbitrary")),
        name="proj_pool",
    )(hb, w)


def _proj_small(hb, w, kv_g):
    B, tp, d = hb.shape
    n = w.shape[1]
    return pl.pallas_call(
        _proj_small_kernel,
        grid=(B, 1),
        in_specs=[_batch_block(tp, d), pl.BlockSpec((d, n), lambda b, j: (0, 0)),
                  pl.BlockSpec((1, KV_RANK), lambda b, j: (0, 0))],
        out_specs=[_batch_block(tp, KV_RANK), _batch_block(tp, 256), _batch_block(tp, 128)],
        out_shape=[jax.ShapeDtypeStruct((B, tp, KV_RANK), BF16),
                   jax.ShapeDtypeStruct((B, tp, 256), BF16),
                   jax.ShapeDtypeStruct((B, tp, 128), F32)],
        compiler_params=_compiler_params(("parallel", "arbitrary")),
        name="proj_small",
    )(hb, w, kv_g)


def _attn_kernel(qq_ref, wi_ref, kk_ref, c_ref, wuk_ref, nb_ref, o_ref,
                 key_scr, keyt_scr, mb_scr, wb_scr, qa_scr, s_scr, m_scr, l_scr, acc_scr, *, top_k):
    i = pl.program_id(1)
    n_chunks = i // (K_CHUNK // Q_TILE) + 1
    attn_w = N_HEADS * HEAD_DIM
    scale = HEAD_DIM ** -0.5
    n_pairs = IDX_HEADS // 2

    t_col = i * Q_TILE + lax.broadcasted_iota(jnp.int32, (Q_TILE, 1), 0)
    t_row = i * Q_TILE + lax.broadcasted_iota(jnp.int32, (1, Q_TILE), 1)
    s_row = lax.broadcasted_iota(jnp.int32, (1, K_CHUNK), 1)
    s_col = lax.broadcasted_iota(jnp.int32, (K_CHUNK, 1), 0)

    for h in range(N_HEADS):
        qa_scr[h * Q_TILE:(h + 1) * Q_TILE, :] = _dot(
            qq_ref[:, h * HEAD_DIM:(h + 1) * HEAD_DIM], wuk_ref[h]).astype(BF16)
    w_idx = wi_ref[...]
    for hh in range(IDX_HEADS):
        wb_scr[hh] = jnp.broadcast_to(w_idx[:, hh:hh + 1], (Q_TILE, LANES))

    def head_weight(hh):
        wb = wb_scr[hh]
        return jnp.concatenate([wb] * (K_CHUNK // LANES), axis=1)

    def idx_chunk(j, carry):
        ks = kk_ref[pl.ds(pl.multiple_of(j * K_CHUNK, K_CHUNK), K_CHUNK), :]
        k_even, k_odd = ks[:, :LANES], ks[:, LANES:]
        score = jnp.zeros((Q_TILE, K_CHUNK), F32)
        for p in range(n_pairs):
            q_pair = qq_ref[:, attn_w + p * LANES:attn_w + (p + 1) * LANES]
            score = score + jnp.maximum(_dot_nt(q_pair, k_even), 0.0) * head_weight(2 * p)
            score = score + jnp.maximum(_dot_nt(q_pair, k_odd), 0.0) * head_weight(2 * p + 1)
        causal = (j * K_CHUNK + s_row) <= t_col
        score = jnp.where(causal, score, NEG_INF)
        bits = lax.bitcast_convert_type(score, jnp.int32)
        key = jnp.where(bits < 0, bits ^ jnp.int32(0x7FFFFFFF), bits)
        key_scr[j] = key
        keyt_scr[j] = key.T
        return carry

    lax.fori_loop(0, n_chunks, idx_chunk, 0)

    def count_where(pred):
        acc_rows = 4 * SUBLANES

        def body(j, acc):
            hit = jnp.where(pred(j, keyt_scr[j]), 1.0, 0.0)
            return acc + jnp.sum(hit.reshape(K_CHUNK // acc_rows, acc_rows, Q_TILE), axis=0)

        acc = lax.fori_loop(0, n_chunks, body, jnp.zeros((acc_rows, Q_TILE), F32))
        return jnp.sum(acc, axis=0, keepdims=True)

    k_f = float(top_k)

    def bit_step(_, carry):
        thr, bit = carry
        cand = thr + bit
        cnt = count_where(lambda j, key: key >= cand)
        return jnp.where(cnt >= k_f, cand, thr), lax.shift_right_logical(bit, jnp.int32(1))

    thr, _ = lax.fori_loop(0, 32, bit_step,
                           (jnp.full((1, Q_TILE), INT_MIN, jnp.int32), jnp.int32(INT_MIN)))

    n_gt = count_where(lambda j, key: key > thr)
    n_ge = count_where(lambda j, key: key >= thr)
    need = k_f - n_gt
    neg_key = jnp.int32(np.array(NEG_INF, np.float32).view(np.int32) ^ 0x7FFFFFFF)
    has_tie = jnp.max(jnp.where((n_ge > k_f) & (thr > neg_key), 1.0, 0.0)) > 0.0
    idx_bits = int(math.ceil(math.log2(key_scr.shape[0] * K_CHUNK)))

    def tie_cut():
        def step(_, carry):
            cut, bit = carry
            cand = cut + bit
            cnt = count_where(lambda j, key: (key == thr) & ((j * K_CHUNK + s_col) < cand))
            return jnp.where(cnt < need, cand, cut), lax.shift_right_logical(bit, jnp.int32(1))

        cut, _ = lax.fori_loop(0, idx_bits, step,
                               (jnp.zeros((1, Q_TILE), jnp.int32), jnp.int32(2 ** (idx_bits - 1))))
        return cut

    cut = lax.cond(has_tie, tie_cut, lambda: jnp.full((1, Q_TILE), 2 ** 30, jnp.int32))

    def to_rows(v):
        t = jnp.broadcast_to(v, (Q_TILE, Q_TILE)).T
        return jnp.concatenate([t] * (K_CHUNK // Q_TILE), axis=1)

    thr_b, cut_b = to_rows(thr), to_rows(cut)

    def mask_chunk(j, carry):
        key = key_scr[j]
        s_pos = j * K_CHUNK + s_row
        keep = ((key > thr_b) | ((key == thr_b) & (s_pos <= cut_b))) & (s_pos <= t_col)
        mb_scr[j] = jnp.where(keep, 0.0, NEG_INF)
        return carry

    lax.fori_loop(0, n_chunks, mask_chunk, 0)

    n_near = nb_ref.shape[0]
    rows_h = N_HEADS * Q_TILE
    lane_fold = lambda v, op: functools.reduce(op, [v[:, k * LANES:(k + 1) * LANES] for k in range(K_CHUNK // LANES)])

    def key_rows(j):
        return pl.ds(pl.multiple_of(j * K_CHUNK, K_CHUNK), K_CHUNK)

    def logit_chunk(j, carry):
        near = jnp.minimum(i - (K_CHUNK // Q_TILE) * j, n_near - 1)
        s = _dot_nt(qa_scr[...], c_ref[key_rows(j), :]) * scale + nb_ref[near]
        s = s + jnp.concatenate([mb_scr[j]] * N_HEADS, axis=0)
        s_scr[j] = s
        m_scr[...] = jnp.maximum(m_scr[...], lane_fold(s, jnp.maximum))
        return carry

    m_scr[...] = jnp.full(m_scr.shape, NEG_INF, F32)
    lax.fori_loop(0, n_chunks, logit_chunk, 0)
    m_b = jnp.broadcast_to(jnp.max(m_scr[...], axis=-1, keepdims=True), (rows_h, LANES))
    m_scr[...] = m_b
    l_scr[...] = jnp.zeros(l_scr.shape, F32)
    acc_scr[...] = jnp.zeros(acc_scr.shape, F32)

    def value_chunk(j, carry):
        p = jnp.exp(s_scr[j] - jnp.concatenate([m_scr[...]] * (K_CHUNK // LANES), axis=1))
        l_scr[...] += lane_fold(p, jnp.add)
        acc_scr[...] += _dot(p.astype(BF16), c_ref[key_rows(j), :])
        return carry

    lax.fori_loop(0, n_chunks, value_chunk, 0)
    out = acc_scr[...] / jnp.sum(l_scr[...], axis=-1, keepdims=True)
    for h in range(N_HEADS):
        o_ref[:, h * KV_RANK:(h + 1) * KV_RANK] = out[h * Q_TILE:(h + 1) * Q_TILE].astype(o_ref.dtype)


def _dsa_attention(qq, wi, kk, c_kv, wuk, near_bias, top_k):
    B, tp, _ = qq.shape
    n_q = tp // Q_TILE
    n_chunks_max = tp // K_CHUNK
    rows_h = N_HEADS * Q_TILE
    return pl.pallas_call(
        functools.partial(_attn_kernel, top_k=top_k),
        grid=(B, n_q),
        in_specs=[
            pl.BlockSpec((None, Q_TILE, qq.shape[2]), lambda b, i: (b, i, 0)),
            pl.BlockSpec((None, Q_TILE, 128), lambda b, i: (b, i, 0)),
            pl.BlockSpec((None, tp, 256), lambda b, i: (b, 0, 0)),
            pl.BlockSpec((None, tp, KV_RANK), lambda b, i: (b, 0, 0)),
            pl.BlockSpec(wuk.shape, lambda b, i: (0, 0, 0)),
            pl.BlockSpec(near_bias.shape, lambda b, i: (0, 0, 0)),
        ],
        out_specs=pl.BlockSpec((None, Q_TILE, N_HEADS * KV_RANK), lambda b, i: (b, i, 0)),
        out_shape=jax.ShapeDtypeStruct((B, tp, N_HEADS * KV_RANK), BF16),
        scratch_shapes=[pltpu.VMEM((n_chunks_max, Q_TILE, K_CHUNK), jnp.int32),
                        pltpu.VMEM((n_chunks_max, K_CHUNK, Q_TILE), jnp.int32),
                        pltpu.VMEM((n_chunks_max, Q_TILE, K_CHUNK), F32),
                        pltpu.VMEM((IDX_HEADS, Q_TILE, LANES), F32),
                        pltpu.VMEM((rows_h, KV_RANK), BF16),
                        pltpu.VMEM((n_chunks_max, rows_h, K_CHUNK), F32),
                        pltpu.VMEM((rows_h, LANES), F32),
                        pltpu.VMEM((rows_h, LANES), F32),
                        pltpu.VMEM((rows_h, KV_RANK), F32)],
        compiler_params=_compiler_params(("parallel", "arbitrary")),
        name="dsa_attention",
    )(qq, wi, kk, c_kv, wuk, near_bias)


def _mix_kernel(ol_ref, pd_ref, h_ref, wuv_ref, wp_ref, ps_ref, wo_ref, g_ref, b_ref, h1_ref, h1b_ref):
    attn = [_dot(ol_ref[:, h * KV_RANK:(h + 1) * KV_RANK], wuv_ref[h]) for h in range(N_HEADS)]
    pool = [_dot(pd_ref[:, g * POOL_GROUP:(g + 1) * POOL_GROUP], wp_ref[g]) for g in range(len(POOL_WINDOWS))]
    pool = jnp.concatenate(pool, axis=-1) * ps_ref[...]
    cat = jnp.concatenate(attn + [pool], axis=-1).astype(BF16)
    y = ALPHA * h_ref[...] + _dot(cat, wo_ref[...])
    h1 = _layer_norm(y, g_ref[...], b_ref[...])
    h1_ref[...] = h1
    h1b_ref[...] = h1.astype(h1b_ref.dtype)


def _mix_ln1(o_lat, pool_diff, hp, wuv, wpool, pool_scale, wo, g, b):
    B, tp, d = hp.shape
    row = lambda n: pl.BlockSpec((None, MIX_ROWS, n), lambda bi, r: (bi, r, 0))
    full = lambda a: pl.BlockSpec(a.shape, lambda bi, r: (0,) * a.ndim)
    return pl.pallas_call(
        _mix_kernel,
        grid=(B, tp // MIX_ROWS),
        in_specs=[row(o_lat.shape[2]), row(pool_diff.shape[2]), row(d),
                  full(wuv), full(wpool), full(pool_scale), full(wo), full(g), full(b)],
        out_specs=[row(d), row(d)],
        out_shape=[jax.ShapeDtypeStruct((B, tp, d), F32), jax.ShapeDtypeStruct((B, tp, d), BF16)],
        compiler_params=_compiler_params(("parallel", "arbitrary")),
        name="mix_ln1",
    )(o_lat, pool_diff, hp, wuv, wpool, pool_scale, wo, g, b)


def _gelu_tanh(x):
    return 0.5 * x * (1.0 + jnp.tanh(math.sqrt(2.0 / math.pi) * (x + 0.044715 * (x * x * x))))


def _ffn_kernel(hw_ref, hres_ref, wa_ref, wg_ref, cwa_ref, cwg_ref, cba_ref, cbg_ref, wd_ref, g_ref, b_ref, o_ref):
    c = pl.program_id(2)
    x = hw_ref[...]

    def conv(z, cw_ref, cb_ref):
        cw = cw_ref[...]
        n = z.shape[0]
        out = z[HALO - 2:n - 2] * cw[0:1] + z[HALO - 1:n - 1] * cw[1:2] + z[HALO:] * cw[2:3]
        return out + cb_ref[...]

    a = conv(_dot(x, wa_ref[...]), cwa_ref, cba_ref)
    gate = conv(_dot(x, wg_ref[...]), cwg_ref, cbg_ref)
    act = (_gelu_tanh(a) * gate).astype(BF16)
    part = _dot(act, wd_ref[...])

    @pl.when(c == 0)
    def _():
        o_ref[...] = part

    @pl.when(c > 0)
    def _():
        o_ref[...] += part

    @pl.when(c == pl.num_programs(2) - 1)
    def _():
        o_ref[...] = _layer_norm(ALPHA * hres_ref[...] + o_ref[...], g_ref[...], b_ref[...])


def _ffn_ln2(h1, h1b, w_up, conv_w, conv_b, w_down, g, b, seq):
    B, tp, d = h1.shape
    d_ff = w_down.shape[0]
    n_c = d_ff // FFN_COLS
    vec = lambda off: pl.BlockSpec((1, FFN_COLS), lambda bi, r, c: (0, c + off))
    return pl.pallas_call(
        _ffn_kernel,
        grid=(B, seq // FFN_ROWS, n_c),
        in_specs=[
            pl.BlockSpec((None, pl.Element(FFN_ROWS + HALO), pl.Element(d)),
                         lambda bi, r, c: (bi, r * FFN_ROWS + N_META - HALO, 0)),
            pl.BlockSpec((None, pl.Element(FFN_ROWS), pl.Element(d)),
                         lambda bi, r, c: (bi, pl.multiple_of(r * FFN_ROWS + N_META, N_META), 0)),
            pl.BlockSpec((d, FFN_COLS), lambda bi, r, c: (0, c)),
            pl.BlockSpec((d, FFN_COLS), lambda bi, r, c: (0, c + n_c)),
            pl.BlockSpec((CONV_WIDTH, FFN_COLS), lambda bi, r, c: (0, c)),
            pl.BlockSpec((CONV_WIDTH, FFN_COLS), lambda bi, r, c: (0, c + n_c)),
            vec(0), vec(n_c),
            pl.BlockSpec((FFN_COLS, d), lambda bi, r, c: (c, 0)),
            pl.BlockSpec((1, d), lambda bi, r, c: (0, 0)),
            pl.BlockSpec((1, d), lambda bi, r, c: (0, 0)),
        ],
        out_specs=pl.BlockSpec((None, FFN_ROWS, d), lambda bi, r, c: (bi, r, 0)),
        out_shape=jax.ShapeDtypeStruct((B, seq, d), F32),
        compiler_params=_compiler_params(("parallel", "parallel", "arbitrary")),
        name="ffn_ln2",
    )(h1b, h1, w_up, w_up, conv_w, conv_w, conv_b, conv_b, w_down, g, b)


def _t5_bucket_table(n):
    dist = np.arange(n, dtype=np.int32)
    max_exact = REL_BUCKETS // 2
    d_f = np.maximum(dist, 1).astype(np.float32)
    large = max_exact + (np.log(d_f / np.float32(max_exact)) / np.float32(math.log(REL_MAX_DIST / max_exact))
                         * np.float32(REL_BUCKETS - max_exact)).astype(np.int32)
    return np.where(dist < max_exact, dist, np.minimum(large, REL_BUCKETS - 1))


def _near_bias_tiles(rel_bias):
    n_cls = 2 * (K_CHUNK // Q_TILE)
    buckets = _t5_bucket_table(n_cls * Q_TILE + K_CHUNK)
    first_far = int(np.argmax(buckets == REL_BUCKETS - 1))
    assert np.all(buckets[first_far:] == REL_BUCKETS - 1) and (n_cls - 1) * Q_TILE - (K_CHUNK - 1) >= first_far
    width = 2 * K_CHUNK
    period = width + 1
    u = np.arange(period)
    k = np.arange(n_cls)[:, None]
    dist = np.where(u < K_CHUNK, k * Q_TILE - u, k * Q_TILE + period - u)
    idx = buckets[np.clip(dist, 0, len(buckets) - 1)]
    rel = rel_bias.astype(F32) - rel_bias[REL_BUCKETS - 1:].astype(F32)
    diag = jnp.transpose(rel[idx], (0, 2, 1))
    tiles = jnp.tile(diag, (1, 1, Q_TILE))[:, :, :Q_TILE * width].reshape(n_cls, N_HEADS, Q_TILE, width)
    return tiles[..., :K_CHUNK].reshape(n_cls, N_HEADS * Q_TILE, K_CHUNK)


def kernel(x, meta, rel_bias, w_in, kv_norm_g, w_uk, w_uv, w_pool, pool_scale, w_o, ln1_g, ln1_b, w_up, conv_w,
           conv_b, w_down, ln2_g, ln2_b):
    B, S, D = x.shape
    assert w_in.shape[0] == DEPTH and S % FFN_ROWS == 0
    T = S + N_META
    tp = -(-T // ROW_ALIGN) * ROW_ALIGN
    assert tp - T >= max(POOL_WINDOWS)
    top_k = min(TOPK_MAX, S // 4)
    assert top_k <= K_CHUNK

    h = jnp.concatenate([jnp.broadcast_to(meta[None].astype(x.dtype), (B, N_META, D)), x], axis=1)
    hp = jnp.pad(h, ((0, 0), (0, tp - T), (0, 0)))
    hb = hp.astype(BF16)

    attn_w = N_HEADS * HEAD_DIM
    idx_w = IDX_HEADS * IDX_DIM
    o_c, o_qi = attn_w, attn_w + KV_RANK
    o_ki = o_qi + idx_w
    o_wi = o_ki + IDX_DIM
    o_u = o_wi + IDX_HEADS
    w = w_in[0]
    w_qq = jnp.concatenate([w[:, :o_c], w[:, o_qi:o_ki] * (IDX_DIM ** -0.5)], axis=1).astype(BF16)
    w_u = w[:, o_u:].astype(BF16)
    z64 = jnp.zeros((D, IDX_DIM), w.dtype)
    w_small = jnp.concatenate([w[:, o_c:o_qi], w[:, o_ki:o_wi], z64, z64, w[:, o_ki:o_wi],
                               w[:, o_wi:o_u] * (IDX_HEADS ** -0.5), jnp.zeros((D, 128 - IDX_HEADS), w.dtype)],
                              axis=1).astype(BF16)

    qq = _proj_qq(hb, w_qq)
    pool_diff = _proj_pool(hb, w_u)
    c_kv, kk, wi = _proj_small(hb, w_small, kv_norm_g[0].reshape(1, KV_RANK))

    wuk = jnp.transpose(w_uk[0], (1, 2, 0)).astype(BF16)
    o_lat = _dsa_attention(qq, wi, kk, c_kv, wuk, _near_bias_tiles(rel_bias), top_k)

    wuv = jnp.transpose(w_uv[0], (1, 0, 2)).astype(BF16)
    h1, h1b = _mix_ln1(o_lat, pool_diff, hp, wuv, w_pool[0].astype(BF16), pool_scale[0].reshape(1, -1),
                       w_o[0].astype(BF16), ln1_g[0].reshape(1, D), ln1_b[0].reshape(1, D))

    return _ffn_ln2(h1, h1b, w_up[0].astype(BF16), conv_w[0], conv_b[0].reshape(1, -1), w_down[0].astype(BF16),
                    ln2_g[0].reshape(1, D), ln2_b[0].reshape(1, D), S)
```

```python
import functools
import math

import numpy as np
import jax
import jax.numpy as jnp
from jax import lax
from jax.experimental import pallas as pl
from jax.experimental.pallas import tpu as pltpu

F32 = jnp.float32
BF16 = jnp.bfloat16

N_META = 16
N_HEADS = 8
HEAD_DIM = 128
KV_RANK = 256
IDX_HEADS = 16
IDX_DIM = 64
TOPK_MAX = 256
POOL_WINDOWS = (2, 4, 8, 16)
POOL_GROUP = 256
CONV_WIDTH = 3
REL_BUCKETS = 32
REL_MAX_DIST = 128
DEPTH = 1
ALPHA = (2.0 * DEPTH) ** 0.25
LN_EPS = 1e-5
NEG_INF = -1e30

VMEM_LIMIT_BYTES = 56 * 1024 * 1024
SUBLANES = 8
LANES = 128
ROW_ALIGN = 768
PROJ_ROWS = 768
Q_TILE = 256
K_CHUNK = 256
MIX_ROWS = 384
FFN_ROWS = 1024
FFN_COLS = 512
HALO = 16
INT_MIN = -(2 ** 31)


def _dot(a, b):
    return jnp.dot(a, b, preferred_element_type=F32)


def _dot_nt(a, b):
    return lax.dot_general(a, b, (((1,), (1,)), ((), ())), preferred_element_type=F32)


def _layer_norm(y, g, b):
    mu = jnp.mean(y, axis=-1, keepdims=True)
    yc = y - mu
    var = jnp.mean(yc * yc, axis=-1, keepdims=True)
    return yc * lax.rsqrt(var + LN_EPS) * g + b


def _proj_rows(a_ref, w_ref, store):
    n_steps = a_ref.shape[0] // PROJ_ROWS

    def body(r, carry):
        rows = pl.ds(pl.multiple_of(r * PROJ_ROWS, PROJ_ROWS), PROJ_ROWS)
        store(rows, _dot(a_ref[rows, :], w_ref[...]))
        return carry

    lax.fori_loop(0, n_steps, body, 0)


def _proj_cast_kernel(a_ref, w_ref, o_ref):
    def store(rows, acc):
        o_ref[rows, :] = acc.astype(o_ref.dtype)

    _proj_rows(a_ref, w_ref, store)


def _proj_pool_kernel(a_ref, w_ref, o_ref, u_scr):
    def store(rows, acc):
        u_scr[rows, :] = acc

    _proj_rows(a_ref, w_ref, store)
    group = pl.program_id(1)
    tp = u_scr.shape[0]
    pos = lax.broadcasted_iota(jnp.int32, (tp, 1), 0)
    for g, window in enumerate(POOL_WINDOWS):

        @pl.when(group == g)
        def _():
            u = u_scr[...]
            s = u
            shift = 1
            while shift < window:
                s = s + pltpu.roll(s, shift, axis=0)
                shift *= 2
            count = jnp.minimum(pos + 1, window).astype(F32)
            o_ref[...] = (s / count - u).astype(o_ref.dtype)


def _proj_small_kernel(a_ref, w_ref, g_ref, c_ref, kk_ref, wi_ref):
    def store(rows, acc):
        c = acc[:, :KV_RANK]
        ms = jnp.mean(c * c, axis=-1, keepdims=True)
        c_ref[rows, :] = (c * lax.rsqrt(ms + LN_EPS) * g_ref[...]).astype(c_ref.dtype)
        kk_ref[rows, :] = acc[:, KV_RANK:KV_RANK + 256].astype(kk_ref.dtype)
        wi_ref[rows, :] = acc[:, KV_RANK + 256:]

    _proj_rows(a_ref, w_ref, store)


def _batch_block(tp, n):
    return pl.BlockSpec((None, tp, n), lambda b, j: (b, 0, 0))


def _compiler_params(semantics):
    return pltpu.CompilerParams(dimension_semantics=semantics, vmem_limit_bytes=VMEM_LIMIT_BYTES)


def _proj_qq(hb, w):
    B, tp, d = hb.shape
    n = w.shape[1]
    tn = 512
    return pl.pallas_call(
        _proj_cast_kernel,
        grid=(B, n // tn),
        in_specs=[_batch_block(tp, d), pl.BlockSpec((d, tn), lambda b, j: (0, j))],
        out_specs=pl.BlockSpec((None, tp, tn), lambda b, j: (b, 0, j)),
        out_shape=jax.ShapeDtypeStruct((B, tp, n), BF16),
        compiler_params=_compiler_params(("parallel", "arbitrary")),
        name="proj_qq",
    )(hb, w)


def _proj_pool(hb, w):
    B, tp, d = hb.shape
    n = w.shape[1]
    return pl.pallas_call(
        _proj_pool_kernel,
        grid=(B, n // POOL_GROUP),
        in_specs=[_batch_block(tp, d), pl.BlockSpec((d, POOL_GROUP), lambda b, j: (0, j))],
        out_specs=pl.BlockSpec((None, tp, POOL_GROUP), lambda b, j: (b, 0, j)),
        out_shape=jax.ShapeDtypeStruct((B, tp, n), BF16),
        scratch_shapes=[pltpu.VMEM((tp, POOL_GROUP), F32)],
        compiler_params=_compiler_params(("parallel", "arbitrary")),
        name="proj_pool",
    )(hb, w)


def _proj_small(hb, w, kv_g):
    B, tp, d = hb.shape
    n = w.shape[1]
    return pl.pallas_call(
        _proj_small_kernel,
        grid=(B, 1),
        in_specs=[_batch_block(tp, d), pl.BlockSpec((d, n), lambda b, j: (0, 0)),
                  pl.BlockSpec((1, KV_RANK), lambda b, j: (0, 0))],
        out_specs=[_batch_block(tp, KV_RANK), _batch_block(tp, 256), _batch_block(tp, 128)],
        out_shape=[jax.ShapeDtypeStruct((B, tp, KV_RANK), BF16),
                   jax.ShapeDtypeStruct((B, tp, 256), BF16),
                   jax.ShapeDtypeStruct((B, tp, 128), F32)],
        compiler_params=_compiler_params(("parallel", "arbitrary")),
        name="proj_small",
    )(hb, w, kv_g)


def _attn_kernel(qq_ref, wi_ref, kk_ref, c_ref, wuk_ref, nb_ref, o_ref,
                 key_scr, keyt_scr, mb_scr, wt_scr, qa_scr, s_scr, m_scr, l_scr, acc_scr, *, top_k):
    i = pl.program_id(1)
    n_chunks = i // (K_CHUNK // Q_TILE) + 1
    attn_w = N_HEADS * HEAD_DIM
    scale = HEAD_DIM ** -0.5
    n_pairs = IDX_HEADS // 2

    t_col = i * Q_TILE + lax.broadcasted_iota(jnp.int32, (Q_TILE, 1), 0)
    t_row = i * Q_TILE + lax.broadcasted_iota(jnp.int32, (1, Q_TILE), 1)
    s_row = lax.broadcasted_iota(jnp.int32, (1, K_CHUNK), 1)
    s_col = lax.broadcasted_iota(jnp.int32, (K_CHUNK, 1), 0)

    for h in range(N_HEADS):
        qa_scr[h * Q_TILE:(h + 1) * Q_TILE, :] = _dot(
            qq_ref[:, h * HEAD_DIM:(h + 1) * HEAD_DIM], wuk_ref[h]).astype(BF16)
    wt_scr[...] = wi_ref[...].T

    def idx_chunk(j, carry):
        ks = kk_ref[pl.ds(pl.multiple_of(j * K_CHUNK, K_CHUNK), K_CHUNK), :]
        k_even, k_odd = ks[:, :LANES], ks[:, LANES:]
        score = jnp.zeros((K_CHUNK, Q_TILE), F32)
        for p in range(n_pairs):
            q_pair = qq_ref[:, attn_w + p * LANES:attn_w + (p + 1) * LANES]
            score = score + jnp.maximum(_dot_nt(k_even, q_pair), 0.0) * wt_scr[2 * p:2 * p + 1, :]
            score = score + jnp.maximum(_dot_nt(k_odd, q_pair), 0.0) * wt_scr[2 * p + 1:2 * p + 2, :]
        causal = (j * K_CHUNK + s_col) <= t_row
        score = jnp.where(causal, score, NEG_INF)
        bits = lax.bitcast_convert_type(score, jnp.int32)
        key = jnp.where(bits < 0, bits ^ jnp.int32(0x7FFFFFFF), bits)
        keyt_scr[j] = key
        key_scr[j] = key.T
        return carry

    lax.fori_loop(0, n_chunks, idx_chunk, 0)

    def count_where(pred):
        acc_rows = 4 * SUBLANES

        def body(j, acc):
            hit = jnp.where(pred(j, keyt_scr[j]), 1.0, 0.0)
            return acc + jnp.sum(hit.reshape(K_CHUNK // acc_rows, acc_rows, Q_TILE), axis=0)

        acc = lax.fori_loop(0, n_chunks, body, jnp.zeros((acc_rows, Q_TILE), F32))
        return jnp.sum(acc, axis=0, keepdims=True)

    k_f = float(top_k)

    def bit_step(_, carry):
        thr, bit = carry
        cand = thr + bit
        cnt = count_where(lambda j, key: key >= cand)
        return jnp.where(cnt >= k_f, cand, thr), lax.shift_right_logical(bit, jnp.int32(1))

    thr, _ = lax.fori_loop(0, 32, bit_step,
                           (jnp.full((1, Q_TILE), INT_MIN, jnp.int32), jnp.int32(INT_MIN)))

    n_gt = count_where(lambda j, key: key > thr)
    n_ge = count_where(lambda j, key: key >= thr)
    need = k_f - n_gt
    neg_key = jnp.int32(np.array(NEG_INF, np.float32).view(np.int32) ^ 0x7FFFFFFF)
    has_tie = jnp.max(jnp.where((n_ge > k_f) & (thr > neg_key), 1.0, 0.0)) > 0.0
    idx_bits = int(math.ceil(math.log2(key_scr.shape[0] * K_CHUNK)))

    def tie_cut():
        def step(_, carry):
            cut, bit = carry
            cand = cut + bit
            cnt = count_where(lambda j, key: (key == thr) & ((j * K_CHUNK + s_col) < cand))
            return jnp.where(cnt < need, cand, cut), lax.shift_right_logical(bit, jnp.int32(1))

        cut, _ = lax.fori_loop(0, idx_bits, step,
                               (jnp.zeros((1, Q_TILE), jnp.int32), jnp.int32(2 ** (idx_bits - 1))))
        return cut

    cut = lax.cond(has_tie, tie_cut, lambda: jnp.full((1, Q_TILE), 2 ** 30, jnp.int32))

    def to_rows(v):
        t = jnp.broadcast_to(v, (Q_TILE, Q_TILE)).T
        return jnp.concatenate([t] * (K_CHUNK // Q_TILE), axis=1)

    thr_b, cut_b = to_rows(thr), to_rows(cut)

    def mask_chunk(j, carry):
        key = key_scr[j]
        s_pos = j * K_CHUNK + s_row
        keep = ((key > thr_b) | ((key == thr_b) & (s_pos <= cut_b))) & (s_pos <= t_col)
        mb_scr[j] = jnp.where(keep, 0.0, NEG_INF)
        return carry

    lax.fori_loop(0, n_chunks, mask_chunk, 0)

    n_near = nb_ref.shape[0]
    rows_h = N_HEADS * Q_TILE
    lane_fold = lambda v, op: functools.reduce(op, [v[:, k * LANES:(k + 1) * LANES] for k in range(K_CHUNK // LANES)])

    def key_rows(j):
        return pl.ds(pl.multiple_of(j * K_CHUNK, K_CHUNK), K_CHUNK)

    def logit_chunk(j, carry):
        near = jnp.minimum(i - (K_CHUNK // Q_TILE) * j, n_near - 1)
        s = _dot_nt(qa_scr[...], c_ref[key_rows(j), :]) * scale + nb_ref[near]
        s = s + jnp.concatenate([mb_scr[j]] * N_HEADS, axis=0)
        s_scr[j] = s
        m_scr[...] = jnp.maximum(m_scr[...], lane_fold(s, jnp.maximum))
        return carry

    m_scr[...] = jnp.full(m_scr.shape, NEG_INF, F32)
    lax.fori_loop(0, n_chunks, logit_chunk, 0)
    m_b = jnp.broadcast_to(jnp.max(m_scr[...], axis=-1, keepdims=True), (rows_h, LANES))
    m_scr[...] = m_b
    l_scr[...] = jnp.zeros(l_scr.shape, F32)
    acc_scr[...] = jnp.zeros(acc_scr.shape, F32)

    def value_chunk(j, carry):
        p = jnp.exp(s_scr[j] - jnp.concatenate([m_scr[...]] * (K_CHUNK // LANES), axis=1))
        l_scr[...] += lane_fold(p, jnp.add)
        acc_scr[...] += _dot(p.astype(BF16), c_ref[key_rows(j), :])
        return carry

    lax.fori_loop(0, n_chunks, value_chunk, 0)
    out = acc_scr[...] * (1.0 / jnp.sum(l_scr[...], axis=-1, keepdims=True))
    for h in range(N_HEADS):
        o_ref[:, h * KV_RANK:(h + 1) * KV_RANK] = out[h * Q_TILE:(h + 1) * Q_TILE].astype(o_ref.dtype)


def _dsa_attention(qq, wi, kk, c_kv, wuk, near_bias, top_k):
    B, tp, _ = qq.shape
    n_q = tp // Q_TILE
    n_chunks_max = tp // K_CHUNK
    rows_h = N_HEADS * Q_TILE
    return pl.pallas_call(
        functools.partial(_attn_kernel, top_k=top_k),
        grid=(B, n_q),
        in_specs=[
            pl.BlockSpec((None, Q_TILE, qq.shape[2]), lambda b, i: (b, i, 0)),
            pl.BlockSpec((None, Q_TILE, 128), lambda b, i: (b, i, 0)),
            pl.BlockSpec((None, tp, 256), lambda b, i: (b, 0, 0)),
            pl.BlockSpec((None, tp, KV_RANK), lambda b, i: (b, 0, 0)),
            pl.BlockSpec(wuk.shape, lambda b, i: (0, 0, 0)),
            pl.BlockSpec(near_bias.shape, lambda b, i: (0, 0, 0), pipeline_mode=pl.Buffered(1)),
        ],
        out_specs=pl.BlockSpec((None, Q_TILE, N_HEADS * KV_RANK), lambda b, i: (b, i, 0)),
        out_shape=jax.ShapeDtypeStruct((B, tp, N_HEADS * KV_RANK), BF16),
        scratch_shapes=[pltpu.VMEM((n_chunks_max, Q_TILE, K_CHUNK), jnp.int32),
                        pltpu.VMEM((n_chunks_max, K_CHUNK, Q_TILE), jnp.int32),
                        pltpu.VMEM((n_chunks_max, Q_TILE, K_CHUNK), F32),
                        pltpu.VMEM((LANES, Q_TILE), F32),
                        pltpu.VMEM((rows_h, KV_RANK), BF16),
                        pltpu.VMEM((n_chunks_max, rows_h, K_CHUNK), F32),
                        pltpu.VMEM((rows_h, LANES), F32),
                        pltpu.VMEM((rows_h, LANES), F32),
                        pltpu.VMEM((rows_h, KV_RANK), F32)],
        compiler_params=_compiler_params(("parallel", "arbitrary")),
        name="dsa_attention",
    )(qq, wi, kk, c_kv, wuk, near_bias)


def _mix_kernel(ol_ref, pd_ref, h_ref, wuv_ref, wp_ref, ps_ref, wo_ref, g_ref, b_ref, h1_ref, h1b_ref):
    attn = [_dot(ol_ref[:, h * KV_RANK:(h + 1) * KV_RANK], wuv_ref[h]) for h in range(N_HEADS)]
    pool = [_dot(pd_ref[:, g * POOL_GROUP:(g + 1) * POOL_GROUP], wp_ref[g]) for g in range(len(POOL_WINDOWS))]
    pool = jnp.concatenate(pool, axis=-1) * ps_ref[...]
    cat = jnp.concatenate(attn + [pool], axis=-1).astype(BF16)
    y = ALPHA * h_ref[...] + _dot(cat, wo_ref[...])
    h1 = _layer_norm(y, g_ref[...], b_ref[...])
    h1_ref[...] = h1
    h1b_ref[...] = h1.astype(h1b_ref.dtype)


def _mix_ln1(o_lat, pool_diff, hp, wuv, wpool, pool_scale, wo, g, b):
    B, tp, d = hp.shape
    row = lambda n: pl.BlockSpec((None, MIX_ROWS, n), lambda bi, r: (bi, r, 0))
    full = lambda a: pl.BlockSpec(a.shape, lambda bi, r: (0,) * a.ndim)
    return pl.pallas_call(
        _mix_kernel,
        grid=(B, tp // MIX_ROWS),
        in_specs=[row(o_lat.shape[2]), row(pool_diff.shape[2]), row(d),
                  full(wuv), full(wpool), full(pool_scale), full(wo), full(g), full(b)],
        out_specs=[row(d), row(d)],
        out_shape=[jax.ShapeDtypeStruct((B, tp, d), F32), jax.ShapeDtypeStruct((B, tp, d), BF16)],
        compiler_params=_compiler_params(("parallel", "arbitrary")),
        name="mix_ln1",
    )(o_lat, pool_diff, hp, wuv, wpool, pool_scale, wo, g, b)


def _gelu_tanh(x):
    return 0.5 * x * (1.0 + jnp.tanh(math.sqrt(2.0 / math.pi) * (x + 0.044715 * (x * x * x))))


def _ffn_kernel(hw_ref, hres_ref, wa_ref, wg_ref, cwa_ref, cwg_ref, cba_ref, cbg_ref, wd_ref, g_ref, b_ref, o_ref):
    c = pl.program_id(2)
    x = hw_ref[...]

    def conv(z, cw_ref, cb_ref):
        cw = cw_ref[...]
        n = z.shape[0]
        out = z[HALO - 2:n - 2] * cw[0:1] + z[HALO - 1:n - 1] * cw[1:2] + z[HALO:] * cw[2:3]
        return out + cb_ref[...]

    @pl.when(c == 0)
    def _():
        o_ref[...] = jnp.zeros(o_ref.shape, o_ref.dtype)

    a = conv(_dot(x, wa_ref[...]), cwa_ref, cba_ref)
    gate = conv(_dot(x, wg_ref[...]), cwg_ref, cbg_ref)
    act = (_gelu_tanh(a) * gate).astype(BF16)
    o_ref[...] += _dot(act, wd_ref[...])

    @pl.when(c == pl.num_programs(2) - 1)
    def _():
        o_ref[...] = _layer_norm(ALPHA * hres_ref[...] + o_ref[...], g_ref[...], b_ref[...])


def _ffn_ln2(h1, h1b, w_up, conv_w, conv_b, w_down, g, b, seq):
    B, tp, d = h1.shape
    d_ff = w_down.shape[0]
    n_c = d_ff // FFN_COLS
    vec = lambda off: pl.BlockSpec((1, FFN_COLS), lambda bi, r, c: (0, c + off))
    return pl.pallas_call(
        _ffn_kernel,
        grid=(B, seq // FFN_ROWS, n_c),
        in_specs=[
            pl.BlockSpec((None, pl.Element(FFN_ROWS + HALO), pl.Element(d)),
                         lambda bi, r, c: (bi, r * FFN_ROWS + N_META - HALO, 0)),
            pl.BlockSpec((None, pl.Element(FFN_ROWS), pl.Element(d)),
                         lambda bi, r, c: (bi, pl.multiple_of(r * FFN_ROWS + N_META, N_META), 0),
                         pipeline_mode=pl.Buffered(1)),
            pl.BlockSpec((d, FFN_COLS), lambda bi, r, c: (0, c)),
            pl.BlockSpec((d, FFN_COLS), lambda bi, r, c: (0, c + n_c)),
            pl.BlockSpec((CONV_WIDTH, FFN_COLS), lambda bi, r, c: (0, c)),
            pl.BlockSpec((CONV_WIDTH, FFN_COLS), lambda bi, r, c: (0, c + n_c)),
            vec(0), vec(n_c),
            pl.BlockSpec((FFN_COLS, d), lambda bi, r, c: (c, 0)),
            pl.BlockSpec((1, d), lambda bi, r, c: (0, 0)),
            pl.BlockSpec((1, d), lambda bi, r, c: (0, 0)),
        ],
        out_specs=pl.BlockSpec((None, FFN_ROWS, d), lambda bi, r, c: (bi, r, 0), pipeline_mode=pl.Buffered(1)),
        out_shape=jax.ShapeDtypeStruct((B, seq, d), F32),
        compiler_params=_compiler_params(("parallel", "parallel", "arbitrary")),
        name="ffn_ln2",
    )(h1b, h1, w_up, w_up, conv_w, conv_w, conv_b, conv_b, w_down, g, b)


def _t5_bucket_table(n):
    dist = np.arange(n, dtype=np.int32)
    max_exact = REL_BUCKETS // 2
    d_f = np.maximum(dist, 1).astype(np.float32)
    large = max_exact + (np.log(d_f / np.float32(max_exact)) / np.float32(math.log(REL_MAX_DIST / max_exact))
                         * np.float32(REL_BUCKETS - max_exact)).astype(np.int32)
    return np.where(dist < max_exact, dist, np.minimum(large, REL_BUCKETS - 1))


def _near_bias_tiles(rel_bias):
    probe = _t5_bucket_table(4 * REL_MAX_DIST)
    first_far = int(np.argmax(probe == REL_BUCKETS - 1))
    assert np.all(probe[first_far:] == REL_BUCKETS - 1)
    n_cls = -(-(first_far + K_CHUNK - 1) // Q_TILE) + 1
    buckets = _t5_bucket_table(n_cls * Q_TILE + K_CHUNK)
    width = 2 * K_CHUNK
    period = width + 1
    u = np.arange(period)
    k = np.arange(n_cls)[:, None]
    dist = np.where(u < K_CHUNK, k * Q_TILE - u, k * Q_TILE + period - u)
    idx = buckets[np.clip(dist, 0, len(buckets) - 1)]
    rel = rel_bias.astype(F32) - rel_bias[REL_BUCKETS - 1:].astype(F32)
    diag = jnp.transpose(rel[idx], (0, 2, 1))
    tiles = jnp.tile(diag, (1, 1, Q_TILE))[:, :, :Q_TILE * width].reshape(n_cls, N_HEADS, Q_TILE, width)
    return tiles[..., :K_CHUNK].reshape(n_cls, N_HEADS * Q_TILE, K_CHUNK)


def kernel(x, meta, rel_bias, w_in, kv_norm_g, w_uk, w_uv, w_pool, pool_scale, w_o, ln1_g, ln1_b, w_up, conv_w,
           conv_b, w_down, ln2_g, ln2_b):
    B, S, D = x.shape
    assert w_in.shape[0] == DEPTH and S % FFN_ROWS == 0
    T = S + N_META
    tp = -(-T // ROW_ALIGN) * ROW_ALIGN
    assert tp - T >= max(POOL_WINDOWS)
    top_k = min(TOPK_MAX, S // 4)
    assert top_k <= K_CHUNK

    h = jnp.concatenate([jnp.broadcast_to(meta[None].astype(x.dtype), (B, N_META, D)), x], axis=1)
    hp = jnp.pad(h, ((0, 0), (0, tp - T), (0, 0)))
    hb = hp.astype(BF16)

    attn_w = N_HEADS * HEAD_DIM
    idx_w = IDX_HEADS * IDX_DIM
    o_c, o_qi = attn_w, attn_w + KV_RANK
    o_ki = o_qi + idx_w
    o_wi = o_ki + IDX_DIM
    o_u = o_wi + IDX_HEADS
    w = w_in[0]
    w_qq = jnp.concatenate([w[:, :o_c], w[:, o_qi:o_ki] * (IDX_DIM ** -0.5)], axis=1).astype(BF16)
    w_u = w[:, o_u:].astype(BF16)
    z64 = jnp.zeros((D, IDX_DIM), w.dtype)
    w_small = jnp.concatenate([w[:, o_c:o_qi], w[:, o_ki:o_wi], z64, z64, w[:, o_ki:o_wi],
                               w[:, o_wi:o_u] * (IDX_HEADS ** -0.5), jnp.zeros((D, 128 - IDX_HEADS), w.dtype)],
                              axis=1).astype(BF16)

    qq = _proj_qq(hb, w_qq)
    pool_diff = _proj_pool(hb, w_u)
    c_kv, kk, wi = _proj_small(hb, w_small, kv_norm_g[0].reshape(1, KV_RANK))

    wuk = jnp.transpose(w_uk[0], (1, 2, 0)).astype(BF16)
    o_lat = _dsa_attention(qq, wi, kk, c_kv, wuk, _near_bias_tiles(rel_bias), top_k)

    wuv = jnp.transpose(w_uv[0], (1, 0, 2)).astype(BF16)
    h1, h1b = _mix_ln1(o_lat, pool_diff, hp, wuv, w_pool[0].astype(BF16), pool_scale[0].reshape(1, -1),
                       w_o[0].astype(BF16), ln1_g[0].reshape(1, D), ln1_b[0].reshape(1, D))

    return _ffn_ln2(h1, h1b, w_up[0].astype(BF16), conv_w[0], conv_b[0].reshape(1, -1), w_down[0].astype(BF16),
                    ln2_g[0].reshape(1, D), ln2_b[0].reshape(1, D), S)
```

```python
import functools
import math

import numpy as np
import jax
import jax.numpy as jnp
from jax import lax
from jax.experimental import pallas as pl
from jax.experimental.pallas import tpu as pltpu

F32 = jnp.float32
BF16 = jnp.bfloat16

N_META = 16
N_HEADS = 8
HEAD_DIM = 128
KV_RANK = 256
IDX_HEADS = 16
IDX_DIM = 64
TOPK_MAX = 256
POOL_WINDOWS = (2, 4, 8, 16)
POOL_GROUP = 256
CONV_WIDTH = 3
REL_BUCKETS = 32
REL_MAX_DIST = 128
DEPTH = 1
ALPHA = (2.0 * DEPTH) ** 0.25
LN_EPS = 1e-5
NEG_INF = -1e30

VMEM_LIMIT_BYTES = 56 * 1024 * 1024
SUBLANES = 8
LANES = 128
ROW_ALIGN = 768
PROJ_ROWS = 768
Q_TILE = 256
K_CHUNK = 256
MIX_ROWS = 384
FFN_ROWS = 1024
FFN_COLS = 512
HALO = 16
INT_MIN = -(2 ** 31)


def _dot(a, b):
    return jnp.dot(a, b, preferred_element_type=F32)


def _dot_nt(a, b):
    return lax.dot_general(a, b, (((1,), (1,)), ((), ())), preferred_element_type=F32)


def _layer_norm(y, g, b):
    mu = jnp.mean(y, axis=-1, keepdims=True)
    yc = y - mu
    var = jnp.mean(yc * yc, axis=-1, keepdims=True)
    return yc * lax.rsqrt(var + LN_EPS) * g + b


def _proj_rows(a_ref, w_ref, store):
    n_steps = a_ref.shape[0] // PROJ_ROWS

    def body(r, carry):
        rows = pl.ds(pl.multiple_of(r * PROJ_ROWS, PROJ_ROWS), PROJ_ROWS)
        store(rows, _dot(a_ref[rows, :], w_ref[...]))
        return carry

    lax.fori_loop(0, n_steps, body, 0)


def _proj_cast_kernel(a_ref, w_ref, o_ref):
    def store(rows, acc):
        o_ref[rows, :] = acc.astype(o_ref.dtype)

    _proj_rows(a_ref, w_ref, store)


def _proj_pool_kernel(a_ref, w_ref, o_ref, u_scr):
    def store(rows, acc):
        u_scr[rows, :] = acc

    _proj_rows(a_ref, w_ref, store)
    group = pl.program_id(1)
    tp = u_scr.shape[0]
    pos = lax.broadcasted_iota(jnp.int32, (tp, 1), 0)
    for g, window in enumerate(POOL_WINDOWS):

        @pl.when(group == g)
        def _():
            u = u_scr[...]
            s = u
            shift = 1
            while shift < window:
                s = s + pltpu.roll(s, shift, axis=0)
                shift *= 2
            count = jnp.minimum(pos + 1, window).astype(F32)
            o_ref[...] = (s / count - u).astype(o_ref.dtype)


def _proj_small_kernel(a_ref, w_ref, g_ref, c_ref, kk_ref, wi_ref):
    def store(rows, acc):
        c = acc[:, :KV_RANK]
        ms = jnp.mean(c * c, axis=-1, keepdims=True)
        c_ref[rows, :] = (c * lax.rsqrt(ms + LN_EPS) * g_ref[...]).astype(c_ref.dtype)
        kk_ref[rows, :] = acc[:, KV_RANK:KV_RANK + 256].astype(kk_ref.dtype)
        wi_ref[rows, :] = acc[:, KV_RANK + 256:]

    _proj_rows(a_ref, w_ref, store)


def _batch_block(tp, n):
    return pl.BlockSpec((None, tp, n), lambda b, j: (b, 0, 0))


def _compiler_params(semantics):
    return pltpu.CompilerParams(dimension_semantics=semantics, vmem_limit_bytes=VMEM_LIMIT_BYTES)


def _proj_qq(hb, w):
    B, tp, d = hb.shape
    n = w.shape[1]
    tn = 512
    return pl.pallas_call(
        _proj_cast_kernel,
        grid=(B, n // tn),
        in_specs=[_batch_block(tp, d), pl.BlockSpec((d, tn), lambda b, j: (0, j))],
        out_specs=pl.BlockSpec((None, tp, tn), lambda b, j: (b, 0, j)),
        out_shape=jax.ShapeDtypeStruct((B, tp, n), BF16),
        compiler_params=_compiler_params(("parallel", "arbitrary")),
        name="proj_qq",
    )(hb, w)


def _proj_pool(hb, w):
    B, tp, d = hb.shape
    n = w.shape[1]
    return pl.pallas_call(
        _proj_pool_kernel,
        grid=(B, n // POOL_GROUP),
        in_specs=[_batch_block(tp, d), pl.BlockSpec((d, POOL_GROUP), lambda b, j: (0, j))],
        out_specs=pl.BlockSpec((None, tp, POOL_GROUP), lambda b, j: (b, 0, j)),
        out_shape=jax.ShapeDtypeStruct((B, tp, n), BF16),
        scratch_shapes=[pltpu.VMEM((tp, POOL_GROUP), F32)],
        compiler_params=_compiler_params(("parallel", "arbitrary")),
        name="proj_pool",
    )(hb, w)


def _proj_small(hb, w, kv_g):
    B, tp, d = hb.shape
    n = w.shape[1]
    return pl.pallas_call(
        _proj_small_kernel,
        grid=(B, 1),
        in_specs=[_batch_block(tp, d), pl.BlockSpec((d, n), lambda b, j: (0, 0)),
                  pl.BlockSpec((1, KV_RANK), lambda b, j: (0, 0))],
        out_specs=[_batch_block(tp, KV_RANK), _batch_block(tp, 256), _batch_block(tp, 128)],
        out_shape=[jax.ShapeDtypeStruct((B, tp, KV_RANK), BF16),
                   jax.ShapeDtypeStruct((B, tp, 256), BF16),
                   jax.ShapeDtypeStruct((B, tp, 128), F32)],
        compiler_params=_compiler_params(("parallel", "arbitrary")),
        name="proj_small",
    )(hb, w, kv_g)


def _attn_kernel(qq_ref, wi_ref, kk_ref, c_ref, wuk_ref, diag_ref, o_ref,
                 nb_scr, key_scr, keyt_scr, mb_scr, wt_scr, qa_scr, s_scr, m_scr, l_scr, acc_scr, *, top_k):
    i = pl.program_id(1)
    n_chunks = i // (K_CHUNK // Q_TILE) + 1
    attn_w = N_HEADS * HEAD_DIM
    scale = HEAD_DIM ** -0.5
    n_pairs = IDX_HEADS // 2

    t_col = i * Q_TILE + lax.broadcasted_iota(jnp.int32, (Q_TILE, 1), 0)
    t_row = i * Q_TILE + lax.broadcasted_iota(jnp.int32, (1, Q_TILE), 1)
    s_row = lax.broadcasted_iota(jnp.int32, (1, K_CHUNK), 1)
    s_col = lax.broadcasted_iota(jnp.int32, (K_CHUNK, 1), 0)

    n_near = nb_scr.shape[0]

    @pl.when((pl.program_id(0) == 0) & (i == 0))
    def _():
        for k in range(n_near - 1):
            for h in range(N_HEADS):
                v = jnp.broadcast_to(diag_ref[k, h:h + 1, :], (Q_TILE, 2 * K_CHUNK))
                t = pltpu.roll(v, 0, 1, stride=1, stride_axis=0)
                nb_scr[k, h * Q_TILE:(h + 1) * Q_TILE, :] = t[:, :K_CHUNK]
        nb_scr[n_near - 1] = jnp.zeros(nb_scr.shape[1:], F32)

    for h in range(N_HEADS):
        qa_scr[h * Q_TILE:(h + 1) * Q_TILE, :] = _dot(
            qq_ref[:, h * HEAD_DIM:(h + 1) * HEAD_DIM], wuk_ref[h]).astype(BF16)
    wt_scr[...] = wi_ref[...].T

    def idx_chunk(j, carry):
        ks = kk_ref[pl.ds(pl.multiple_of(j * K_CHUNK, K_CHUNK), K_CHUNK), :]
        k_even, k_odd = ks[:, :LANES], ks[:, LANES:]
        score = jnp.zeros((K_CHUNK, Q_TILE), F32)
        for p in range(n_pairs):
            q_pair = qq_ref[:, attn_w + p * LANES:attn_w + (p + 1) * LANES]
            score = score + jnp.maximum(_dot_nt(k_even, q_pair), 0.0) * wt_scr[2 * p:2 * p + 1, :]
            score = score + jnp.maximum(_dot_nt(k_odd, q_pair), 0.0) * wt_scr[2 * p + 1:2 * p + 2, :]
        causal = (j * K_CHUNK + s_col) <= t_row
        score = jnp.where(causal, score, NEG_INF)
        bits = lax.bitcast_convert_type(score, jnp.int32)
        key = jnp.where(bits < 0, bits ^ jnp.int32(0x7FFFFFFF), bits)
        keyt_scr[j] = key
        key_scr[j] = key.T
        return carry

    lax.fori_loop(0, n_chunks, idx_chunk, 0)

    def count_where(pred):
        acc_rows = 4 * SUBLANES

        def body(j, acc):
            hit = jnp.where(pred(j, keyt_scr[j]), 1.0, 0.0)
            return acc + jnp.sum(hit.reshape(K_CHUNK // acc_rows, acc_rows, Q_TILE), axis=0)

        acc = lax.fori_loop(0, n_chunks, body, jnp.zeros((acc_rows, Q_TILE), F32))
        return jnp.sum(acc, axis=0, keepdims=True)

    k_f = float(top_k)

    def bit_step(_, carry):
        thr, bit = carry
        cand = thr + bit
        cnt = count_where(lambda j, key: key >= cand)
        return jnp.where(cnt >= k_f, cand, thr), lax.shift_right_logical(bit, jnp.int32(1))

    thr, _ = lax.fori_loop(0, 32, bit_step,
                           (jnp.full((1, Q_TILE), INT_MIN, jnp.int32), jnp.int32(INT_MIN)))

    n_gt = count_where(lambda j, key: key > thr)
    n_ge = count_where(lambda j, key: key >= thr)
    need = k_f - n_gt
    neg_key = jnp.int32(np.array(NEG_INF, np.float32).view(np.int32) ^ 0x7FFFFFFF)
    has_tie = jnp.max(jnp.where((n_ge > k_f) & (thr > neg_key), 1.0, 0.0)) > 0.0
    idx_bits = int(math.ceil(math.log2(key_scr.shape[0] * K_CHUNK)))

    def tie_cut():
        def step(_, carry):
            cut, bit = carry
            cand = cut + bit
            cnt = count_where(lambda j, key: (key == thr) & ((j * K_CHUNK + s_col) < cand))
            return jnp.where(cnt < need, cand, cut), lax.shift_right_logical(bit, jnp.int32(1))

        cut, _ = lax.fori_loop(0, idx_bits, step,
                               (jnp.zeros((1, Q_TILE), jnp.int32), jnp.int32(2 ** (idx_bits - 1))))
        return cut

    cut = lax.cond(has_tie, tie_cut, lambda: jnp.full((1, Q_TILE), 2 ** 30, jnp.int32))

    def to_rows(v):
        t = jnp.broadcast_to(v, (Q_TILE, Q_TILE)).T
        return jnp.concatenate([t] * (K_CHUNK // Q_TILE), axis=1)

    thr_b, cut_b = to_rows(thr), to_rows(cut)

    def mask_chunk(j, carry):
        key = key_scr[j]
        s_pos = j * K_CHUNK + s_row
        keep = ((key > thr_b) | ((key == thr_b) & (s_pos <= cut_b))) & (s_pos <= t_col)
        mb_scr[j] = jnp.where(keep, 0.0, NEG_INF)
        return carry

    lax.fori_loop(0, n_chunks, mask_chunk, 0)

    rows_h = N_HEADS * Q_TILE
    lane_fold = lambda v, op: functools.reduce(op, [v[:, k * LANES:(k + 1) * LANES] for k in range(K_CHUNK // LANES)])

    def key_rows(j):
        return pl.ds(pl.multiple_of(j * K_CHUNK, K_CHUNK), K_CHUNK)

    def logit_chunk(j, carry):
        near = jnp.minimum(i - (K_CHUNK // Q_TILE) * j, n_near - 1)
        s = _dot_nt(qa_scr[...], c_ref[key_rows(j), :]) * scale + nb_scr[near]
        s = s + jnp.concatenate([mb_scr[j]] * N_HEADS, axis=0)
        s_scr[j] = s
        m_scr[...] = jnp.maximum(m_scr[...], lane_fold(s, jnp.maximum))
        return carry

    m_scr[...] = jnp.full(m_scr.shape, NEG_INF, F32)
    lax.fori_loop(0, n_chunks, logit_chunk, 0)
    m_b = jnp.broadcast_to(jnp.max(m_scr[...], axis=-1, keepdims=True), (rows_h, LANES))
    m_scr[...] = m_b
    l_scr[...] = jnp.zeros(l_scr.shape, F32)
    acc_scr[...] = jnp.zeros(acc_scr.shape, F32)

    def value_chunk(j, carry):
        p = jnp.exp(s_scr[j] - jnp.concatenate([m_scr[...]] * (K_CHUNK // LANES), axis=1))
        l_scr[...] += lane_fold(p, jnp.add)
        acc_scr[...] += _dot(p.astype(BF16), c_ref[key_rows(j), :])
        return carry

    lax.fori_loop(0, n_chunks, value_chunk, 0)
    out = acc_scr[...] * (1.0 / jnp.sum(l_scr[...], axis=-1, keepdims=True))
    for h in range(N_HEADS):
        o_ref[:, h * KV_RANK:(h + 1) * KV_RANK] = out[h * Q_TILE:(h + 1) * Q_TILE].astype(o_ref.dtype)


def _dsa_attention(qq, wi, kk, c_kv, wuk, bias_diag, top_k):
    B, tp, _ = qq.shape
    n_q = tp // Q_TILE
    n_chunks_max = tp // K_CHUNK
    rows_h = N_HEADS * Q_TILE
    return pl.pallas_call(
        functools.partial(_attn_kernel, top_k=top_k),
        grid=(B, n_q),
        in_specs=[
            pl.BlockSpec((None, Q_TILE, qq.shape[2]), lambda b, i: (b, i, 0)),
            pl.BlockSpec((None, Q_TILE, 128), lambda b, i: (b, i, 0)),
            pl.BlockSpec((None, tp, 256), lambda b, i: (b, 0, 0)),
            pl.BlockSpec((None, tp, KV_RANK), lambda b, i: (b, 0, 0)),
            pl.BlockSpec(wuk.shape, lambda b, i: (0, 0, 0)),
            pl.BlockSpec(bias_diag.shape, lambda b, i: (0, 0, 0)),
        ],
        out_specs=pl.BlockSpec((None, Q_TILE, N_HEADS * KV_RANK), lambda b, i: (b, i, 0)),
        out_shape=jax.ShapeDtypeStruct((B, tp, N_HEADS * KV_RANK), BF16),
        scratch_shapes=[pltpu.VMEM((bias_diag.shape[0] + 1, rows_h, K_CHUNK), F32),
                        pltpu.VMEM((n_chunks_max, Q_TILE, K_CHUNK), jnp.int32),
                        pltpu.VMEM((n_chunks_max, K_CHUNK, Q_TILE), jnp.int32),
                        pltpu.VMEM((n_chunks_max, Q_TILE, K_CHUNK), F32),
                        pltpu.VMEM((LANES, Q_TILE), F32),
                        pltpu.VMEM((rows_h, KV_RANK), BF16),
                        pltpu.VMEM((n_chunks_max, rows_h, K_CHUNK), F32),
                        pltpu.VMEM((rows_h, LANES), F32),
                        pltpu.VMEM((rows_h, LANES), F32),
                        pltpu.VMEM((rows_h, KV_RANK), F32)],
        compiler_params=_compiler_params(("arbitrary", "arbitrary")),
        name="dsa_attention",
    )(qq, wi, kk, c_kv, wuk, bias_diag)


def _mix_window_start(r, seq):
    return min(max(r * MIX_ROWS - N_META, 0), seq - MIX_ROWS)


def _residual_rows(x_ref, meta_ref, h_scr, seq):
    r = pl.program_id(1)
    n_real = -(-(seq + N_META) // MIX_ROWS)
    for rv in range(n_real):

        @pl.when(r == rv)
        def _(rv=rv):
            skip = rv * MIX_ROWS - N_META - _mix_window_start(rv, seq)
            if rv == 0:
                h_scr[:N_META, :] = meta_ref[...]
                h_scr[N_META:, :] = x_ref[:MIX_ROWS - N_META, :]
            elif skip == 0:
                h_scr[...] = x_ref[...]
            else:
                h_scr[:MIX_ROWS - skip, :] = x_ref[skip:, :]
                h_scr[MIX_ROWS - skip:, :] = jnp.zeros((skip, h_scr.shape[1]), F32)

    @pl.when(r >= n_real)
    def _():
        h_scr[...] = jnp.zeros(h_scr.shape, F32)


def _mix_kernel(ol_ref, pd_ref, x_ref, meta_ref, wuv_ref, wp_ref, ps_ref, wo_ref, g_ref, b_ref, h1_ref, h1b_ref,
                h_scr, *, seq):
    _residual_rows(x_ref, meta_ref, h_scr, seq)
    attn = [_dot(ol_ref[:, h * KV_RANK:(h + 1) * KV_RANK], wuv_ref[h]) for h in range(N_HEADS)]
    pool = [_dot(pd_ref[:, g * POOL_GROUP:(g + 1) * POOL_GROUP], wp_ref[g]) for g in range(len(POOL_WINDOWS))]
    pool = jnp.concatenate(pool, axis=-1) * ps_ref[...]
    cat = jnp.concatenate(attn + [pool], axis=-1).astype(BF16)
    y = ALPHA * h_scr[...] + _dot(cat, wo_ref[...])
    h1 = _layer_norm(y, g_ref[...], b_ref[...])
    h1_ref[...] = h1
    h1b_ref[...] = h1.astype(h1b_ref.dtype)


def _mix_ln1(o_lat, pool_diff, x, meta, wuv, wpool, pool_scale, wo, g, b):
    B, tp, _ = o_lat.shape
    seq, d = x.shape[1:]
    row = lambda n: pl.BlockSpec((None, MIX_ROWS, n), lambda bi, r: (bi, r, 0))
    full = lambda a: pl.BlockSpec(a.shape, lambda bi, r: (0,) * a.ndim)
    window = pl.BlockSpec(
        (None, pl.Element(MIX_ROWS), pl.Element(d)),
        lambda bi, r: (bi, pl.multiple_of(jnp.clip(r * MIX_ROWS - N_META, 0, seq - MIX_ROWS), N_META), 0))
    return pl.pallas_call(
        functools.partial(_mix_kernel, seq=seq),
        grid=(B, tp // MIX_ROWS),
        in_specs=[row(o_lat.shape[2]), row(pool_diff.shape[2]), window, full(meta),
                  full(wuv), full(wpool), full(pool_scale), full(wo), full(g), full(b)],
        out_specs=[row(d), row(d)],
        out_shape=[jax.ShapeDtypeStruct((B, tp, d), F32), jax.ShapeDtypeStruct((B, tp, d), BF16)],
        scratch_shapes=[pltpu.VMEM((MIX_ROWS, d), F32)],
        compiler_params=_compiler_params(("parallel", "arbitrary")),
        name="mix_ln1",
    )(o_lat, pool_diff, x, meta, wuv, wpool, pool_scale, wo, g, b)


def _gelu_tanh(x):
    return 0.5 * x * (1.0 + jnp.tanh(math.sqrt(2.0 / math.pi) * (x + 0.044715 * (x * x * x))))


def _ffn_kernel(hw_ref, hres_ref, wa_ref, wg_ref, cwa_ref, cwg_ref, cba_ref, cbg_ref, wd_ref, g_ref, b_ref, o_ref):
    c = pl.program_id(2)
    x = hw_ref[...]

    def conv(z, cw_ref, cb_ref):
        cw = cw_ref[...]
        n = z.shape[0]
        out = z[HALO - 2:n - 2] * cw[0:1] + z[HALO - 1:n - 1] * cw[1:2] + z[HALO:] * cw[2:3]
        return out + cb_ref[...]

    @pl.when(c == 0)
    def _():
        o_ref[...] = jnp.zeros(o_ref.shape, o_ref.dtype)

    a = conv(_dot(x, wa_ref[...]), cwa_ref, cba_ref)
    gate = conv(_dot(x, wg_ref[...]), cwg_ref, cbg_ref)
    act = (_gelu_tanh(a) * gate).astype(BF16)
    o_ref[...] += _dot(act, wd_ref[...])

    @pl.when(c == pl.num_programs(2) - 1)
    def _():
        o_ref[...] = _layer_norm(ALPHA * hres_ref[...] + o_ref[...], g_ref[...], b_ref[...])


def _ffn_ln2(h1, h1b, w_up, conv_w, conv_b, w_down, g, b, seq):
    B, tp, d = h1.shape
    d_ff = w_down.shape[0]
    n_c = d_ff // FFN_COLS
    vec = lambda off: pl.BlockSpec((1, FFN_COLS), lambda bi, r, c: (0, c + off))
    return pl.pallas_call(
        _ffn_kernel,
        grid=(B, seq // FFN_ROWS, n_c),
        in_specs=[
            pl.BlockSpec((None, pl.Element(FFN_ROWS + HALO), pl.Element(d)),
                         lambda bi, r, c: (bi, r * FFN_ROWS + N_META - HALO, 0)),
            pl.BlockSpec((None, pl.Element(FFN_ROWS), pl.Element(d)),
                         lambda bi, r, c: (bi, pl.multiple_of(r * FFN_ROWS + N_META, N_META), 0),
                         pipeline_mode=pl.Buffered(1)),
            pl.BlockSpec((d, FFN_COLS), lambda bi, r, c: (0, c)),
            pl.BlockSpec((d, FFN_COLS), lambda bi, r, c: (0, c + n_c)),
            pl.BlockSpec((CONV_WIDTH, FFN_COLS), lambda bi, r, c: (0, c)),
            pl.BlockSpec((CONV_WIDTH, FFN_COLS), lambda bi, r, c: (0, c + n_c)),
            vec(0), vec(n_c),
            pl.BlockSpec((FFN_COLS, d), lambda bi, r, c: (c, 0)),
            pl.BlockSpec((1, d), lambda bi, r, c: (0, 0)),
            pl.BlockSpec((1, d), lambda bi, r, c: (0, 0)),
        ],
        out_specs=pl.BlockSpec((None, FFN_ROWS, d), lambda bi, r, c: (bi, r, 0), pipeline_mode=pl.Buffered(1)),
        out_shape=jax.ShapeDtypeStruct((B, seq, d), F32),
        compiler_params=_compiler_params(("parallel", "parallel", "arbitrary")),
        name="ffn_ln2",
    )(h1b, h1, w_up, w_up, conv_w, conv_w, conv_b, conv_b, w_down, g, b)


def _t5_bucket_table(n):
    dist = np.arange(n, dtype=np.int32)
    max_exact = REL_BUCKETS // 2
    d_f = np.maximum(dist, 1).astype(np.float32)
    large = max_exact + (np.log(d_f / np.float32(max_exact)) / np.float32(math.log(REL_MAX_DIST / max_exact))
                         * np.float32(REL_BUCKETS - max_exact)).astype(np.int32)
    return np.where(dist < max_exact, dist, np.minimum(large, REL_BUCKETS - 1))


def _near_bias_diagonals(rel_bias):
    probe = _t5_bucket_table(4 * REL_MAX_DIST)
    first_far = int(np.argmax(probe == REL_BUCKETS - 1))
    assert np.all(probe[first_far:] == REL_BUCKETS - 1)
    n_real = -(-(first_far + K_CHUNK - 1) // Q_TILE)
    buckets = _t5_bucket_table(n_real * Q_TILE + K_CHUNK)
    period = 2 * K_CHUNK
    u = np.arange(period)
    k = np.arange(n_real)[:, None]
    dist = np.where(u < K_CHUNK, k * Q_TILE - u, k * Q_TILE + period - u)
    idx = buckets[np.clip(dist, 0, len(buckets) - 1)]
    rel = rel_bias.astype(F32) - rel_bias[REL_BUCKETS - 1:].astype(F32)
    return jnp.transpose(rel[idx], (0, 2, 1))


def kernel(x, meta, rel_bias, w_in, kv_norm_g, w_uk, w_uv, w_pool, pool_scale, w_o, ln1_g, ln1_b, w_up, conv_w,
           conv_b, w_down, ln2_g, ln2_b):
    B, S, D = x.shape
    assert w_in.shape[0] == DEPTH and S % FFN_ROWS == 0
    T = S + N_META
    tp = -(-T // ROW_ALIGN) * ROW_ALIGN
    assert tp - T >= max(POOL_WINDOWS)
    top_k = min(TOPK_MAX, S // 4)
    assert top_k <= K_CHUNK

    h = jnp.concatenate([jnp.broadcast_to(meta[None].astype(x.dtype), (B, N_META, D)), x], axis=1)
    hb = jnp.pad(h.astype(BF16), ((0, 0), (0, tp - T), (0, 0)))

    attn_w = N_HEADS * HEAD_DIM
    idx_w = IDX_HEADS * IDX_DIM
    o_c, o_qi = attn_w, attn_w + KV_RANK
    o_ki = o_qi + idx_w
    o_wi = o_ki + IDX_DIM
    o_u = o_wi + IDX_HEADS
    w = w_in[0]
    w_qq = jnp.concatenate([w[:, :o_c], w[:, o_qi:o_ki] * (IDX_DIM ** -0.5)], axis=1).astype(BF16)
    w_u = w[:, o_u:].astype(BF16)
    z64 = jnp.zeros((D, IDX_DIM), w.dtype)
    w_small = jnp.concatenate([w[:, o_c:o_qi], w[:, o_ki:o_wi], z64, z64, w[:, o_ki:o_wi],
                               w[:, o_wi:o_u] * (IDX_HEADS ** -0.5), jnp.zeros((D, 128 - IDX_HEADS), w.dtype)],
                              axis=1).astype(BF16)

    qq = _proj_qq(hb, w_qq)
    pool_diff = _proj_pool(hb, w_u)
    c_kv, kk, wi = _proj_small(hb, w_small, kv_norm_g[0].reshape(1, KV_RANK))

    wuk = jnp.transpose(w_uk[0], (1, 2, 0)).astype(BF16)
    o_lat = _dsa_attention(qq, wi, kk, c_kv, wuk, _near_bias_diagonals(rel_bias), top_k)

    wuv = jnp.transpose(w_uv[0], (1, 0, 2)).astype(BF16)
    h1, h1b = _mix_ln1(o_lat, pool_diff, x, meta, wuv, w_pool[0].astype(BF16), pool_scale[0].reshape(1, -1),
                       w_o[0].astype(BF16), ln1_g[0].reshape(1, D), ln1_b[0].reshape(1, D))

    return _ffn_ln2(h1, h1b, w_up[0].astype(BF16), conv_w[0], conv_b[0].reshape(1, -1), w_down[0].astype(BF16),
                    ln2_g[0].reshape(1, D), ln2_b[0].reshape(1, D), S)
```

```python
import functools
import math

import numpy as np
import jax
import jax.numpy as jnp
from jax import lax
from jax.experimental import pallas as pl
from jax.experimental.pallas import tpu as pltpu

F32 = jnp.float32
BF16 = jnp.bfloat16

N_META = 16
N_HEADS = 8
HEAD_DIM = 128
KV_RANK = 256
IDX_HEADS = 16
IDX_DIM = 64
TOPK_MAX = 256
POOL_WINDOWS = (2, 4, 8, 16)
POOL_GROUP = 256
CONV_WIDTH = 3
REL_BUCKETS = 32
REL_MAX_DIST = 128
DEPTH = 1
ALPHA = (2.0 * DEPTH) ** 0.25
LN_EPS = 1e-5
NEG_INF = -1e30

VMEM_LIMIT_BYTES = 56 * 1024 * 1024
SUBLANES = 8
LANES = 128
ROW_ALIGN = 768
PROJ_ROWS = 1024
Q_TILE = 256
K_CHUNK = 256
MIX_ROWS = 384
FFN_ROWS = 1024
FFN_COLS = 512
HALO = 16
INT_MIN = -(2 ** 31)


def _dot(a, b):
    return jnp.dot(a, b, preferred_element_type=F32)


def _dot_nt(a, b):
    return lax.dot_general(a, b, (((1,), (1,)), ((), ())), preferred_element_type=F32)


def _layer_norm(y, g, b):
    mu = jnp.mean(y, axis=-1, keepdims=True)
    yc = y - mu
    var = jnp.mean(yc * yc, axis=-1, keepdims=True)
    return yc * lax.rsqrt(var + LN_EPS) * g + b


def _proj_rows(x_ref, meta_ref, w_ref, tp, store):
    seq = x_ref.shape[0]
    store(pl.ds(0, N_META), _dot(meta_ref[...].astype(BF16), w_ref[...]))
    for r in range(seq // PROJ_ROWS):
        acc = _dot(x_ref[r * PROJ_ROWS:(r + 1) * PROJ_ROWS, :].astype(BF16), w_ref[...])
        store(pl.ds(N_META + r * PROJ_ROWS, PROJ_ROWS), acc)
    n_pad = tp - seq - N_META
    store(pl.ds(seq + N_META, n_pad), jnp.zeros((n_pad, w_ref.shape[1]), F32))


def _proj_cast_kernel(x_ref, meta_ref, w_ref, o_ref):
    def store(rows, acc):
        o_ref[rows, :] = acc.astype(o_ref.dtype)

    _proj_rows(x_ref, meta_ref, w_ref, o_ref.shape[0], store)


def _proj_pool_kernel(x_ref, meta_ref, w_ref, o_ref, u_scr):
    def store(rows, acc):
        u_scr[rows, :] = acc

    _proj_rows(x_ref, meta_ref, w_ref, u_scr.shape[0], store)
    group = pl.program_id(1)
    tp = u_scr.shape[0]
    pos = lax.broadcasted_iota(jnp.int32, (tp, 1), 0)
    for g, window in enumerate(POOL_WINDOWS):

        @pl.when(group == g)
        def _():
            u = u_scr[...]
            s = u
            shift = 1
            while shift < window:
                s = s + pltpu.roll(s, shift, axis=0)
                shift *= 2
            count = jnp.minimum(pos + 1, window).astype(F32)
            o_ref[...] = (s / count - u).astype(o_ref.dtype)


def _proj_small_kernel(x_ref, meta_ref, w_ref, g_ref, c_ref, kk_ref, wi_ref):
    def store(rows, acc):
        c = acc[:, :KV_RANK]
        ms = jnp.mean(c * c, axis=-1, keepdims=True)
        c_ref[rows, :] = (c * lax.rsqrt(ms + LN_EPS) * g_ref[...]).astype(c_ref.dtype)
        kk_ref[rows, :] = acc[:, KV_RANK:KV_RANK + 256].astype(kk_ref.dtype)
        wi_ref[rows, :] = acc[:, KV_RANK + 256:]

    _proj_rows(x_ref, meta_ref, w_ref, c_ref.shape[0], store)


def _batch_block(tp, n):
    return pl.BlockSpec((None, tp, n), lambda b, j: (b, 0, 0))


def _compiler_params(semantics):
    return pltpu.CompilerParams(dimension_semantics=semantics, vmem_limit_bytes=VMEM_LIMIT_BYTES)


def _proj_call(body, x, meta, w, tn, extra_in, extra_specs, out_blocks, out_shapes, scratch, name):
    B, seq, d = x.shape
    return pl.pallas_call(
        body,
        grid=(B, w.shape[1] // tn),
        in_specs=[_batch_block(seq, d), pl.BlockSpec(meta.shape, lambda b, j: (0, 0)),
                  pl.BlockSpec((d, tn), lambda b, j: (0, j))] + extra_specs,
        out_specs=out_blocks,
        out_shape=out_shapes,
        scratch_shapes=scratch,
        compiler_params=_compiler_params(("parallel", "arbitrary")),
        name=name,
    )(x, meta, w, *extra_in)


def _proj_qq(x, meta, w, tp):
    B, n, tn = x.shape[0], w.shape[1], 512
    return _proj_call(_proj_cast_kernel, x, meta, w, tn, [], [],
                      pl.BlockSpec((None, tp, tn), lambda b, j: (b, 0, j)),
                      jax.ShapeDtypeStruct((B, tp, n), BF16), [], "proj_qq")


def _proj_pool(x, meta, w, tp):
    B, n = x.shape[0], w.shape[1]
    return _proj_call(_proj_pool_kernel, x, meta, w, POOL_GROUP, [], [],
                      pl.BlockSpec((None, tp, POOL_GROUP), lambda b, j: (b, 0, j)),
                      jax.ShapeDtypeStruct((B, tp, n), BF16), [pltpu.VMEM((tp, POOL_GROUP), F32)], "proj_pool")


def _proj_small(x, meta, w, kv_g, tp):
    B = x.shape[0]
    return _proj_call(_proj_small_kernel, x, meta, w, w.shape[1], [kv_g],
                      [pl.BlockSpec((1, KV_RANK), lambda b, j: (0, 0))],
                      [_batch_block(tp, KV_RANK), _batch_block(tp, 256), _batch_block(tp, 128)],
                      [jax.ShapeDtypeStruct((B, tp, KV_RANK), BF16), jax.ShapeDtypeStruct((B, tp, 256), BF16),
                       jax.ShapeDtypeStruct((B, tp, 128), F32)], [], "proj_small")


def _attn_kernel(qq_ref, wi_ref, kk_ref, c_ref, wuk_ref, diag_ref, o_ref,
                 nb_scr, key_scr, keyt_scr, mb_scr, wt_scr, qa_scr, s_scr, m_scr, l_scr, acc_scr, *, top_k):
    i = pl.program_id(1)
    n_chunks = i // (K_CHUNK // Q_TILE) + 1
    attn_w = N_HEADS * HEAD_DIM
    scale = HEAD_DIM ** -0.5
    n_pairs = IDX_HEADS // 2

    t_col = i * Q_TILE + lax.broadcasted_iota(jnp.int32, (Q_TILE, 1), 0)
    t_row = i * Q_TILE + lax.broadcasted_iota(jnp.int32, (1, Q_TILE), 1)
    s_row = lax.broadcasted_iota(jnp.int32, (1, K_CHUNK), 1)
    s_col = lax.broadcasted_iota(jnp.int32, (K_CHUNK, 1), 0)

    n_near = nb_scr.shape[0]

    @pl.when((pl.program_id(0) == 0) & (i == 0))
    def _():
        for k in range(n_near - 1):
            for h in range(N_HEADS):
                v = jnp.broadcast_to(diag_ref[k, h:h + 1, :], (Q_TILE, 2 * K_CHUNK))
                t = pltpu.roll(v, 0, 1, stride=1, stride_axis=0)
                nb_scr[k, h * Q_TILE:(h + 1) * Q_TILE, :] = t[:, :K_CHUNK]
        nb_scr[n_near - 1] = jnp.zeros(nb_scr.shape[1:], F32)

    for h in range(N_HEADS):
        qa_scr[h * Q_TILE:(h + 1) * Q_TILE, :] = _dot(
            qq_ref[:, h * HEAD_DIM:(h + 1) * HEAD_DIM], wuk_ref[h]).astype(BF16)
    wt_scr[...] = wi_ref[...].T

    def idx_chunk(j, carry):
        ks = kk_ref[pl.ds(pl.multiple_of(j * K_CHUNK, K_CHUNK), K_CHUNK), :]
        k_even, k_odd = ks[:, :LANES], ks[:, LANES:]
        score = jnp.zeros((K_CHUNK, Q_TILE), F32)
        for p in range(n_pairs):
            q_pair = qq_ref[:, attn_w + p * LANES:attn_w + (p + 1) * LANES]
            score = score + jnp.maximum(_dot_nt(k_even, q_pair), 0.0) * wt_scr[2 * p:2 * p + 1, :]
            score = score + jnp.maximum(_dot_nt(k_odd, q_pair), 0.0) * wt_scr[2 * p + 1:2 * p + 2, :]
        causal = (j * K_CHUNK + s_col) <= t_row
        score = jnp.where(causal, score, NEG_INF)
        bits = lax.bitcast_convert_type(score, jnp.int32)
        key = jnp.where(bits < 0, bits ^ jnp.int32(0x7FFFFFFF), bits)
        keyt_scr[j] = key
        key_scr[j] = key.T
        return carry

    lax.fori_loop(0, n_chunks, idx_chunk, 0)

    def count_where(pred):
        acc_rows = 4 * SUBLANES

        def body(j, acc):
            hit = jnp.where(pred(j, keyt_scr[j]), 1.0, 0.0)
            return acc + jnp.sum(hit.reshape(K_CHUNK // acc_rows, acc_rows, Q_TILE), axis=0)

        acc = lax.fori_loop(0, n_chunks, body, jnp.zeros((acc_rows, Q_TILE), F32))
        return jnp.sum(acc, axis=0, keepdims=True)

    k_f = float(top_k)

    def bit_step(_, carry):
        thr, bit = carry
        cand = thr + bit
        cnt = count_where(lambda j, key: key >= cand)
        return jnp.where(cnt >= k_f, cand, thr), lax.shift_right_logical(bit, jnp.int32(1))

    thr, _ = lax.fori_loop(0, 32, bit_step,
                           (jnp.full((1, Q_TILE), INT_MIN, jnp.int32), jnp.int32(INT_MIN)))

    n_gt = count_where(lambda j, key: key > thr)
    n_ge = count_where(lambda j, key: key >= thr)
    need = k_f - n_gt
    neg_key = jnp.int32(np.array(NEG_INF, np.float32).view(np.int32) ^ 0x7FFFFFFF)
    has_tie = jnp.max(jnp.where((n_ge > k_f) & (thr > neg_key), 1.0, 0.0)) > 0.0
    idx_bits = int(math.ceil(math.log2(key_scr.shape[0] * K_CHUNK)))

    def tie_cut():
        def step(_, carry):
            cut, bit = carry
            cand = cut + bit
            cnt = count_where(lambda j, key: (key == thr) & ((j * K_CHUNK + s_col) < cand))
            return jnp.where(cnt < need, cand, cut), lax.shift_right_logical(bit, jnp.int32(1))

        cut, _ = lax.fori_loop(0, idx_bits, step,
                               (jnp.zeros((1, Q_TILE), jnp.int32), jnp.int32(2 ** (idx_bits - 1))))
        return cut

    cut = lax.cond(has_tie, tie_cut, lambda: jnp.full((1, Q_TILE), 2 ** 30, jnp.int32))

    def to_rows(v):
        t = jnp.broadcast_to(v, (Q_TILE, Q_TILE)).T
        return jnp.concatenate([t] * (K_CHUNK // Q_TILE), axis=1)

    thr_b, cut_b = to_rows(thr), to_rows(cut)

    def mask_chunk(j, carry):
        key = key_scr[j]
        s_pos = j * K_CHUNK + s_row
        keep = ((key > thr_b) | ((key == thr_b) & (s_pos <= cut_b))) & (s_pos <= t_col)
        mb_scr[j] = jnp.where(keep, 0.0, NEG_INF)
        return carry

    lax.fori_loop(0, n_chunks, mask_chunk, 0)

    rows_h = N_HEADS * Q_TILE
    lane_fold = lambda v, op: functools.reduce(op, [v[:, k * LANES:(k + 1) * LANES] for k in range(K_CHUNK // LANES)])

    def key_rows(j):
        return pl.ds(pl.multiple_of(j * K_CHUNK, K_CHUNK), K_CHUNK)

    def logit_chunk(j, carry):
        near = jnp.minimum(i - (K_CHUNK // Q_TILE) * j, n_near - 1)
        s = _dot_nt(qa_scr[...], c_ref[key_rows(j), :]) * scale + nb_scr[near]
        s = s + jnp.concatenate([mb_scr[j]] * N_HEADS, axis=0)
        s_scr[j] = s
        m_scr[...] = jnp.maximum(m_scr[...], lane_fold(s, jnp.maximum))
        return carry

    m_scr[...] = jnp.full(m_scr.shape, NEG_INF, F32)
    lax.fori_loop(0, n_chunks, logit_chunk, 0)
    m_b = jnp.broadcast_to(jnp.max(m_scr[...], axis=-1, keepdims=True), (rows_h, LANES))
    m_scr[...] = m_b
    l_scr[...] = jnp.zeros(l_scr.shape, F32)
    acc_scr[...] = jnp.zeros(acc_scr.shape, F32)

    def value_chunk(j, carry):
        p = jnp.exp(s_scr[j] - jnp.concatenate([m_scr[...]] * (K_CHUNK // LANES), axis=1))
        l_scr[...] += lane_fold(p, jnp.add)
        acc_scr[...] += _dot(p.astype(BF16), c_ref[key_rows(j), :])
        return carry

    lax.fori_loop(0, n_chunks, value_chunk, 0)
    out = acc_scr[...] * (1.0 / jnp.sum(l_scr[...], axis=-1, keepdims=True))
    for h in range(N_HEADS):
        o_ref[:, h * KV_RANK:(h + 1) * KV_RANK] = out[h * Q_TILE:(h + 1) * Q_TILE].astype(o_ref.dtype)


def _dsa_attention(qq, wi, kk, c_kv, wuk, bias_diag, top_k):
    B, tp, _ = qq.shape
    n_q = tp // Q_TILE
    n_chunks_max = tp // K_CHUNK
    rows_h = N_HEADS * Q_TILE
    return pl.pallas_call(
        functools.partial(_attn_kernel, top_k=top_k),
        grid=(B, n_q),
        in_specs=[
            pl.BlockSpec((None, Q_TILE, qq.shape[2]), lambda b, i: (b, i, 0)),
            pl.BlockSpec((None, Q_TILE, 128), lambda b, i: (b, i, 0)),
            pl.BlockSpec((None, tp, 256), lambda b, i: (b, 0, 0)),
            pl.BlockSpec((None, tp, KV_RANK), lambda b, i: (b, 0, 0)),
            pl.BlockSpec(wuk.shape, lambda b, i: (0, 0, 0)),
            pl.BlockSpec(bias_diag.shape, lambda b, i: (0, 0, 0)),
        ],
        out_specs=pl.BlockSpec((None, Q_TILE, N_HEADS * KV_RANK), lambda b, i: (b, i, 0)),
        out_shape=jax.ShapeDtypeStruct((B, tp, N_HEADS * KV_RANK), BF16),
        scratch_shapes=[pltpu.VMEM((bias_diag.shape[0] + 1, rows_h, K_CHUNK), F32),
                        pltpu.VMEM((n_chunks_max, Q_TILE, K_CHUNK), jnp.int32),
                        pltpu.VMEM((n_chunks_max, K_CHUNK, Q_TILE), jnp.int32),
                        pltpu.VMEM((n_chunks_max, Q_TILE, K_CHUNK), F32),
                        pltpu.VMEM((LANES, Q_TILE), F32),
                        pltpu.VMEM((rows_h, KV_RANK), BF16),
                        pltpu.VMEM((n_chunks_max, rows_h, K_CHUNK), F32),
                        pltpu.VMEM((rows_h, LANES), F32),
                        pltpu.VMEM((rows_h, LANES), F32),
                        pltpu.VMEM((rows_h, KV_RANK), F32)],
        compiler_params=_compiler_params(("arbitrary", "arbitrary")),
        name="dsa_attention",
    )(qq, wi, kk, c_kv, wuk, bias_diag)


def _mix_window_start(r, seq):
    return min(max(r * MIX_ROWS - N_META, 0), seq - MIX_ROWS)


def _residual_rows(x_ref, meta_ref, h_scr, seq):
    r = pl.program_id(1)
    n_real = -(-(seq + N_META) // MIX_ROWS)
    for rv in range(n_real):

        @pl.when(r == rv)
        def _(rv=rv):
            skip = rv * MIX_ROWS - N_META - _mix_window_start(rv, seq)
            if rv == 0:
                h_scr[:N_META, :] = meta_ref[...]
                h_scr[N_META:, :] = x_ref[:MIX_ROWS - N_META, :]
            elif skip == 0:
                h_scr[...] = x_ref[...]
            else:
                h_scr[:MIX_ROWS - skip, :] = x_ref[skip:, :]
                h_scr[MIX_ROWS - skip:, :] = jnp.zeros((skip, h_scr.shape[1]), F32)

    @pl.when(r >= n_real)
    def _():
        h_scr[...] = jnp.zeros(h_scr.shape, F32)


def _mix_kernel(ol_ref, pd_ref, x_ref, meta_ref, wuv_ref, wp_ref, ps_ref, wo_ref, g_ref, b_ref, h1_ref, h1b_ref,
                h_scr, *, seq):
    _residual_rows(x_ref, meta_ref, h_scr, seq)
    attn = [_dot(ol_ref[:, h * KV_RANK:(h + 1) * KV_RANK], wuv_ref[h]) for h in range(N_HEADS)]
    pool = [_dot(pd_ref[:, g * POOL_GROUP:(g + 1) * POOL_GROUP], wp_ref[g]) for g in range(len(POOL_WINDOWS))]
    pool = jnp.concatenate(pool, axis=-1) * ps_ref[...]
    cat = jnp.concatenate(attn + [pool], axis=-1).astype(BF16)
    y = ALPHA * h_scr[...] + _dot(cat, wo_ref[...])
    h1 = _layer_norm(y, g_ref[...], b_ref[...])
    h1_ref[...] = h1
    h1b_ref[...] = h1.astype(h1b_ref.dtype)


def _mix_ln1(o_lat, pool_diff, x, meta, wuv, wpool, pool_scale, wo, g, b):
    B, tp, _ = o_lat.shape
    seq, d = x.shape[1:]
    row = lambda n: pl.BlockSpec((None, MIX_ROWS, n), lambda bi, r: (bi, r, 0))
    full = lambda a: pl.BlockSpec(a.shape, lambda bi, r: (0,) * a.ndim)
    window = pl.BlockSpec(
        (None, pl.Element(MIX_ROWS), pl.Element(d)),
        lambda bi, r: (bi, pl.multiple_of(jnp.clip(r * MIX_ROWS - N_META, 0, seq - MIX_ROWS), N_META), 0))
    return pl.pallas_call(
        functools.partial(_mix_kernel, seq=seq),
        grid=(B, tp // MIX_ROWS),
        in_specs=[row(o_lat.shape[2]), row(pool_diff.shape[2]), window, full(meta),
                  full(wuv), full(wpool), full(pool_scale), full(wo), full(g), full(b)],
        out_specs=[row(d), row(d)],
        out_shape=[jax.ShapeDtypeStruct((B, tp, d), F32), jax.ShapeDtypeStruct((B, tp, d), BF16)],
        scratch_shapes=[pltpu.VMEM((MIX_ROWS, d), F32)],
        compiler_params=_compiler_params(("parallel", "arbitrary")),
        name="mix_ln1",
    )(o_lat, pool_diff, x, meta, wuv, wpool, pool_scale, wo, g, b)


def _gelu_tanh(x):
    return 0.5 * x * (1.0 + jnp.tanh(math.sqrt(2.0 / math.pi) * (x + 0.044715 * (x * x * x))))


def _ffn_kernel(hw_ref, hres_ref, wa_ref, wg_ref, cwa_ref, cwg_ref, cba_ref, cbg_ref, wd_ref, g_ref, b_ref, o_ref,
                za_scr, zg_scr):
    c = pl.program_id(2)
    n_c = pl.num_programs(2) - 1

    def up(slot):
        x = hw_ref[...]
        za_scr[slot] = _dot(x, wa_ref[...])
        zg_scr[slot] = _dot(x, wg_ref[...])

    def conv(z, cw_ref, cb_ref):
        cw = cw_ref[...]
        n = z.shape[0]
        out = z[HALO - 2:n - 2] * cw[0:1] + z[HALO - 1:n - 1] * cw[1:2] + z[HALO:] * cw[2:3]
        return out + cb_ref[...]

    def down(slot):
        a = conv(za_scr[slot], cwa_ref, cba_ref)
        gate = conv(zg_scr[slot], cwg_ref, cbg_ref)
        act = (_gelu_tanh(a) * gate).astype(BF16)
        o_ref[...] += _dot(act, wd_ref[...])

    @pl.when(c == 0)
    def _():
        o_ref[...] = jnp.zeros(o_ref.shape, o_ref.dtype)
        up(0)

    @pl.when((c > 0) & (c < n_c))
    def _():
        up(c % 2)
        down((c - 1) % 2)

    @pl.when(c == n_c)
    def _():
        down((c - 1) % 2)
        o_ref[...] = _layer_norm(ALPHA * hres_ref[...] + o_ref[...], g_ref[...], b_ref[...])


def _ffn_ln2(h1, h1b, w_up, conv_w, conv_b, w_down, g, b, seq):
    B, tp, d = h1.shape
    d_ff = w_down.shape[0]
    n_c = d_ff // FFN_COLS
    up_c = lambda c: jnp.minimum(c, n_c - 1)
    dn_c = lambda c: jnp.maximum(c - 1, 0)
    vec = lambda off: pl.BlockSpec((1, FFN_COLS), lambda bi, r, c: (0, dn_c(c) + off))
    return pl.pallas_call(
        _ffn_kernel,
        grid=(B, seq // FFN_ROWS, n_c + 1),
        in_specs=[
            pl.BlockSpec((None, pl.Element(FFN_ROWS + HALO), pl.Element(d)),
                         lambda bi, r, c: (bi, r * FFN_ROWS + N_META - HALO, 0)),
            pl.BlockSpec((None, pl.Element(FFN_ROWS), pl.Element(d)),
                         lambda bi, r, c: (bi, pl.multiple_of(r * FFN_ROWS + N_META, N_META), 0),
                         pipeline_mode=pl.Buffered(1)),
            pl.BlockSpec((d, FFN_COLS), lambda bi, r, c: (0, up_c(c))),
            pl.BlockSpec((d, FFN_COLS), lambda bi, r, c: (0, up_c(c) + n_c)),
            pl.BlockSpec((CONV_WIDTH, FFN_COLS), lambda bi, r, c: (0, dn_c(c))),
            pl.BlockSpec((CONV_WIDTH, FFN_COLS), lambda bi, r, c: (0, dn_c(c) + n_c)),
            vec(0), vec(n_c),
            pl.BlockSpec((FFN_COLS, d), lambda bi, r, c: (dn_c(c), 0)),
            pl.BlockSpec((1, d), lambda bi, r, c: (0, 0)),
            pl.BlockSpec((1, d), lambda bi, r, c: (0, 0)),
        ],
        out_specs=pl.BlockSpec((None, FFN_ROWS, d), lambda bi, r, c: (bi, r, 0), pipeline_mode=pl.Buffered(1)),
        out_shape=jax.ShapeDtypeStruct((B, seq, d), F32),
        scratch_shapes=[pltpu.VMEM((2, FFN_ROWS + HALO, FFN_COLS), F32),
                        pltpu.VMEM((2, FFN_ROWS + HALO, FFN_COLS), F32)],
        compiler_params=_compiler_params(("parallel", "parallel", "arbitrary")),
        name="ffn_ln2",
    )(h1b, h1, w_up, w_up, conv_w, conv_w, conv_b, conv_b, w_down, g, b)


def _t5_bucket_table(n):
    dist = np.arange(n, dtype=np.int32)
    max_exact = REL_BUCKETS // 2
    d_f = np.maximum(dist, 1).astype(np.float32)
    large = max_exact + (np.log(d_f / np.float32(max_exact)) / np.float32(math.log(REL_MAX_DIST / max_exact))
                         * np.float32(REL_BUCKETS - max_exact)).astype(np.int32)
    return np.where(dist < max_exact, dist, np.minimum(large, REL_BUCKETS - 1))


def _near_bias_diagonals(rel_bias):
    probe = _t5_bucket_table(4 * REL_MAX_DIST)
    first_far = int(np.argmax(probe == REL_BUCKETS - 1))
    assert np.all(probe[first_far:] == REL_BUCKETS - 1)
    n_real = -(-(first_far + K_CHUNK - 1) // Q_TILE)
    buckets = _t5_bucket_table(n_real * Q_TILE + K_CHUNK)
    period = 2 * K_CHUNK
    u = np.arange(period)
    k = np.arange(n_real)[:, None]
    dist = np.where(u < K_CHUNK, k * Q_TILE - u, k * Q_TILE + period - u)
    idx = buckets[np.clip(dist, 0, len(buckets) - 1)]
    rel = rel_bias.astype(F32) - rel_bias[REL_BUCKETS - 1:].astype(F32)
    return jnp.transpose(rel[idx], (0, 2, 1))


def kernel(x, meta, rel_bias, w_in, kv_norm_g, w_uk, w_uv, w_pool, pool_scale, w_o, ln1_g, ln1_b, w_up, conv_w,
           conv_b, w_down, ln2_g, ln2_b):
    B, S, D = x.shape
    assert w_in.shape[0] == DEPTH and S % FFN_ROWS == 0
    T = S + N_META
    tp = -(-T // ROW_ALIGN) * ROW_ALIGN
    assert tp - T >= max(POOL_WINDOWS)
    top_k = min(TOPK_MAX, S // 4)
    assert top_k <= K_CHUNK

    attn_w = N_HEADS * HEAD_DIM
    idx_w = IDX_HEADS * IDX_DIM
    o_c, o_qi = attn_w, attn_w + KV_RANK
    o_ki = o_qi + idx_w
    o_wi = o_ki + IDX_DIM
    o_u = o_wi + IDX_HEADS
    w = w_in[0]
    w_qq = jnp.concatenate([w[:, :o_c], w[:, o_qi:o_ki] * (IDX_DIM ** -0.5)], axis=1).astype(BF16)
    w_u = w[:, o_u:].astype(BF16)
    z64 = jnp.zeros((D, IDX_DIM), w.dtype)
    w_small = jnp.concatenate([w[:, o_c:o_qi], w[:, o_ki:o_wi], z64, z64, w[:, o_ki:o_wi],
                               w[:, o_wi:o_u] * (IDX_HEADS ** -0.5), jnp.zeros((D, 128 - IDX_HEADS), w.dtype)],
                              axis=1).astype(BF16)

    qq = _proj_qq(x, meta, w_qq, tp)
    pool_diff = _proj_pool(x, meta, w_u, tp)
    c_kv, kk, wi = _proj_small(x, meta, w_small, kv_norm_g[0].reshape(1, KV_RANK), tp)

    wuk = jnp.transpose(w_uk[0], (1, 2, 0)).astype(BF16)
    o_lat = _dsa_attention(qq, wi, kk, c_kv, wuk, _near_bias_diagonals(rel_bias), top_k)

    wuv = jnp.transpose(w_uv[0], (1, 0, 2)).astype(BF16)
    h1, h1b = _mix_ln1(o_lat, pool_diff, x, meta, wuv, w_pool[0].astype(BF16), pool_scale[0].reshape(1, -1),
                       w_o[0].astype(BF16), ln1_g[0].reshape(1, D), ln1_b[0].reshape(1, D))

    return _ffn_ln2(h1, h1b, w_up[0].astype(BF16), conv_w[0], conv_b[0].reshape(1, -1), w_down[0].astype(BF16),
                    ln2_g[0].reshape(1, D), ln2_b[0].reshape(1, D), S)
```

```python
import functools
import math

import numpy as np
import jax
import jax.numpy as jnp
from jax import lax
from jax.experimental import pallas as pl
from jax.experimental.pallas import tpu as pltpu

F32 = jnp.float32
BF16 = jnp.bfloat16

N_META = 16
N_HEADS = 8
HEAD_DIM = 128
KV_RANK = 256
IDX_HEADS = 16
IDX_DIM = 64
TOPK_MAX = 256
POOL_WINDOWS = (2, 4, 8, 16)
POOL_GROUP = 256
CONV_WIDTH = 3
REL_BUCKETS = 32
REL_MAX_DIST = 128
DEPTH = 1
ALPHA = (2.0 * DEPTH) ** 0.25
LN_EPS = 1e-5
NEG_INF = -1e30

VMEM_LIMIT_BYTES = 56 * 1024 * 1024
SUBLANES = 8
LANES = 128
ROW_ALIGN = 768
PROJ_ROWS = 1024
Q_TILE = 256
TAIL_TILE = 16
K_CHUNK = 256
MIX_ROWS = 384
FFN_ROWS = 1024
FFN_COLS = 512
HALO = 16
INT_MIN = -(2 ** 31)


def _dot(a, b):
    return jnp.dot(a, b, preferred_element_type=F32)


def _dot_nt(a, b):
    return lax.dot_general(a, b, (((1,), (1,)), ((), ())), preferred_element_type=F32)


def _layer_norm(y, g, b):
    mu = jnp.mean(y, axis=-1, keepdims=True)
    yc = y - mu
    var = jnp.mean(yc * yc, axis=-1, keepdims=True)
    return yc * lax.rsqrt(var + LN_EPS) * g + b


def _proj_rows(x_ref, meta_ref, w_ref, tp, store):
    seq = x_ref.shape[0]
    store(pl.ds(0, N_META), _dot(meta_ref[...].astype(BF16), w_ref[...]))
    for r in range(seq // PROJ_ROWS):
        acc = _dot(x_ref[r * PROJ_ROWS:(r + 1) * PROJ_ROWS, :].astype(BF16), w_ref[...])
        store(pl.ds(N_META + r * PROJ_ROWS, PROJ_ROWS), acc)
    n_pad = tp - seq - N_META
    store(pl.ds(seq + N_META, n_pad), jnp.zeros((n_pad, w_ref.shape[1]), F32))


def _proj_cast_kernel(x_ref, meta_ref, w_ref, o_ref):
    def store(rows, acc):
        o_ref[rows, :] = acc.astype(o_ref.dtype)

    _proj_rows(x_ref, meta_ref, w_ref, o_ref.shape[0], store)


def _proj_pool_kernel(x_ref, meta_ref, w_ref, o_ref, u_scr):
    def store(rows, acc):
        u_scr[rows, :] = acc

    _proj_rows(x_ref, meta_ref, w_ref, u_scr.shape[0], store)
    group = pl.program_id(1)
    tp = u_scr.shape[0]
    pos = lax.broadcasted_iota(jnp.int32, (tp, 1), 0)
    for g, window in enumerate(POOL_WINDOWS):

        @pl.when(group == g)
        def _():
            u = u_scr[...]
            s = u
            shift = 1
            while shift < window:
                s = s + pltpu.roll(s, shift, axis=0)
                shift *= 2
            count = jnp.minimum(pos + 1, window).astype(F32)
            o_ref[...] = (s / count - u).astype(o_ref.dtype)


def _proj_small_kernel(x_ref, meta_ref, w_ref, g_ref, c_ref, kk_ref, wi_ref):
    def store(rows, acc):
        c = acc[:, :KV_RANK]
        ms = jnp.mean(c * c, axis=-1, keepdims=True)
        c_ref[rows, :] = (c * lax.rsqrt(ms + LN_EPS) * g_ref[...]).astype(c_ref.dtype)
        kk_ref[rows, :] = acc[:, KV_RANK:KV_RANK + 256].astype(kk_ref.dtype)
        wi_ref[rows, :] = acc[:, KV_RANK + 256:]

    _proj_rows(x_ref, meta_ref, w_ref, c_ref.shape[0], store)


def _batch_block(tp, n):
    return pl.BlockSpec((None, tp, n), lambda b, j: (b, 0, 0))


def _compiler_params(semantics):
    return pltpu.CompilerParams(dimension_semantics=semantics, vmem_limit_bytes=VMEM_LIMIT_BYTES)


def _proj_call(body, x, meta, w, tn, extra_in, extra_specs, out_blocks, out_shapes, scratch, name):
    B, seq, d = x.shape
    return pl.pallas_call(
        body,
        grid=(B, w.shape[1] // tn),
        in_specs=[_batch_block(seq, d), pl.BlockSpec(meta.shape, lambda b, j: (0, 0)),
                  pl.BlockSpec((d, tn), lambda b, j: (0, j))] + extra_specs,
        out_specs=out_blocks,
        out_shape=out_shapes,
        scratch_shapes=scratch,
        compiler_params=_compiler_params(("parallel", "arbitrary")),
        name=name,
    )(x, meta, w, *extra_in)


def _proj_qq(x, meta, w, tp):
    B, n, tn = x.shape[0], w.shape[1], 512
    return _proj_call(_proj_cast_kernel, x, meta, w, tn, [], [],
                      pl.BlockSpec((None, tp, tn), lambda b, j: (b, 0, j)),
                      jax.ShapeDtypeStruct((B, tp, n), BF16), [], "proj_qq")


def _proj_pool(x, meta, w, tp):
    B, n = x.shape[0], w.shape[1]
    return _proj_call(_proj_pool_kernel, x, meta, w, POOL_GROUP, [], [],
                      pl.BlockSpec((None, tp, POOL_GROUP), lambda b, j: (b, 0, j)),
                      jax.ShapeDtypeStruct((B, tp, n), BF16), [pltpu.VMEM((tp, POOL_GROUP), F32)], "proj_pool")


def _proj_small(x, meta, w, kv_g, tp):
    B = x.shape[0]
    return _proj_call(_proj_small_kernel, x, meta, w, w.shape[1], [kv_g],
                      [pl.BlockSpec((1, KV_RANK), lambda b, j: (0, 0))],
                      [_batch_block(tp, KV_RANK), _batch_block(tp, 256), _batch_block(tp, 128)],
                      [jax.ShapeDtypeStruct((B, tp, KV_RANK), BF16), jax.ShapeDtypeStruct((B, tp, 256), BF16),
                       jax.ShapeDtypeStruct((B, tp, 128), F32)], [], "proj_small")


def _attn_kernel(qq_ref, wi_ref, kk_ref, c_ref, wuk_ref, diag_ref, *rest, top_k, qt, first_tile, aliased):
    o_ref, nb_scr, key_scr, keyt_scr, mb_scr, wt_scr, qa_scr, s_scr, m_scr, l_scr, acc_scr = rest[int(aliased):]
    i = first_tile + pl.program_id(1)
    n_chunks = ((i + 1) * qt - 1) // K_CHUNK + 1
    attn_w = N_HEADS * HEAD_DIM
    scale = HEAD_DIM ** -0.5
    n_pairs = IDX_HEADS // 2
    lanes_are_queries = qt % LANES == 0

    t_col = i * qt + lax.broadcasted_iota(jnp.int32, (qt, 1), 0)
    t_row = i * qt + lax.broadcasted_iota(jnp.int32, (1, qt), 1)
    s_row = lax.broadcasted_iota(jnp.int32, (1, K_CHUNK), 1)
    s_col = lax.broadcasted_iota(jnp.int32, (K_CHUNK, 1), 0)

    n_near = nb_scr.shape[0]

    @pl.when((pl.program_id(0) == 0) & (pl.program_id(1) == 0))
    def _():
        for k in range(n_near - 1):
            for h in range(N_HEADS):
                v = jnp.broadcast_to(diag_ref[k, h:h + 1, :], (qt, 2 * K_CHUNK))
                t = pltpu.roll(v, 0, 1, stride=1, stride_axis=0)
                nb_scr[k, h * qt:(h + 1) * qt, :] = t[:, :K_CHUNK]
        nb_scr[n_near - 1] = jnp.zeros(nb_scr.shape[1:], F32)

    for h in range(N_HEADS):
        qa_scr[h * qt:(h + 1) * qt, :] = _dot(
            qq_ref[:, h * HEAD_DIM:(h + 1) * HEAD_DIM], wuk_ref[h]).astype(BF16)
    if lanes_are_queries:
        wt_scr[...] = wi_ref[...].T

    def idx_chunk(j, carry):
        ks = kk_ref[pl.ds(pl.multiple_of(j * K_CHUNK, K_CHUNK), K_CHUNK), :]
        k_even, k_odd = ks[:, :LANES], ks[:, LANES:]
        score = jnp.zeros((K_CHUNK, qt) if lanes_are_queries else (qt, K_CHUNK), F32)
        for p in range(n_pairs):
            q_pair = qq_ref[:, attn_w + p * LANES:attn_w + (p + 1) * LANES]
            for hh, k_half in ((2 * p, k_even), (2 * p + 1, k_odd)):
                if lanes_are_queries:
                    score = score + jnp.maximum(_dot_nt(k_half, q_pair), 0.0) * wt_scr[hh:hh + 1, :]
                else:
                    score = score + jnp.maximum(_dot_nt(q_pair, k_half), 0.0) * wi_ref[:, hh:hh + 1]
        s_pos = j * K_CHUNK + (s_col if lanes_are_queries else s_row)
        score = jnp.where(s_pos <= (t_row if lanes_are_queries else t_col), score, NEG_INF)
        bits = lax.bitcast_convert_type(score, jnp.int32)
        key = jnp.where(bits < 0, bits ^ jnp.int32(0x7FFFFFFF), bits)
        if lanes_are_queries:
            keyt_scr[j] = key
            key_scr[j] = key.T
        else:
            key_scr[j] = key
        return carry

    lax.fori_loop(0, n_chunks, idx_chunk, 0)

    per_query = (1, qt) if lanes_are_queries else (qt, 1)
    s_idx = s_col if lanes_are_queries else s_row

    def count_where(pred):
        acc_rows = 4 * SUBLANES

        def body(j, acc):
            if lanes_are_queries:
                hit = jnp.where(pred(j, keyt_scr[j]), 1.0, 0.0)
                return acc + jnp.sum(hit.reshape(K_CHUNK // acc_rows, acc_rows, qt), axis=0)
            return acc + jnp.where(pred(j, key_scr[j]), 1.0, 0.0)

        if lanes_are_queries:
            acc = lax.fori_loop(0, n_chunks, body, jnp.zeros((acc_rows, qt), F32))
            return jnp.sum(acc, axis=0, keepdims=True)
        acc = lax.fori_loop(0, n_chunks, body, jnp.zeros((qt, K_CHUNK), F32))
        return jnp.sum(acc, axis=1, keepdims=True)

    k_f = float(top_k)

    def bit_step(_, carry):
        thr, bit = carry
        cand = thr + bit
        cnt = count_where(lambda j, key: key >= cand)
        return jnp.where(cnt >= k_f, cand, thr), lax.shift_right_logical(bit, jnp.int32(1))

    thr, _ = lax.fori_loop(0, 32, bit_step,
                           (jnp.full(per_query, INT_MIN, jnp.int32), jnp.int32(INT_MIN)))

    n_gt = count_where(lambda j, key: key > thr)
    n_ge = count_where(lambda j, key: key >= thr)
    need = k_f - n_gt
    neg_key = jnp.int32(np.array(NEG_INF, np.float32).view(np.int32) ^ 0x7FFFFFFF)
    has_tie = jnp.max(jnp.where((n_ge > k_f) & (thr > neg_key), 1.0, 0.0)) > 0.0
    idx_bits = int(math.ceil(math.log2(key_scr.shape[0] * K_CHUNK)))

    def tie_cut():
        def step(_, carry):
            cut, bit = carry
            cand = cut + bit
            cnt = count_where(lambda j, key: (key == thr) & ((j * K_CHUNK + s_idx) < cand))
            return jnp.where(cnt < need, cand, cut), lax.shift_right_logical(bit, jnp.int32(1))

        cut, _ = lax.fori_loop(0, idx_bits, step,
                               (jnp.zeros(per_query, jnp.int32), jnp.int32(2 ** (idx_bits - 1))))
        return cut

    cut = lax.cond(has_tie, tie_cut, lambda: jnp.full(per_query, 2 ** 30, jnp.int32))

    def to_rows(v):
        if not lanes_are_queries:
            return jnp.broadcast_to(v, (qt, K_CHUNK))
        t = jnp.broadcast_to(v, (qt, qt)).T
        return jnp.concatenate([t] * (K_CHUNK // qt), axis=1)

    thr_b, cut_b = to_rows(thr), to_rows(cut)

    def mask_chunk(j, carry):
        key = key_scr[j]
        s_pos = j * K_CHUNK + s_row
        keep = ((key > thr_b) | ((key == thr_b) & (s_pos <= cut_b))) & (s_pos <= t_col)
        mb_scr[j] = jnp.where(keep, 0.0, NEG_INF)
        return carry

    lax.fori_loop(0, n_chunks, mask_chunk, 0)

    rows_h = N_HEADS * qt
    lane_fold = lambda v, op: functools.reduce(op, [v[:, k * LANES:(k + 1) * LANES] for k in range(K_CHUNK // LANES)])

    def key_rows(j):
        return pl.ds(pl.multiple_of(j * K_CHUNK, K_CHUNK), K_CHUNK)

    def logit_chunk(j, carry):
        near = jnp.minimum((i * qt - j * K_CHUNK) // qt, n_near - 1)
        s = _dot_nt(qa_scr[...], c_ref[key_rows(j), :]) * scale + nb_scr[near]
        s = s + jnp.concatenate([mb_scr[j]] * N_HEADS, axis=0)
        s_scr[j] = s
        m_scr[...] = jnp.maximum(m_scr[...], lane_fold(s, jnp.maximum))
        return carry

    m_scr[...] = jnp.full(m_scr.shape, NEG_INF, F32)
    lax.fori_loop(0, n_chunks, logit_chunk, 0)
    m_b = jnp.broadcast_to(jnp.max(m_scr[...], axis=-1, keepdims=True), (rows_h, LANES))
    m_scr[...] = m_b
    l_scr[...] = jnp.zeros(l_scr.shape, F32)
    acc_scr[...] = jnp.zeros(acc_scr.shape, F32)

    def value_chunk(j, carry):
        p = jnp.exp(s_scr[j] - jnp.concatenate([m_scr[...]] * (K_CHUNK // LANES), axis=1))
        l_scr[...] += lane_fold(p, jnp.add)
        acc_scr[...] += _dot(p.astype(BF16), c_ref[key_rows(j), :])
        return carry

    lax.fori_loop(0, n_chunks, value_chunk, 0)
    out = acc_scr[...] * (1.0 / jnp.sum(l_scr[...], axis=-1, keepdims=True))
    for h in range(N_HEADS):
        o_ref[:qt, h * KV_RANK:(h + 1) * KV_RANK] = out[h * qt:(h + 1) * qt].astype(o_ref.dtype)
    if o_ref.shape[0] > qt:
        o_ref[qt:, :] = jnp.zeros((o_ref.shape[0] - qt, o_ref.shape[1]), o_ref.dtype)


def _dsa_attention(qq, wi, kk, c_kv, wuk, rel_bias, top_k, seq):
    B, tp, _ = qq.shape
    n_main = (seq + N_META) // Q_TILE
    assert (seq + N_META) - n_main * Q_TILE <= TAIL_TILE and (n_main * Q_TILE) % TAIL_TILE == 0
    n_chunks_max = tp // K_CHUNK
    width = N_HEADS * KV_RANK

    def call(qt, first_tile, n_tiles, out_rows, prev):
        rows_h = N_HEADS * qt
        diag = _near_bias_diagonals(rel_bias, qt)
        out_tile0 = first_tile * qt // out_rows
        in_specs = [
            pl.BlockSpec((None, qt, qq.shape[2]), lambda b, i: (b, first_tile + i, 0)),
            pl.BlockSpec((None, qt, 128), lambda b, i: (b, first_tile + i, 0)),
            pl.BlockSpec((None, tp, 256), lambda b, i: (b, 0, 0)),
            pl.BlockSpec((None, tp, KV_RANK), lambda b, i: (b, 0, 0)),
            pl.BlockSpec(wuk.shape, lambda b, i: (0, 0, 0)),
            pl.BlockSpec(diag.shape, lambda b, i: (0, 0, 0)),
        ]
        args = [qq, wi, kk, c_kv, wuk, diag]
        if prev is not None:
            in_specs.append(pl.BlockSpec(memory_space=pl.ANY))
            args.append(prev)
        return pl.pallas_call(
            functools.partial(_attn_kernel, top_k=top_k, qt=qt, first_tile=first_tile, aliased=prev is not None),
            grid=(B, n_tiles),
            in_specs=in_specs,
            out_specs=pl.BlockSpec((None, out_rows, width), lambda b, i: (b, out_tile0 + i, 0)),
            out_shape=jax.ShapeDtypeStruct((B, tp, width), BF16),
            input_output_aliases={} if prev is None else {len(args) - 1: 0},
            scratch_shapes=[pltpu.VMEM((diag.shape[0] + 1, rows_h, K_CHUNK), F32),
                            pltpu.VMEM((n_chunks_max, qt, K_CHUNK), jnp.int32),
                            pltpu.VMEM((n_chunks_max, K_CHUNK, qt), jnp.int32),
                            pltpu.VMEM((n_chunks_max, qt, K_CHUNK), F32),
                            pltpu.VMEM((LANES, qt), F32),
                            pltpu.VMEM((rows_h, KV_RANK), BF16),
                            pltpu.VMEM((n_chunks_max, rows_h, K_CHUNK), F32),
                            pltpu.VMEM((rows_h, LANES), F32),
                            pltpu.VMEM((rows_h, LANES), F32),
                            pltpu.VMEM((rows_h, KV_RANK), F32)],
            compiler_params=_compiler_params(("arbitrary", "arbitrary")),
            name="dsa_attention" if prev is None else "dsa_attention_tail",
        )(*args)

    o_lat = call(Q_TILE, 0, n_main, Q_TILE, None)
    return call(TAIL_TILE, n_main * Q_TILE // TAIL_TILE, 1, tp - n_main * Q_TILE, o_lat)


def _mix_window_start(r, seq):
    return min(max(r * MIX_ROWS - N_META, 0), seq - MIX_ROWS)


def _residual_rows(x_ref, meta_ref, h_scr, seq):
    r = pl.program_id(1)
    n_real = -(-(seq + N_META) // MIX_ROWS)
    for rv in range(n_real):

        @pl.when(r == rv)
        def _(rv=rv):
            skip = rv * MIX_ROWS - N_META - _mix_window_start(rv, seq)
            if rv == 0:
                h_scr[:N_META, :] = meta_ref[...]
                h_scr[N_META:, :] = x_ref[:MIX_ROWS - N_META, :]
            elif skip == 0:
                h_scr[...] = x_ref[...]
            else:
                h_scr[:MIX_ROWS - skip, :] = x_ref[skip:, :]
                h_scr[MIX_ROWS - skip:, :] = jnp.zeros((skip, h_scr.shape[1]), F32)

    @pl.when(r >= n_real)
    def _():
        h_scr[...] = jnp.zeros(h_scr.shape, F32)


def _mix_kernel(ol_ref, pd_ref, x_ref, meta_ref, wuv_ref, wp_ref, ps_ref, wo_ref, g_ref, b_ref, h1_ref, h1b_ref,
                h_scr, *, seq):
    _residual_rows(x_ref, meta_ref, h_scr, seq)
    attn = [_dot(ol_ref[:, h * KV_RANK:(h + 1) * KV_RANK], wuv_ref[h]) for h in range(N_HEADS)]
    pool = [_dot(pd_ref[:, g * POOL_GROUP:(g + 1) * POOL_GROUP], wp_ref[g]) for g in range(len(POOL_WINDOWS))]
    pool = jnp.concatenate(pool, axis=-1) * ps_ref[...]
    cat = jnp.concatenate(attn + [pool], axis=-1).astype(BF16)
    y = ALPHA * h_scr[...] + _dot(cat, wo_ref[...])
    h1 = _layer_norm(y, g_ref[...], b_ref[...])
    h1_ref[...] = h1
    h1b_ref[...] = h1.astype(h1b_ref.dtype)


def _mix_ln1(o_lat, pool_diff, x, meta, wuv, wpool, pool_scale, wo, g, b):
    B, tp, _ = o_lat.shape
    seq, d = x.shape[1:]
    row = lambda n: pl.BlockSpec((None, MIX_ROWS, n), lambda bi, r: (bi, r, 0))
    full = lambda a: pl.BlockSpec(a.shape, lambda bi, r: (0,) * a.ndim)
    window = pl.BlockSpec(
        (None, pl.Element(MIX_ROWS), pl.Element(d)),
        lambda bi, r: (bi, pl.multiple_of(jnp.clip(r * MIX_ROWS - N_META, 0, seq - MIX_ROWS), N_META), 0))
    return pl.pallas_call(
        functools.partial(_mix_kernel, seq=seq),
        grid=(B, tp // MIX_ROWS),
        in_specs=[row(o_lat.shape[2]), row(pool_diff.shape[2]), window, full(meta),
                  full(wuv), full(wpool), full(pool_scale), full(wo), full(g), full(b)],
        out_specs=[row(d), row(d)],
        out_shape=[jax.ShapeDtypeStruct((B, tp, d), F32), jax.ShapeDtypeStruct((B, tp, d), BF16)],
        scratch_shapes=[pltpu.VMEM((MIX_ROWS, d), F32)],
        compiler_params=_compiler_params(("parallel", "arbitrary")),
        name="mix_ln1",
    )(o_lat, pool_diff, x, meta, wuv, wpool, pool_scale, wo, g, b)


def _gelu_tanh(x):
    return 0.5 * x * (1.0 + jnp.tanh(math.sqrt(2.0 / math.pi) * (x + 0.044715 * (x * x * x))))


def _ffn_kernel(hw_ref, hres_ref, wa_ref, wg_ref, cwa_ref, cwg_ref, cba_ref, cbg_ref, wd_ref, g_ref, b_ref, o_ref,
                za_scr, zg_scr):
    c = pl.program_id(2)
    n_c = pl.num_programs(2) - 1

    def up(slot):
        x = hw_ref[...]
        za_scr[slot] = _dot(x, wa_ref[...])
        zg_scr[slot] = _dot(x, wg_ref[...])

    def conv(z, cw_ref, cb_ref):
        cw = cw_ref[...]
        n = z.shape[0]
        out = z[HALO - 2:n - 2] * cw[0:1] + z[HALO - 1:n - 1] * cw[1:2] + z[HALO:] * cw[2:3]
        return out + cb_ref[...]

    def down(slot):
        a = conv(za_scr[slot], cwa_ref, cba_ref)
        gate = conv(zg_scr[slot], cwg_ref, cbg_ref)
        act = (_gelu_tanh(a) * gate).astype(BF16)
        o_ref[...] += _dot(act, wd_ref[...])

    @pl.when(c == 0)
    def _():
        o_ref[...] = jnp.zeros(o_ref.shape, o_ref.dtype)
        up(0)

    @pl.when((c > 0) & (c < n_c))
    def _():
        up(c % 2)
        down((c - 1) % 2)

    @pl.when(c == n_c)
    def _():
        down((c - 1) % 2)
        o_ref[...] = _layer_norm(ALPHA * hres_ref[...] + o_ref[...], g_ref[...], b_ref[...])


def _ffn_ln2(h1, h1b, w_up, conv_w, conv_b, w_down, g, b, seq):
    B, tp, d = h1.shape
    d_ff = w_down.shape[0]
    n_c = d_ff // FFN_COLS
    up_c = lambda c: jnp.minimum(c, n_c - 1)
    dn_c = lambda c: jnp.maximum(c - 1, 0)
    vec = lambda off: pl.BlockSpec((1, FFN_COLS), lambda bi, r, c: (0, dn_c(c) + off))
    return pl.pallas_call(
        _ffn_kernel,
        grid=(B, seq // FFN_ROWS, n_c + 1),
        in_specs=[
            pl.BlockSpec((None, pl.Element(FFN_ROWS + HALO), pl.Element(d)),
                         lambda bi, r, c: (bi, r * FFN_ROWS + N_META - HALO, 0)),
            pl.BlockSpec((None, pl.Element(FFN_ROWS), pl.Element(d)),
                         lambda bi, r, c: (bi, pl.multiple_of(r * FFN_ROWS + N_META, N_META), 0),
                         pipeline_mode=pl.Buffered(1)),
            pl.BlockSpec((d, FFN_COLS), lambda bi, r, c: (0, up_c(c))),
            pl.BlockSpec((d, FFN_COLS), lambda bi, r, c: (0, up_c(c) + n_c)),
            pl.BlockSpec((CONV_WIDTH, FFN_COLS), lambda bi, r, c: (0, dn_c(c))),
            pl.BlockSpec((CONV_WIDTH, FFN_COLS), lambda bi, r, c: (0, dn_c(c) + n_c)),
            vec(0), vec(n_c),
            pl.BlockSpec((FFN_COLS, d), lambda bi, r, c: (dn_c(c), 0)),
            pl.BlockSpec((1, d), lambda bi, r, c: (0, 0)),
            pl.BlockSpec((1, d), lambda bi, r, c: (0, 0)),
        ],
        out_specs=pl.BlockSpec((None, FFN_ROWS, d), lambda bi, r, c: (bi, r, 0), pipeline_mode=pl.Buffered(1)),
        out_shape=jax.ShapeDtypeStruct((B, seq, d), F32),
        scratch_shapes=[pltpu.VMEM((2, FFN_ROWS + HALO, FFN_COLS), F32),
                        pltpu.VMEM((2, FFN_ROWS + HALO, FFN_COLS), F32)],
        compiler_params=_compiler_params(("parallel", "parallel", "arbitrary")),
        name="ffn_ln2",
    )(h1b, h1, w_up, w_up, conv_w, conv_w, conv_b, conv_b, w_down, g, b)


def _t5_bucket_table(n):
    dist = np.arange(n, dtype=np.int32)
    max_exact = REL_BUCKETS // 2
    d_f = np.maximum(dist, 1).astype(np.float32)
    large = max_exact + (np.log(d_f / np.float32(max_exact)) / np.float32(math.log(REL_MAX_DIST / max_exact))
                         * np.float32(REL_BUCKETS - max_exact)).astype(np.int32)
    return np.where(dist < max_exact, dist, np.minimum(large, REL_BUCKETS - 1))


def _near_bias_diagonals(rel_bias, qt):
    probe = _t5_bucket_table(4 * REL_MAX_DIST)
    first_far = int(np.argmax(probe == REL_BUCKETS - 1))
    assert np.all(probe[first_far:] == REL_BUCKETS - 1)
    n_real = -(-(first_far + K_CHUNK - 1) // qt)
    buckets = _t5_bucket_table(n_real * qt + K_CHUNK)
    period = 2 * K_CHUNK
    u = np.arange(period)
    k = np.arange(n_real)[:, None]
    dist = np.where(u < K_CHUNK, k * qt - u, k * qt + period - u)
    idx = buckets[np.clip(dist, 0, len(buckets) - 1)]
    rel = rel_bias.astype(F32) - rel_bias[REL_BUCKETS - 1:].astype(F32)
    return jnp.transpose(rel[idx], (0, 2, 1))


def kernel(x, meta, rel_bias, w_in, kv_norm_g, w_uk, w_uv, w_pool, pool_scale, w_o, ln1_g, ln1_b, w_up, conv_w,
           conv_b, w_down, ln2_g, ln2_b):
    B, S, D = x.shape
    assert w_in.shape[0] == DEPTH and S % FFN_ROWS == 0
    T = S + N_META
    tp = -(-T // ROW_ALIGN) * ROW_ALIGN
    assert tp - T >= max(POOL_WINDOWS)
    top_k = min(TOPK_MAX, S // 4)
    assert top_k <= K_CHUNK

    attn_w = N_HEADS * HEAD_DIM
    idx_w = IDX_HEADS * IDX_DIM
    o_c, o_qi = attn_w, attn_w + KV_RANK
    o_ki = o_qi + idx_w
    o_wi = o_ki + IDX_DIM
    o_u = o_wi + IDX_HEADS
    w = w_in[0]
    w_qq = jnp.concatenate([w[:, :o_c], w[:, o_qi:o_ki] * (IDX_DIM ** -0.5)], axis=1).astype(BF16)
    w_u = w[:, o_u:].astype(BF16)
    z64 = jnp.zeros((D, IDX_DIM), w.dtype)
    w_small = jnp.concatenate([w[:, o_c:o_qi], w[:, o_ki:o_wi], z64, z64, w[:, o_ki:o_wi],
                               w[:, o_wi:o_u] * (IDX_HEADS ** -0.5), jnp.zeros((D, 128 - IDX_HEADS), w.dtype)],
                              axis=1).astype(BF16)

    qq = _proj_qq(x, meta, w_qq, tp)
    pool_diff = _proj_pool(x, meta, w_u, tp)
    c_kv, kk, wi = _proj_small(x, meta, w_small, kv_norm_g[0].reshape(1, KV_RANK), tp)

    wuk = jnp.transpose(w_uk[0], (1, 2, 0)).astype(BF16)
    o_lat = _dsa_attention(qq, wi, kk, c_kv, wuk, rel_bias, top_k, S)

    wuv = jnp.transpose(w_uv[0], (1, 0, 2)).astype(BF16)
    h1, h1b = _mix_ln1(o_lat, pool_diff, x, meta, wuv, w_pool[0].astype(BF16), pool_scale[0].reshape(1, -1),
                       w_o[0].astype(BF16), ln1_g[0].reshape(1, D), ln1_b[0].reshape(1, D))

    return _ffn_ln2(h1, h1b, w_up[0].astype(BF16), conv_w[0], conv_b[0].reshape(1, -1), w_down[0].astype(BF16),
                    ln2_g[0].reshape(1, D), ln2_b[0].reshape(1, D), S)
```

```python
import functools
import math

import numpy as np
import jax
import jax.numpy as jnp
from jax import lax
from jax.experimental import pallas as pl
from jax.experimental.pallas import tpu as pltpu

F32 = jnp.float32
BF16 = jnp.bfloat16

N_META = 16
N_HEADS = 8
HEAD_DIM = 128
KV_RANK = 256
IDX_HEADS = 16
IDX_DIM = 64
TOPK_MAX = 256
POOL_WINDOWS = (2, 4, 8, 16)
POOL_GROUP = 256
CONV_WIDTH = 3
REL_BUCKETS = 32
REL_MAX_DIST = 128
DEPTH = 1
ALPHA = (2.0 * DEPTH) ** 0.25
LN_EPS = 1e-5
NEG_INF = -1e30

VMEM_LIMIT_BYTES = 56 * 1024 * 1024
SUBLANES = 8
LANES = 128
ROW_ALIGN = 768
PROJ_ROWS = 1024
Q_TILE = 256
TAIL_TILE = 16
K_CHUNK = 256
MIX_ROWS = 384
FFN_ROWS = 1024
FFN_COLS = 512
HALO = 16
INT_MIN = -(2 ** 31)


def _dot(a, b):
    return jnp.dot(a, b, preferred_element_type=F32)


def _dot_nt(a, b):
    return lax.dot_general(a, b, (((1,), (1,)), ((), ())), preferred_element_type=F32)


def _layer_norm(y, g, b):
    mu = jnp.mean(y, axis=-1, keepdims=True)
    yc = y - mu
    var = jnp.mean(yc * yc, axis=-1, keepdims=True)
    return yc * lax.rsqrt(var + LN_EPS) * g + b


def _proj_rows(x_ref, meta_ref, w_ref, tp, store):
    seq = x_ref.shape[0]
    store(pl.ds(0, N_META), _dot(meta_ref[...].astype(BF16), w_ref[...]))
    for r in range(seq // PROJ_ROWS):
        acc = _dot(x_ref[r * PROJ_ROWS:(r + 1) * PROJ_ROWS, :].astype(BF16), w_ref[...])
        store(pl.ds(N_META + r * PROJ_ROWS, PROJ_ROWS), acc)
    n_pad = tp - seq - N_META
    store(pl.ds(seq + N_META, n_pad), jnp.zeros((n_pad, w_ref.shape[1]), F32))


def _proj_cast_kernel(x_ref, meta_ref, w_ref, o_ref):
    def store(rows, acc):
        o_ref[rows, :] = acc.astype(o_ref.dtype)

    _proj_rows(x_ref, meta_ref, w_ref, o_ref.shape[0], store)


def _proj_pool_kernel(x_ref, meta_ref, w_ref, o_ref, u_scr):
    def store(rows, acc):
        u_scr[rows, :] = acc

    _proj_rows(x_ref, meta_ref, w_ref, u_scr.shape[0], store)
    group = pl.program_id(1)
    tp = u_scr.shape[0]
    pos = lax.broadcasted_iota(jnp.int32, (tp, 1), 0)
    for g, window in enumerate(POOL_WINDOWS):

        @pl.when(group == g)
        def _():
            u = u_scr[...]
            s = u
            shift = 1
            while shift < window:
                s = s + pltpu.roll(s, shift, axis=0)
                shift *= 2
            count = jnp.minimum(pos + 1, window).astype(F32)
            o_ref[...] = (s / count - u).astype(o_ref.dtype)


def _proj_small_kernel(x_ref, meta_ref, w_ref, g_ref, c_ref, kk_ref, wi_ref):
    def store(rows, acc):
        c = acc[:, :KV_RANK]
        ms = jnp.mean(c * c, axis=-1, keepdims=True)
        c_ref[rows, :] = (c * lax.rsqrt(ms + LN_EPS) * g_ref[...]).astype(c_ref.dtype)
        kk_ref[rows, :] = acc[:, KV_RANK:KV_RANK + LANES].astype(kk_ref.dtype)
        wi_ref[rows, :] = acc[:, KV_RANK + LANES:]

    _proj_rows(x_ref, meta_ref, w_ref, c_ref.shape[0], store)


def _batch_block(tp, n):
    return pl.BlockSpec((None, tp, n), lambda b, j: (b, 0, 0))


def _compiler_params(semantics):
    return pltpu.CompilerParams(dimension_semantics=semantics, vmem_limit_bytes=VMEM_LIMIT_BYTES)


def _proj_call(body, x, meta, w, tn, extra_in, extra_specs, out_blocks, out_shapes, scratch, name):
    B, seq, d = x.shape
    return pl.pallas_call(
        body,
        grid=(B, w.shape[1] // tn),
        in_specs=[_batch_block(seq, d), pl.BlockSpec(meta.shape, lambda b, j: (0, 0)),
                  pl.BlockSpec((d, tn), lambda b, j: (0, j))] + extra_specs,
        out_specs=out_blocks,
        out_shape=out_shapes,
        scratch_shapes=scratch,
        compiler_params=_compiler_params(("parallel", "arbitrary")),
        name=name,
    )(x, meta, w, *extra_in)


def _proj_qq(x, meta, w, tp):
    B, n, tn = x.shape[0], w.shape[1], 512
    return _proj_call(_proj_cast_kernel, x, meta, w, tn, [], [],
                      pl.BlockSpec((None, tp, tn), lambda b, j: (b, 0, j)),
                      jax.ShapeDtypeStruct((B, tp, n), BF16), [], "proj_qq")


def _proj_pool(x, meta, w, tp):
    B, n = x.shape[0], w.shape[1]
    return _proj_call(_proj_pool_kernel, x, meta, w, POOL_GROUP, [], [],
                      pl.BlockSpec((None, tp, POOL_GROUP), lambda b, j: (b, 0, j)),
                      jax.ShapeDtypeStruct((B, tp, n), BF16), [pltpu.VMEM((tp, POOL_GROUP), F32)], "proj_pool")


def _proj_small(x, meta, w, kv_g, tp):
    B = x.shape[0]
    return _proj_call(_proj_small_kernel, x, meta, w, w.shape[1], [kv_g],
                      [pl.BlockSpec((1, KV_RANK), lambda b, j: (0, 0))],
                      [_batch_block(tp, KV_RANK), _batch_block(tp, LANES), _batch_block(tp, 128)],
                      [jax.ShapeDtypeStruct((B, tp, KV_RANK), BF16), jax.ShapeDtypeStruct((B, tp, LANES), BF16),
                       jax.ShapeDtypeStruct((B, tp, 128), F32)], [], "proj_small")


def _attn_kernel(qq_ref, wi_ref, kk_ref, c_ref, wuk_ref, diag_ref, *rest, top_k, qt, first_tile, aliased):
    o_ref, nb_scr, key_scr, keyt_scr, mb_scr, wt_scr, qa_scr, s_scr, m_scr, l_scr, acc_scr = rest[int(aliased):]
    i = first_tile + pl.program_id(1)
    n_chunks = ((i + 1) * qt - 1) // K_CHUNK + 1
    attn_w = N_HEADS * HEAD_DIM
    scale = HEAD_DIM ** -0.5
    n_pairs = IDX_HEADS // 2
    lanes_are_queries = qt % LANES == 0

    t_col = i * qt + lax.broadcasted_iota(jnp.int32, (qt, 1), 0)
    t_row = i * qt + lax.broadcasted_iota(jnp.int32, (1, qt), 1)
    s_row = lax.broadcasted_iota(jnp.int32, (1, K_CHUNK), 1)
    s_col = lax.broadcasted_iota(jnp.int32, (K_CHUNK, 1), 0)
    lane_half = lax.broadcasted_iota(jnp.int32, (K_CHUNK, LANES), 1) // IDX_DIM

    n_near = nb_scr.shape[0]

    @pl.when((pl.program_id(0) == 0) & (pl.program_id(1) == 0))
    def _():
        for k in range(n_near - 1):
            for h in range(N_HEADS):
                v = jnp.broadcast_to(diag_ref[k, h:h + 1, :], (qt, 2 * K_CHUNK))
                t = pltpu.roll(v, 0, 1, stride=1, stride_axis=0)
                nb_scr[k, h * qt:(h + 1) * qt, :] = t[:, :K_CHUNK]
        nb_scr[n_near - 1] = jnp.zeros(nb_scr.shape[1:], F32)

    for h in range(N_HEADS):
        qa_scr[h * qt:(h + 1) * qt, :] = _dot(
            qq_ref[:, h * HEAD_DIM:(h + 1) * HEAD_DIM], wuk_ref[h]).astype(BF16)
    if lanes_are_queries:
        wt_scr[...] = wi_ref[...].T

    def idx_chunk(j, carry):
        ks = kk_ref[pl.ds(pl.multiple_of(j * K_CHUNK, K_CHUNK), K_CHUNK), :]
        zero = jnp.zeros(ks.shape, ks.dtype)
        k_even = jnp.where(lane_half == 0, ks, zero)
        k_odd = jnp.where(lane_half == 1, ks, zero)
        score = jnp.zeros((K_CHUNK, qt) if lanes_are_queries else (qt, K_CHUNK), F32)
        for p in range(n_pairs):
            q_pair = qq_ref[:, attn_w + p * LANES:attn_w + (p + 1) * LANES]
            for hh, k_half in ((2 * p, k_even), (2 * p + 1, k_odd)):
                if lanes_are_queries:
                    score = score + jnp.maximum(_dot_nt(k_half, q_pair), 0.0) * wt_scr[hh:hh + 1, :]
                else:
                    score = score + jnp.maximum(_dot_nt(q_pair, k_half), 0.0) * wi_ref[:, hh:hh + 1]
        s_pos = j * K_CHUNK + (s_col if lanes_are_queries else s_row)
        score = jnp.where(s_pos <= (t_row if lanes_are_queries else t_col), score, NEG_INF)
        bits = lax.bitcast_convert_type(score, jnp.int32)
        key = jnp.where(bits < 0, bits ^ jnp.int32(0x7FFFFFFF), bits)
        if lanes_are_queries:
            keyt_scr[j] = key
            key_scr[j] = key.T
        else:
            key_scr[j] = key
        return carry

    lax.fori_loop(0, n_chunks, idx_chunk, 0)

    per_query = (1, qt) if lanes_are_queries else (qt, 1)
    s_idx = s_col if lanes_are_queries else s_row

    def count_where(pred):
        acc_rows = 4 * SUBLANES

        def body(j, acc):
            if lanes_are_queries:
                hit = jnp.where(pred(j, keyt_scr[j]), 1.0, 0.0)
                return acc + jnp.sum(hit.reshape(K_CHUNK // acc_rows, acc_rows, qt), axis=0)
            return acc + jnp.where(pred(j, key_scr[j]), 1.0, 0.0)

        if lanes_are_queries:
            acc = lax.fori_loop(0, n_chunks, body, jnp.zeros((acc_rows, qt), F32))
            return jnp.sum(acc, axis=0, keepdims=True)
        acc = lax.fori_loop(0, n_chunks, body, jnp.zeros((qt, K_CHUNK), F32))
        return jnp.sum(acc, axis=1, keepdims=True)

    k_f = float(top_k)

    def bit_step(_, carry):
        thr, bit = carry
        cand = thr + bit
        cnt = count_where(lambda j, key: key >= cand)
        return jnp.where(cnt >= k_f, cand, thr), lax.shift_right_logical(bit, jnp.int32(1))

    thr, _ = lax.fori_loop(0, 32, bit_step,
                           (jnp.full(per_query, INT_MIN, jnp.int32), jnp.int32(INT_MIN)))

    n_gt = count_where(lambda j, key: key > thr)
    n_ge = count_where(lambda j, key: key >= thr)
    need = k_f - n_gt
    neg_key = jnp.int32(np.array(NEG_INF, np.float32).view(np.int32) ^ 0x7FFFFFFF)
    has_tie = jnp.max(jnp.where((n_ge > k_f) & (thr > neg_key), 1.0, 0.0)) > 0.0
    idx_bits = int(math.ceil(math.log2(key_scr.shape[0] * K_CHUNK)))

    def tie_cut():
        def step(_, carry):
            cut, bit = carry
            cand = cut + bit
            cnt = count_where(lambda j, key: (key == thr) & ((j * K_CHUNK + s_idx) < cand))
            return jnp.where(cnt < need, cand, cut), lax.shift_right_logical(bit, jnp.int32(1))

        cut, _ = lax.fori_loop(0, idx_bits, step,
                               (jnp.zeros(per_query, jnp.int32), jnp.int32(2 ** (idx_bits - 1))))
        return cut

    cut = lax.cond(has_tie, tie_cut, lambda: jnp.full(per_query, 2 ** 30, jnp.int32))

    def to_rows(v):
        if not lanes_are_queries:
            return jnp.broadcast_to(v, (qt, K_CHUNK))
        t = jnp.broadcast_to(v, (qt, qt)).T
        return jnp.concatenate([t] * (K_CHUNK // qt), axis=1)

    thr_b, cut_b = to_rows(thr), to_rows(cut)

    def mask_chunk(j, carry):
        key = key_scr[j]
        s_pos = j * K_CHUNK + s_row
        keep = ((key > thr_b) | ((key == thr_b) & (s_pos <= cut_b))) & (s_pos <= t_col)
        mb_scr[j] = jnp.where(keep, 0.0, NEG_INF)
        return carry

    lax.fori_loop(0, n_chunks, mask_chunk, 0)

    rows_h = N_HEADS * qt
    lane_fold = lambda v, op: functools.reduce(op, [v[:, k * LANES:(k + 1) * LANES] for k in range(K_CHUNK // LANES)])

    def key_rows(j):
        return pl.ds(pl.multiple_of(j * K_CHUNK, K_CHUNK), K_CHUNK)

    def logit_chunk(j, carry):
        near = jnp.minimum((i * qt) // K_CHUNK - j, n_near - 1)
        s = _dot_nt(qa_scr[...], c_ref[key_rows(j), :]) * scale + nb_scr[near]
        s = s + jnp.concatenate([mb_scr[j]] * N_HEADS, axis=0)
        s_scr[j] = s
        m_scr[...] = jnp.maximum(m_scr[...], lane_fold(s, jnp.maximum))
        return carry

    m_scr[...] = jnp.full(m_scr.shape, NEG_INF, F32)
    lax.fori_loop(0, n_chunks, logit_chunk, 0)
    m_b = jnp.broadcast_to(jnp.max(m_scr[...], axis=-1, keepdims=True), (rows_h, LANES))
    m_scr[...] = m_b
    l_scr[...] = jnp.zeros(l_scr.shape, F32)
    acc_scr[...] = jnp.zeros(acc_scr.shape, F32)

    def value_chunk(j, carry):
        p = jnp.exp(s_scr[j] - jnp.concatenate([m_scr[...]] * (K_CHUNK // LANES), axis=1))
        l_scr[...] += lane_fold(p, jnp.add)
        acc_scr[...] += _dot(p.astype(BF16), c_ref[key_rows(j), :])
        return carry

    lax.fori_loop(0, n_chunks, value_chunk, 0)
    out = acc_scr[...] * (1.0 / jnp.sum(l_scr[...], axis=-1, keepdims=True))
    for h in range(N_HEADS):
        o_ref[:qt, h * KV_RANK:(h + 1) * KV_RANK] = out[h * qt:(h + 1) * qt].astype(o_ref.dtype)
    if o_ref.shape[0] > qt:
        o_ref[qt:, :] = jnp.zeros((o_ref.shape[0] - qt, o_ref.shape[1]), o_ref.dtype)


def _dsa_attention(qq, wi, kk, c_kv, wuk, rel_bias, top_k, seq):
    B, tp, _ = qq.shape
    n_main = (seq + N_META) // Q_TILE
    assert (seq + N_META) - n_main * Q_TILE <= TAIL_TILE and Q_TILE % K_CHUNK == 0 and K_CHUNK % TAIL_TILE == 0
    n_chunks_max = tp // K_CHUNK
    width = N_HEADS * KV_RANK

    def call(qt, first_tile, n_tiles, out_rows, prev):
        rows_h = N_HEADS * qt
        diag = _near_bias_diagonals(rel_bias, qt)
        out_tile0 = first_tile * qt // out_rows
        in_specs = [
            pl.BlockSpec((None, qt, qq.shape[2]), lambda b, i: (b, first_tile + i, 0)),
            pl.BlockSpec((None, qt, 128), lambda b, i: (b, first_tile + i, 0)),
            pl.BlockSpec((None, tp, LANES), lambda b, i: (b, 0, 0)),
            pl.BlockSpec((None, tp, KV_RANK), lambda b, i: (b, 0, 0)),
            pl.BlockSpec(wuk.shape, lambda b, i: (0, 0, 0)),
            pl.BlockSpec(diag.shape, lambda b, i: (0, 0, 0)),
        ]
        args = [qq, wi, kk, c_kv, wuk, diag]
        if prev is not None:
            in_specs.append(pl.BlockSpec(memory_space=pl.ANY))
            args.append(prev)
        return pl.pallas_call(
            functools.partial(_attn_kernel, top_k=top_k, qt=qt, first_tile=first_tile, aliased=prev is not None),
            grid=(B, n_tiles),
            in_specs=in_specs,
            out_specs=pl.BlockSpec((None, out_rows, width), lambda b, i: (b, out_tile0 + i, 0)),
            out_shape=jax.ShapeDtypeStruct((B, tp, width), BF16),
            input_output_aliases={} if prev is None else {len(args) - 1: 0},
            scratch_shapes=[pltpu.VMEM((diag.shape[0] + 1, rows_h, K_CHUNK), F32),
                            pltpu.VMEM((n_chunks_max, qt, K_CHUNK), jnp.int32),
                            pltpu.VMEM((n_chunks_max, K_CHUNK, qt), jnp.int32),
                            pltpu.VMEM((n_chunks_max, qt, K_CHUNK), F32),
                            pltpu.VMEM((LANES, qt), F32),
                            pltpu.VMEM((rows_h, KV_RANK), BF16),
                            pltpu.VMEM((n_chunks_max, rows_h, K_CHUNK), F32),
                            pltpu.VMEM((rows_h, LANES), F32),
                            pltpu.VMEM((rows_h, LANES), F32),
                            pltpu.VMEM((rows_h, KV_RANK), F32)],
            compiler_params=_compiler_params(("arbitrary", "arbitrary")),
            name="dsa_attention" if prev is None else "dsa_attention_tail",
        )(*args)

    o_lat = call(Q_TILE, 0, n_main, Q_TILE, None)
    return call(TAIL_TILE, n_main * Q_TILE // TAIL_TILE, 1, tp - n_main * Q_TILE, o_lat)


def _mix_window_start(r, seq):
    return min(max(r * MIX_ROWS - N_META, 0), seq - MIX_ROWS)


def _residual_rows(x_ref, meta_ref, h_scr, seq):
    r = pl.program_id(1)
    n_real = -(-(seq + N_META) // MIX_ROWS)
    for rv in range(n_real):

        @pl.when(r == rv)
        def _(rv=rv):
            skip = rv * MIX_ROWS - N_META - _mix_window_start(rv, seq)
            if rv == 0:
                h_scr[:N_META, :] = meta_ref[...]
                h_scr[N_META:, :] = x_ref[:MIX_ROWS - N_META, :]
            elif skip == 0:
                h_scr[...] = x_ref[...]
            else:
                h_scr[:MIX_ROWS - skip, :] = x_ref[skip:, :]
                h_scr[MIX_ROWS - skip:, :] = jnp.zeros((skip, h_scr.shape[1]), F32)

    @pl.when(r >= n_real)
    def _():
        h_scr[...] = jnp.zeros(h_scr.shape, F32)


def _mix_kernel(ol_ref, pd_ref, x_ref, meta_ref, wuv_ref, wp_ref, ps_ref, wo_ref, g_ref, b_ref, h1_ref, h1b_ref,
                h_scr, *, seq):
    _residual_rows(x_ref, meta_ref, h_scr, seq)
    attn = [_dot(ol_ref[:, h * KV_RANK:(h + 1) * KV_RANK], wuv_ref[h]) for h in range(N_HEADS)]
    pool = [_dot(pd_ref[:, g * POOL_GROUP:(g + 1) * POOL_GROUP], wp_ref[g]) for g in range(len(POOL_WINDOWS))]
    pool = jnp.concatenate(pool, axis=-1) * ps_ref[...]
    cat = jnp.concatenate(attn + [pool], axis=-1).astype(BF16)
    y = ALPHA * h_scr[...] + _dot(cat, wo_ref[...])
    h1 = _layer_norm(y, g_ref[...], b_ref[...])
    h1_ref[...] = h1
    h1b_ref[...] = h1.astype(h1b_ref.dtype)


def _mix_ln1(o_lat, pool_diff, x, meta, wuv, wpool, pool_scale, wo, g, b):
    B, tp, _ = o_lat.shape
    seq, d = x.shape[1:]
    row = lambda n: pl.BlockSpec((None, MIX_ROWS, n), lambda bi, r: (bi, r, 0))
    full = lambda a: pl.BlockSpec(a.shape, lambda bi, r: (0,) * a.ndim)
    window = pl.BlockSpec(
        (None, pl.Element(MIX_ROWS), pl.Element(d)),
        lambda bi, r: (bi, pl.multiple_of(jnp.clip(r * MIX_ROWS - N_META, 0, seq - MIX_ROWS), N_META), 0))
    return pl.pallas_call(
        functools.partial(_mix_kernel, seq=seq),
        grid=(B, tp // MIX_ROWS),
        in_specs=[row(o_lat.shape[2]), row(pool_diff.shape[2]), window, full(meta),
                  full(wuv), full(wpool), full(pool_scale), full(wo), full(g), full(b)],
        out_specs=[row(d), row(d)],
        out_shape=[jax.ShapeDtypeStruct((B, tp, d), F32), jax.ShapeDtypeStruct((B, tp, d), BF16)],
        scratch_shapes=[pltpu.VMEM((MIX_ROWS, d), F32)],
        compiler_params=_compiler_params(("parallel", "arbitrary")),
        name="mix_ln1",
    )(o_lat, pool_diff, x, meta, wuv, wpool, pool_scale, wo, g, b)


def _gelu_tanh(x):
    return 0.5 * x * (1.0 + jnp.tanh(math.sqrt(2.0 / math.pi) * (x + 0.044715 * (x * x * x))))


def _ffn_kernel(hw_ref, hres_ref, wa_ref, wg_ref, cwa_ref, cwg_ref, cba_ref, cbg_ref, wd_ref, g_ref, b_ref, o_ref,
                za_scr, zg_scr):
    c = pl.program_id(2)
    n_c = pl.num_programs(2) - 1

    def up(slot):
        x = hw_ref[...]
        za_scr[slot] = _dot(x, wa_ref[...])
        zg_scr[slot] = _dot(x, wg_ref[...])

    def conv(z, cw_ref, cb_ref):
        cw = cw_ref[...]
        n = z.shape[0]
        out = z[HALO - 2:n - 2] * cw[0:1] + z[HALO - 1:n - 1] * cw[1:2] + z[HALO:] * cw[2:3]
        return out + cb_ref[...]

    def down(slot):
        a = conv(za_scr[slot], cwa_ref, cba_ref)
        gate = conv(zg_scr[slot], cwg_ref, cbg_ref)
        act = (_gelu_tanh(a) * gate).astype(BF16)
        o_ref[...] += _dot(act, wd_ref[...])

    @pl.when(c == 0)
    def _():
        o_ref[...] = jnp.zeros(o_ref.shape, o_ref.dtype)
        up(0)

    @pl.when((c > 0) & (c < n_c))
    def _():
        up(c % 2)
        down((c - 1) % 2)

    @pl.when(c == n_c)
    def _():
        down((c - 1) % 2)
        o_ref[...] = _layer_norm(ALPHA * hres_ref[...] + o_ref[...], g_ref[...], b_ref[...])


def _ffn_ln2(h1, h1b, w_up, conv_w, conv_b, w_down, g, b, seq):
    B, tp, d = h1.shape
    d_ff = w_down.shape[0]
    n_c = d_ff // FFN_COLS
    up_c = lambda c: jnp.minimum(c, n_c - 1)
    dn_c = lambda c: jnp.maximum(c - 1, 0)
    vec = lambda off: pl.BlockSpec((1, FFN_COLS), lambda bi, r, c: (0, dn_c(c) + off))
    return pl.pallas_call(
        _ffn_kernel,
        grid=(B, seq // FFN_ROWS, n_c + 1),
        in_specs=[
            pl.BlockSpec((None, pl.Element(FFN_ROWS + HALO), pl.Element(d)),
                         lambda bi, r, c: (bi, r * FFN_ROWS + N_META - HALO, 0)),
            pl.BlockSpec((None, pl.Element(FFN_ROWS), pl.Element(d)),
                         lambda bi, r, c: (bi, pl.multiple_of(r * FFN_ROWS + N_META, N_META), 0),
                         pipeline_mode=pl.Buffered(1)),
            pl.BlockSpec((None, d, FFN_COLS), lambda bi, r, c: (up_c(c), 0, 0)),
            pl.BlockSpec((None, d, FFN_COLS), lambda bi, r, c: (up_c(c) + n_c, 0, 0)),
            pl.BlockSpec((CONV_WIDTH, FFN_COLS), lambda bi, r, c: (0, dn_c(c))),
            pl.BlockSpec((CONV_WIDTH, FFN_COLS), lambda bi, r, c: (0, dn_c(c) + n_c)),
            vec(0), vec(n_c),
            pl.BlockSpec((FFN_COLS, d), lambda bi, r, c: (dn_c(c), 0)),
            pl.BlockSpec((1, d), lambda bi, r, c: (0, 0)),
            pl.BlockSpec((1, d), lambda bi, r, c: (0, 0)),
        ],
        out_specs=pl.BlockSpec((None, FFN_ROWS, d), lambda bi, r, c: (bi, r, 0), pipeline_mode=pl.Buffered(1)),
        out_shape=jax.ShapeDtypeStruct((B, seq, d), F32),
        scratch_shapes=[pltpu.VMEM((2, FFN_ROWS + HALO, FFN_COLS), F32),
                        pltpu.VMEM((2, FFN_ROWS + HALO, FFN_COLS), F32)],
        compiler_params=_compiler_params(("parallel", "parallel", "arbitrary")),
        name="ffn_ln2",
    )(h1b, h1, w_up, w_up, conv_w, conv_w, conv_b, conv_b, w_down, g, b)


def _t5_bucket_table(n):
    dist = np.arange(n, dtype=np.int32)
    max_exact = REL_BUCKETS // 2
    d_f = np.maximum(dist, 1).astype(np.float32)
    large = max_exact + (np.log(d_f / np.float32(max_exact)) / np.float32(math.log(REL_MAX_DIST / max_exact))
                         * np.float32(REL_BUCKETS - max_exact)).astype(np.int32)
    return np.where(dist < max_exact, dist, np.minimum(large, REL_BUCKETS - 1))


def _near_bias_diagonals(rel_bias, qt):
    assert qt <= K_CHUNK
    probe = _t5_bucket_table(4 * REL_MAX_DIST)
    first_far = int(np.argmax(probe == REL_BUCKETS - 1))
    assert np.all(probe[first_far:] == REL_BUCKETS - 1)
    n_real = -(-(first_far + K_CHUNK - 1) // K_CHUNK)
    buckets = _t5_bucket_table((n_real + 1) * K_CHUNK)
    period = 2 * K_CHUNK
    u = np.arange(period)
    k = np.arange(n_real)[:, None]
    dist = np.where(u < K_CHUNK, k * K_CHUNK - u, k * K_CHUNK + period - u)
    idx = buckets[np.clip(dist, 0, len(buckets) - 1)]
    rel = rel_bias.astype(F32) - rel_bias[REL_BUCKETS - 1:].astype(F32)
    return jnp.transpose(rel[idx], (0, 2, 1))


def kernel(x, meta, rel_bias, w_in, kv_norm_g, w_uk, w_uv, w_pool, pool_scale, w_o, ln1_g, ln1_b, w_up, conv_w,
           conv_b, w_down, ln2_g, ln2_b):
    B, S, D = x.shape
    assert w_in.shape[0] == DEPTH and S % FFN_ROWS == 0
    T = S + N_META
    tp = -(-T // ROW_ALIGN) * ROW_ALIGN
    assert tp - T >= max(POOL_WINDOWS)
    top_k = min(TOPK_MAX, S // 4)
    assert top_k <= K_CHUNK

    attn_w = N_HEADS * HEAD_DIM
    idx_w = IDX_HEADS * IDX_DIM
    o_c, o_qi = attn_w, attn_w + KV_RANK
    o_ki = o_qi + idx_w
    o_wi = o_ki + IDX_DIM
    o_u = o_wi + IDX_HEADS
    w = w_in[0]
    w_qq = jnp.concatenate([w[:, :o_c], w[:, o_qi:o_ki] * (IDX_DIM ** -0.5)], axis=1).astype(BF16)
    w_u = w[:, o_u:].astype(BF16)
    w_small = jnp.concatenate([w[:, o_c:o_qi], w[:, o_ki:o_wi], w[:, o_ki:o_wi],
                               w[:, o_wi:o_u] * (IDX_HEADS ** -0.5), jnp.zeros((D, 128 - IDX_HEADS), w.dtype)],
                              axis=1).astype(BF16)

    qq = _proj_qq(x, meta, w_qq, tp)
    pool_diff = _proj_pool(x, meta, w_u, tp)
    c_kv, kk, wi = _proj_small(x, meta, w_small, kv_norm_g[0].reshape(1, KV_RANK), tp)

    wuk = jnp.transpose(w_uk[0], (1, 2, 0)).astype(BF16)
    o_lat = _dsa_attention(qq, wi, kk, c_kv, wuk, rel_bias, top_k, S)

    wuv = jnp.transpose(w_uv[0], (1, 0, 2)).astype(BF16)
    h1, h1b = _mix_ln1(o_lat, pool_diff, x, meta, wuv, w_pool[0].astype(BF16), pool_scale[0].reshape(1, -1),
                       w_o[0].astype(BF16), ln1_g[0].reshape(1, D), ln1_b[0].reshape(1, D))

    w_up_chunks = jnp.transpose(w_up[0].astype(BF16).reshape(D, -1, FFN_COLS), (1, 0, 2))
    return _ffn_ln2(h1, h1b, w_up_chunks, conv_w[0], conv_b[0].reshape(1, -1), w_down[0].astype(BF16),
                    ln2_g[0].reshape(1, D), ln2_b[0].reshape(1, D), S)
```

```python
import functools
import math

import numpy as np
import jax
import jax.numpy as jnp
from jax import lax
from jax.experimental import pallas as pl
from jax.experimental.pallas import tpu as pltpu

F32 = jnp.float32
BF16 = jnp.bfloat16

N_META = 16
N_HEADS = 8
HEAD_DIM = 128
KV_RANK = 256
IDX_HEADS = 16
IDX_DIM = 64
TOPK_MAX = 256
POOL_WINDOWS = (2, 4, 8, 16)
POOL_GROUP = 256
CONV_WIDTH = 3
REL_BUCKETS = 32
REL_MAX_DIST = 128
DEPTH = 1
ALPHA = (2.0 * DEPTH) ** 0.25
LN_EPS = 1e-5
NEG_INF = -1e30

VMEM_LIMIT_BYTES = 56 * 1024 * 1024
FFN_VMEM_LIMIT_BYTES = 60 * 1024 * 1024
SUBLANES = 8
LANES = 128
ROW_ALIGN = 768
PROJ_ROWS = 1024
Q_TILE = 256
TAIL_TILE = 16
K_CHUNK = 256
MIX_ROWS = 384
FFN_ROWS = 1024
FFN_COLS = 512
HALO = 16
INT_MIN = -(2 ** 31)
INT16_MIN = -(2 ** 15)
PACKED_SUBLANES = 16


def _dot(a, b):
    return jnp.dot(a, b, preferred_element_type=F32)


def _dot_nt(a, b):
    return lax.dot_general(a, b, (((1,), (1,)), ((), ())), preferred_element_type=F32)


def _layer_norm(y, g, b):
    mu = jnp.mean(y, axis=-1, keepdims=True)
    yc = y - mu
    var = jnp.mean(yc * yc, axis=-1, keepdims=True)
    return yc * lax.rsqrt(var + LN_EPS) * g + b


def _proj_rows(x_ref, meta_ref, w_ref, tp, store):
    seq = x_ref.shape[0]
    store(pl.ds(0, N_META), _dot(meta_ref[...].astype(BF16), w_ref[...]))
    for r in range(seq // PROJ_ROWS):
        acc = _dot(x_ref[r * PROJ_ROWS:(r + 1) * PROJ_ROWS, :].astype(BF16), w_ref[...])
        store(pl.ds(N_META + r * PROJ_ROWS, PROJ_ROWS), acc)
    n_pad = tp - seq - N_META
    store(pl.ds(seq + N_META, n_pad), jnp.zeros((n_pad, w_ref.shape[1]), F32))


def _proj_cast_kernel(x_ref, meta_ref, w_ref, o_ref):
    def store(rows, acc):
        o_ref[rows, :] = acc.astype(o_ref.dtype)

    _proj_rows(x_ref, meta_ref, w_ref, o_ref.shape[0], store)


def _proj_pool_kernel(x_ref, meta_ref, w_ref, o_ref, u_scr):
    def store(rows, acc):
        u_scr[rows, :] = acc

    _proj_rows(x_ref, meta_ref, w_ref, u_scr.shape[0], store)
    group = pl.program_id(1)
    tp = u_scr.shape[0]
    pos = lax.broadcasted_iota(jnp.int32, (tp, 1), 0)
    for g, window in enumerate(POOL_WINDOWS):

        @pl.when(group == g)
        def _():
            u = u_scr[...]
            s = u
            shift = 1
            while shift < window:
                s = s + pltpu.roll(s, shift, axis=0)
                shift *= 2
            count = jnp.minimum(pos + 1, window).astype(F32)
            o_ref[...] = (s / count - u).astype(o_ref.dtype)


def _proj_small_kernel(x_ref, meta_ref, w_ref, g_ref, c_ref, kk_ref, wi_ref):
    def store(rows, acc):
        c = acc[:, :KV_RANK]
        ms = jnp.mean(c * c, axis=-1, keepdims=True)
        c_ref[rows, :] = (c * lax.rsqrt(ms + LN_EPS) * g_ref[...]).astype(c_ref.dtype)
        kk_ref[rows, :] = acc[:, KV_RANK:KV_RANK + LANES].astype(kk_ref.dtype)
        wi_ref[rows, :] = acc[:, KV_RANK + LANES:]

    _proj_rows(x_ref, meta_ref, w_ref, c_ref.shape[0], store)


def _batch_block(tp, n):
    return pl.BlockSpec((None, tp, n), lambda b, j: (b, 0, 0))


def _compiler_params(semantics, vmem_limit_bytes=VMEM_LIMIT_BYTES):
    return pltpu.CompilerParams(dimension_semantics=semantics, vmem_limit_bytes=vmem_limit_bytes)


def _proj_call(body, x, meta, w, tn, extra_in, extra_specs, out_blocks, out_shapes, scratch, name):
    B, seq, d = x.shape
    return pl.pallas_call(
        body,
        grid=(B, w.shape[1] // tn),
        in_specs=[_batch_block(seq, d), pl.BlockSpec(meta.shape, lambda b, j: (0, 0)),
                  pl.BlockSpec((d, tn), lambda b, j: (0, j))] + extra_specs,
        out_specs=out_blocks,
        out_shape=out_shapes,
        scratch_shapes=scratch,
        compiler_params=_compiler_params(("parallel", "arbitrary")),
        name=name,
    )(x, meta, w, *extra_in)


def _proj_qq(x, meta, w, tp):
    B, n, tn = x.shape[0], w.shape[1], 512
    return _proj_call(_proj_cast_kernel, x, meta, w, tn, [], [],
                      pl.BlockSpec((None, tp, tn), lambda b, j: (b, 0, j)),
                      jax.ShapeDtypeStruct((B, tp, n), BF16), [], "proj_qq")


def _proj_pool(x, meta, w, tp):
    B, n = x.shape[0], w.shape[1]
    return _proj_call(_proj_pool_kernel, x, meta, w, POOL_GROUP, [], [],
                      pl.BlockSpec((None, tp, POOL_GROUP), lambda b, j: (b, 0, j)),
                      jax.ShapeDtypeStruct((B, tp, n), BF16), [pltpu.VMEM((tp, POOL_GROUP), F32)], "proj_pool")


def _proj_small(x, meta, w, kv_g, tp):
    B = x.shape[0]
    return _proj_call(_proj_small_kernel, x, meta, w, w.shape[1], [kv_g],
                      [pl.BlockSpec((1, KV_RANK), lambda b, j: (0, 0))],
                      [_batch_block(tp, KV_RANK), _batch_block(tp, LANES), _batch_block(tp, 128)],
                      [jax.ShapeDtypeStruct((B, tp, KV_RANK), BF16), jax.ShapeDtypeStruct((B, tp, LANES), BF16),
                       jax.ShapeDtypeStruct((B, tp, 128), F32)], [], "proj_small")


def _attn_kernel(qq_ref, wi_ref, kk_ref, c_ref, wuk_ref, diag_ref, *rest, top_k, qt, first_tile, aliased):
    o_ref, nb_scr, key_scr, khi_scr, klo_scr, mb_scr, wt_scr, qa_scr, s_scr, m_scr, l_scr, acc_scr = rest[int(aliased):]
    i = first_tile + pl.program_id(1)
    n_chunks = ((i + 1) * qt - 1) // K_CHUNK + 1
    attn_w = N_HEADS * HEAD_DIM
    scale = HEAD_DIM ** -0.5
    n_pairs = IDX_HEADS // 2
    lanes_are_queries = qt % LANES == 0

    t_col = i * qt + lax.broadcasted_iota(jnp.int32, (qt, 1), 0)
    t_row = i * qt + lax.broadcasted_iota(jnp.int32, (1, qt), 1)
    s_row = lax.broadcasted_iota(jnp.int32, (1, K_CHUNK), 1)
    s_col = lax.broadcasted_iota(jnp.int32, (K_CHUNK, 1), 0)
    lane_half = lax.broadcasted_iota(jnp.int32, (K_CHUNK, LANES), 1) // IDX_DIM

    n_near = nb_scr.shape[0]

    @pl.when((pl.program_id(0) == 0) & (pl.program_id(1) == 0))
    def _():
        for k in range(n_near - 1):
            for h in range(N_HEADS):
                v = jnp.broadcast_to(diag_ref[k, h:h + 1, :], (qt, 2 * K_CHUNK))
                t = pltpu.roll(v, 0, 1, stride=1, stride_axis=0)
                nb_scr[k, h * qt:(h + 1) * qt, :] = t[:, :K_CHUNK]
        nb_scr[n_near - 1] = jnp.zeros(nb_scr.shape[1:], F32)

    for h in range(N_HEADS):
        qa_scr[h * qt:(h + 1) * qt, :] = _dot(
            qq_ref[:, h * HEAD_DIM:(h + 1) * HEAD_DIM], wuk_ref[h]).astype(BF16)
    if lanes_are_queries:
        wt_scr[...] = wi_ref[...].T

    def idx_chunk(j, carry):
        ks = kk_ref[pl.ds(pl.multiple_of(j * K_CHUNK, K_CHUNK), K_CHUNK), :]
        zero = jnp.zeros(ks.shape, ks.dtype)
        k_even = jnp.where(lane_half == 0, ks, zero)
        k_odd = jnp.where(lane_half == 1, ks, zero)
        score = jnp.zeros((K_CHUNK, qt) if lanes_are_queries else (qt, K_CHUNK), F32)
        for p in range(n_pairs):
            q_pair = qq_ref[:, attn_w + p * LANES:attn_w + (p + 1) * LANES]
            for hh, k_half in ((2 * p, k_even), (2 * p + 1, k_odd)):
                if lanes_are_queries:
                    score = score + jnp.maximum(_dot_nt(k_half, q_pair), 0.0) * wt_scr[hh:hh + 1, :]
                else:
                    score = score + jnp.maximum(_dot_nt(q_pair, k_half), 0.0) * wi_ref[:, hh:hh + 1]
        s_pos = j * K_CHUNK + (s_col if lanes_are_queries else s_row)
        score = jnp.where(s_pos <= (t_row if lanes_are_queries else t_col), score, NEG_INF)
        bits = lax.bitcast_convert_type(score, jnp.int32)
        key = jnp.where(bits < 0, bits ^ jnp.int32(0x7FFFFFFF), bits)
        if lanes_are_queries:
            khi_scr[j] = lax.shift_right_arithmetic(key, jnp.int32(16)).astype(jnp.int16)
            klo_scr[j] = ((key & jnp.int32(0xFFFF)) + jnp.int32(INT16_MIN)).astype(jnp.int16)
            key_scr[j] = key.T
        else:
            key_scr[j] = key
        return carry

    lax.fori_loop(0, n_chunks, idx_chunk, 0)

    k_f = float(top_k)
    neg_key = jnp.int32(np.array(NEG_INF, np.float32).view(np.int32) ^ 0x7FFFFFFF)
    idx_bits = int(math.ceil(math.log2(key_scr.shape[0] * K_CHUNK)))
    per_query = (1, qt) if lanes_are_queries else (qt, 1)

    def bit_search(count_ge, n_bits, base):
        lowest = -(2 ** (n_bits - 1))

        def step(_, carry):
            v, bit = carry
            cand = v + bit
            return (jnp.where(base + count_ge(cand) >= k_f, cand, v), lax.shift_right_logical(bit, jnp.int32(1)))

        top = jnp.int32(lowest) if n_bits == 32 else jnp.int32(2 ** (n_bits - 1))
        v, _ = lax.fori_loop(0, n_bits, step, (jnp.full(per_query, lowest, jnp.int32), top))
        return v

    if lanes_are_queries:
        acc_rows = 4 * PACKED_SUBLANES

        def count16(scr, pred):
            def body(j, acc):
                hit = jnp.where(pred(j, scr[j]), jnp.int16(1), jnp.int16(0))
                hit = hit.reshape(K_CHUNK // acc_rows, acc_rows, qt)
                return acc + functools.reduce(jnp.add, [hit[t] for t in range(K_CHUNK // acc_rows)])

            acc = lax.fori_loop(0, n_chunks, body, jnp.zeros((acc_rows, qt), jnp.int16))
            return jnp.sum(acc.astype(F32), axis=0, keepdims=True)

        def digit(v):
            return jnp.broadcast_to(v, (acc_rows, qt)).astype(jnp.int16)

        def tiled(v16):
            return jnp.concatenate([v16] * (K_CHUNK // acc_rows), axis=0)

        zero = jnp.zeros(per_query, F32)
        thr_hi = bit_search(lambda cand: count16(khi_scr, lambda j, d, c=tiled(digit(cand)): d >= c), 16, zero)
        hi16 = tiled(digit(thr_hi))
        n_hi_gt = count16(khi_scr, lambda j, d: d > hi16)

        def keep_class(j, carry):
            klo_scr[j] = jnp.where(khi_scr[j] == hi16, klo_scr[j], jnp.int16(INT16_MIN))
            return carry

        lax.fori_loop(0, n_chunks, keep_class, 0)
        thr_lo = bit_search(lambda cand: count16(klo_scr, lambda j, d, c=tiled(digit(cand)): d >= c), 16, n_hi_gt)
        lo16 = tiled(digit(thr_lo))
        in_class = lambda j: khi_scr[j] == hi16
        n_gt = n_hi_gt + count16(klo_scr, lambda j, d: in_class(j) & (d > lo16))
        n_ge = n_hi_gt + count16(klo_scr, lambda j, d: in_class(j) & (d >= lo16))
        thr = thr_hi * jnp.int32(2 ** 16) + (thr_lo - jnp.int32(INT16_MIN))

        def count_tied_before(cand):
            return count16(klo_scr, lambda j, d: in_class(j) & (d == lo16) & ((j * K_CHUNK + s_col) < cand))
    else:
        def count32(pred):
            def body(j, acc):
                return acc + jnp.where(pred(j, key_scr[j]), 1.0, 0.0)

            acc = lax.fori_loop(0, n_chunks, body, jnp.zeros((qt, K_CHUNK), F32))
            return jnp.sum(acc, axis=1, keepdims=True)

        thr = bit_search(lambda cand: count32(lambda j, key: key >= cand), 32, jnp.zeros(per_query, F32))
        n_gt = count32(lambda j, key: key > thr)
        n_ge = count32(lambda j, key: key >= thr)

        def count_tied_before(cand):
            return count32(lambda j, key: (key == thr) & ((j * K_CHUNK + s_row) < cand))

    need = k_f - n_gt
    has_tie = jnp.max(jnp.where((n_ge > k_f) & (thr > neg_key), 1.0, 0.0)) > 0.0

    def tie_cut():
        def step(_, carry):
            cut, bit = carry
            cand = cut + bit
            return (jnp.where(count_tied_before(cand) < need, cand, cut), lax.shift_right_logical(bit, jnp.int32(1)))

        cut, _ = lax.fori_loop(0, idx_bits, step,
                               (jnp.zeros(per_query, jnp.int32), jnp.int32(2 ** (idx_bits - 1))))
        return cut

    cut = lax.cond(has_tie, tie_cut, lambda: jnp.full(per_query, 2 ** 30, jnp.int32))

    def to_rows(v):
        if not lanes_are_queries:
            return jnp.broadcast_to(v, (qt, K_CHUNK))
        t = jnp.broadcast_to(v, (qt, qt)).T
        return jnp.concatenate([t] * (K_CHUNK // qt), axis=1)

    thr_b, cut_b = to_rows(thr), to_rows(cut)

    def mask_chunk(j, carry):
        key = key_scr[j]
        s_pos = j * K_CHUNK + s_row
        keep = ((key > thr_b) | ((key == thr_b) & (s_pos <= cut_b))) & (s_pos <= t_col)
        mb_scr[j] = jnp.where(keep, 0.0, NEG_INF)
        return carry

    lax.fori_loop(0, n_chunks, mask_chunk, 0)

    rows_h = N_HEADS * qt
    lane_fold = lambda v, op: functools.reduce(op, [v[:, k * LANES:(k + 1) * LANES] for k in range(K_CHUNK // LANES)])

    def key_rows(j):
        return pl.ds(pl.multiple_of(j * K_CHUNK, K_CHUNK), K_CHUNK)

    def logit_chunk(j, carry):
        near = jnp.minimum((i * qt) // K_CHUNK - j, n_near - 1)
        s = _dot_nt(qa_scr[...], c_ref[key_rows(j), :]) * scale + nb_scr[near]
        s = s + jnp.concatenate([mb_scr[j]] * N_HEADS, axis=0)
        s_scr[j] = s
        m_scr[...] = jnp.maximum(m_scr[...], lane_fold(s, jnp.maximum))
        return carry

    m_scr[...] = jnp.full(m_scr.shape, NEG_INF, F32)
    lax.fori_loop(0, n_chunks, logit_chunk, 0)
    m_b = jnp.broadcast_to(jnp.max(m_scr[...], axis=-1, keepdims=True), (rows_h, LANES))
    m_scr[...] = m_b
    l_scr[...] = jnp.zeros(l_scr.shape, F32)
    acc_scr[...] = jnp.zeros(acc_scr.shape, F32)

    def value_chunk(j, carry):
        p = jnp.exp(s_scr[j] - jnp.concatenate([m_scr[...]] * (K_CHUNK // LANES), axis=1))
        l_scr[...] += lane_fold(p, jnp.add)
        acc_scr[...] += _dot(p.astype(BF16), c_ref[key_rows(j), :])
        return carry

    lax.fori_loop(0, n_chunks, value_chunk, 0)
    out = acc_scr[...] * (1.0 / jnp.sum(l_scr[...], axis=-1, keepdims=True))
    for h in range(N_HEADS):
        o_ref[:qt, h * KV_RANK:(h + 1) * KV_RANK] = out[h * qt:(h + 1) * qt].astype(o_ref.dtype)
    if o_ref.shape[0] > qt:
        o_ref[qt:, :] = jnp.zeros((o_ref.shape[0] - qt, o_ref.shape[1]), o_ref.dtype)


def _dsa_attention(qq, wi, kk, c_kv, wuk, rel_bias, top_k, seq):
    B, tp, _ = qq.shape
    n_main = (seq + N_META) // Q_TILE
    assert (seq + N_META) - n_main * Q_TILE <= TAIL_TILE and Q_TILE % K_CHUNK == 0 and K_CHUNK % TAIL_TILE == 0
    n_chunks_max = tp // K_CHUNK
    width = N_HEADS * KV_RANK

    def call(qt, first_tile, n_tiles, out_rows, prev):
        rows_h = N_HEADS * qt
        diag = _near_bias_diagonals(rel_bias, qt)
        out_tile0 = first_tile * qt // out_rows
        in_specs = [
            pl.BlockSpec((None, qt, qq.shape[2]), lambda b, i: (b, first_tile + i, 0)),
            pl.BlockSpec((None, qt, 128), lambda b, i: (b, first_tile + i, 0)),
            pl.BlockSpec((None, tp, LANES), lambda b, i: (b, 0, 0)),
            pl.BlockSpec((None, tp, KV_RANK), lambda b, i: (b, 0, 0)),
            pl.BlockSpec(wuk.shape, lambda b, i: (0, 0, 0)),
            pl.BlockSpec(diag.shape, lambda b, i: (0, 0, 0)),
        ]
        args = [qq, wi, kk, c_kv, wuk, diag]
        if prev is not None:
            in_specs.append(pl.BlockSpec(memory_space=pl.ANY))
            args.append(prev)
        return pl.pallas_call(
            functools.partial(_attn_kernel, top_k=top_k, qt=qt, first_tile=first_tile, aliased=prev is not None),
            grid=(B, n_tiles),
            in_specs=in_specs,
            out_specs=pl.BlockSpec((None, out_rows, width), lambda b, i: (b, out_tile0 + i, 0)),
            out_shape=jax.ShapeDtypeStruct((B, tp, width), BF16),
            input_output_aliases={} if prev is None else {len(args) - 1: 0},
            scratch_shapes=[pltpu.VMEM((diag.shape[0] + 1, rows_h, K_CHUNK), F32),
                            pltpu.VMEM((n_chunks_max, qt, K_CHUNK), jnp.int32),
                            pltpu.VMEM((n_chunks_max, K_CHUNK, qt), jnp.int16),
                            pltpu.VMEM((n_chunks_max, K_CHUNK, qt), jnp.int16),
                            pltpu.VMEM((n_chunks_max, qt, K_CHUNK), F32),
                            pltpu.VMEM((LANES, qt), F32),
                            pltpu.VMEM((rows_h, KV_RANK), BF16),
                            pltpu.VMEM((n_chunks_max, rows_h, K_CHUNK), F32),
                            pltpu.VMEM((rows_h, LANES), F32),
                            pltpu.VMEM((rows_h, LANES), F32),
                            pltpu.VMEM((rows_h, KV_RANK), F32)],
            compiler_params=_compiler_params(("arbitrary", "arbitrary")),
            name="dsa_attention" if prev is None else "dsa_attention_tail",
        )(*args)

    o_lat = call(Q_TILE, 0, n_main, Q_TILE, None)
    return call(TAIL_TILE, n_main * Q_TILE // TAIL_TILE, 1, tp - n_main * Q_TILE, o_lat)


def _mix_window_start(r, seq):
    return min(max(r * MIX_ROWS - N_META, 0), seq - MIX_ROWS)


def _residual_rows(x_ref, meta_ref, h_scr, seq):
    r = pl.program_id(1)
    n_real = -(-(seq + N_META) // MIX_ROWS)
    for rv in range(n_real):

        @pl.when(r == rv)
        def _(rv=rv):
            skip = rv * MIX_ROWS - N_META - _mix_window_start(rv, seq)
            if rv == 0:
                h_scr[:N_META, :] = meta_ref[...]
                h_scr[N_META:, :] = x_ref[:MIX_ROWS - N_META, :]
            elif skip == 0:
                h_scr[...] = x_ref[...]
            else:
                h_scr[:MIX_ROWS - skip, :] = x_ref[skip:, :]
                h_scr[MIX_ROWS - skip:, :] = jnp.zeros((skip, h_scr.shape[1]), F32)

    @pl.when(r >= n_real)
    def _():
        h_scr[...] = jnp.zeros(h_scr.shape, F32)


def _mix_kernel(ol_ref, pd_ref, x_ref, meta_ref, wuv_ref, wp_ref, ps_ref, wo_ref, g_ref, b_ref, h1_ref, h1b_ref,
                h_scr, *, seq):
    _residual_rows(x_ref, meta_ref, h_scr, seq)
    attn = [_dot(ol_ref[:, h * KV_RANK:(h + 1) * KV_RANK], wuv_ref[h]) for h in range(N_HEADS)]
    pool = [_dot(pd_ref[:, g * POOL_GROUP:(g + 1) * POOL_GROUP], wp_ref[g]) for g in range(len(POOL_WINDOWS))]
    pool = jnp.concatenate(pool, axis=-1) * ps_ref[...]
    cat = jnp.concatenate(attn + [pool], axis=-1).astype(BF16)
    y = ALPHA * h_scr[...] + _dot(cat, wo_ref[...])
    h1 = _layer_norm(y, g_ref[...], b_ref[...])
    h1_ref[...] = h1
    h1b_ref[...] = h1.astype(h1b_ref.dtype)


def _mix_ln1(o_lat, pool_diff, x, meta, wuv, wpool, pool_scale, wo, g, b):
    B, tp, _ = o_lat.shape
    seq, d = x.shape[1:]
    row = lambda n: pl.BlockSpec((None, MIX_ROWS, n), lambda bi, r: (bi, r, 0))
    full = lambda a: pl.BlockSpec(a.shape, lambda bi, r: (0,) * a.ndim)
    window = pl.BlockSpec(
        (None, pl.Element(MIX_ROWS), pl.Element(d)),
        lambda bi, r: (bi, pl.multiple_of(jnp.clip(r * MIX_ROWS - N_META, 0, seq - MIX_ROWS), N_META), 0))
    return pl.pallas_call(
        functools.partial(_mix_kernel, seq=seq),
        grid=(B, tp // MIX_ROWS),
        in_specs=[row(o_lat.shape[2]), row(pool_diff.shape[2]), window, full(meta),
                  full(wuv), full(wpool), full(pool_scale), full(wo), full(g), full(b)],
        out_specs=[row(d), row(d)],
        out_shape=[jax.ShapeDtypeStruct((B, tp, d), F32), jax.ShapeDtypeStruct((B, tp, d), BF16)],
        scratch_shapes=[pltpu.VMEM((MIX_ROWS, d), F32)],
        compiler_params=_compiler_params(("parallel", "arbitrary")),
        name="mix_ln1",
    )(o_lat, pool_diff, x, meta, wuv, wpool, pool_scale, wo, g, b)


def _gelu_tanh(x):
    return 0.5 * x * (1.0 + jnp.tanh(math.sqrt(2.0 / math.pi) * (x + 0.044715 * (x * x * x))))


def _ffn_kernel(hw_ref, hres_ref, wa_ref, wg_ref, cwa_ref, cwg_ref, cba_ref, cbg_ref, wd_ref, g_ref, b_ref, o_ref,
                za_scr, zg_scr):
    c = pl.program_id(2)
    n_c = pl.num_programs(2) - 1

    def up(slot):
        x = hw_ref[...]
        za_scr[slot] = _dot(x, wa_ref[...])
        zg_scr[slot] = _dot(x, wg_ref[...])

    def conv(z, cw_ref, cb_ref):
        cw = cw_ref[...]
        n = z.shape[0]
        out = z[HALO - 2:n - 2] * cw[0:1] + z[HALO - 1:n - 1] * cw[1:2] + z[HALO:] * cw[2:3]
        return out + cb_ref[...]

    def down(slot):
        a = conv(za_scr[slot], cwa_ref, cba_ref)
        gate = conv(zg_scr[slot], cwg_ref, cbg_ref)
        act = (_gelu_tanh(a) * gate).astype(BF16)
        o_ref[...] += _dot(act, wd_ref[...])

    @pl.when(c == 0)
    def _():
        o_ref[...] = jnp.zeros(o_ref.shape, o_ref.dtype)
        up(0)

    @pl.when((c > 0) & (c < n_c))
    def _():
        up(c % 2)
        down((c - 1) % 2)

    @pl.when(c == n_c)
    def _():
        down((c - 1) % 2)
        o_ref[...] = _layer_norm(ALPHA * hres_ref[...] + o_ref[...], g_ref[...], b_ref[...])


def _ffn_ln2(h1, h1b, w_up, conv_w, conv_b, w_down, g, b, seq):
    B, tp, d = h1.shape
    d_ff = w_down.shape[0]
    n_c = d_ff // FFN_COLS
    up_c = lambda c: jnp.minimum(c, n_c - 1)
    dn_c = lambda c: jnp.maximum(c - 1, 0)
    vec = lambda off: pl.BlockSpec((1, FFN_COLS), lambda bi, r, c: (0, dn_c(c) + off))
    return pl.pallas_call(
        _ffn_kernel,
        grid=(B, seq // FFN_ROWS, n_c + 1),
        in_specs=[
            pl.BlockSpec((None, pl.Element(FFN_ROWS + HALO), pl.Element(d)),
                         lambda bi, r, c: (bi, r * FFN_ROWS + N_META - HALO, 0)),
            pl.BlockSpec((None, pl.Element(FFN_ROWS), pl.Element(d)),
                         lambda bi, r, c: (bi, pl.multiple_of(r * FFN_ROWS + N_META, N_META), 0)),
            pl.BlockSpec((d, FFN_COLS), lambda bi, r, c: (0, up_c(c))),
            pl.BlockSpec((d, FFN_COLS), lambda bi, r, c: (0, up_c(c) + n_c)),
            pl.BlockSpec((CONV_WIDTH, FFN_COLS), lambda bi, r, c: (0, dn_c(c))),
            pl.BlockSpec((CONV_WIDTH, FFN_COLS), lambda bi, r, c: (0, dn_c(c) + n_c)),
            vec(0), vec(n_c),
            pl.BlockSpec((FFN_COLS, d), lambda bi, r, c: (dn_c(c), 0)),
            pl.BlockSpec((1, d), lambda bi, r, c: (0, 0)),
            pl.BlockSpec((1, d), lambda bi, r, c: (0, 0)),
        ],
        out_specs=pl.BlockSpec((None, FFN_ROWS, d), lambda bi, r, c: (bi, r, 0), pipeline_mode=pl.Buffered(1)),
        out_shape=jax.ShapeDtypeStruct((B, seq, d), F32),
        scratch_shapes=[pltpu.VMEM((2, FFN_ROWS + HALO, FFN_COLS), F32),
                        pltpu.VMEM((2, FFN_ROWS + HALO, FFN_COLS), F32)],
        compiler_params=_compiler_params(("parallel", "parallel", "arbitrary"), FFN_VMEM_LIMIT_BYTES),
        name="ffn_ln2",
    )(h1b, h1, w_up, w_up, conv_w, conv_w, conv_b, conv_b, w_down, g, b)


def _t5_bucket_table(n):
    dist = np.arange(n, dtype=np.int32)
    max_exact = REL_BUCKETS // 2
    d_f = np.maximum(dist, 1).astype(np.float32)
    large = max_exact + (np.log(d_f / np.float32(max_exact)) / np.float32(math.log(REL_MAX_DIST / max_exact))
                         * np.float32(REL_BUCKETS - max_exact)).astype(np.int32)
    return np.where(dist < max_exact, dist, np.minimum(large, REL_BUCKETS - 1))


def _near_bias_diagonals(rel_bias, qt):
    assert qt <= K_CHUNK
    probe = _t5_bucket_table(4 * REL_MAX_DIST)
    first_far = int(np.argmax(probe == REL_BUCKETS - 1))
    assert np.all(probe[first_far:] == REL_BUCKETS - 1)
    n_real = -(-(first_far + K_CHUNK - 1) // K_CHUNK)
    buckets = _t5_bucket_table((n_real + 1) * K_CHUNK)
    period = 2 * K_CHUNK
    u = np.arange(period)
    k = np.arange(n_real)[:, None]
    dist = np.where(u < K_CHUNK, k * K_CHUNK - u, k * K_CHUNK + period - u)
    idx = buckets[np.clip(dist, 0, len(buckets) - 1)]
    rel = rel_bias.astype(F32) - rel_bias[REL_BUCKETS - 1:].astype(F32)
    return jnp.transpose(rel[idx], (0, 2, 1))


def kernel(x, meta, rel_bias, w_in, kv_norm_g, w_uk, w_uv, w_pool, pool_scale, w_o, ln1_g, ln1_b, w_up, conv_w,
           conv_b, w_down, ln2_g, ln2_b):
    B, S, D = x.shape
    assert w_in.shape[0] == DEPTH and S % FFN_ROWS == 0
    T = S + N_META
    tp = -(-T // ROW_ALIGN) * ROW_ALIGN
    assert tp - T >= max(POOL_WINDOWS)
    top_k = min(TOPK_MAX, S // 4)
    assert top_k <= K_CHUNK

    attn_w = N_HEADS * HEAD_DIM
    idx_w = IDX_HEADS * IDX_DIM
    o_c, o_qi = attn_w, attn_w + KV_RANK
    o_ki = o_qi + idx_w
    o_wi = o_ki + IDX_DIM
    o_u = o_wi + IDX_HEADS
    w = w_in[0]
    w_qq = jnp.concatenate([w[:, :o_c], w[:, o_qi:o_ki] * (IDX_DIM ** -0.5)], axis=1).astype(BF16)
    w_u = w[:, o_u:].astype(BF16)
    w_small = jnp.concatenate([w[:, o_c:o_qi], w[:, o_ki:o_wi], w[:, o_ki:o_wi],
                               w[:, o_wi:o_u] * (IDX_HEADS ** -0.5), jnp.zeros((D, 128 - IDX_HEADS), w.dtype)],
                              axis=1).astype(BF16)

    qq = _proj_qq(x, meta, w_qq, tp)
    pool_diff = _proj_pool(x, meta, w_u, tp)
    c_kv, kk, wi = _proj_small(x, meta, w_small, kv_norm_g[0].reshape(1, KV_RANK), tp)

    wuk = jnp.transpose(w_uk[0], (1, 2, 0)).astype(BF16)
    o_lat = _dsa_attention(qq, wi, kk, c_kv, wuk, rel_bias, top_k, S)

    wuv = jnp.transpose(w_uv[0], (1, 0, 2)).astype(BF16)
    h1, h1b = _mix_ln1(o_lat, pool_diff, x, meta, wuv, w_pool[0].astype(BF16), pool_scale[0].reshape(1, -1),
                       w_o[0].astype(BF16), ln1_g[0].reshape(1, D), ln1_b[0].reshape(1, D))

    return _ffn_ln2(h1, h1b, w_up[0].astype(BF16), conv_w[0], conv_b[0].reshape(1, -1), w_down[0].astype(BF16),
                    ln2_g[0].reshape(1, D), ln2_b[0].reshape(1, D), S)
```

```python
import functools
import math

import numpy as np
import jax
import jax.numpy as jnp
from jax import lax
from jax.experimental import pallas as pl
from jax.experimental.pallas import tpu as pltpu

F32 = jnp.float32
BF16 = jnp.bfloat16

N_META = 16
N_HEADS = 8
HEAD_DIM = 128
KV_RANK = 256
IDX_HEADS = 16
IDX_DIM = 64
TOPK_MAX = 256
POOL_WINDOWS = (2, 4, 8, 16)
POOL_GROUP = 256
CONV_WIDTH = 3
REL_BUCKETS = 32
REL_MAX_DIST = 128
DEPTH = 1
ALPHA = (2.0 * DEPTH) ** 0.25
LN_EPS = 1e-5
NEG_INF = -1e30

VMEM_LIMIT_BYTES = 56 * 1024 * 1024
BIG_VMEM_LIMIT_BYTES = 60 * 1024 * 1024
SUBLANES = 8
LANES = 128
ROW_ALIGN = 256
PROJ_ROWS = 1024
Q_TILE = 256
TAIL_TILE = 16
K_CHUNK = 256
MIX_ROWS = 688
FFN_ROWS = 1024
FFN_COLS = 512
HALO = 16
INT_MIN = -(2 ** 31)
INT16_MIN = -(2 ** 15)
PACKED_SUBLANES = 16


def _dot(a, b):
    return jnp.dot(a, b, preferred_element_type=F32)


def _dot_nt(a, b):
    return lax.dot_general(a, b, (((1,), (1,)), ((), ())), preferred_element_type=F32)


def _layer_norm(y, g, b):
    mu = jnp.mean(y, axis=-1, keepdims=True)
    yc = y - mu
    var = jnp.mean(yc * yc, axis=-1, keepdims=True)
    return yc * lax.rsqrt(var + LN_EPS) * g + b


def _proj_rows(x_ref, meta_ref, w_ref, tp, store):
    seq = x_ref.shape[0]
    store(pl.ds(0, N_META), _dot(meta_ref[...].astype(BF16), w_ref[...]))
    for r in range(seq // PROJ_ROWS):
        acc = _dot(x_ref[r * PROJ_ROWS:(r + 1) * PROJ_ROWS, :].astype(BF16), w_ref[...])
        store(pl.ds(N_META + r * PROJ_ROWS, PROJ_ROWS), acc)
    n_pad = tp - seq - N_META
    store(pl.ds(seq + N_META, n_pad), jnp.zeros((n_pad, w_ref.shape[1]), F32))


def _proj_cast_kernel(x_ref, meta_ref, w_ref, o_ref):
    def store(rows, acc):
        o_ref[rows, :] = acc.astype(o_ref.dtype)

    _proj_rows(x_ref, meta_ref, w_ref, o_ref.shape[0], store)


def _proj_pool_kernel(x_ref, meta_ref, w_ref, o_ref, u_scr):
    def store(rows, acc):
        u_scr[rows, :] = acc

    _proj_rows(x_ref, meta_ref, w_ref, u_scr.shape[0], store)
    group = pl.program_id(1)
    tp = u_scr.shape[0]
    pos = lax.broadcasted_iota(jnp.int32, (tp, 1), 0)
    for g, window in enumerate(POOL_WINDOWS):

        @pl.when(group == g)
        def _():
            u = u_scr[...]
            s = u
            shift = 1
            while shift < window:
                s = s + pltpu.roll(s, shift, axis=0)
                shift *= 2
            count = jnp.minimum(pos + 1, window).astype(F32)
            o_ref[...] = (s / count - u).astype(o_ref.dtype)


def _proj_small_kernel(x_ref, meta_ref, w_ref, g_ref, c_ref, kk_ref, wi_ref):
    def store(rows, acc):
        c = acc[:, :KV_RANK]
        ms = jnp.mean(c * c, axis=-1, keepdims=True)
        c_ref[rows, :] = (c * lax.rsqrt(ms + LN_EPS) * g_ref[...]).astype(c_ref.dtype)
        kk_ref[rows, :] = acc[:, KV_RANK:KV_RANK + LANES].astype(kk_ref.dtype)
        wi_ref[rows, :] = acc[:, KV_RANK + LANES:]

    _proj_rows(x_ref, meta_ref, w_ref, c_ref.shape[0], store)


def _batch_block(tp, n):
    return pl.BlockSpec((None, tp, n), lambda b, j: (b, 0, 0))


def _compiler_params(semantics, vmem_limit_bytes=VMEM_LIMIT_BYTES):
    return pltpu.CompilerParams(dimension_semantics=semantics, vmem_limit_bytes=vmem_limit_bytes)


def _proj_call(body, x, meta, w, tn, extra_in, extra_specs, out_blocks, out_shapes, scratch, name):
    B, seq, d = x.shape
    return pl.pallas_call(
        body,
        grid=(B, w.shape[1] // tn),
        in_specs=[_batch_block(seq, d), pl.BlockSpec(meta.shape, lambda b, j: (0, 0)),
                  pl.BlockSpec((d, tn), lambda b, j: (0, j))] + extra_specs,
        out_specs=out_blocks,
        out_shape=out_shapes,
        scratch_shapes=scratch,
        compiler_params=_compiler_params(("parallel", "arbitrary")),
        name=name,
    )(x, meta, w, *extra_in)


def _proj_qq(x, meta, w, tp):
    B, n, tn = x.shape[0], w.shape[1], 512
    return _proj_call(_proj_cast_kernel, x, meta, w, tn, [], [],
                      pl.BlockSpec((None, tp, tn), lambda b, j: (b, 0, j)),
                      jax.ShapeDtypeStruct((B, tp, n), BF16), [], "proj_qq")


def _proj_pool(x, meta, w, tp):
    B, n = x.shape[0], w.shape[1]
    return _proj_call(_proj_pool_kernel, x, meta, w, POOL_GROUP, [], [],
                      pl.BlockSpec((None, tp, POOL_GROUP), lambda b, j: (b, 0, j)),
                      jax.ShapeDtypeStruct((B, tp, n), BF16), [pltpu.VMEM((tp, POOL_GROUP), F32)], "proj_pool")


def _proj_small(x, meta, w, kv_g, tp):
    B = x.shape[0]
    return _proj_call(_proj_small_kernel, x, meta, w, w.shape[1], [kv_g],
                      [pl.BlockSpec((1, KV_RANK), lambda b, j: (0, 0))],
                      [_batch_block(tp, KV_RANK), _batch_block(tp, LANES), _batch_block(tp, 128)],
                      [jax.ShapeDtypeStruct((B, tp, KV_RANK), BF16), jax.ShapeDtypeStruct((B, tp, LANES), BF16),
                       jax.ShapeDtypeStruct((B, tp, 128), F32)], [], "proj_small")


def _attn_kernel(qq_ref, wi_ref, kk_ref, c_ref, wuk_ref, diag_ref, *rest, top_k, qt, first_tile, aliased):
    o_ref, nb_scr, key_scr, khi_scr, klo_scr, mb_scr, wt_scr, qa_scr, s_scr, m_scr, l_scr, acc_scr = rest[int(aliased):]
    i = first_tile + pl.program_id(1)
    n_chunks = ((i + 1) * qt - 1) // K_CHUNK + 1
    attn_w = N_HEADS * HEAD_DIM
    scale = HEAD_DIM ** -0.5
    n_pairs = IDX_HEADS // 2
    lanes_are_queries = qt % LANES == 0

    t_col = i * qt + lax.broadcasted_iota(jnp.int32, (qt, 1), 0)
    t_row = i * qt + lax.broadcasted_iota(jnp.int32, (1, qt), 1)
    s_row = lax.broadcasted_iota(jnp.int32, (1, K_CHUNK), 1)
    s_col = lax.broadcasted_iota(jnp.int32, (K_CHUNK, 1), 0)
    lane_half = lax.broadcasted_iota(jnp.int32, (K_CHUNK, LANES), 1) // IDX_DIM

    n_near = nb_scr.shape[0]

    @pl.when((pl.program_id(0) == 0) & (pl.program_id(1) == 0))
    def _():
        for k in range(n_near - 1):
            for h in range(N_HEADS):
                v = jnp.broadcast_to(diag_ref[k, h:h + 1, :], (qt, 2 * K_CHUNK))
                t = pltpu.roll(v, 0, 1, stride=1, stride_axis=0)
                nb_scr[k, h * qt:(h + 1) * qt, :] = t[:, :K_CHUNK]
        nb_scr[n_near - 1] = jnp.zeros(nb_scr.shape[1:], F32)

    for h in range(N_HEADS):
        qa_scr[h * qt:(h + 1) * qt, :] = _dot(
            qq_ref[:, h * HEAD_DIM:(h + 1) * HEAD_DIM], wuk_ref[h]).astype(BF16)
    if lanes_are_queries:
        wt_scr[...] = wi_ref[...].T

    def idx_chunk(j, carry):
        ks = kk_ref[pl.ds(pl.multiple_of(j * K_CHUNK, K_CHUNK), K_CHUNK), :]
        zero = jnp.zeros(ks.shape, ks.dtype)
        k_even = jnp.where(lane_half == 0, ks, zero)
        k_odd = jnp.where(lane_half == 1, ks, zero)
        score = jnp.zeros((K_CHUNK, qt) if lanes_are_queries else (qt, K_CHUNK), F32)
        for p in range(n_pairs):
            q_pair = qq_ref[:, attn_w + p * LANES:attn_w + (p + 1) * LANES]
            for hh, k_half in ((2 * p, k_even), (2 * p + 1, k_odd)):
                if lanes_are_queries:
                    score = score + jnp.maximum(_dot_nt(k_half, q_pair), 0.0) * wt_scr[hh:hh + 1, :]
                else:
                    score = score + jnp.maximum(_dot_nt(q_pair, k_half), 0.0) * wi_ref[:, hh:hh + 1]
        s_pos = j * K_CHUNK + (s_col if lanes_are_queries else s_row)
        score = jnp.where(s_pos <= (t_row if lanes_are_queries else t_col), score, NEG_INF)
        bits = lax.bitcast_convert_type(score, jnp.int32)
        key = jnp.where(bits < 0, bits ^ jnp.int32(0x7FFFFFFF), bits)
        if lanes_are_queries:
            khi_scr[j] = lax.shift_right_arithmetic(key, jnp.int32(16)).astype(jnp.int16)
            klo_scr[j] = ((key & jnp.int32(0xFFFF)) + jnp.int32(INT16_MIN)).astype(jnp.int16)
            key_scr[j] = key.T
        else:
            key_scr[j] = key
        return carry

    lax.fori_loop(0, n_chunks, idx_chunk, 0)

    k_f = float(top_k)
    neg_key = jnp.int32(np.array(NEG_INF, np.float32).view(np.int32) ^ 0x7FFFFFFF)
    idx_bits = int(math.ceil(math.log2(key_scr.shape[0] * K_CHUNK)))
    per_query = (1, qt) if lanes_are_queries else (qt, 1)

    def bit_search(count_ge, n_bits, base):
        lowest = -(2 ** (n_bits - 1))

        def step(_, carry):
            v, bit = carry
            cand = v + bit
            return (jnp.where(base + count_ge(cand) >= k_f, cand, v), lax.shift_right_logical(bit, jnp.int32(1)))

        top = jnp.int32(lowest) if n_bits == 32 else jnp.int32(2 ** (n_bits - 1))
        v, _ = lax.fori_loop(0, n_bits, step, (jnp.full(per_query, lowest, jnp.int32), top))
        return v

    if lanes_are_queries:
        acc_rows = 4 * PACKED_SUBLANES

        def count16(scr, pred):
            def body(j, acc):
                hit = jnp.where(pred(j, scr[j]), jnp.int16(1), jnp.int16(0))
                hit = hit.reshape(K_CHUNK // acc_rows, acc_rows, qt)
                return acc + functools.reduce(jnp.add, [hit[t] for t in range(K_CHUNK // acc_rows)])

            acc = lax.fori_loop(0, n_chunks, body, jnp.zeros((acc_rows, qt), jnp.int16))
            return jnp.sum(acc.astype(F32), axis=0, keepdims=True)

        def digit(v):
            return jnp.broadcast_to(v, (acc_rows, qt)).astype(jnp.int16)

        def tiled(v16):
            return jnp.concatenate([v16] * (K_CHUNK // acc_rows), axis=0)

        zero = jnp.zeros(per_query, F32)
        thr_hi = bit_search(lambda cand: count16(khi_scr, lambda j, d, c=tiled(digit(cand)): d >= c), 16, zero)
        hi16 = tiled(digit(thr_hi))
        n_hi_gt = count16(khi_scr, lambda j, d: d > hi16)

        def keep_class(j, carry):
            klo_scr[j] = jnp.where(khi_scr[j] == hi16, klo_scr[j], jnp.int16(INT16_MIN))
            return carry

        lax.fori_loop(0, n_chunks, keep_class, 0)
        thr_lo = bit_search(lambda cand: count16(klo_scr, lambda j, d, c=tiled(digit(cand)): d >= c), 16, n_hi_gt)
        lo16 = tiled(digit(thr_lo))
        in_class = lambda j: khi_scr[j] == hi16
        n_gt = n_hi_gt + count16(klo_scr, lambda j, d: in_class(j) & (d > lo16))
        n_ge = n_hi_gt + count16(klo_scr, lambda j, d: in_class(j) & (d >= lo16))
        thr = thr_hi * jnp.int32(2 ** 16) + (thr_lo - jnp.int32(INT16_MIN))

        def count_tied_before(cand):
            return count16(klo_scr, lambda j, d: in_class(j) & (d == lo16) & ((j * K_CHUNK + s_col) < cand))
    else:
        def count32(pred):
            def body(j, acc):
                return acc + jnp.where(pred(j, key_scr[j]), 1.0, 0.0)

            acc = lax.fori_loop(0, n_chunks, body, jnp.zeros((qt, K_CHUNK), F32))
            return jnp.sum(acc, axis=1, keepdims=True)

        thr = bit_search(lambda cand: count32(lambda j, key: key >= cand), 32, jnp.zeros(per_query, F32))
        n_gt = count32(lambda j, key: key > thr)
        n_ge = count32(lambda j, key: key >= thr)

        def count_tied_before(cand):
            return count32(lambda j, key: (key == thr) & ((j * K_CHUNK + s_row) < cand))

    need = k_f - n_gt
    has_tie = jnp.max(jnp.where((n_ge > k_f) & (thr > neg_key), 1.0, 0.0)) > 0.0

    def tie_cut():
        def step(_, carry):
            cut, bit = carry
            cand = cut + bit
            return (jnp.where(count_tied_before(cand) < need, cand, cut), lax.shift_right_logical(bit, jnp.int32(1)))

        cut, _ = lax.fori_loop(0, idx_bits, step,
                               (jnp.zeros(per_query, jnp.int32), jnp.int32(2 ** (idx_bits - 1))))
        return cut

    cut = lax.cond(has_tie, tie_cut, lambda: jnp.full(per_query, 2 ** 30, jnp.int32))

    def to_rows(v):
        if not lanes_are_queries:
            return jnp.broadcast_to(v, (qt, K_CHUNK))
        t = jnp.broadcast_to(v, (qt, qt)).T
        return jnp.concatenate([t] * (K_CHUNK // qt), axis=1)

    thr_b, cut_b = to_rows(thr), to_rows(cut)

    def mask_chunk(j, carry):
        key = key_scr[j]
        s_pos = j * K_CHUNK + s_row
        keep = ((key > thr_b) | ((key == thr_b) & (s_pos <= cut_b))) & (s_pos <= t_col)
        mb_scr[j] = jnp.where(keep, 0.0, NEG_INF)
        return carry

    lax.fori_loop(0, n_chunks, mask_chunk, 0)

    rows_h = N_HEADS * qt
    lane_fold = lambda v, op: functools.reduce(op, [v[:, k * LANES:(k + 1) * LANES] for k in range(K_CHUNK // LANES)])

    def key_rows(j):
        start = j * K_CHUNK
        return pl.ds(start if isinstance(j, int) else pl.multiple_of(start, K_CHUNK), K_CHUNK)

    def over_chunks(chunk_fn):
        chunk_fn(0, True)
        lax.fori_loop(1, n_chunks, lambda j, carry: (chunk_fn(j, False), carry)[1], 0)

    def logit_chunk(j, first):
        near = jnp.minimum((i * qt) // K_CHUNK - j, n_near - 1)
        s = _dot_nt(qa_scr[...], c_ref[key_rows(j), :]) * scale + nb_scr[near]
        s = s + jnp.concatenate([mb_scr[j]] * N_HEADS, axis=0)
        s_scr[j] = s
        fold = lane_fold(s, jnp.maximum)
        m_scr[...] = fold if first else jnp.maximum(m_scr[...], fold)

    over_chunks(logit_chunk)
    m_b = jnp.broadcast_to(jnp.max(m_scr[...], axis=-1, keepdims=True), (rows_h, LANES))
    m_scr[...] = m_b

    def value_chunk(j, first):
        p = jnp.exp(s_scr[j] - jnp.concatenate([m_scr[...]] * (K_CHUNK // LANES), axis=1))
        pv = _dot(p.astype(BF16), c_ref[key_rows(j), :])
        l_scr[...] = lane_fold(p, jnp.add) if first else l_scr[...] + lane_fold(p, jnp.add)
        acc_scr[...] = pv if first else acc_scr[...] + pv

    over_chunks(value_chunk)
    out = acc_scr[...] * (1.0 / jnp.sum(l_scr[...], axis=-1, keepdims=True))
    for h in range(N_HEADS):
        o_ref[:qt, h * KV_RANK:(h + 1) * KV_RANK] = out[h * qt:(h + 1) * qt].astype(o_ref.dtype)
    if o_ref.shape[0] > qt:
        o_ref[qt:, :] = jnp.zeros((o_ref.shape[0] - qt, o_ref.shape[1]), o_ref.dtype)


def _dsa_attention(qq, wi, kk, c_kv, wuk, rel_bias, top_k, seq):
    B, tp, _ = qq.shape
    n_main = (seq + N_META) // Q_TILE
    assert (seq + N_META) - n_main * Q_TILE <= TAIL_TILE and Q_TILE % K_CHUNK == 0 and K_CHUNK % TAIL_TILE == 0
    n_chunks_max = tp // K_CHUNK
    width = N_HEADS * KV_RANK

    def call(qt, first_tile, n_tiles, out_rows, prev):
        rows_h = N_HEADS * qt
        diag = _near_bias_diagonals(rel_bias, qt)
        out_tile0 = first_tile * qt // out_rows
        in_specs = [
            pl.BlockSpec((None, qt, qq.shape[2]), lambda b, i: (b, first_tile + i, 0)),
            pl.BlockSpec((None, qt, 128), lambda b, i: (b, first_tile + i, 0)),
            pl.BlockSpec((None, tp, LANES), lambda b, i: (b, 0, 0)),
            pl.BlockSpec((None, tp, KV_RANK), lambda b, i: (b, 0, 0)),
            pl.BlockSpec(wuk.shape, lambda b, i: (0, 0, 0)),
            pl.BlockSpec(diag.shape, lambda b, i: (0, 0, 0)),
        ]
        args = [qq, wi, kk, c_kv, wuk, diag]
        if prev is not None:
            in_specs.append(pl.BlockSpec(memory_space=pl.ANY))
            args.append(prev)
        return pl.pallas_call(
            functools.partial(_attn_kernel, top_k=top_k, qt=qt, first_tile=first_tile, aliased=prev is not None),
            grid=(B, n_tiles),
            in_specs=in_specs,
            out_specs=pl.BlockSpec((None, out_rows, width), lambda b, i: (b, out_tile0 + i, 0)),
            out_shape=jax.ShapeDtypeStruct((B, tp, width), BF16),
            input_output_aliases={} if prev is None else {len(args) - 1: 0},
            scratch_shapes=[pltpu.VMEM((diag.shape[0] + 1, rows_h, K_CHUNK), F32),
                            pltpu.VMEM((n_chunks_max, qt, K_CHUNK), jnp.int32),
                            pltpu.VMEM((n_chunks_max, K_CHUNK, qt), jnp.int16),
                            pltpu.VMEM((n_chunks_max, K_CHUNK, qt), jnp.int16),
                            pltpu.VMEM((n_chunks_max, qt, K_CHUNK), F32),
                            pltpu.VMEM((LANES, qt), F32),
                            pltpu.VMEM((rows_h, KV_RANK), BF16),
                            pltpu.VMEM((n_chunks_max, rows_h, K_CHUNK), F32),
                            pltpu.VMEM((rows_h, LANES), F32),
                            pltpu.VMEM((rows_h, LANES), F32),
                            pltpu.VMEM((rows_h, KV_RANK), F32)],
            compiler_params=_compiler_params(("arbitrary", "arbitrary")),
            name="dsa_attention" if prev is None else "dsa_attention_tail",
        )(*args)

    o_lat = call(Q_TILE, 0, n_main, Q_TILE, None)
    return call(TAIL_TILE, n_main * Q_TILE // TAIL_TILE, 1, tp - n_main * Q_TILE, o_lat)


def _mix_window_start(r, seq):
    return min(max(r * MIX_ROWS - N_META, 0), seq - MIX_ROWS)


def _residual_rows(x_ref, meta_ref, h_scr, seq):
    r = pl.program_id(1)
    n_real = -(-(seq + N_META) // MIX_ROWS)
    for rv in range(n_real):

        @pl.when(r == rv)
        def _(rv=rv):
            skip = rv * MIX_ROWS - N_META - _mix_window_start(rv, seq)
            if rv == 0:
                h_scr[:N_META, :] = meta_ref[...]
                h_scr[N_META:, :] = x_ref[:MIX_ROWS - N_META, :]
            elif skip == 0:
                h_scr[...] = x_ref[...]
            else:
                h_scr[:MIX_ROWS - skip, :] = x_ref[skip:, :]
                h_scr[MIX_ROWS - skip:, :] = jnp.zeros((skip, h_scr.shape[1]), F32)

    @pl.when(r >= n_real)
    def _():
        h_scr[...] = jnp.zeros(h_scr.shape, F32)


def _mix_kernel(ol_ref, pd_ref, x_ref, meta_ref, wuv_ref, wp_ref, ps_ref, wo_ref, g_ref, b_ref, h1_ref, h1b_ref,
                h_scr, *, seq):
    _residual_rows(x_ref, meta_ref, h_scr, seq)
    attn = [_dot(ol_ref[:, h * KV_RANK:(h + 1) * KV_RANK], wuv_ref[h]) for h in range(N_HEADS)]
    pool = [_dot(pd_ref[:, g * POOL_GROUP:(g + 1) * POOL_GROUP], wp_ref[g]) for g in range(len(POOL_WINDOWS))]
    pool = jnp.concatenate(pool, axis=-1) * ps_ref[...]
    cat = jnp.concatenate(attn + [pool], axis=-1).astype(BF16)
    y = ALPHA * h_scr[...] + _dot(cat, wo_ref[...])
    h1 = _layer_norm(y, g_ref[...], b_ref[...])
    h1_ref[...] = h1
    h1b_ref[...] = h1.astype(h1b_ref.dtype)


def _mix_ln1(o_lat, pool_diff, x, meta, wuv, wpool, pool_scale, wo, g, b):
    B, tp, _ = o_lat.shape
    seq, d = x.shape[1:]
    row = lambda n: pl.BlockSpec((None, MIX_ROWS, n), lambda bi, r: (bi, r, 0))
    full = lambda a: pl.BlockSpec(a.shape, lambda bi, r: (0,) * a.ndim, pipeline_mode=pl.Buffered(1))
    window = pl.BlockSpec(
        (None, pl.Element(MIX_ROWS), pl.Element(d)),
        lambda bi, r: (bi, pl.multiple_of(jnp.clip(r * MIX_ROWS - N_META, 0, seq - MIX_ROWS), N_META), 0))
    return pl.pallas_call(
        functools.partial(_mix_kernel, seq=seq),
        grid=(B, -(-(seq + N_META) // MIX_ROWS)),
        in_specs=[row(o_lat.shape[2]), row(pool_diff.shape[2]), window, full(meta),
                  full(wuv), full(wpool), full(pool_scale), full(wo), full(g), full(b)],
        out_specs=[row(d), row(d)],
        out_shape=[jax.ShapeDtypeStruct((B, tp, d), F32), jax.ShapeDtypeStruct((B, tp, d), BF16)],
        scratch_shapes=[pltpu.VMEM((MIX_ROWS, d), F32)],
        compiler_params=_compiler_params(("parallel", "arbitrary"), BIG_VMEM_LIMIT_BYTES),
        name="mix_ln1",
    )(o_lat, pool_diff, x, meta, wuv, wpool, pool_scale, wo, g, b)


def _gelu_tanh(x):
    return 0.5 * x * (1.0 + jnp.tanh(math.sqrt(2.0 / math.pi) * (x + 0.044715 * (x * x * x))))


def _ffn_kernel(hw_ref, hres_ref, wa_ref, wg_ref, cwa_ref, cwg_ref, cba_ref, cbg_ref, wd_ref, g_ref, b_ref, o_ref,
                za_scr, zg_scr):
    c = pl.program_id(2)
    n_c = pl.num_programs(2) - 1

    def up(slot):
        x = hw_ref[...]
        za_scr[slot] = _dot(x, wa_ref[...])
        zg_scr[slot] = _dot(x, wg_ref[...])

    def conv(z, cw_ref, cb_ref):
        cw = cw_ref[...]
        n = z.shape[0]
        out = z[HALO - 2:n - 2] * cw[0:1] + z[HALO - 1:n - 1] * cw[1:2] + z[HALO:] * cw[2:3]
        return out + cb_ref[...]

    def down(slot):
        a = conv(za_scr[slot], cwa_ref, cba_ref)
        gate = conv(zg_scr[slot], cwg_ref, cbg_ref)
        act = (_gelu_tanh(a) * gate).astype(BF16)
        o_ref[...] += _dot(act, wd_ref[...])

    @pl.when(c == 0)
    def _():
        o_ref[...] = jnp.zeros(o_ref.shape, o_ref.dtype)
        up(0)

    @pl.when((c > 0) & (c < n_c))
    def _():
        up(c % 2)
        down((c - 1) % 2)

    @pl.when(c == n_c)
    def _():
        down((c - 1) % 2)
        o_ref[...] = _layer_norm(ALPHA * hres_ref[...] + o_ref[...], g_ref[...], b_ref[...])


def _ffn_ln2(h1, h1b, w_up, conv_w, conv_b, w_down, g, b, seq):
    B, tp, d = h1.shape
    d_ff = w_down.shape[0]
    n_c = d_ff // FFN_COLS
    up_c = lambda c: jnp.minimum(c, n_c - 1)
    dn_c = lambda c: jnp.maximum(c - 1, 0)
    vec = lambda off: pl.BlockSpec((1, FFN_COLS), lambda bi, r, c: (0, dn_c(c) + off))
    return pl.pallas_call(
        _ffn_kernel,
        grid=(B, seq // FFN_ROWS, n_c + 1),
        in_specs=[
            pl.BlockSpec((None, pl.Element(FFN_ROWS + HALO), pl.Element(d)),
                         lambda bi, r, c: (bi, r * FFN_ROWS + N_META - HALO, 0)),
            pl.BlockSpec((None, pl.Element(FFN_ROWS), pl.Element(d)),
                         lambda bi, r, c: (bi, pl.multiple_of(r * FFN_ROWS + N_META, N_META), 0)),
            pl.BlockSpec((d, FFN_COLS), lambda bi, r, c: (0, up_c(c))),
            pl.BlockSpec((d, FFN_COLS), lambda bi, r, c: (0, up_c(c) + n_c)),
            pl.BlockSpec((CONV_WIDTH, FFN_COLS), lambda bi, r, c: (0, dn_c(c))),
            pl.BlockSpec((CONV_WIDTH, FFN_COLS), lambda bi, r, c: (0, dn_c(c) + n_c)),
            vec(0), vec(n_c),
            pl.BlockSpec((FFN_COLS, d), lambda bi, r, c: (dn_c(c), 0)),
            pl.BlockSpec((1, d), lambda bi, r, c: (0, 0)),
            pl.BlockSpec((1, d), lambda bi, r, c: (0, 0)),
        ],
        out_specs=pl.BlockSpec((None, FFN_ROWS, d), lambda bi, r, c: (bi, r, 0), pipeline_mode=pl.Buffered(1)),
        out_shape=jax.ShapeDtypeStruct((B, seq, d), F32),
        scratch_shapes=[pltpu.VMEM((2, FFN_ROWS + HALO, FFN_COLS), F32),
                        pltpu.VMEM((2, FFN_ROWS + HALO, FFN_COLS), F32)],
        compiler_params=_compiler_params(("parallel", "parallel", "arbitrary"), BIG_VMEM_LIMIT_BYTES),
        name="ffn_ln2",
    )(h1b, h1, w_up, w_up, conv_w, conv_w, conv_b, conv_b, w_down, g, b)


def _t5_bucket_table(n):
    dist = np.arange(n, dtype=np.int32)
    max_exact = REL_BUCKETS // 2
    d_f = np.maximum(dist, 1).astype(np.float32)
    large = max_exact + (np.log(d_f / np.float32(max_exact)) / np.float32(math.log(REL_MAX_DIST / max_exact))
                         * np.float32(REL_BUCKETS - max_exact)).astype(np.int32)
    return np.where(dist < max_exact, dist, np.minimum(large, REL_BUCKETS - 1))


def _near_bias_diagonals(rel_bias, qt):
    assert qt <= K_CHUNK
    probe = _t5_bucket_table(4 * REL_MAX_DIST)
    first_far = int(np.argmax(probe == REL_BUCKETS - 1))
    assert np.all(probe[first_far:] == REL_BUCKETS - 1)
    n_real = -(-(first_far + K_CHUNK - 1) // K_CHUNK)
    buckets = _t5_bucket_table((n_real + 1) * K_CHUNK)
    period = 2 * K_CHUNK
    u = np.arange(period)
    k = np.arange(n_real)[:, None]
    dist = np.where(u < K_CHUNK, k * K_CHUNK - u, k * K_CHUNK + period - u)
    idx = buckets[np.clip(dist, 0, len(buckets) - 1)]
    rel = rel_bias.astype(F32) - rel_bias[REL_BUCKETS - 1:].astype(F32)
    return jnp.transpose(rel[idx], (0, 2, 1))


def kernel(x, meta, rel_bias, w_in, kv_norm_g, w_uk, w_uv, w_pool, pool_scale, w_o, ln1_g, ln1_b, w_up, conv_w,
           conv_b, w_down, ln2_g, ln2_b):
    B, S, D = x.shape
    assert w_in.shape[0] == DEPTH and S % FFN_ROWS == 0
    T = S + N_META
    tp = -(-T // ROW_ALIGN) * ROW_ALIGN
    assert tp - T >= max(POOL_WINDOWS)
    top_k = min(TOPK_MAX, S // 4)
    assert top_k <= K_CHUNK

    attn_w = N_HEADS * HEAD_DIM
    idx_w = IDX_HEADS * IDX_DIM
    o_c, o_qi = attn_w, attn_w + KV_RANK
    o_ki = o_qi + idx_w
    o_wi = o_ki + IDX_DIM
    o_u = o_wi + IDX_HEADS
    w = w_in[0]
    w_qq = jnp.concatenate([w[:, :o_c], w[:, o_qi:o_ki] * (IDX_DIM ** -0.5)], axis=1).astype(BF16)
    w_u = w[:, o_u:].astype(BF16)
    w_small = jnp.concatenate([w[:, o_c:o_qi], w[:, o_ki:o_wi], w[:, o_ki:o_wi],
                               w[:, o_wi:o_u] * (IDX_HEADS ** -0.5), jnp.zeros((D, 128 - IDX_HEADS), w.dtype)],
                              axis=1).astype(BF16)

    qq = _proj_qq(x, meta, w_qq, tp)
    pool_diff = _proj_pool(x, meta, w_u, tp)
    c_kv, kk, wi = _proj_small(x, meta, w_small, kv_norm_g[0].reshape(1, KV_RANK), tp)

    wuk = jnp.transpose(w_uk[0], (1, 2, 0)).astype(BF16)
    o_lat = _dsa_attention(qq, wi, kk, c_kv, wuk, rel_bias, top_k, S)

    wuv = jnp.transpose(w_uv[0], (1, 0, 2)).astype(BF16)
    h1, h1b = _mix_ln1(o_lat, pool_diff, x, meta, wuv, w_pool[0].astype(BF16), pool_scale[0].reshape(1, -1),
                       w_o[0].astype(BF16), ln1_g[0].reshape(1, D), ln1_b[0].reshape(1, D))

    return _ffn_ln2(h1, h1b, w_up[0].astype(BF16), conv_w[0], conv_b[0].reshape(1, -1), w_down[0].astype(BF16),
                    ln2_g[0].reshape(1, D), ln2_b[0].reshape(1, D), S)
```

```python
import functools
import math

import numpy as np
import jax
import jax.numpy as jnp
from jax import lax
from jax.experimental import pallas as pl
from jax.experimental.pallas import tpu as pltpu

F32 = jnp.float32
BF16 = jnp.bfloat16

N_META = 16
N_HEADS = 8
HEAD_DIM = 128
KV_RANK = 256
IDX_HEADS = 16
IDX_DIM = 64
TOPK_MAX = 256
POOL_WINDOWS = (2, 4, 8, 16)
POOL_GROUP = 256
CONV_WIDTH = 3
REL_BUCKETS = 32
REL_MAX_DIST = 128
DEPTH = 1
ALPHA = (2.0 * DEPTH) ** 0.25
LN_EPS = 1e-5
NEG_INF = -1e30

VMEM_LIMIT_BYTES = 56 * 1024 * 1024
BIG_VMEM_LIMIT_BYTES = 60 * 1024 * 1024
SUBLANES = 8
LANES = 128
ROW_ALIGN = 256
PROJ_ROWS = 1024
Q_TILE = 256
TAIL_TILE = 16
K_CHUNK = 256
MIX_ROWS = 688
FFN_ROWS = 1024
FFN_COLS = 512
HALO = 16
INT_MIN = -(2 ** 31)
TIE_ROUNDS = 8


def _dot(a, b):
    return jnp.dot(a, b, preferred_element_type=F32)


def _dot_nt(a, b):
    return lax.dot_general(a, b, (((1,), (1,)), ((), ())), preferred_element_type=F32)


def _layer_norm(y, g, b):
    mu = jnp.mean(y, axis=-1, keepdims=True)
    yc = y - mu
    var = jnp.mean(yc * yc, axis=-1, keepdims=True)
    return yc * lax.rsqrt(var + LN_EPS) * g + b


def _proj_rows(x_ref, meta_ref, w_ref, tp, store):
    seq = x_ref.shape[0]
    store(pl.ds(0, N_META), _dot(meta_ref[...].astype(BF16), w_ref[...]))
    for r in range(seq // PROJ_ROWS):
        acc = _dot(x_ref[r * PROJ_ROWS:(r + 1) * PROJ_ROWS, :].astype(BF16), w_ref[...])
        store(pl.ds(N_META + r * PROJ_ROWS, PROJ_ROWS), acc)
    n_pad = tp - seq - N_META
    store(pl.ds(seq + N_META, n_pad), jnp.zeros((n_pad, w_ref.shape[1]), F32))


def _proj_cast_kernel(x_ref, meta_ref, w_ref, o_ref):
    def store(rows, acc):
        o_ref[rows, :] = acc.astype(o_ref.dtype)

    _proj_rows(x_ref, meta_ref, w_ref, o_ref.shape[0], store)


def _proj_pool_kernel(x_ref, meta_ref, w_ref, o_ref, u_scr):
    def store(rows, acc):
        u_scr[rows, :] = acc

    _proj_rows(x_ref, meta_ref, w_ref, u_scr.shape[0], store)
    group = pl.program_id(1)
    tp = u_scr.shape[0]
    pos = lax.broadcasted_iota(jnp.int32, (tp, 1), 0)
    for g, window in enumerate(POOL_WINDOWS):

        @pl.when(group == g)
        def _():
            u = u_scr[...]
            s = u
            shift = 1
            while shift < window:
                s = s + pltpu.roll(s, shift, axis=0)
                shift *= 2
            count = jnp.minimum(pos + 1, window).astype(F32)
            o_ref[...] = (s / count - u).astype(o_ref.dtype)


def _proj_small_kernel(x_ref, meta_ref, w_ref, g_ref, c_ref, kk_ref, wi_ref):
    def store(rows, acc):
        c = acc[:, :KV_RANK]
        ms = jnp.mean(c * c, axis=-1, keepdims=True)
        c_ref[rows, :] = (c * lax.rsqrt(ms + LN_EPS) * g_ref[...]).astype(c_ref.dtype)
        kk_ref[rows, :] = acc[:, KV_RANK:KV_RANK + LANES].astype(kk_ref.dtype)
        wi_ref[rows, :] = acc[:, KV_RANK + LANES:]

    _proj_rows(x_ref, meta_ref, w_ref, c_ref.shape[0], store)


def _batch_block(tp, n):
    return pl.BlockSpec((None, tp, n), lambda b, j: (b, 0, 0))


def _compiler_params(semantics, vmem_limit_bytes=VMEM_LIMIT_BYTES):
    return pltpu.CompilerParams(dimension_semantics=semantics, vmem_limit_bytes=vmem_limit_bytes)


def _proj_call(body, x, meta, w, tn, extra_in, extra_specs, out_blocks, out_shapes, scratch, name):
    B, seq, d = x.shape
    return pl.pallas_call(
        body,
        grid=(B, w.shape[1] // tn),
        in_specs=[_batch_block(seq, d), pl.BlockSpec(meta.shape, lambda b, j: (0, 0)),
                  pl.BlockSpec((d, tn), lambda b, j: (0, j))] + extra_specs,
        out_specs=out_blocks,
        out_shape=out_shapes,
        scratch_shapes=scratch,
        compiler_params=_compiler_params(("parallel", "arbitrary")),
        name=name,
    )(x, meta, w, *extra_in)


def _proj_qq(x, meta, w, tp):
    B, n, tn = x.shape[0], w.shape[1], 512
    return _proj_call(_proj_cast_kernel, x, meta, w, tn, [], [],
                      pl.BlockSpec((None, tp, tn), lambda b, j: (b, 0, j)),
                      jax.ShapeDtypeStruct((B, tp, n), BF16), [], "proj_qq")


def _proj_pool(x, meta, w, tp):
    B, n = x.shape[0], w.shape[1]
    return _proj_call(_proj_pool_kernel, x, meta, w, POOL_GROUP, [], [],
                      pl.BlockSpec((None, tp, POOL_GROUP), lambda b, j: (b, 0, j)),
                      jax.ShapeDtypeStruct((B, tp, n), BF16), [pltpu.VMEM((tp, POOL_GROUP), F32)], "proj_pool")


def _proj_small(x, meta, w, kv_g, tp):
    B = x.shape[0]
    return _proj_call(_proj_small_kernel, x, meta, w, w.shape[1], [kv_g],
                      [pl.BlockSpec((1, KV_RANK), lambda b, j: (0, 0))],
                      [_batch_block(tp, KV_RANK), _batch_block(tp, LANES), _batch_block(tp, 128)],
                      [jax.ShapeDtypeStruct((B, tp, KV_RANK), BF16), jax.ShapeDtypeStruct((B, tp, LANES), BF16),
                       jax.ShapeDtypeStruct((B, tp, 128), F32)], [], "proj_small")


def _attn_kernel(*refs, n_tiles, aliased, **static):
    refs = refs[:6] + refs[6 + int(aliased):]
    o_ref = refs[6]

    @pl.when(pl.program_id(1) < n_tiles)
    def _():
        _attn_tile(*refs, **static)

    @pl.when(pl.program_id(1) >= n_tiles)
    def _():
        o_ref[...] = jnp.zeros(o_ref.shape, o_ref.dtype)


def _attn_tile(qq_ref, wi_ref, kk_ref, c_ref, wuk_ref, diag_ref, o_ref, nb_scr, sc_scr, sct_scr, mb_scr,
               wt_scr, qa_scr, s_scr, m_scr, l_scr, acc_scr, *, top_k, qt, first_tile):
    i = first_tile + pl.program_id(1)
    n_chunks = ((i + 1) * qt - 1) // K_CHUNK + 1
    attn_w = N_HEADS * HEAD_DIM
    scale = HEAD_DIM ** -0.5
    n_pairs = IDX_HEADS // 2
    lanes_are_queries = qt % LANES == 0

    t_col = i * qt + lax.broadcasted_iota(jnp.int32, (qt, 1), 0)
    t_row = i * qt + lax.broadcasted_iota(jnp.int32, (1, qt), 1)
    s_row = lax.broadcasted_iota(jnp.int32, (1, K_CHUNK), 1)
    s_col = lax.broadcasted_iota(jnp.int32, (K_CHUNK, 1), 0)
    lane_half = lax.broadcasted_iota(jnp.int32, (K_CHUNK, LANES), 1) // IDX_DIM

    n_near = nb_scr.shape[0]

    @pl.when((pl.program_id(0) == 0) & (pl.program_id(1) == 0))
    def _():
        for k in range(n_near - 1):
            for h in range(N_HEADS):
                v = jnp.broadcast_to(diag_ref[k, h:h + 1, :], (qt, 2 * K_CHUNK))
                t = pltpu.roll(v, 0, 1, stride=1, stride_axis=0)
                nb_scr[k, h * qt:(h + 1) * qt, :] = t[:, :K_CHUNK]
        nb_scr[n_near - 1] = jnp.zeros(nb_scr.shape[1:], F32)

    for h in range(N_HEADS):
        qa_scr[h * qt:(h + 1) * qt, :] = _dot(
            qq_ref[:, h * HEAD_DIM:(h + 1) * HEAD_DIM], wuk_ref[h]).astype(BF16)
    if lanes_are_queries:
        wt_scr[...] = wi_ref[...].T

    def idx_chunk(j, carry):
        ks = kk_ref[pl.ds(pl.multiple_of(j * K_CHUNK, K_CHUNK), K_CHUNK), :]
        zero = jnp.zeros(ks.shape, ks.dtype)
        k_even = jnp.where(lane_half == 0, ks, zero)
        k_odd = jnp.where(lane_half == 1, ks, zero)
        score = jnp.zeros((K_CHUNK, qt) if lanes_are_queries else (qt, K_CHUNK), F32)
        if lanes_are_queries:
            for p in range(n_pairs):
                q_pair = qq_ref[:, attn_w + p * LANES:attn_w + (p + 1) * LANES]
                for hh, k_half in ((2 * p, k_even), (2 * p + 1, k_odd)):
                    score = score + jnp.maximum(_dot_nt(k_half, q_pair), 0.0) * wt_scr[hh:hh + 1, :]
        else:
            q_pairs = jnp.concatenate(
                [qq_ref[:, attn_w + p * LANES:attn_w + (p + 1) * LANES] for p in range(n_pairs)], axis=0)
            for half, k_half in enumerate((k_even, k_odd)):
                dots = jnp.maximum(_dot_nt(q_pairs, k_half), 0.0)
                for p in range(n_pairs):
                    hh = 2 * p + half
                    score = score + dots[p * qt:(p + 1) * qt] * wi_ref[:, hh:hh + 1]
        s_pos = j * K_CHUNK + (s_col if lanes_are_queries else s_row)
        score = jnp.where(s_pos <= (t_row if lanes_are_queries else t_col), score, NEG_INF)
        if lanes_are_queries:
            sct_scr[j] = score
            sc_scr[j] = score.T
        else:
            sc_scr[j] = score
        return carry

    lax.fori_loop(0, n_chunks, idx_chunk, 0)

    k_f = float(top_k)
    idx_bits = int(math.ceil(math.log2(sc_scr.shape[0] * K_CHUNK)))
    per_query = (1, qt) if lanes_are_queries else (qt, 1)
    key_axis = 0 if lanes_are_queries else 1
    s_idx = s_col if lanes_are_queries else s_row
    search_scr = sct_scr if lanes_are_queries else sc_scr

    def fold_chunks(chunk_fn, combine, init):
        if lanes_are_queries:
            acc_rows = 4 * SUBLANES

            def body(j, acc):
                v = chunk_fn(j, search_scr[j]).reshape(K_CHUNK // acc_rows, acc_rows, qt)
                return combine(acc, functools.reduce(combine, [v[t] for t in range(K_CHUNK // acc_rows)]))

            acc = lax.fori_loop(0, n_chunks, body, jnp.full((acc_rows, qt), init, F32))
        else:
            acc = lax.fori_loop(0, n_chunks, lambda j, acc: combine(acc, chunk_fn(j, search_scr[j])),
                                jnp.full((qt, K_CHUNK), init, F32))
        reduce = jnp.sum if combine is jnp.add else jnp.min
        return reduce(acc, axis=key_axis, keepdims=True)

    def count(pred):
        return fold_chunks(lambda j, sc: jnp.where(pred(j, sc), 1.0, 0.0), jnp.add, 0.0)

    def key_to_float(key):
        return lax.bitcast_convert_type(jnp.where(key < 0, key ^ jnp.int32(0x7FFFFFFF), key), F32)

    def bit_step(_, carry):
        key, bit = carry
        cand = key + bit
        cand_f = key_to_float(cand)
        return (jnp.where(count(lambda j, sc: sc >= cand_f) >= k_f, cand, key),
                lax.shift_right_logical(bit, jnp.int32(1)))

    thr_key, _ = lax.fori_loop(0, 32, bit_step, (jnp.full(per_query, INT_MIN, jnp.int32), jnp.int32(INT_MIN)))
    thr = key_to_float(thr_key)

    n_ge = count(lambda j, sc: sc >= thr)
    has_tie = jnp.max(jnp.where((n_ge > k_f) & (thr > NEG_INF), 1.0, 0.0)) > 0.0

    def tie_break():
        def next_value(_, m):
            n_gt = count(lambda j, sc: sc > m)
            above = fold_chunks(lambda j, sc: jnp.where(sc > m, sc, jnp.inf), jnp.minimum, jnp.inf)
            return jnp.where(n_gt >= k_f, above, m)

        m = lax.fori_loop(0, TIE_ROUNDS, next_value, thr)
        need = k_f - count(lambda j, sc: sc > m)

        def step(_, carry):
            cut, bit = carry
            cand = cut + bit
            n_before = count(lambda j, sc: (sc == m) & ((j * K_CHUNK + s_idx) < cand))
            return jnp.where(n_before < need, cand, cut), lax.shift_right_logical(bit, jnp.int32(1))

        cut, _ = lax.fori_loop(0, idx_bits, step,
                               (jnp.zeros(per_query, jnp.int32), jnp.int32(2 ** (idx_bits - 1))))
        return m, cut

    thr, cut = lax.cond(has_tie, tie_break, lambda: (thr, jnp.full(per_query, 2 ** 30, jnp.int32)))

    def to_rows(v):
        if not lanes_are_queries:
            return jnp.broadcast_to(v, (qt, K_CHUNK))
        t = jnp.broadcast_to(v, (qt, qt)).T
        return jnp.concatenate([t] * (K_CHUNK // qt), axis=1)

    thr_b, cut_b = to_rows(thr), to_rows(cut)

    def mask_chunk(j, carry):
        sc = sc_scr[j]
        s_pos = j * K_CHUNK + s_row
        keep = ((sc > thr_b) | ((sc == thr_b) & (s_pos <= cut_b))) & (s_pos <= t_col)
        mb_scr[j] = jnp.where(keep, 0.0, NEG_INF)
        return carry

    lax.fori_loop(0, n_chunks, mask_chunk, 0)

    rows_h = N_HEADS * qt
    lane_fold = lambda v, op: functools.reduce(op, [v[:, k * LANES:(k + 1) * LANES] for k in range(K_CHUNK // LANES)])

    def key_rows(j):
        start = j * K_CHUNK
        return pl.ds(start if isinstance(j, int) else pl.multiple_of(start, K_CHUNK), K_CHUNK)

    def over_chunks(chunk_fn):
        chunk_fn(0, True)
        lax.fori_loop(1, n_chunks, lambda j, carry: (chunk_fn(j, False), carry)[1], 0)

    def logit_chunk(j, first):
        near = jnp.minimum((i * qt) // K_CHUNK - j, n_near - 1)
        s = _dot_nt(qa_scr[...], c_ref[key_rows(j), :]) * scale + nb_scr[near]
        s = s + jnp.concatenate([mb_scr[j]] * N_HEADS, axis=0)
        s_scr[j] = s
        fold = lane_fold(s, jnp.maximum)
        m_scr[...] = fold if first else jnp.maximum(m_scr[...], fold)

    over_chunks(logit_chunk)
    m_b = jnp.broadcast_to(jnp.max(m_scr[...], axis=-1, keepdims=True), (rows_h, LANES))
    m_scr[...] = m_b

    def value_chunk(j, first):
        p = jnp.exp(s_scr[j] - jnp.concatenate([m_scr[...]] * (K_CHUNK // LANES), axis=1))
        pv = _dot(p.astype(BF16), c_ref[key_rows(j), :])
        l_scr[...] = lane_fold(p, jnp.add) if first else l_scr[...] + lane_fold(p, jnp.add)
        acc_scr[...] = pv if first else acc_scr[...] + pv

    over_chunks(value_chunk)
    out = acc_scr[...] * (1.0 / jnp.sum(l_scr[...], axis=-1, keepdims=True))
    for h in range(N_HEADS):
        o_ref[:, h * KV_RANK:(h + 1) * KV_RANK] = out[h * qt:(h + 1) * qt].astype(o_ref.dtype)


def _dsa_attention(qq, wi, kk, c_kv, wuk, rel_bias, top_k, seq):
    B, tp, _ = qq.shape
    n_main = (seq + N_META) // Q_TILE
    assert (seq + N_META) - n_main * Q_TILE <= TAIL_TILE and Q_TILE % K_CHUNK == 0 and K_CHUNK % TAIL_TILE == 0
    n_chunks_max = tp // K_CHUNK
    width = N_HEADS * KV_RANK

    def call(qt, first_tile, n_tiles, n_fill, prev):
        rows_h = N_HEADS * qt
        diag = _near_bias_diagonals(rel_bias, qt)
        tile = lambda i: first_tile + jnp.minimum(i, n_tiles - 1)
        in_specs = [
            pl.BlockSpec((None, qt, qq.shape[2]), lambda b, i: (b, tile(i), 0)),
            pl.BlockSpec((None, qt, 128), lambda b, i: (b, tile(i), 0)),
            pl.BlockSpec((None, tp, LANES), lambda b, i: (b, 0, 0)),
            pl.BlockSpec((None, tp, KV_RANK), lambda b, i: (b, 0, 0)),
            pl.BlockSpec(wuk.shape, lambda b, i: (0, 0, 0)),
            pl.BlockSpec(diag.shape, lambda b, i: (0, 0, 0)),
        ]
        args = [qq, wi, kk, c_kv, wuk, diag]
        if prev is not None:
            in_specs.append(pl.BlockSpec(memory_space=pl.ANY))
            args.append(prev)
        return pl.pallas_call(
            functools.partial(_attn_kernel, n_tiles=n_tiles, aliased=prev is not None,
                              top_k=top_k, qt=qt, first_tile=first_tile),
            grid=(B, n_tiles + n_fill),
            in_specs=in_specs,
            out_specs=pl.BlockSpec((None, qt, width), lambda b, i: (b, first_tile + i, 0)),
            out_shape=jax.ShapeDtypeStruct((B, tp, width), BF16),
            input_output_aliases={} if prev is None else {len(args) - 1: 0},
            scratch_shapes=[pltpu.VMEM((diag.shape[0] + 1, rows_h, K_CHUNK), F32),
                            pltpu.VMEM((n_chunks_max, qt, K_CHUNK), F32),
                            pltpu.VMEM((n_chunks_max, K_CHUNK, qt), F32),
                            pltpu.VMEM((n_chunks_max, qt, K_CHUNK), F32),
                            pltpu.VMEM((LANES, qt), F32),
                            pltpu.VMEM((rows_h, KV_RANK), BF16),
                            pltpu.VMEM((n_chunks_max, rows_h, K_CHUNK), F32),
                            pltpu.VMEM((rows_h, LANES), F32),
                            pltpu.VMEM((rows_h, LANES), F32),
                            pltpu.VMEM((rows_h, KV_RANK), F32)],
            compiler_params=_compiler_params(("arbitrary", "arbitrary")),
            name="dsa_attention" if prev is None else "dsa_attention_tail",
        )(*args)

    o_lat = call(Q_TILE, 0, n_main, tp // Q_TILE - n_main, None)
    return call(TAIL_TILE, n_main * Q_TILE // TAIL_TILE, 1, 0, o_lat)


def _mix_window_start(r, seq):
    return min(max(r * MIX_ROWS - N_META, 0), seq - MIX_ROWS)


def _residual_rows(x_ref, meta_ref, h_scr, seq):
    r = pl.program_id(1)
    n_real = -(-(seq + N_META) // MIX_ROWS)
    for rv in range(n_real):

        @pl.when(r == rv)
        def _(rv=rv):
            skip = rv * MIX_ROWS - N_META - _mix_window_start(rv, seq)
            if rv == 0:
                h_scr[:N_META, :] = meta_ref[...]
                h_scr[N_META:, :] = x_ref[:MIX_ROWS - N_META, :]
            elif skip == 0:
                h_scr[...] = x_ref[...]
            else:
                h_scr[:MIX_ROWS - skip, :] = x_ref[skip:, :]
                h_scr[MIX_ROWS - skip:, :] = jnp.zeros((skip, h_scr.shape[1]), F32)

    @pl.when(r >= n_real)
    def _():
        h_scr[...] = jnp.zeros(h_scr.shape, F32)


def _mix_kernel(ol_ref, pd_ref, x_ref, meta_ref, wuv_ref, wp_ref, ps_ref, wo_ref, g_ref, b_ref, h1_ref, h1b_ref,
                h_scr, *, seq):
    _residual_rows(x_ref, meta_ref, h_scr, seq)
    attn = [_dot(ol_ref[:, h * KV_RANK:(h + 1) * KV_RANK], wuv_ref[h]) for h in range(N_HEADS)]
    pool = [_dot(pd_ref[:, g * POOL_GROUP:(g + 1) * POOL_GROUP], wp_ref[g]) for g in range(len(POOL_WINDOWS))]
    pool = jnp.concatenate(pool, axis=-1) * ps_ref[...]
    cat = jnp.concatenate(attn + [pool], axis=-1).astype(BF16)
    y = ALPHA * h_scr[...] + _dot(cat, wo_ref[...])
    h1 = _layer_norm(y, g_ref[...], b_ref[...])
    h1_ref[...] = h1
    h1b_ref[...] = h1.astype(h1b_ref.dtype)


def _mix_ln1(o_lat, pool_diff, x, meta, wuv, wpool, pool_scale, wo, g, b):
    B, seq, d = x.shape
    n_rows = seq + N_META
    assert n_rows % MIX_ROWS == 0
    row = lambda n: pl.BlockSpec((None, MIX_ROWS, n), lambda bi, r: (bi, r, 0))
    full = lambda a: pl.BlockSpec(a.shape, lambda bi, r: (0,) * a.ndim, pipeline_mode=pl.Buffered(1))
    window = pl.BlockSpec(
        (None, pl.Element(MIX_ROWS), pl.Element(d)),
        lambda bi, r: (bi, pl.multiple_of(jnp.clip(r * MIX_ROWS - N_META, 0, seq - MIX_ROWS), N_META), 0))
    return pl.pallas_call(
        functools.partial(_mix_kernel, seq=seq),
        grid=(B, n_rows // MIX_ROWS),
        in_specs=[row(o_lat.shape[2]), row(pool_diff.shape[2]), window, full(meta),
                  full(wuv), full(wpool), full(pool_scale), full(wo), full(g), full(b)],
        out_specs=[row(d), row(d)],
        out_shape=[jax.ShapeDtypeStruct((B, n_rows, d), F32), jax.ShapeDtypeStruct((B, n_rows, d), BF16)],
        scratch_shapes=[pltpu.VMEM((MIX_ROWS, d), F32)],
        compiler_params=_compiler_params(("parallel", "arbitrary"), BIG_VMEM_LIMIT_BYTES),
        name="mix_ln1",
    )(o_lat, pool_diff, x, meta, wuv, wpool, pool_scale, wo, g, b)


def _gelu_tanh(x):
    return 0.5 * x * (1.0 + jnp.tanh(math.sqrt(2.0 / math.pi) * (x + 0.044715 * (x * x * x))))


def _ffn_kernel(hw_ref, hres_ref, wa_ref, wg_ref, cwa_ref, cwg_ref, cba_ref, cbg_ref, wd_ref, g_ref, b_ref, o_ref,
                za_scr, zg_scr):
    c = pl.program_id(2)
    n_c = pl.num_programs(2) - 1

    def up(slot):
        x = hw_ref[...]
        za_scr[slot] = _dot(x, wa_ref[...])
        zg_scr[slot] = _dot(x, wg_ref[...])

    def conv(z, cw_ref, cb_ref):
        cw = cw_ref[...]
        n = z.shape[0]
        out = z[HALO - 2:n - 2] * cw[0:1] + z[HALO - 1:n - 1] * cw[1:2] + z[HALO:] * cw[2:3]
        return out + cb_ref[...]

    def down(slot):
        a = conv(za_scr[slot], cwa_ref, cba_ref)
        gate = conv(zg_scr[slot], cwg_ref, cbg_ref)
        act = (_gelu_tanh(a) * gate).astype(BF16)
        o_ref[...] += _dot(act, wd_ref[...])

    @pl.when(c == 0)
    def _():
        o_ref[...] = jnp.zeros(o_ref.shape, o_ref.dtype)
        up(0)

    @pl.when((c > 0) & (c < n_c))
    def _():
        up(c % 2)
        down((c - 1) % 2)

    @pl.when(c == n_c)
    def _():
        down((c - 1) % 2)
        o_ref[...] = _layer_norm(ALPHA * hres_ref[...] + o_ref[...], g_ref[...], b_ref[...])


def _ffn_ln2(h1, h1b, w_up, conv_w, conv_b, w_down, g, b, seq):
    B, tp, d = h1.shape
    d_ff = w_down.shape[0]
    n_c = d_ff // FFN_COLS
    up_c = lambda c: jnp.minimum(c, n_c - 1)
    dn_c = lambda c: jnp.maximum(c - 1, 0)
    vec = lambda off: pl.BlockSpec((1, FFN_COLS), lambda bi, r, c: (0, dn_c(c) + off))
    return pl.pallas_call(
        _ffn_kernel,
        grid=(B, seq // FFN_ROWS, n_c + 1),
        in_specs=[
            pl.BlockSpec((None, pl.Element(FFN_ROWS + HALO), pl.Element(d)),
                         lambda bi, r, c: (bi, r * FFN_ROWS + N_META - HALO, 0)),
            pl.BlockSpec((None, pl.Element(FFN_ROWS), pl.Element(d)),
                         lambda bi, r, c: (bi, pl.multiple_of(r * FFN_ROWS + N_META, N_META), 0)),
            pl.BlockSpec((d, FFN_COLS), lambda bi, r, c: (0, up_c(c))),
            pl.BlockSpec((d, FFN_COLS), lambda bi, r, c: (0, up_c(c) + n_c)),
            pl.BlockSpec((CONV_WIDTH, FFN_COLS), lambda bi, r, c: (0, dn_c(c))),
            pl.BlockSpec((CONV_WIDTH, FFN_COLS), lambda bi, r, c: (0, dn_c(c) + n_c)),
            vec(0), vec(n_c),
            pl.BlockSpec((FFN_COLS, d), lambda bi, r, c: (dn_c(c), 0)),
            pl.BlockSpec((1, d), lambda bi, r, c: (0, 0)),
            pl.BlockSpec((1, d), lambda bi, r, c: (0, 0)),
        ],
        out_specs=pl.BlockSpec((None, FFN_ROWS, d), lambda bi, r, c: (bi, r, 0), pipeline_mode=pl.Buffered(1)),
        out_shape=jax.ShapeDtypeStruct((B, seq, d), F32),
        scratch_shapes=[pltpu.VMEM((2, FFN_ROWS + HALO, FFN_COLS), F32),
                        pltpu.VMEM((2, FFN_ROWS + HALO, FFN_COLS), F32)],
        compiler_params=_compiler_params(("parallel", "parallel", "arbitrary"), BIG_VMEM_LIMIT_BYTES),
        name="ffn_ln2",
    )(h1b, h1, w_up, w_up, conv_w, conv_w, conv_b, conv_b, w_down, g, b)


def _t5_bucket_table(n):
    dist = np.arange(n, dtype=np.int32)
    max_exact = REL_BUCKETS // 2
    d_f = np.maximum(dist, 1).astype(np.float32)
    large = max_exact + (np.log(d_f / np.float32(max_exact)) / np.float32(math.log(REL_MAX_DIST / max_exact))
                         * np.float32(REL_BUCKETS - max_exact)).astype(np.int32)
    return np.where(dist < max_exact, dist, np.minimum(large, REL_BUCKETS - 1))


def _near_bias_diagonals(rel_bias, qt):
    assert qt <= K_CHUNK
    probe = _t5_bucket_table(4 * REL_MAX_DIST)
    first_far = int(np.argmax(probe == REL_BUCKETS - 1))
    assert np.all(probe[first_far:] == REL_BUCKETS - 1)
    n_real = -(-(first_far + K_CHUNK - 1) // K_CHUNK)
    buckets = _t5_bucket_table((n_real + 1) * K_CHUNK)
    period = 2 * K_CHUNK
    u = np.arange(period)
    k = np.arange(n_real)[:, None]
    dist = np.where(u < K_CHUNK, k * K_CHUNK - u, k * K_CHUNK + period - u)
    idx = buckets[np.clip(dist, 0, len(buckets) - 1)]
    rel = rel_bias.astype(F32) - rel_bias[REL_BUCKETS - 1:].astype(F32)
    return jnp.transpose(rel[idx], (0, 2, 1))


def kernel(x, meta, rel_bias, w_in, kv_norm_g, w_uk, w_uv, w_pool, pool_scale, w_o, ln1_g, ln1_b, w_up, conv_w,
           conv_b, w_down, ln2_g, ln2_b):
    B, S, D = x.shape
    assert w_in.shape[0] == DEPTH and S % FFN_ROWS == 0
    T = S + N_META
    tp = -(-T // ROW_ALIGN) * ROW_ALIGN
    assert tp - T >= max(POOL_WINDOWS)
    top_k = min(TOPK_MAX, S // 4)
    assert top_k <= K_CHUNK

    attn_w = N_HEADS * HEAD_DIM
    idx_w = IDX_HEADS * IDX_DIM
    o_c, o_qi = attn_w, attn_w + KV_RANK
    o_ki = o_qi + idx_w
    o_wi = o_ki + IDX_DIM
    o_u = o_wi + IDX_HEADS
    w = w_in[0]
    w_qq = jnp.concatenate([w[:, :o_c], w[:, o_qi:o_ki] * (IDX_DIM ** -0.5)], axis=1).astype(BF16)
    w_u = w[:, o_u:].astype(BF16)
    w_small = jnp.concatenate([w[:, o_c:o_qi], w[:, o_ki:o_wi], w[:, o_ki:o_wi],
                               w[:, o_wi:o_u] * (IDX_HEADS ** -0.5), jnp.zeros((D, 128 - IDX_HEADS), w.dtype)],
                              axis=1).astype(BF16)

    qq = _proj_qq(x, meta, w_qq, tp)
    pool_diff = _proj_pool(x, meta, w_u, tp)
    c_kv, kk, wi = _proj_small(x, meta, w_small, kv_norm_g[0].reshape(1, KV_RANK), tp)

    wuk = jnp.transpose(w_uk[0], (1, 2, 0)).astype(BF16)
    o_lat = _dsa_attention(qq, wi, kk, c_kv, wuk, rel_bias, top_k, S)

    wuv = jnp.transpose(w_uv[0], (1, 0, 2)).astype(BF16)
    h1, h1b = _mix_ln1(o_lat, pool_diff, x, meta, wuv, w_pool[0].astype(BF16), pool_scale[0].reshape(1, -1),
                       w_o[0].astype(BF16), ln1_g[0].reshape(1, D), ln1_b[0].reshape(1, D))

    return _ffn_ln2(h1, h1b, w_up[0].astype(BF16), conv_w[0], conv_b[0].reshape(1, -1), w_down[0].astype(BF16),
                    ln2_g[0].reshape(1, D), ln2_b[0].reshape(1, D), S)
```

```python
import functools
import math

import numpy as np
import jax
import jax.numpy as jnp
from jax import lax
from jax.experimental import pallas as pl
from jax.experimental.pallas import tpu as pltpu

F32 = jnp.float32
BF16 = jnp.bfloat16

N_META = 16
N_HEADS = 8
HEAD_DIM = 128
KV_RANK = 256
IDX_HEADS = 16
IDX_DIM = 64
TOPK_MAX = 256
POOL_WINDOWS = (2, 4, 8, 16)
POOL_GROUP = 256
CONV_WIDTH = 3
REL_BUCKETS = 32
REL_MAX_DIST = 128
DEPTH = 1
ALPHA = (2.0 * DEPTH) ** 0.25
LN_EPS = 1e-5
NEG_INF = -1e30

VMEM_LIMIT_BYTES = 56 * 1024 * 1024
BIG_VMEM_LIMIT_BYTES = 60 * 1024 * 1024
SUBLANES = 8
LANES = 128
ROW_ALIGN = 256
PROJ_ROWS = 1024
Q_TILE = 256
TAIL_TILE = 16
K_CHUNK = 256
MIX_ROWS = 688
FFN_ROWS = 1024
FFN_COLS = 512
HALO = 16
INT_MIN = -(2 ** 31)
PACKED_SUBLANES = 16
TIE_ROUNDS = 8


def _dot(a, b):
    return jnp.dot(a, b, preferred_element_type=F32)


def _dot_nt(a, b):
    return lax.dot_general(a, b, (((1,), (1,)), ((), ())), preferred_element_type=F32)


def _layer_norm(y, g, b):
    mu = jnp.mean(y, axis=-1, keepdims=True)
    yc = y - mu
    var = jnp.mean(yc * yc, axis=-1, keepdims=True)
    return yc * lax.rsqrt(var + LN_EPS) * g + b


def _proj_rows(x_ref, meta_ref, w_ref, tp, store):
    seq = x_ref.shape[0]
    store(pl.ds(0, N_META), _dot(meta_ref[...].astype(BF16), w_ref[...]))
    for r in range(seq // PROJ_ROWS):
        acc = _dot(x_ref[r * PROJ_ROWS:(r + 1) * PROJ_ROWS, :].astype(BF16), w_ref[...])
        store(pl.ds(N_META + r * PROJ_ROWS, PROJ_ROWS), acc)
    n_pad = tp - seq - N_META
    store(pl.ds(seq + N_META, n_pad), jnp.zeros((n_pad, w_ref.shape[1]), F32))


def _proj_cast_kernel(x_ref, meta_ref, w_ref, o_ref):
    def store(rows, acc):
        o_ref[rows, :] = acc.astype(o_ref.dtype)

    _proj_rows(x_ref, meta_ref, w_ref, o_ref.shape[0], store)


def _proj_pool_kernel(x_ref, meta_ref, w_ref, o_ref, u_scr):
    def store(rows, acc):
        u_scr[rows, :] = acc

    _proj_rows(x_ref, meta_ref, w_ref, u_scr.shape[0], store)
    group = pl.program_id(1)
    tp = u_scr.shape[0]
    pos = lax.broadcasted_iota(jnp.int32, (tp, 1), 0)
    for g, window in enumerate(POOL_WINDOWS):

        @pl.when(group == g)
        def _():
            u = u_scr[...]
            s = u
            shift = 1
            while shift < window:
                s = s + pltpu.roll(s, shift, axis=0)
                shift *= 2
            count = jnp.minimum(pos + 1, window).astype(F32)
            o_ref[...] = (s / count - u).astype(o_ref.dtype)


def _proj_small_kernel(x_ref, meta_ref, w_ref, g_ref, c_ref, kk_ref, wi_ref):
    def store(rows, acc):
        c = acc[:, :KV_RANK]
        ms = jnp.mean(c * c, axis=-1, keepdims=True)
        c_ref[rows, :] = (c * lax.rsqrt(ms + LN_EPS) * g_ref[...]).astype(c_ref.dtype)
        kk_ref[rows, :] = acc[:, KV_RANK:KV_RANK + LANES].astype(kk_ref.dtype)
        wi_ref[rows, :] = acc[:, KV_RANK + LANES:]

    _proj_rows(x_ref, meta_ref, w_ref, c_ref.shape[0], store)


def _batch_block(tp, n):
    return pl.BlockSpec((None, tp, n), lambda b, j: (b, 0, 0))


def _compiler_params(semantics, vmem_limit_bytes=VMEM_LIMIT_BYTES):
    return pltpu.CompilerParams(dimension_semantics=semantics, vmem_limit_bytes=vmem_limit_bytes)


def _proj_call(body, x, meta, w, tn, extra_in, extra_specs, out_blocks, out_shapes, scratch, name):
    B, seq, d = x.shape
    return pl.pallas_call(
        body,
        grid=(B, w.shape[1] // tn),
        in_specs=[_batch_block(seq, d), pl.BlockSpec(meta.shape, lambda b, j: (0, 0)),
                  pl.BlockSpec((d, tn), lambda b, j: (0, j))] + extra_specs,
        out_specs=out_blocks,
        out_shape=out_shapes,
        scratch_shapes=scratch,
        compiler_params=_compiler_params(("parallel", "arbitrary")),
        name=name,
    )(x, meta, w, *extra_in)


def _proj_qq(x, meta, w, tp):
    B, n, tn = x.shape[0], w.shape[1], 512
    return _proj_call(_proj_cast_kernel, x, meta, w, tn, [], [],
                      pl.BlockSpec((None, tp, tn), lambda b, j: (b, 0, j)),
                      jax.ShapeDtypeStruct((B, tp, n), BF16), [], "proj_qq")


def _proj_pool(x, meta, w, tp):
    B, n = x.shape[0], w.shape[1]
    return _proj_call(_proj_pool_kernel, x, meta, w, POOL_GROUP, [], [],
                      pl.BlockSpec((None, tp, POOL_GROUP), lambda b, j: (b, 0, j)),
                      jax.ShapeDtypeStruct((B, tp, n), BF16), [pltpu.VMEM((tp, POOL_GROUP), F32)], "proj_pool")


def _proj_small(x, meta, w, kv_g, tp):
    B = x.shape[0]
    return _proj_call(_proj_small_kernel, x, meta, w, w.shape[1], [kv_g],
                      [pl.BlockSpec((1, KV_RANK), lambda b, j: (0, 0))],
                      [_batch_block(tp, KV_RANK), _batch_block(tp, LANES), _batch_block(tp, 128)],
                      [jax.ShapeDtypeStruct((B, tp, KV_RANK), BF16), jax.ShapeDtypeStruct((B, tp, LANES), BF16),
                       jax.ShapeDtypeStruct((B, tp, 128), F32)], [], "proj_small")


def _attn_kernel(*refs, n_tiles, aliased, **static):
    refs = refs[:6] + refs[6 + int(aliased):]
    o_ref = refs[6]

    @pl.when(pl.program_id(1) < n_tiles)
    def _():
        _attn_tile(*refs, **static)

    @pl.when(pl.program_id(1) >= n_tiles)
    def _():
        o_ref[...] = jnp.zeros(o_ref.shape, o_ref.dtype)


def _attn_tile(qq_ref, wi_ref, kk_ref, c_ref, wuk_ref, diag_ref, o_ref, nb_scr, sc_scr, sct_scr, scf_scr, mb_scr,
               wt_scr, qa_scr, s_scr, m_scr, l_scr, acc_scr, *, top_k, qt, first_tile):
    i = first_tile + pl.program_id(1)
    n_chunks = ((i + 1) * qt - 1) // K_CHUNK + 1
    attn_w = N_HEADS * HEAD_DIM
    scale = HEAD_DIM ** -0.5
    n_pairs = IDX_HEADS // 2
    lanes_are_queries = qt % LANES == 0

    t_col = i * qt + lax.broadcasted_iota(jnp.int32, (qt, 1), 0)
    t_row = i * qt + lax.broadcasted_iota(jnp.int32, (1, qt), 1)
    s_row = lax.broadcasted_iota(jnp.int32, (1, K_CHUNK), 1)
    s_col = lax.broadcasted_iota(jnp.int32, (K_CHUNK, 1), 0)
    lane_half = lax.broadcasted_iota(jnp.int32, (K_CHUNK, LANES), 1) // IDX_DIM

    n_near = nb_scr.shape[0]

    @pl.when((pl.program_id(0) == 0) & (pl.program_id(1) == 0))
    def _():
        for k in range(n_near - 1):
            for h in range(N_HEADS):
                v = jnp.broadcast_to(diag_ref[k, h:h + 1, :], (qt, 2 * K_CHUNK))
                t = pltpu.roll(v, 0, 1, stride=1, stride_axis=0)
                nb_scr[k, h * qt:(h + 1) * qt, :] = t[:, :K_CHUNK]
        nb_scr[n_near - 1] = jnp.zeros(nb_scr.shape[1:], F32)

    for h in range(N_HEADS):
        qa_scr[h * qt:(h + 1) * qt, :] = _dot(
            qq_ref[:, h * HEAD_DIM:(h + 1) * HEAD_DIM], wuk_ref[h]).astype(BF16)
    if lanes_are_queries:
        wt_scr[...] = wi_ref[...].T

    def idx_chunk(j, carry):
        ks = kk_ref[pl.ds(pl.multiple_of(j * K_CHUNK, K_CHUNK), K_CHUNK), :]
        zero = jnp.zeros(ks.shape, ks.dtype)
        k_even = jnp.where(lane_half == 0, ks, zero)
        k_odd = jnp.where(lane_half == 1, ks, zero)
        score = jnp.zeros((K_CHUNK, qt) if lanes_are_queries else (qt, K_CHUNK), F32)
        if lanes_are_queries:
            for p in range(n_pairs):
                q_pair = qq_ref[:, attn_w + p * LANES:attn_w + (p + 1) * LANES]
                for hh, k_half in ((2 * p, k_even), (2 * p + 1, k_odd)):
                    score = score + jnp.maximum(_dot_nt(k_half, q_pair), 0.0) * wt_scr[hh:hh + 1, :]
        else:
            q_pairs = jnp.concatenate(
                [qq_ref[:, attn_w + p * LANES:attn_w + (p + 1) * LANES] for p in range(n_pairs)], axis=0)
            for half, k_half in enumerate((k_even, k_odd)):
                dots = jnp.maximum(_dot_nt(q_pairs, k_half), 0.0)
                for p in range(n_pairs):
                    hh = 2 * p + half
                    score = score + dots[p * qt:(p + 1) * qt] * wi_ref[:, hh:hh + 1]
        s_pos = j * K_CHUNK + (s_col if lanes_are_queries else s_row)
        score = jnp.where(s_pos <= (t_row if lanes_are_queries else t_col), score, NEG_INF)
        if lanes_are_queries:
            sct_scr[j] = score
            sc_scr[j] = score.T
            near = score.astype(BF16)
            bits = lax.bitcast_convert_type(near, jnp.int16)
            below = lax.bitcast_convert_type(bits + jnp.where(bits < 0, jnp.int16(1), jnp.int16(-1)), BF16)
            scf_scr[j] = jnp.where(near.astype(F32) > score, below, near)
        else:
            sc_scr[j] = score
        return carry

    lax.fori_loop(0, n_chunks, idx_chunk, 0)

    k_f = float(top_k)
    idx_bits = int(math.ceil(math.log2(sc_scr.shape[0] * K_CHUNK)))
    per_query = (1, qt) if lanes_are_queries else (qt, 1)
    key_axis = 0 if lanes_are_queries else 1
    s_idx = s_col if lanes_are_queries else s_row
    search_scr = sct_scr if lanes_are_queries else sc_scr

    def fold_chunks(chunk_fn, combine, init):
        if lanes_are_queries:
            acc_rows = 4 * SUBLANES

            def body(j, acc):
                v = chunk_fn(j, search_scr[j]).reshape(K_CHUNK // acc_rows, acc_rows, qt)
                return combine(acc, functools.reduce(combine, [v[t] for t in range(K_CHUNK // acc_rows)]))

            acc = lax.fori_loop(0, n_chunks, body, jnp.full((acc_rows, qt), init, F32))
        else:
            acc = lax.fori_loop(0, n_chunks, lambda j, acc: combine(acc, chunk_fn(j, search_scr[j])),
                                jnp.full((qt, K_CHUNK), init, F32))
        reduce = jnp.sum if combine is jnp.add else jnp.min
        return reduce(acc, axis=key_axis, keepdims=True)

    def count(pred):
        return fold_chunks(lambda j, sc: jnp.where(pred(j, sc), 1.0, 0.0), jnp.add, 0.0)

    def key_to_float(key):
        return lax.bitcast_convert_type(jnp.where(key < 0, key ^ jnp.int32(0x7FFFFFFF), key), F32)

    def count_coarse(cand_f):
        acc_rows = 4 * PACKED_SUBLANES
        cand_b = jnp.broadcast_to(cand_f, (acc_rows, qt)).astype(BF16)

        def body(j, acc):
            hit = jnp.where(scf_scr[j].reshape(K_CHUNK // acc_rows, acc_rows, qt) >= cand_b[None],
                            jnp.ones((), BF16), jnp.zeros((), BF16))
            return acc + functools.reduce(jnp.add, [hit[t] for t in range(K_CHUNK // acc_rows)])

        acc = lax.fori_loop(0, n_chunks, body, jnp.zeros((acc_rows, qt), BF16))
        return jnp.sum(acc.astype(F32), axis=0, keepdims=True)

    def bit_step(coarse, carry):
        key, bit = carry
        cand = key + bit
        cand_f = key_to_float(cand)
        n = count_coarse(cand_f) if coarse else count(lambda j, sc: sc >= cand_f)
        return jnp.where(n >= k_f, cand, key), lax.shift_right_logical(bit, jnp.int32(1))

    state = (jnp.full(per_query, INT_MIN, jnp.int32), jnp.int32(INT_MIN))
    n_coarse = 0
    if lanes_are_queries:
        n_coarse = 16
        state = lax.fori_loop(0, n_coarse, lambda _, c: bit_step(True, c), state)
    thr_key, _ = lax.fori_loop(n_coarse, 32, lambda _, c: bit_step(False, c), state)
    thr = key_to_float(thr_key)

    n_ge = count(lambda j, sc: sc >= thr)
    has_tie = jnp.max(jnp.where((n_ge > k_f) & (thr > NEG_INF), 1.0, 0.0)) > 0.0

    def tie_break():
        def next_value(_, m):
            n_gt = count(lambda j, sc: sc > m)
            above = fold_chunks(lambda j, sc: jnp.where(sc > m, sc, jnp.inf), jnp.minimum, jnp.inf)
            return jnp.where(n_gt >= k_f, above, m)

        m = lax.fori_loop(0, TIE_ROUNDS, next_value, thr)
        need = k_f - count(lambda j, sc: sc > m)

        def step(_, carry):
            cut, bit = carry
            cand = cut + bit
            n_before = count(lambda j, sc: (sc == m) & ((j * K_CHUNK + s_idx) < cand))
            return jnp.where(n_before < need, cand, cut), lax.shift_right_logical(bit, jnp.int32(1))

        cut, _ = lax.fori_loop(0, idx_bits, step,
                               (jnp.zeros(per_query, jnp.int32), jnp.int32(2 ** (idx_bits - 1))))
        return m, cut

    thr, cut = lax.cond(has_tie, tie_break, lambda: (thr, jnp.full(per_query, 2 ** 30, jnp.int32)))

    def to_rows(v):
        if not lanes_are_queries:
            return jnp.broadcast_to(v, (qt, K_CHUNK))
        t = jnp.broadcast_to(v, (qt, qt)).T
        return jnp.concatenate([t] * (K_CHUNK // qt), axis=1)

    thr_b, cut_b = to_rows(thr), to_rows(cut)

    def mask_chunk(j, carry):
        sc = sc_scr[j]
        s_pos = j * K_CHUNK + s_row
        keep = ((sc > thr_b) | ((sc == thr_b) & (s_pos <= cut_b))) & (s_pos <= t_col)
        mb_scr[j] = jnp.where(keep, 0.0, NEG_INF)
        return carry

    lax.fori_loop(0, n_chunks, mask_chunk, 0)

    rows_h = N_HEADS * qt
    lane_fold = lambda v, op: functools.reduce(op, [v[:, k * LANES:(k + 1) * LANES] for k in range(K_CHUNK // LANES)])

    def key_rows(j):
        start = j * K_CHUNK
        return pl.ds(start if isinstance(j, int) else pl.multiple_of(start, K_CHUNK), K_CHUNK)

    def over_chunks(chunk_fn):
        chunk_fn(0, True)
        lax.fori_loop(1, n_chunks, lambda j, carry: (chunk_fn(j, False), carry)[1], 0)

    def logit_chunk(j, first):
        near = jnp.minimum((i * qt) // K_CHUNK - j, n_near - 1)
        s = _dot_nt(qa_scr[...], c_ref[key_rows(j), :]) * scale + nb_scr[near]
        s = s + jnp.concatenate([mb_scr[j]] * N_HEADS, axis=0)
        s_scr[j] = s
        fold = lane_fold(s, jnp.maximum)
        m_scr[...] = fold if first else jnp.maximum(m_scr[...], fold)

    over_chunks(logit_chunk)
    m_b = jnp.broadcast_to(jnp.max(m_scr[...], axis=-1, keepdims=True), (rows_h, LANES))
    m_scr[...] = m_b

    def value_chunk(j, first):
        p = jnp.exp(s_scr[j] - jnp.concatenate([m_scr[...]] * (K_CHUNK // LANES), axis=1))
        pv = _dot(p.astype(BF16), c_ref[key_rows(j), :])
        l_scr[...] = lane_fold(p, jnp.add) if first else l_scr[...] + lane_fold(p, jnp.add)
        acc_scr[...] = pv if first else acc_scr[...] + pv

    over_chunks(value_chunk)
    out = acc_scr[...] * (1.0 / jnp.sum(l_scr[...], axis=-1, keepdims=True))
    for h in range(N_HEADS):
        o_ref[:, h * KV_RANK:(h + 1) * KV_RANK] = out[h * qt:(h + 1) * qt].astype(o_ref.dtype)


def _dsa_attention(qq, wi, kk, c_kv, wuk, rel_bias, top_k, seq):
    B, tp, _ = qq.shape
    n_main = (seq + N_META) // Q_TILE
    assert (seq + N_META) - n_main * Q_TILE <= TAIL_TILE and Q_TILE % K_CHUNK == 0 and K_CHUNK % TAIL_TILE == 0
    n_chunks_max = tp // K_CHUNK
    width = N_HEADS * KV_RANK

    def call(qt, first_tile, n_tiles, n_fill, prev):
        rows_h = N_HEADS * qt
        diag = _near_bias_diagonals(rel_bias, qt)
        tile = lambda i: first_tile + jnp.minimum(i, n_tiles - 1)
        in_specs = [
            pl.BlockSpec((None, qt, qq.shape[2]), lambda b, i: (b, tile(i), 0)),
            pl.BlockSpec((None, qt, 128), lambda b, i: (b, tile(i), 0)),
            pl.BlockSpec((None, tp, LANES), lambda b, i: (b, 0, 0)),
            pl.BlockSpec((None, tp, KV_RANK), lambda b, i: (b, 0, 0)),
            pl.BlockSpec(wuk.shape, lambda b, i: (0, 0, 0)),
            pl.BlockSpec(diag.shape, lambda b, i: (0, 0, 0)),
        ]
        args = [qq, wi, kk, c_kv, wuk, diag]
        if prev is not None:
            in_specs.append(pl.BlockSpec(memory_space=pl.ANY))
            args.append(prev)
        return pl.pallas_call(
            functools.partial(_attn_kernel, n_tiles=n_tiles, aliased=prev is not None,
                              top_k=top_k, qt=qt, first_tile=first_tile),
            grid=(B, n_tiles + n_fill),
            in_specs=in_specs,
            out_specs=pl.BlockSpec((None, qt, width), lambda b, i: (b, first_tile + i, 0)),
            out_shape=jax.ShapeDtypeStruct((B, tp, width), BF16),
            input_output_aliases={} if prev is None else {len(args) - 1: 0},
            scratch_shapes=[pltpu.VMEM((diag.shape[0] + 1, rows_h, K_CHUNK), F32),
                            pltpu.VMEM((n_chunks_max, qt, K_CHUNK), F32),
                            pltpu.VMEM((n_chunks_max, K_CHUNK, qt), F32),
                            pltpu.VMEM((n_chunks_max, K_CHUNK, qt), BF16),
                            pltpu.VMEM((n_chunks_max, qt, K_CHUNK), F32),
                            pltpu.VMEM((LANES, qt), F32),
                            pltpu.VMEM((rows_h, KV_RANK), BF16),
                            pltpu.VMEM((n_chunks_max, rows_h, K_CHUNK), F32),
                            pltpu.VMEM((rows_h, LANES), F32),
                            pltpu.VMEM((rows_h, LANES), F32),
                            pltpu.VMEM((rows_h, KV_RANK), F32)],
            compiler_params=_compiler_params(("arbitrary", "arbitrary")),
            name="dsa_attention" if prev is None else "dsa_attention_tail",
        )(*args)

    o_lat = call(Q_TILE, 0, n_main, tp // Q_TILE - n_main, None)
    return call(TAIL_TILE, n_main * Q_TILE // TAIL_TILE, 1, 0, o_lat)


def _mix_window_start(r, seq):
    return min(max(r * MIX_ROWS - N_META, 0), seq - MIX_ROWS)


def _residual_rows(x_ref, meta_ref, h_scr, seq):
    r = pl.program_id(1)
    n_real = -(-(seq + N_META) // MIX_ROWS)
    for rv in range(n_real):

        @pl.when(r == rv)
        def _(rv=rv):
            skip = rv * MIX_ROWS - N_META - _mix_window_start(rv, seq)
            if rv == 0:
                h_scr[:N_META, :] = meta_ref[...]
                h_scr[N_META:, :] = x_ref[:MIX_ROWS - N_META, :]
            elif skip == 0:
                h_scr[...] = x_ref[...]
            else:
                h_scr[:MIX_ROWS - skip, :] = x_ref[skip:, :]
                h_scr[MIX_ROWS - skip:, :] = jnp.zeros((skip, h_scr.shape[1]), F32)

    @pl.when(r >= n_real)
    def _():
        h_scr[...] = jnp.zeros(h_scr.shape, F32)


def _mix_kernel(ol_ref, pd_ref, x_ref, meta_ref, wuv_ref, wp_ref, ps_ref, wo_ref, g_ref, b_ref, h1_ref, h1b_ref,
                h_scr, *, seq):
    _residual_rows(x_ref, meta_ref, h_scr, seq)
    attn = [_dot(ol_ref[:, h * KV_RANK:(h + 1) * KV_RANK], wuv_ref[h]) for h in range(N_HEADS)]
    pool = [_dot(pd_ref[:, g * POOL_GROUP:(g + 1) * POOL_GROUP], wp_ref[g]) for g in range(len(POOL_WINDOWS))]
    pool = jnp.concatenate(pool, axis=-1) * ps_ref[...]
    cat = jnp.concatenate(attn + [pool], axis=-1).astype(BF16)
    y = ALPHA * h_scr[...] + _dot(cat, wo_ref[...])
    h1 = _layer_norm(y, g_ref[...], b_ref[...])
    h1_ref[...] = h1
    h1b_ref[...] = h1.astype(h1b_ref.dtype)


def _mix_ln1(o_lat, pool_diff, x, meta, wuv, wpool, pool_scale, wo, g, b):
    B, seq, d = x.shape
    n_rows = seq + N_META
    assert n_rows % MIX_ROWS == 0
    row = lambda n: pl.BlockSpec((None, MIX_ROWS, n), lambda bi, r: (bi, r, 0))
    full = lambda a: pl.BlockSpec(a.shape, lambda bi, r: (0,) * a.ndim, pipeline_mode=pl.Buffered(1))
    window = pl.BlockSpec(
        (None, pl.Element(MIX_ROWS), pl.Element(d)),
        lambda bi, r: (bi, pl.multiple_of(jnp.clip(r * MIX_ROWS - N_META, 0, seq - MIX_ROWS), N_META), 0))
    return pl.pallas_call(
        functools.partial(_mix_kernel, seq=seq),
        grid=(B, n_rows // MIX_ROWS),
        in_specs=[row(o_lat.shape[2]), row(pool_diff.shape[2]), window, full(meta),
                  full(wuv), full(wpool), full(pool_scale), full(wo), full(g), full(b)],
        out_specs=[row(d), row(d)],
        out_shape=[jax.ShapeDtypeStruct((B, n_rows, d), F32), jax.ShapeDtypeStruct((B, n_rows, d), BF16)],
        scratch_shapes=[pltpu.VMEM((MIX_ROWS, d), F32)],
        compiler_params=_compiler_params(("parallel", "arbitrary"), BIG_VMEM_LIMIT_BYTES),
        name="mix_ln1",
    )(o_lat, pool_diff, x, meta, wuv, wpool, pool_scale, wo, g, b)


def _gelu_tanh(x):
    return 0.5 * x * (1.0 + jnp.tanh(math.sqrt(2.0 / math.pi) * (x + 0.044715 * (x * x * x))))


def _ffn_kernel(hw_ref, hres_ref, wa_ref, wg_ref, cwa_ref, cwg_ref, cba_ref, cbg_ref, wd_ref, g_ref, b_ref, o_ref,
                za_scr, zg_scr):
    c = pl.program_id(2)
    n_c = pl.num_programs(2) - 1

    def up(slot):
        x = hw_ref[...]
        za_scr[slot] = _dot(x, wa_ref[...])
        zg_scr[slot] = _dot(x, wg_ref[...])

    def conv(z, cw_ref, cb_ref):
        cw = cw_ref[...]
        n = z.shape[0]
        out = z[HALO - 2:n - 2] * cw[0:1] + z[HALO - 1:n - 1] * cw[1:2] + z[HALO:] * cw[2:3]
        return out + cb_ref[...]

    def down(slot):
        a = conv(za_scr[slot], cwa_ref, cba_ref)
        gate = conv(zg_scr[slot], cwg_ref, cbg_ref)
        act = (_gelu_tanh(a) * gate).astype(BF16)
        o_ref[...] += _dot(act, wd_ref[...])

    @pl.when(c == 0)
    def _():
        o_ref[...] = jnp.zeros(o_ref.shape, o_ref.dtype)
        up(0)

    @pl.when((c > 0) & (c < n_c))
    def _():
        up(c % 2)
        down((c - 1) % 2)

    @pl.when(c == n_c)
    def _():
        down((c - 1) % 2)
        o_ref[...] = _layer_norm(ALPHA * hres_ref[...] + o_ref[...], g_ref[...], b_ref[...])


def _ffn_ln2(h1, h1b, w_up, conv_w, conv_b, w_down, g, b, seq):
    B, tp, d = h1.shape
    d_ff = w_down.shape[0]
    n_c = d_ff // FFN_COLS
    up_c = lambda c: jnp.minimum(c, n_c - 1)
    dn_c = lambda c: jnp.maximum(c - 1, 0)
    vec = lambda off: pl.BlockSpec((1, FFN_COLS), lambda bi, r, c: (0, dn_c(c) + off))
    return pl.pallas_call(
        _ffn_kernel,
        grid=(B, seq // FFN_ROWS, n_c + 1),
        in_specs=[
            pl.BlockSpec((None, pl.Element(FFN_ROWS + HALO), pl.Element(d)),
                         lambda bi, r, c: (bi, r * FFN_ROWS + N_META - HALO, 0)),
            pl.BlockSpec((None, pl.Element(FFN_ROWS), pl.Element(d)),
                         lambda bi, r, c: (bi, pl.multiple_of(r * FFN_ROWS + N_META, N_META), 0)),
            pl.BlockSpec((d, FFN_COLS), lambda bi, r, c: (0, up_c(c))),
            pl.BlockSpec((d, FFN_COLS), lambda bi, r, c: (0, up_c(c) + n_c)),
            pl.BlockSpec((CONV_WIDTH, FFN_COLS), lambda bi, r, c: (0, dn_c(c))),
            pl.BlockSpec((CONV_WIDTH, FFN_COLS), lambda bi, r, c: (0, dn_c(c) + n_c)),
            vec(0), vec(n_c),
            pl.BlockSpec((FFN_COLS, d), lambda bi, r, c: (dn_c(c), 0)),
            pl.BlockSpec((1, d), lambda bi, r, c: (0, 0)),
            pl.BlockSpec((1, d), lambda bi, r, c: (0, 0)),
        ],
        out_specs=pl.BlockSpec((None, FFN_ROWS, d), lambda bi, r, c: (bi, r, 0), pipeline_mode=pl.Buffered(1)),
        out_shape=jax.ShapeDtypeStruct((B, seq, d), F32),
        scratch_shapes=[pltpu.VMEM((2, FFN_ROWS + HALO, FFN_COLS), F32),
                        pltpu.VMEM((2, FFN_ROWS + HALO, FFN_COLS), F32)],
        compiler_params=_compiler_params(("parallel", "parallel", "arbitrary"), BIG_VMEM_LIMIT_BYTES),
        name="ffn_ln2",
    )(h1b, h1, w_up, w_up, conv_w, conv_w, conv_b, conv_b, w_down, g, b)


def _t5_bucket_table(n):
    dist = np.arange(n, dtype=np.int32)
    max_exact = REL_BUCKETS // 2
    d_f = np.maximum(dist, 1).astype(np.float32)
    large = max_exact + (np.log(d_f / np.float32(max_exact)) / np.float32(math.log(REL_MAX_DIST / max_exact))
                         * np.float32(REL_BUCKETS - max_exact)).astype(np.int32)
    return np.where(dist < max_exact, dist, np.minimum(large, REL_BUCKETS - 1))


def _near_bias_diagonals(rel_bias, qt):
    assert qt <= K_CHUNK
    probe = _t5_bucket_table(4 * REL_MAX_DIST)
    first_far = int(np.argmax(probe == REL_BUCKETS - 1))
    assert np.all(probe[first_far:] == REL_BUCKETS - 1)
    n_real = -(-(first_far + K_CHUNK - 1) // K_CHUNK)
    buckets = _t5_bucket_table((n_real + 1) * K_CHUNK)
    period = 2 * K_CHUNK
    u = np.arange(period)
    k = np.arange(n_real)[:, None]
    dist = np.where(u < K_CHUNK, k * K_CHUNK - u, k * K_CHUNK + period - u)
    idx = buckets[np.clip(dist, 0, len(buckets) - 1)]
    rel = rel_bias.astype(F32) - rel_bias[REL_BUCKETS - 1:].astype(F32)
    return jnp.transpose(rel[idx], (0, 2, 1))


def kernel(x, meta, rel_bias, w_in, kv_norm_g, w_uk, w_uv, w_pool, pool_scale, w_o, ln1_g, ln1_b, w_up, conv_w,
           conv_b, w_down, ln2_g, ln2_b):
    B, S, D = x.shape
    assert w_in.shape[0] == DEPTH and S % FFN_ROWS == 0
    T = S + N_META
    tp = -(-T // ROW_ALIGN) * ROW_ALIGN
    assert tp - T >= max(POOL_WINDOWS)
    top_k = min(TOPK_MAX, S // 4)
    assert top_k <= K_CHUNK

    attn_w = N_HEADS * HEAD_DIM
    idx_w = IDX_HEADS * IDX_DIM
    o_c, o_qi = attn_w, attn_w + KV_RANK
    o_ki = o_qi + idx_w
    o_wi = o_ki + IDX_DIM
    o_u = o_wi + IDX_HEADS
    w = w_in[0]
    w_qq = jnp.concatenate([w[:, :o_c], w[:, o_qi:o_ki] * (IDX_DIM ** -0.5)], axis=1).astype(BF16)
    w_u = w[:, o_u:].astype(BF16)
    w_small = jnp.concatenate([w[:, o_c:o_qi], w[:, o_ki:o_wi], w[:, o_ki:o_wi],
                               w[:, o_wi:o_u] * (IDX_HEADS ** -0.5), jnp.zeros((D, 128 - IDX_HEADS), w.dtype)],
                              axis=1).astype(BF16)

    qq = _proj_qq(x, meta, w_qq, tp)
    pool_diff = _proj_pool(x, meta, w_u, tp)
    c_kv, kk, wi = _proj_small(x, meta, w_small, kv_norm_g[0].reshape(1, KV_RANK), tp)

    wuk = jnp.transpose(w_uk[0], (1, 2, 0)).astype(BF16)
    o_lat = _dsa_attention(qq, wi, kk, c_kv, wuk, rel_bias, top_k, S)

    wuv = jnp.transpose(w_uv[0], (1, 0, 2)).astype(BF16)
    h1, h1b = _mix_ln1(o_lat, pool_diff, x, meta, wuv, w_pool[0].astype(BF16), pool_scale[0].reshape(1, -1),
                       w_o[0].astype(BF16), ln1_g[0].reshape(1, D), ln1_b[0].reshape(1, D))

    return _ffn_ln2(h1, h1b, w_up[0].astype(BF16), conv_w[0], conv_b[0].reshape(1, -1), w_down[0].astype(BF16),
                    ln2_g[0].reshape(1, D), ln2_b[0].reshape(1, D), S)
```

```python
import functools
import math

import numpy as np
import jax
import jax.numpy as jnp
from jax import lax
from jax.experimental import pallas as pl
from jax.experimental.pallas import tpu as pltpu

F32 = jnp.float32
BF16 = jnp.bfloat16

N_META = 16
N_HEADS = 8
HEAD_DIM = 128
KV_RANK = 256
IDX_HEADS = 16
IDX_DIM = 64
TOPK_MAX = 256
POOL_WINDOWS = (2, 4, 8, 16)
POOL_GROUP = 256
CONV_WIDTH = 3
REL_BUCKETS = 32
REL_MAX_DIST = 128
DEPTH = 1
ALPHA = (2.0 * DEPTH) ** 0.25
LN_EPS = 1e-5
NEG_INF = -1e30

VMEM_LIMIT_BYTES = 56 * 1024 * 1024
BIG_VMEM_LIMIT_BYTES = 60 * 1024 * 1024
SUBLANES = 8
LANES = 128
ROW_ALIGN = 256
PROJ_ROWS = 1024
Q_TILE = 256
TAIL_TILE = 16
K_CHUNK = 256
MIX_ROWS = 688
FFN_ROWS = 1024
FFN_COLS = 512
HALO = 16
INT_MIN = -(2 ** 31)
PACKED_SUBLANES = 16
TIE_ROUNDS = 8


def _dot(a, b):
    return jnp.dot(a, b, preferred_element_type=F32)


def _dot_nt(a, b):
    return lax.dot_general(a, b, (((1,), (1,)), ((), ())), preferred_element_type=F32)


def _layer_norm(y, g, b):
    mu = jnp.mean(y, axis=-1, keepdims=True)
    yc = y - mu
    var = jnp.mean(yc * yc, axis=-1, keepdims=True)
    return yc * lax.rsqrt(var + LN_EPS) * g + b


def _proj_rows(x_ref, meta_ref, w_ref, tp, store):
    seq = x_ref.shape[0]
    store(pl.ds(0, N_META), _dot(meta_ref[...].astype(BF16), w_ref[...]))
    for r in range(seq // PROJ_ROWS):
        acc = _dot(x_ref[r * PROJ_ROWS:(r + 1) * PROJ_ROWS, :].astype(BF16), w_ref[...])
        store(pl.ds(N_META + r * PROJ_ROWS, PROJ_ROWS), acc)
    n_pad = tp - seq - N_META
    store(pl.ds(seq + N_META, n_pad), jnp.zeros((n_pad, w_ref.shape[1]), F32))


def _proj_cast_kernel(x_ref, meta_ref, w_ref, o_ref):
    def store(rows, acc):
        o_ref[rows, :] = acc.astype(o_ref.dtype)

    _proj_rows(x_ref, meta_ref, w_ref, o_ref.shape[0], store)


def _proj_pool_kernel(x_ref, meta_ref, w_ref, o_ref, u_scr):
    def store(rows, acc):
        u_scr[rows, :] = acc

    _proj_rows(x_ref, meta_ref, w_ref, u_scr.shape[0], store)
    group = pl.program_id(1)
    tp = u_scr.shape[0]
    pos = lax.broadcasted_iota(jnp.int32, (tp, 1), 0)
    for g, window in enumerate(POOL_WINDOWS):

        @pl.when(group == g)
        def _():
            u = u_scr[...]
            s = u
            shift = 1
            while shift < window:
                s = s + pltpu.roll(s, shift, axis=0)
                shift *= 2
            count = jnp.minimum(pos + 1, window).astype(F32)
            o_ref[...] = (s / count - u).astype(o_ref.dtype)


def _proj_small_kernel(x_ref, meta_ref, w_ref, g_ref, c_ref, kk_ref, wi_ref):
    def store(rows, acc):
        c = acc[:, :KV_RANK]
        ms = jnp.mean(c * c, axis=-1, keepdims=True)
        c_ref[rows, :] = (c * lax.rsqrt(ms + LN_EPS) * g_ref[...]).astype(c_ref.dtype)
        kk_ref[rows, :] = acc[:, KV_RANK:KV_RANK + LANES].astype(kk_ref.dtype)
        wi_ref[rows, :] = acc[:, KV_RANK + LANES:]

    _proj_rows(x_ref, meta_ref, w_ref, c_ref.shape[0], store)


def _batch_block(tp, n):
    return pl.BlockSpec((None, tp, n), lambda b, j: (b, 0, 0))


def _compiler_params(semantics, vmem_limit_bytes=VMEM_LIMIT_BYTES):
    return pltpu.CompilerParams(dimension_semantics=semantics, vmem_limit_bytes=vmem_limit_bytes)


def _with_side_cast(body, n_in, n_out):
    def kernel(*refs):
        side_in, side_out = refs[n_in], refs[n_in + 1 + n_out]
        side_out[...] = side_in[...].astype(side_out.dtype)
        body(*refs[:n_in], *refs[n_in + 1:n_in + 1 + n_out], *refs[n_in + 2 + n_out:])

    return kernel


def _proj_call(body, x, meta, w, tn, extra_in, extra_specs, out_blocks, out_shapes, scratch, name, side):
    B, seq, d = x.shape
    n_j = w.shape[1] // tn
    out_blocks = list(out_blocks) if isinstance(out_blocks, (list, tuple)) else [out_blocks]
    out_shapes = list(out_shapes) if isinstance(out_shapes, (list, tuple)) else [out_shapes]
    slab = side.shape[0] // (B * n_j)
    assert slab * B * n_j == side.shape[0] and slab % PACKED_SUBLANES == 0
    side_block = pl.BlockSpec((slab, side.shape[1]), lambda b, j: (b * n_j + j, 0))
    outs = pl.pallas_call(
        _with_side_cast(body, 3 + len(extra_in), len(out_blocks)),
        grid=(B, n_j),
        in_specs=[_batch_block(seq, d), pl.BlockSpec(meta.shape, lambda b, j: (0, 0)),
                  pl.BlockSpec((d, tn), lambda b, j: (0, j))] + extra_specs + [side_block],
        out_specs=out_blocks + [side_block],
        out_shape=out_shapes + [jax.ShapeDtypeStruct(side.shape, BF16)],
        scratch_shapes=scratch,
        compiler_params=_compiler_params(("parallel", "arbitrary"), BIG_VMEM_LIMIT_BYTES),
        name=name,
    )(x, meta, w, *extra_in, side)
    return outs


def _proj_qq(x, meta, w, tp, side):
    B, n, tn = x.shape[0], w.shape[1], 512
    return _proj_call(_proj_cast_kernel, x, meta, w, tn, [], [],
                      pl.BlockSpec((None, tp, tn), lambda b, j: (b, 0, j)),
                      jax.ShapeDtypeStruct((B, tp, n), BF16), [], "proj_qq", side)


def _proj_pool(x, meta, w, tp, side):
    B, n = x.shape[0], w.shape[1]
    return _proj_call(_proj_pool_kernel, x, meta, w, POOL_GROUP, [], [],
                      pl.BlockSpec((None, tp, POOL_GROUP), lambda b, j: (b, 0, j)),
                      jax.ShapeDtypeStruct((B, tp, n), BF16), [pltpu.VMEM((tp, POOL_GROUP), F32)], "proj_pool", side)


def _proj_small(x, meta, w, kv_g, tp, side):
    B = x.shape[0]
    return _proj_call(_proj_small_kernel, x, meta, w, w.shape[1], [kv_g],
                      [pl.BlockSpec((1, KV_RANK), lambda b, j: (0, 0))],
                      [_batch_block(tp, KV_RANK), _batch_block(tp, LANES), _batch_block(tp, 128)],
                      [jax.ShapeDtypeStruct((B, tp, KV_RANK), BF16), jax.ShapeDtypeStruct((B, tp, LANES), BF16),
                       jax.ShapeDtypeStruct((B, tp, 128), F32)], [], "proj_small", side)


def _attn_kernel(*refs, n_tiles, aliased, **static):
    refs = refs[:6] + refs[6 + int(aliased):]
    o_ref = refs[6]

    @pl.when(pl.program_id(1) < n_tiles)
    def _():
        _attn_tile(*refs, **static)

    @pl.when(pl.program_id(1) >= n_tiles)
    def _():
        o_ref[...] = jnp.zeros(o_ref.shape, o_ref.dtype)


def _attn_tile(qq_ref, wi_ref, kk_ref, c_ref, wuk_ref, diag_ref, o_ref, nb_scr, sc_scr, sct_scr, scf_scr, mb_scr,
               wt_scr, qa_scr, s_scr, m_scr, l_scr, acc_scr, *, top_k, qt, first_tile):
    i = first_tile + pl.program_id(1)
    n_chunks = ((i + 1) * qt - 1) // K_CHUNK + 1
    attn_w = N_HEADS * HEAD_DIM
    scale = HEAD_DIM ** -0.5
    n_pairs = IDX_HEADS // 2
    lanes_are_queries = qt % LANES == 0

    t_col = i * qt + lax.broadcasted_iota(jnp.int32, (qt, 1), 0)
    t_row = i * qt + lax.broadcasted_iota(jnp.int32, (1, qt), 1)
    s_row = lax.broadcasted_iota(jnp.int32, (1, K_CHUNK), 1)
    s_col = lax.broadcasted_iota(jnp.int32, (K_CHUNK, 1), 0)
    lane_half = lax.broadcasted_iota(jnp.int32, (K_CHUNK, LANES), 1) // IDX_DIM

    n_near = nb_scr.shape[0]

    @pl.when((pl.program_id(0) == 0) & (pl.program_id(1) == 0))
    def _():
        for k in range(n_near - 1):
            for h in range(N_HEADS):
                v = jnp.broadcast_to(diag_ref[k, h:h + 1, :], (qt, 2 * K_CHUNK))
                t = pltpu.roll(v, 0, 1, stride=1, stride_axis=0)
                nb_scr[k, h * qt:(h + 1) * qt, :] = t[:, :K_CHUNK]
        nb_scr[n_near - 1] = jnp.zeros(nb_scr.shape[1:], F32)

    for h in range(N_HEADS):
        qa_scr[h * qt:(h + 1) * qt, :] = _dot(
            qq_ref[:, h * HEAD_DIM:(h + 1) * HEAD_DIM], wuk_ref[h]).astype(BF16)
    if lanes_are_queries:
        wt_scr[...] = wi_ref[...].T

    def idx_chunk(j, carry):
        ks = kk_ref[pl.ds(pl.multiple_of(j * K_CHUNK, K_CHUNK), K_CHUNK), :]
        zero = jnp.zeros(ks.shape, ks.dtype)
        k_even = jnp.where(lane_half == 0, ks, zero)
        k_odd = jnp.where(lane_half == 1, ks, zero)
        score = jnp.zeros((K_CHUNK, qt) if lanes_are_queries else (qt, K_CHUNK), F32)
        if lanes_are_queries:
            for p in range(n_pairs):
                q_pair = qq_ref[:, attn_w + p * LANES:attn_w + (p + 1) * LANES]
                for hh, k_half in ((2 * p, k_even), (2 * p + 1, k_odd)):
                    score = score + jnp.maximum(_dot_nt(k_half, q_pair), 0.0) * wt_scr[hh:hh + 1, :]
        else:
            q_pairs = jnp.concatenate(
                [qq_ref[:, attn_w + p * LANES:attn_w + (p + 1) * LANES] for p in range(n_pairs)], axis=0)
            for half, k_half in enumerate((k_even, k_odd)):
                dots = jnp.maximum(_dot_nt(q_pairs, k_half), 0.0)
                for p in range(n_pairs):
                    hh = 2 * p + half
                    score = score + dots[p * qt:(p + 1) * qt] * wi_ref[:, hh:hh + 1]
        s_pos = j * K_CHUNK + (s_col if lanes_are_queries else s_row)
        score = jnp.where(s_pos <= (t_row if lanes_are_queries else t_col), score, NEG_INF)
        if lanes_are_queries:
            sct_scr[j] = score
            sc_scr[j] = score.T
            near = score.astype(BF16)
            bits = lax.bitcast_convert_type(near, jnp.int16)
            below = lax.bitcast_convert_type(bits + jnp.where(bits < 0, jnp.int16(1), jnp.int16(-1)), BF16)
            scf_scr[j] = jnp.where(near.astype(F32) > score, below, near)
        else:
            sc_scr[j] = score
        return carry

    lax.fori_loop(0, n_chunks, idx_chunk, 0)

    k_f = float(top_k)
    idx_bits = int(math.ceil(math.log2(sc_scr.shape[0] * K_CHUNK)))
    per_query = (1, qt) if lanes_are_queries else (qt, 1)
    key_axis = 0 if lanes_are_queries else 1
    s_idx = s_col if lanes_are_queries else s_row
    search_scr = sct_scr if lanes_are_queries else sc_scr

    def fold_chunks(chunk_fn, combine, init):
        if lanes_are_queries:
            acc_rows = 4 * SUBLANES

            def body(j, acc):
                v = chunk_fn(j, search_scr[j]).reshape(K_CHUNK // acc_rows, acc_rows, qt)
                return combine(acc, functools.reduce(combine, [v[t] for t in range(K_CHUNK // acc_rows)]))

            acc = lax.fori_loop(0, n_chunks, body, jnp.full((acc_rows, qt), init, F32))
        else:
            acc = lax.fori_loop(0, n_chunks, lambda j, acc: combine(acc, chunk_fn(j, search_scr[j])),
                                jnp.full((qt, K_CHUNK), init, F32))
        reduce = jnp.sum if combine is jnp.add else jnp.min
        return reduce(acc, axis=key_axis, keepdims=True)

    def count(pred):
        return fold_chunks(lambda j, sc: jnp.where(pred(j, sc), 1.0, 0.0), jnp.add, 0.0)

    def key_to_float(key):
        return lax.bitcast_convert_type(jnp.where(key < 0, key ^ jnp.int32(0x7FFFFFFF), key), F32)

    def count_coarse(cand_f):
        acc_rows = 4 * PACKED_SUBLANES
        cand_b = jnp.broadcast_to(cand_f, (acc_rows, qt)).astype(BF16)

        def body(j, acc):
            hit = jnp.where(scf_scr[j].reshape(K_CHUNK // acc_rows, acc_rows, qt) >= cand_b[None],
                            jnp.ones((), BF16), jnp.zeros((), BF16))
            return acc + functools.reduce(jnp.add, [hit[t] for t in range(K_CHUNK // acc_rows)])

        acc = lax.fori_loop(0, n_chunks, body, jnp.zeros((acc_rows, qt), BF16))
        return jnp.sum(acc.astype(F32), axis=0, keepdims=True)

    def bit_step(coarse, carry):
        key, bit = carry
        cand = key + bit
        cand_f = key_to_float(cand)
        n = count_coarse(cand_f) if coarse else count(lambda j, sc: sc >= cand_f)
        return jnp.where(n >= k_f, cand, key), lax.shift_right_logical(bit, jnp.int32(1))

    state = (jnp.full(per_query, INT_MIN, jnp.int32), jnp.int32(INT_MIN))
    n_coarse = 0
    if lanes_are_queries:
        n_coarse = 16
        state = lax.fori_loop(0, n_coarse, lambda _, c: bit_step(True, c), state)
    thr_key, _ = lax.fori_loop(n_coarse, 32, lambda _, c: bit_step(False, c), state)
    thr = key_to_float(thr_key)

    n_ge = count(lambda j, sc: sc >= thr)
    has_tie = jnp.max(jnp.where((n_ge > k_f) & (thr > NEG_INF), 1.0, 0.0)) > 0.0

    def tie_break():
        def next_value(_, m):
            n_gt = count(lambda j, sc: sc > m)
            above = fold_chunks(lambda j, sc: jnp.where(sc > m, sc, jnp.inf), jnp.minimum, jnp.inf)
            return jnp.where(n_gt >= k_f, above, m)

        m = lax.fori_loop(0, TIE_ROUNDS, next_value, thr)
        need = k_f - count(lambda j, sc: sc > m)

        def step(_, carry):
            cut, bit = carry
            cand = cut + bit
            n_before = count(lambda j, sc: (sc == m) & ((j * K_CHUNK + s_idx) < cand))
            return jnp.where(n_before < need, cand, cut), lax.shift_right_logical(bit, jnp.int32(1))

        cut, _ = lax.fori_loop(0, idx_bits, step,
                               (jnp.zeros(per_query, jnp.int32), jnp.int32(2 ** (idx_bits - 1))))
        return m, cut

    thr, cut = lax.cond(has_tie, tie_break, lambda: (thr, jnp.full(per_query, 2 ** 30, jnp.int32)))

    def to_rows(v):
        if not lanes_are_queries:
            return jnp.broadcast_to(v, (qt, K_CHUNK))
        t = jnp.broadcast_to(v, (qt, qt)).T
        return jnp.concatenate([t] * (K_CHUNK // qt), axis=1)

    thr_b, cut_b = to_rows(thr), to_rows(cut)

    def mask_chunk(j, carry):
        sc = sc_scr[j]
        s_pos = j * K_CHUNK + s_row
        keep = ((sc > thr_b) | ((sc == thr_b) & (s_pos <= cut_b))) & (s_pos <= t_col)
        mb_scr[j] = jnp.where(keep, 0.0, NEG_INF)
        return carry

    lax.fori_loop(0, n_chunks, mask_chunk, 0)

    rows_h = N_HEADS * qt
    lane_fold = lambda v, op: functools.reduce(op, [v[:, k * LANES:(k + 1) * LANES] for k in range(K_CHUNK // LANES)])

    def key_rows(j):
        start = j * K_CHUNK
        return pl.ds(start if isinstance(j, int) else pl.multiple_of(start, K_CHUNK), K_CHUNK)

    def over_chunks(chunk_fn):
        chunk_fn(0, True)
        lax.fori_loop(1, n_chunks, lambda j, carry: (chunk_fn(j, False), carry)[1], 0)

    def logit_chunk(j, first):
        near = jnp.minimum((i * qt) // K_CHUNK - j, n_near - 1)
        s = _dot_nt(qa_scr[...], c_ref[key_rows(j), :]) * scale + nb_scr[near]
        s = s + jnp.concatenate([mb_scr[j]] * N_HEADS, axis=0)
        s_scr[j] = s
        fold = lane_fold(s, jnp.maximum)
        m_scr[...] = fold if first else jnp.maximum(m_scr[...], fold)

    over_chunks(logit_chunk)
    m_b = jnp.broadcast_to(jnp.max(m_scr[...], axis=-1, keepdims=True), (rows_h, LANES))
    m_scr[...] = m_b

    def value_chunk(j, first):
        p = jnp.exp(s_scr[j] - jnp.concatenate([m_scr[...]] * (K_CHUNK // LANES), axis=1))
        pv = _dot(p.astype(BF16), c_ref[key_rows(j), :])
        l_scr[...] = lane_fold(p, jnp.add) if first else l_scr[...] + lane_fold(p, jnp.add)
        acc_scr[...] = pv if first else acc_scr[...] + pv

    over_chunks(value_chunk)
    out = acc_scr[...] * (1.0 / jnp.sum(l_scr[...], axis=-1, keepdims=True))
    for h in range(N_HEADS):
        o_ref[:, h * KV_RANK:(h + 1) * KV_RANK] = out[h * qt:(h + 1) * qt].astype(o_ref.dtype)


def _dsa_attention(qq, wi, kk, c_kv, wuk, rel_bias, top_k, seq):
    B, tp, _ = qq.shape
    n_main = (seq + N_META) // Q_TILE
    assert (seq + N_META) - n_main * Q_TILE <= TAIL_TILE and Q_TILE % K_CHUNK == 0 and K_CHUNK % TAIL_TILE == 0
    n_chunks_max = tp // K_CHUNK
    width = N_HEADS * KV_RANK

    def call(qt, first_tile, n_tiles, n_fill, prev):
        rows_h = N_HEADS * qt
        diag = _near_bias_diagonals(rel_bias, qt)
        tile = lambda i: first_tile + jnp.minimum(i, n_tiles - 1)
        in_specs = [
            pl.BlockSpec((None, qt, qq.shape[2]), lambda b, i: (b, tile(i), 0)),
            pl.BlockSpec((None, qt, 128), lambda b, i: (b, tile(i), 0)),
            pl.BlockSpec((None, tp, LANES), lambda b, i: (b, 0, 0)),
            pl.BlockSpec((None, tp, KV_RANK), lambda b, i: (b, 0, 0)),
            pl.BlockSpec(wuk.shape, lambda b, i: (0, 0, 0)),
            pl.BlockSpec(diag.shape, lambda b, i: (0, 0, 0)),
        ]
        args = [qq, wi, kk, c_kv, wuk, diag]
        if prev is not None:
            in_specs.append(pl.BlockSpec(memory_space=pl.ANY))
            args.append(prev)
        return pl.pallas_call(
            functools.partial(_attn_kernel, n_tiles=n_tiles, aliased=prev is not None,
                              top_k=top_k, qt=qt, first_tile=first_tile),
            grid=(B, n_tiles + n_fill),
            in_specs=in_specs,
            out_specs=pl.BlockSpec((None, qt, width), lambda b, i: (b, first_tile + i, 0)),
            out_shape=jax.ShapeDtypeStruct((B, tp, width), BF16),
            input_output_aliases={} if prev is None else {len(args) - 1: 0},
            scratch_shapes=[pltpu.VMEM((diag.shape[0] + 1, rows_h, K_CHUNK), F32),
                            pltpu.VMEM((n_chunks_max, qt, K_CHUNK), F32),
                            pltpu.VMEM((n_chunks_max, K_CHUNK, qt), F32),
                            pltpu.VMEM((n_chunks_max, K_CHUNK, qt), BF16),
                            pltpu.VMEM((n_chunks_max, qt, K_CHUNK), F32),
                            pltpu.VMEM((LANES, qt), F32),
                            pltpu.VMEM((rows_h, KV_RANK), BF16),
                            pltpu.VMEM((n_chunks_max, rows_h, K_CHUNK), F32),
                            pltpu.VMEM((rows_h, LANES), F32),
                            pltpu.VMEM((rows_h, LANES), F32),
                            pltpu.VMEM((rows_h, KV_RANK), F32)],
            compiler_params=_compiler_params(("arbitrary", "arbitrary")),
            name="dsa_attention" if prev is None else "dsa_attention_tail",
        )(*args)

    o_lat = call(Q_TILE, 0, n_main, tp // Q_TILE - n_main, None)
    return call(TAIL_TILE, n_main * Q_TILE // TAIL_TILE, 1, 0, o_lat)


def _mix_window_start(r, seq):
    return min(max(r * MIX_ROWS - N_META, 0), seq - MIX_ROWS)


def _residual_rows(x_ref, meta_ref, h_scr, seq):
    r = pl.program_id(1)
    n_real = -(-(seq + N_META) // MIX_ROWS)
    for rv in range(n_real):

        @pl.when(r == rv)
        def _(rv=rv):
            skip = rv * MIX_ROWS - N_META - _mix_window_start(rv, seq)
            if rv == 0:
                h_scr[:N_META, :] = meta_ref[...]
                h_scr[N_META:, :] = x_ref[:MIX_ROWS - N_META, :]
            elif skip == 0:
                h_scr[...] = x_ref[...]
            else:
                h_scr[:MIX_ROWS - skip, :] = x_ref[skip:, :]
                h_scr[MIX_ROWS - skip:, :] = jnp.zeros((skip, h_scr.shape[1]), F32)

    @pl.when(r >= n_real)
    def _():
        h_scr[...] = jnp.zeros(h_scr.shape, F32)


def _mix_kernel(ol_ref, pd_ref, x_ref, meta_ref, wuv_ref, wp_ref, ps_ref, wo_ref, g_ref, b_ref, h1_ref, h1b_ref,
                h_scr, *, seq):
    _residual_rows(x_ref, meta_ref, h_scr, seq)
    attn = [_dot(ol_ref[:, h * KV_RANK:(h + 1) * KV_RANK], wuv_ref[h]) for h in range(N_HEADS)]
    pool = [_dot(pd_ref[:, g * POOL_GROUP:(g + 1) * POOL_GROUP], wp_ref[g]) for g in range(len(POOL_WINDOWS))]
    pool = jnp.concatenate(pool, axis=-1) * ps_ref[...]
    cat = jnp.concatenate(attn + [pool], axis=-1).astype(BF16)
    y = ALPHA * h_scr[...] + _dot(cat, wo_ref[...])
    h1 = _layer_norm(y, g_ref[...], b_ref[...])
    h1_ref[...] = h1
    h1b_ref[...] = h1.astype(h1b_ref.dtype)


def _mix_ln1(o_lat, pool_diff, x, meta, wuv, wpool, pool_scale, wo, g, b):
    B, seq, d = x.shape
    n_rows = seq + N_META
    assert n_rows % MIX_ROWS == 0
    row = lambda n: pl.BlockSpec((None, MIX_ROWS, n), lambda bi, r: (bi, r, 0))
    full = lambda a: pl.BlockSpec(a.shape, lambda bi, r: (0,) * a.ndim, pipeline_mode=pl.Buffered(1))
    window = pl.BlockSpec(
        (None, pl.Element(MIX_ROWS), pl.Element(d)),
        lambda bi, r: (bi, pl.multiple_of(jnp.clip(r * MIX_ROWS - N_META, 0, seq - MIX_ROWS), N_META), 0))
    return pl.pallas_call(
        functools.partial(_mix_kernel, seq=seq),
        grid=(B, n_rows // MIX_ROWS),
        in_specs=[row(o_lat.shape[2]), row(pool_diff.shape[2]), window, full(meta),
                  full(wuv), full(wpool), full(pool_scale), full(wo), full(g), full(b)],
        out_specs=[row(d), row(d)],
        out_shape=[jax.ShapeDtypeStruct((B, n_rows, d), F32), jax.ShapeDtypeStruct((B, n_rows, d), BF16)],
        scratch_shapes=[pltpu.VMEM((MIX_ROWS, d), F32)],
        compiler_params=_compiler_params(("parallel", "arbitrary"), BIG_VMEM_LIMIT_BYTES),
        name="mix_ln1",
    )(o_lat, pool_diff, x, meta, wuv, wpool, pool_scale, wo, g, b)


def _gelu_tanh(x):
    return 0.5 * x * (1.0 + jnp.tanh(math.sqrt(2.0 / math.pi) * (x + 0.044715 * (x * x * x))))


def _ffn_kernel(hw_ref, hres_ref, wa_ref, wg_ref, cwa_ref, cwg_ref, cba_ref, cbg_ref, wd_ref, g_ref, b_ref, o_ref,
                za_scr, zg_scr):
    c = pl.program_id(2)
    n_c = pl.num_programs(2) - 1

    def up(slot):
        x = hw_ref[...]
        za_scr[slot] = _dot(x, wa_ref[...])
        zg_scr[slot] = _dot(x, wg_ref[...])

    def conv(z, cw_ref, cb_ref):
        cw = cw_ref[...]
        n = z.shape[0]
        out = z[HALO - 2:n - 2] * cw[0:1] + z[HALO - 1:n - 1] * cw[1:2] + z[HALO:] * cw[2:3]
        return out + cb_ref[...]

    def down(slot):
        a = conv(za_scr[slot], cwa_ref, cba_ref)
        gate = conv(zg_scr[slot], cwg_ref, cbg_ref)
        act = (_gelu_tanh(a) * gate).astype(BF16)
        o_ref[...] += _dot(act, wd_ref[...])

    @pl.when(c == 0)
    def _():
        o_ref[...] = jnp.zeros(o_ref.shape, o_ref.dtype)
        up(0)

    @pl.when((c > 0) & (c < n_c))
    def _():
        up(c % 2)
        down((c - 1) % 2)

    @pl.when(c == n_c)
    def _():
        down((c - 1) % 2)
        o_ref[...] = _layer_norm(ALPHA * hres_ref[...] + o_ref[...], g_ref[...], b_ref[...])


def _ffn_ln2(h1, h1b, w_up, conv_w, conv_b, w_down, g, b, seq):
    B, tp, d = h1.shape
    d_ff = w_down.shape[0]
    n_c = d_ff // FFN_COLS
    up_c = lambda c: jnp.minimum(c, n_c - 1)
    dn_c = lambda c: jnp.maximum(c - 1, 0)
    vec = lambda off: pl.BlockSpec((1, FFN_COLS), lambda bi, r, c: (0, dn_c(c) + off))
    return pl.pallas_call(
        _ffn_kernel,
        grid=(B, seq // FFN_ROWS, n_c + 1),
        in_specs=[
            pl.BlockSpec((None, pl.Element(FFN_ROWS + HALO), pl.Element(d)),
                         lambda bi, r, c: (bi, r * FFN_ROWS + N_META - HALO, 0)),
            pl.BlockSpec((None, pl.Element(FFN_ROWS), pl.Element(d)),
                         lambda bi, r, c: (bi, pl.multiple_of(r * FFN_ROWS + N_META, N_META), 0)),
            pl.BlockSpec((d, FFN_COLS), lambda bi, r, c: (0, up_c(c))),
            pl.BlockSpec((d, FFN_COLS), lambda bi, r, c: (0, up_c(c) + n_c)),
            pl.BlockSpec((CONV_WIDTH, FFN_COLS), lambda bi, r, c: (0, dn_c(c))),
            pl.BlockSpec((CONV_WIDTH, FFN_COLS), lambda bi, r, c: (0, dn_c(c) + n_c)),
            vec(0), vec(n_c),
            pl.BlockSpec((FFN_COLS, d), lambda bi, r, c: (dn_c(c), 0)),
            pl.BlockSpec((1, d), lambda bi, r, c: (0, 0)),
            pl.BlockSpec((1, d), lambda bi, r, c: (0, 0)),
        ],
        out_specs=pl.BlockSpec((None, FFN_ROWS, d), lambda bi, r, c: (bi, r, 0), pipeline_mode=pl.Buffered(1)),
        out_shape=jax.ShapeDtypeStruct((B, seq, d), F32),
        scratch_shapes=[pltpu.VMEM((2, FFN_ROWS + HALO, FFN_COLS), F32),
                        pltpu.VMEM((2, FFN_ROWS + HALO, FFN_COLS), F32)],
        compiler_params=_compiler_params(("parallel", "parallel", "arbitrary"), BIG_VMEM_LIMIT_BYTES),
        name="ffn_ln2",
    )(h1b, h1, w_up, w_up, conv_w, conv_w, conv_b, conv_b, w_down, g, b)


def _t5_bucket_table(n):
    dist = np.arange(n, dtype=np.int32)
    max_exact = REL_BUCKETS // 2
    d_f = np.maximum(dist, 1).astype(np.float32)
    large = max_exact + (np.log(d_f / np.float32(max_exact)) / np.float32(math.log(REL_MAX_DIST / max_exact))
                         * np.float32(REL_BUCKETS - max_exact)).astype(np.int32)
    return np.where(dist < max_exact, dist, np.minimum(large, REL_BUCKETS - 1))


def _near_bias_diagonals(rel_bias, qt):
    assert qt <= K_CHUNK
    probe = _t5_bucket_table(4 * REL_MAX_DIST)
    first_far = int(np.argmax(probe == REL_BUCKETS - 1))
    assert np.all(probe[first_far:] == REL_BUCKETS - 1)
    n_real = -(-(first_far + K_CHUNK - 1) // K_CHUNK)
    buckets = _t5_bucket_table((n_real + 1) * K_CHUNK)
    period = 2 * K_CHUNK
    u = np.arange(period)
    k = np.arange(n_real)[:, None]
    dist = np.where(u < K_CHUNK, k * K_CHUNK - u, k * K_CHUNK + period - u)
    idx = buckets[np.clip(dist, 0, len(buckets) - 1)]
    rel = rel_bias.astype(F32) - rel_bias[REL_BUCKETS - 1:].astype(F32)
    return jnp.transpose(rel[idx], (0, 2, 1))


def kernel(x, meta, rel_bias, w_in, kv_norm_g, w_uk, w_uv, w_pool, pool_scale, w_o, ln1_g, ln1_b, w_up, conv_w,
           conv_b, w_down, ln2_g, ln2_b):
    B, S, D = x.shape
    assert w_in.shape[0] == DEPTH and S % FFN_ROWS == 0
    T = S + N_META
    tp = -(-T // ROW_ALIGN) * ROW_ALIGN
    assert tp - T >= max(POOL_WINDOWS)
    top_k = min(TOPK_MAX, S // 4)
    assert top_k <= K_CHUNK

    attn_w = N_HEADS * HEAD_DIM
    idx_w = IDX_HEADS * IDX_DIM
    o_c, o_qi = attn_w, attn_w + KV_RANK
    o_ki = o_qi + idx_w
    o_wi = o_ki + IDX_DIM
    o_u = o_wi + IDX_HEADS
    w = w_in[0]
    w_qq = jnp.concatenate([w[:, :o_c], w[:, o_qi:o_ki] * (IDX_DIM ** -0.5)], axis=1).astype(BF16)
    w_u = w[:, o_u:].astype(BF16)
    w_small = jnp.concatenate([w[:, o_c:o_qi], w[:, o_ki:o_wi], w[:, o_ki:o_wi],
                               w[:, o_wi:o_u] * (IDX_HEADS ** -0.5), jnp.zeros((D, 128 - IDX_HEADS), w.dtype)],
                              axis=1).astype(BF16)

    qq, w_down_b = _proj_qq(x, meta, w_qq, tp, w_down[0])
    pool_diff, w_up_b = _proj_pool(x, meta, w_u, tp, w_up[0])
    c_kv, kk, wi, w_o_b = _proj_small(x, meta, w_small, kv_norm_g[0].reshape(1, KV_RANK), tp, w_o[0])

    wuk = jnp.transpose(w_uk[0], (1, 2, 0)).astype(BF16)
    o_lat = _dsa_attention(qq, wi, kk, c_kv, wuk, rel_bias, top_k, S)

    wuv = jnp.transpose(w_uv[0], (1, 0, 2)).astype(BF16)
    h1, h1b = _mix_ln1(o_lat, pool_diff, x, meta, wuv, w_pool[0].astype(BF16), pool_scale[0].reshape(1, -1),
                       w_o_b, ln1_g[0].reshape(1, D), ln1_b[0].reshape(1, D))

    return _ffn_ln2(h1, h1b, w_up_b, conv_w[0], conv_b[0].reshape(1, -1), w_down_b,
                    ln2_g[0].reshape(1, D), ln2_b[0].reshape(1, D), S)
```

```python
import functools
import math

import numpy as np
import jax
import jax.numpy as jnp
from jax import lax
from jax.experimental import pallas as pl
from jax.experimental.pallas import tpu as pltpu

F32 = jnp.float32
BF16 = jnp.bfloat16

N_META = 16
N_HEADS = 8
HEAD_DIM = 128
KV_RANK = 256
IDX_HEADS = 16
IDX_DIM = 64
TOPK_MAX = 256
POOL_WINDOWS = (2, 4, 8, 16)
POOL_GROUP = 256
CONV_WIDTH = 3
REL_BUCKETS = 32
REL_MAX_DIST = 128
DEPTH = 1
ALPHA = (2.0 * DEPTH) ** 0.25
LN_EPS = 1e-5
NEG_INF = -1e30

VMEM_LIMIT_BYTES = 56 * 1024 * 1024
BIG_VMEM_LIMIT_BYTES = 60 * 1024 * 1024
SUBLANES = 8
LANES = 128
ROW_ALIGN = 256
PROJ_ROWS = 1024
Q_TILE = 256
TAIL_TILE = 16
K_CHUNK = 256
MIX_ROWS = 688
FFN_ROWS = 1024
FFN_COLS = 512
HALO = 16
INT_MIN = -(2 ** 31)
PACKED_SUBLANES = 16
TIE_ROUNDS = 8


def _dot(a, b):
    return jnp.dot(a, b, preferred_element_type=F32)


def _dot_nt(a, b):
    return lax.dot_general(a, b, (((1,), (1,)), ((), ())), preferred_element_type=F32)


def _layer_norm(y, g, b):
    mu = jnp.mean(y, axis=-1, keepdims=True)
    yc = y - mu
    var = jnp.mean(yc * yc, axis=-1, keepdims=True)
    return yc * lax.rsqrt(var + LN_EPS) * g + b


def _proj_rows(x_ref, meta_ref, w_ref, tp, store):
    seq = x_ref.shape[0]
    store(pl.ds(0, N_META), _dot_nt(meta_ref[...].astype(BF16), w_ref[...]))
    for r in range(seq // PROJ_ROWS):
        acc = _dot_nt(x_ref[r * PROJ_ROWS:(r + 1) * PROJ_ROWS, :].astype(BF16), w_ref[...])
        store(pl.ds(N_META + r * PROJ_ROWS, PROJ_ROWS), acc)
    n_pad = tp - seq - N_META
    store(pl.ds(seq + N_META, n_pad), jnp.zeros((n_pad, w_ref.shape[0]), F32))


def _proj_cast_kernel(x_ref, meta_ref, w_ref, o_ref):
    def store(rows, acc):
        o_ref[rows, :] = acc.astype(o_ref.dtype)

    _proj_rows(x_ref, meta_ref, w_ref, o_ref.shape[0], store)


def _proj_pool_kernel(x_ref, meta_ref, w_ref, o_ref, u_scr):
    def store(rows, acc):
        u_scr[rows, :] = acc

    _proj_rows(x_ref, meta_ref, w_ref, u_scr.shape[0], store)
    group = pl.program_id(1)
    tp = u_scr.shape[0]
    pos = lax.broadcasted_iota(jnp.int32, (tp, 1), 0)
    for g, window in enumerate(POOL_WINDOWS):

        @pl.when(group == g)
        def _():
            u = u_scr[...]
            s = u
            shift = 1
            while shift < window:
                s = s + pltpu.roll(s, shift, axis=0)
                shift *= 2
            count = jnp.minimum(pos + 1, window).astype(F32)
            o_ref[...] = (s / count - u).astype(o_ref.dtype)


def _proj_small_kernel(x_ref, meta_ref, w_ref, g_ref, c_ref, kk_ref, wi_ref):
    def store(rows, acc):
        c = acc[:, :KV_RANK]
        ms = jnp.mean(c * c, axis=-1, keepdims=True)
        c_ref[rows, :] = (c * lax.rsqrt(ms + LN_EPS) * g_ref[...]).astype(c_ref.dtype)
        kk_ref[rows, :] = acc[:, KV_RANK:KV_RANK + LANES].astype(kk_ref.dtype)
        wi_ref[rows, :] = acc[:, KV_RANK + LANES:]

    _proj_rows(x_ref, meta_ref, w_ref, c_ref.shape[0], store)


def _batch_block(tp, n):
    return pl.BlockSpec((None, tp, n), lambda b, j: (b, 0, 0))


def _compiler_params(semantics, vmem_limit_bytes=VMEM_LIMIT_BYTES):
    return pltpu.CompilerParams(dimension_semantics=semantics, vmem_limit_bytes=vmem_limit_bytes)


def _with_side_cast(body, n_in, n_out):
    def kernel(*refs):
        side_in, side_out = refs[n_in], refs[n_in + 1 + n_out]
        side_out[...] = side_in[...].astype(side_out.dtype)
        body(*refs[:n_in], *refs[n_in + 1:n_in + 1 + n_out], *refs[n_in + 2 + n_out:])

    return kernel


def _proj_call(body, x, meta, w, tn, extra_in, extra_specs, out_blocks, out_shapes, scratch, name, side):
    B, seq, d = x.shape
    n_j = w.shape[0] // tn
    out_blocks = list(out_blocks) if isinstance(out_blocks, (list, tuple)) else [out_blocks]
    out_shapes = list(out_shapes) if isinstance(out_shapes, (list, tuple)) else [out_shapes]
    slab = side.shape[0] // (B * n_j)
    assert slab * B * n_j == side.shape[0] and slab % PACKED_SUBLANES == 0
    side_block = pl.BlockSpec((slab, side.shape[1]), lambda b, j: (b * n_j + j, 0))
    outs = pl.pallas_call(
        _with_side_cast(body, 3 + len(extra_in), len(out_blocks)),
        grid=(B, n_j),
        in_specs=[_batch_block(seq, d), pl.BlockSpec(meta.shape, lambda b, j: (0, 0)),
                  pl.BlockSpec((tn, d), lambda b, j: (j, 0))] + extra_specs + [side_block],
        out_specs=out_blocks + [side_block],
        out_shape=out_shapes + [jax.ShapeDtypeStruct(side.shape, BF16)],
        scratch_shapes=scratch,
        compiler_params=_compiler_params(("parallel", "arbitrary"), BIG_VMEM_LIMIT_BYTES),
        name=name,
    )(x, meta, w, *extra_in, side)
    return outs


def _proj_qq(x, meta, w, tp, side):
    B, n, tn = x.shape[0], w.shape[0], 512
    return _proj_call(_proj_cast_kernel, x, meta, w, tn, [], [],
                      pl.BlockSpec((None, tp, tn), lambda b, j: (b, 0, j)),
                      jax.ShapeDtypeStruct((B, tp, n), BF16), [], "proj_qq", side)


def _proj_pool(x, meta, w, tp, side):
    B, n = x.shape[0], w.shape[0]
    return _proj_call(_proj_pool_kernel, x, meta, w, POOL_GROUP, [], [],
                      pl.BlockSpec((None, tp, POOL_GROUP), lambda b, j: (b, 0, j)),
                      jax.ShapeDtypeStruct((B, tp, n), BF16), [pltpu.VMEM((tp, POOL_GROUP), F32)], "proj_pool", side)


def _proj_small(x, meta, w, kv_g, tp, side):
    B = x.shape[0]
    return _proj_call(_proj_small_kernel, x, meta, w, w.shape[0], [kv_g],
                      [pl.BlockSpec((1, KV_RANK), lambda b, j: (0, 0))],
                      [_batch_block(tp, KV_RANK), _batch_block(tp, LANES), _batch_block(tp, 128)],
                      [jax.ShapeDtypeStruct((B, tp, KV_RANK), BF16), jax.ShapeDtypeStruct((B, tp, LANES), BF16),
                       jax.ShapeDtypeStruct((B, tp, 128), F32)], [], "proj_small", side)


def _attn_kernel(*refs, n_tiles, aliased, **static):
    refs = refs[:6] + refs[6 + int(aliased):]
    o_ref = refs[6]

    @pl.when(pl.program_id(1) < n_tiles)
    def _():
        _attn_tile(*refs, **static)

    @pl.when(pl.program_id(1) >= n_tiles)
    def _():
        o_ref[...] = jnp.zeros(o_ref.shape, o_ref.dtype)


def _attn_tile(qq_ref, wi_ref, kk_ref, c_ref, wuk_ref, diag_ref, o_ref, nb_scr, sc_scr, sct_scr, scf_scr, mb_scr,
               wt_scr, qa_scr, s_scr, m_scr, l_scr, acc_scr, *, top_k, qt, first_tile):
    i = first_tile + pl.program_id(1)
    n_chunks = ((i + 1) * qt - 1) // K_CHUNK + 1
    attn_w = N_HEADS * HEAD_DIM
    scale = HEAD_DIM ** -0.5
    n_pairs = IDX_HEADS // 2
    lanes_are_queries = qt % LANES == 0

    t_col = i * qt + lax.broadcasted_iota(jnp.int32, (qt, 1), 0)
    t_row = i * qt + lax.broadcasted_iota(jnp.int32, (1, qt), 1)
    s_row = lax.broadcasted_iota(jnp.int32, (1, K_CHUNK), 1)
    s_col = lax.broadcasted_iota(jnp.int32, (K_CHUNK, 1), 0)
    lane_half = lax.broadcasted_iota(jnp.int32, (K_CHUNK, LANES), 1) // IDX_DIM

    n_near = nb_scr.shape[0]

    @pl.when((pl.program_id(0) == 0) & (pl.program_id(1) == 0))
    def _():
        for k in range(n_near - 1):
            for h in range(N_HEADS):
                v = jnp.broadcast_to(diag_ref[k, h:h + 1, :], (qt, 2 * K_CHUNK))
                t = pltpu.roll(v, 0, 1, stride=1, stride_axis=0)
                nb_scr[k, h * qt:(h + 1) * qt, :] = t[:, :K_CHUNK]
        nb_scr[n_near - 1] = jnp.zeros(nb_scr.shape[1:], F32)

    for h in range(N_HEADS):
        qa_scr[h * qt:(h + 1) * qt, :] = _dot(
            qq_ref[:, h * HEAD_DIM:(h + 1) * HEAD_DIM], wuk_ref[h]).astype(BF16)
    if lanes_are_queries:
        wt_scr[...] = wi_ref[...].T

    def idx_chunk(j, carry):
        ks = kk_ref[pl.ds(pl.multiple_of(j * K_CHUNK, K_CHUNK), K_CHUNK), :]
        zero = jnp.zeros(ks.shape, ks.dtype)
        k_even = jnp.where(lane_half == 0, ks, zero)
        k_odd = jnp.where(lane_half == 1, ks, zero)
        score = jnp.zeros((K_CHUNK, qt) if lanes_are_queries else (qt, K_CHUNK), F32)
        if lanes_are_queries:
            for p in range(n_pairs):
                q_pair = qq_ref[:, attn_w + p * LANES:attn_w + (p + 1) * LANES]
                for hh, k_half in ((2 * p, k_even), (2 * p + 1, k_odd)):
                    score = score + jnp.maximum(_dot_nt(k_half, q_pair), 0.0) * wt_scr[hh:hh + 1, :]
        else:
            q_pairs = jnp.concatenate(
                [qq_ref[:, attn_w + p * LANES:attn_w + (p + 1) * LANES] for p in range(n_pairs)], axis=0)
            for half, k_half in enumerate((k_even, k_odd)):
                dots = jnp.maximum(_dot_nt(q_pairs, k_half), 0.0)
                for p in range(n_pairs):
                    hh = 2 * p + half
                    score = score + dots[p * qt:(p + 1) * qt] * wi_ref[:, hh:hh + 1]
        s_pos = j * K_CHUNK + (s_col if lanes_are_queries else s_row)
        score = jnp.where(s_pos <= (t_row if lanes_are_queries else t_col), score, NEG_INF)
        if lanes_are_queries:
            sct_scr[j] = score
            sc_scr[j] = score.T
            near = score.astype(BF16)
            bits = lax.bitcast_convert_type(near, jnp.int16)
            below = lax.bitcast_convert_type(bits + jnp.where(bits < 0, jnp.int16(1), jnp.int16(-1)), BF16)
            scf_scr[j] = jnp.where(near.astype(F32) > score, below, near)
        else:
            sc_scr[j] = score
        return carry

    lax.fori_loop(0, n_chunks, idx_chunk, 0)

    k_f = float(top_k)
    idx_bits = int(math.ceil(math.log2(sc_scr.shape[0] * K_CHUNK)))
    per_query = (1, qt) if lanes_are_queries else (qt, 1)
    key_axis = 0 if lanes_are_queries else 1
    s_idx = s_col if lanes_are_queries else s_row
    search_scr = sct_scr if lanes_are_queries else sc_scr

    def fold_chunks(chunk_fn, combine, init):
        if lanes_are_queries:
            acc_rows = 4 * SUBLANES

            def body(j, acc):
                v = chunk_fn(j, search_scr[j]).reshape(K_CHUNK // acc_rows, acc_rows, qt)
                return combine(acc, functools.reduce(combine, [v[t] for t in range(K_CHUNK // acc_rows)]))

            acc = lax.fori_loop(0, n_chunks, body, jnp.full((acc_rows, qt), init, F32))
        else:
            acc = lax.fori_loop(0, n_chunks, lambda j, acc: combine(acc, chunk_fn(j, search_scr[j])),
                                jnp.full((qt, K_CHUNK), init, F32))
        reduce = jnp.sum if combine is jnp.add else jnp.min
        return reduce(acc, axis=key_axis, keepdims=True)

    def count(pred):
        return fold_chunks(lambda j, sc: jnp.where(pred(j, sc), 1.0, 0.0), jnp.add, 0.0)

    def key_to_float(key):
        return lax.bitcast_convert_type(jnp.where(key < 0, key ^ jnp.int32(0x7FFFFFFF), key), F32)

    def count_coarse(cand_f):
        acc_rows = 4 * PACKED_SUBLANES
        cand_b = jnp.broadcast_to(cand_f, (acc_rows, qt)).astype(BF16)

        def body(j, acc):
            hit = jnp.where(scf_scr[j].reshape(K_CHUNK // acc_rows, acc_rows, qt) >= cand_b[None],
                            jnp.ones((), BF16), jnp.zeros((), BF16))
            return acc + functools.reduce(jnp.add, [hit[t] for t in range(K_CHUNK // acc_rows)])

        acc = lax.fori_loop(0, n_chunks, body, jnp.zeros((acc_rows, qt), BF16))
        return jnp.sum(acc.astype(F32), axis=0, keepdims=True)

    def bit_step(coarse, carry):
        key, bit = carry
        cand = key + bit
        cand_f = key_to_float(cand)
        n = count_coarse(cand_f) if coarse else count(lambda j, sc: sc >= cand_f)
        return jnp.where(n >= k_f, cand, key), lax.shift_right_logical(bit, jnp.int32(1))

    state = (jnp.full(per_query, INT_MIN, jnp.int32), jnp.int32(INT_MIN))
    n_coarse = 0
    if lanes_are_queries:
        n_coarse = 16
        state = lax.fori_loop(0, n_coarse, lambda _, c: bit_step(True, c), state)
    thr_key, _ = lax.fori_loop(n_coarse, 32, lambda _, c: bit_step(False, c), state)
    thr = key_to_float(thr_key)

    n_ge = count(lambda j, sc: sc >= thr)
    has_tie = jnp.max(jnp.where((n_ge > k_f) & (thr > NEG_INF), 1.0, 0.0)) > 0.0

    def tie_break():
        def next_value(_, m):
            n_gt = count(lambda j, sc: sc > m)
            above = fold_chunks(lambda j, sc: jnp.where(sc > m, sc, jnp.inf), jnp.minimum, jnp.inf)
            return jnp.where(n_gt >= k_f, above, m)

        m = lax.fori_loop(0, TIE_ROUNDS, next_value, thr)
        need = k_f - count(lambda j, sc: sc > m)

        def step(_, carry):
            cut, bit = carry
            cand = cut + bit
            n_before = count(lambda j, sc: (sc == m) & ((j * K_CHUNK + s_idx) < cand))
            return jnp.where(n_before < need, cand, cut), lax.shift_right_logical(bit, jnp.int32(1))

        cut, _ = lax.fori_loop(0, idx_bits, step,
                               (jnp.zeros(per_query, jnp.int32), jnp.int32(2 ** (idx_bits - 1))))
        return m, cut

    thr, cut = lax.cond(has_tie, tie_break, lambda: (thr, jnp.full(per_query, 2 ** 30, jnp.int32)))

    def to_rows(v):
        if not lanes_are_queries:
            return jnp.broadcast_to(v, (qt, K_CHUNK))
        t = jnp.broadcast_to(v, (qt, qt)).T
        return jnp.concatenate([t] * (K_CHUNK // qt), axis=1)

    thr_b, cut_b = to_rows(thr), to_rows(cut)

    def mask_chunk(j, carry):
        sc = sc_scr[j]
        s_pos = j * K_CHUNK + s_row
        keep = ((sc > thr_b) | ((sc == thr_b) & (s_pos <= cut_b))) & (s_pos <= t_col)
        mb_scr[j] = jnp.where(keep, 0.0, NEG_INF)
        return carry

    lax.fori_loop(0, n_chunks, mask_chunk, 0)

    rows_h = N_HEADS * qt
    lane_fold = lambda v, op: functools.reduce(op, [v[:, k * LANES:(k + 1) * LANES] for k in range(K_CHUNK // LANES)])

    def key_rows(j):
        start = j * K_CHUNK
        return pl.ds(start if isinstance(j, int) else pl.multiple_of(start, K_CHUNK), K_CHUNK)

    def over_chunks(chunk_fn):
        chunk_fn(0, True)
        lax.fori_loop(1, n_chunks, lambda j, carry: (chunk_fn(j, False), carry)[1], 0)

    def logit_chunk(j, first):
        near = jnp.minimum((i * qt) // K_CHUNK - j, n_near - 1)
        s = _dot_nt(qa_scr[...], c_ref[key_rows(j), :]) * scale + nb_scr[near]
        s = s + jnp.concatenate([mb_scr[j]] * N_HEADS, axis=0)
        s_scr[j] = s
        fold = lane_fold(s, jnp.maximum)
        m_scr[...] = fold if first else jnp.maximum(m_scr[...], fold)

    over_chunks(logit_chunk)
    m_b = jnp.broadcast_to(jnp.max(m_scr[...], axis=-1, keepdims=True), (rows_h, LANES))
    m_scr[...] = m_b

    def value_chunk(j, first):
        p = jnp.exp(s_scr[j] - jnp.concatenate([m_scr[...]] * (K_CHUNK // LANES), axis=1))
        pv = _dot(p.astype(BF16), c_ref[key_rows(j), :])
        l_scr[...] = lane_fold(p, jnp.add) if first else l_scr[...] + lane_fold(p, jnp.add)
        acc_scr[...] = pv if first else acc_scr[...] + pv

    over_chunks(value_chunk)
    out = acc_scr[...] * (1.0 / jnp.sum(l_scr[...], axis=-1, keepdims=True))
    for h in range(N_HEADS):
        o_ref[:, h * KV_RANK:(h + 1) * KV_RANK] = out[h * qt:(h + 1) * qt].astype(o_ref.dtype)


def _dsa_attention(qq, wi, kk, c_kv, wuk, rel_bias, top_k, seq):
    B, tp, _ = qq.shape
    n_main = (seq + N_META) // Q_TILE
    assert (seq + N_META) - n_main * Q_TILE <= TAIL_TILE and Q_TILE % K_CHUNK == 0 and K_CHUNK % TAIL_TILE == 0
    n_chunks_max = tp // K_CHUNK
    width = N_HEADS * KV_RANK

    def call(qt, first_tile, n_tiles, n_fill, prev):
        rows_h = N_HEADS * qt
        diag = _near_bias_diagonals(rel_bias, qt)
        tile = lambda i: first_tile + jnp.minimum(i, n_tiles - 1)
        in_specs = [
            pl.BlockSpec((None, qt, qq.shape[2]), lambda b, i: (b, tile(i), 0)),
            pl.BlockSpec((None, qt, 128), lambda b, i: (b, tile(i), 0)),
            pl.BlockSpec((None, tp, LANES), lambda b, i: (b, 0, 0)),
            pl.BlockSpec((None, tp, KV_RANK), lambda b, i: (b, 0, 0)),
            pl.BlockSpec(wuk.shape, lambda b, i: (0, 0, 0)),
            pl.BlockSpec(diag.shape, lambda b, i: (0, 0, 0)),
        ]
        args = [qq, wi, kk, c_kv, wuk, diag]
        if prev is not None:
            in_specs.append(pl.BlockSpec(memory_space=pl.ANY))
            args.append(prev)
        return pl.pallas_call(
            functools.partial(_attn_kernel, n_tiles=n_tiles, aliased=prev is not None,
                              top_k=top_k, qt=qt, first_tile=first_tile),
            grid=(B, n_tiles + n_fill),
            in_specs=in_specs,
            out_specs=pl.BlockSpec((None, qt, width), lambda b, i: (b, first_tile + i, 0)),
            out_shape=jax.ShapeDtypeStruct((B, tp, width), BF16),
            input_output_aliases={} if prev is None else {len(args) - 1: 0},
            scratch_shapes=[pltpu.VMEM((diag.shape[0] + 1, rows_h, K_CHUNK), F32),
                            pltpu.VMEM((n_chunks_max, qt, K_CHUNK), F32),
                            pltpu.VMEM((n_chunks_max, K_CHUNK, qt), F32),
                            pltpu.VMEM((n_chunks_max, K_CHUNK, qt), BF16),
                            pltpu.VMEM((n_chunks_max, qt, K_CHUNK), F32),
                            pltpu.VMEM((LANES, qt), F32),
                            pltpu.VMEM((rows_h, KV_RANK), BF16),
                            pltpu.VMEM((n_chunks_max, rows_h, K_CHUNK), F32),
                            pltpu.VMEM((rows_h, LANES), F32),
                            pltpu.VMEM((rows_h, LANES), F32),
                            pltpu.VMEM((rows_h, KV_RANK), F32)],
            compiler_params=_compiler_params(("arbitrary", "arbitrary")),
            name="dsa_attention" if prev is None else "dsa_attention_tail",
        )(*args)

    o_lat = call(Q_TILE, 0, n_main, tp // Q_TILE - n_main, None)
    return call(TAIL_TILE, n_main * Q_TILE // TAIL_TILE, 1, 0, o_lat)


def _mix_window_start(r, seq):
    return min(max(r * MIX_ROWS - N_META, 0), seq - MIX_ROWS)


def _residual_rows(x_ref, meta_ref, h_scr, seq):
    r = pl.program_id(1)
    n_real = -(-(seq + N_META) // MIX_ROWS)
    for rv in range(n_real):

        @pl.when(r == rv)
        def _(rv=rv):
            skip = rv * MIX_ROWS - N_META - _mix_window_start(rv, seq)
            if rv == 0:
                h_scr[:N_META, :] = meta_ref[...]
                h_scr[N_META:, :] = x_ref[:MIX_ROWS - N_META, :]
            elif skip == 0:
                h_scr[...] = x_ref[...]
            else:
                h_scr[:MIX_ROWS - skip, :] = x_ref[skip:, :]
                h_scr[MIX_ROWS - skip:, :] = jnp.zeros((skip, h_scr.shape[1]), F32)

    @pl.when(r >= n_real)
    def _():
        h_scr[...] = jnp.zeros(h_scr.shape, F32)


def _mix_kernel(ol_ref, pd_ref, x_ref, meta_ref, wuv_ref, wp_ref, ps_ref, wo_ref, g_ref, b_ref, h1_ref, h1b_ref,
                h_scr, *, seq):
    _residual_rows(x_ref, meta_ref, h_scr, seq)
    attn = [_dot(ol_ref[:, h * KV_RANK:(h + 1) * KV_RANK], wuv_ref[h]) for h in range(N_HEADS)]
    pool = [_dot(pd_ref[:, g * POOL_GROUP:(g + 1) * POOL_GROUP], wp_ref[g]) for g in range(len(POOL_WINDOWS))]
    pool = jnp.concatenate(pool, axis=-1) * ps_ref[...]
    cat = jnp.concatenate(attn + [pool], axis=-1).astype(BF16)
    y = ALPHA * h_scr[...] + _dot(cat, wo_ref[...])
    h1 = _layer_norm(y, g_ref[...], b_ref[...])
    h1_ref[...] = h1
    h1b_ref[...] = h1.astype(h1b_ref.dtype)


def _mix_ln1(o_lat, pool_diff, x, meta, wuv, wpool, pool_scale, wo, g, b):
    B, seq, d = x.shape
    n_rows = seq + N_META
    assert n_rows % MIX_ROWS == 0
    row = lambda n: pl.BlockSpec((None, MIX_ROWS, n), lambda bi, r: (bi, r, 0))
    full = lambda a: pl.BlockSpec(a.shape, lambda bi, r: (0,) * a.ndim, pipeline_mode=pl.Buffered(1))
    window = pl.BlockSpec(
        (None, pl.Element(MIX_ROWS), pl.Element(d)),
        lambda bi, r: (bi, pl.multiple_of(jnp.clip(r * MIX_ROWS - N_META, 0, seq - MIX_ROWS), N_META), 0))
    return pl.pallas_call(
        functools.partial(_mix_kernel, seq=seq),
        grid=(B, n_rows // MIX_ROWS),
        in_specs=[row(o_lat.shape[2]), row(pool_diff.shape[2]), window, full(meta),
                  full(wuv), full(wpool), full(pool_scale), full(wo), full(g), full(b)],
        out_specs=[row(d), row(d)],
        out_shape=[jax.ShapeDtypeStruct((B, n_rows, d), F32), jax.ShapeDtypeStruct((B, n_rows, d), BF16)],
        scratch_shapes=[pltpu.VMEM((MIX_ROWS, d), F32)],
        compiler_params=_compiler_params(("parallel", "arbitrary"), BIG_VMEM_LIMIT_BYTES),
        name="mix_ln1",
    )(o_lat, pool_diff, x, meta, wuv, wpool, pool_scale, wo, g, b)


def _gelu_tanh(x):
    return 0.5 * x * (1.0 + jnp.tanh(math.sqrt(2.0 / math.pi) * (x + 0.044715 * (x * x * x))))


def _ffn_kernel(hw_ref, hres_ref, wa_ref, wg_ref, cwa_ref, cwg_ref, cba_ref, cbg_ref, wd_ref, g_ref, b_ref, o_ref,
                za_scr, zg_scr):
    c = pl.program_id(2)
    n_c = pl.num_programs(2) - 1

    def up(slot):
        x = hw_ref[...]
        za_scr[slot] = _dot(x, wa_ref[...])
        zg_scr[slot] = _dot(x, wg_ref[...])

    def conv(z, cw_ref, cb_ref):
        cw = cw_ref[...]
        n = z.shape[0]
        out = z[HALO - 2:n - 2] * cw[0:1] + z[HALO - 1:n - 1] * cw[1:2] + z[HALO:] * cw[2:3]
        return out + cb_ref[...]

    def down(slot):
        a = conv(za_scr[slot], cwa_ref, cba_ref)
        gate = conv(zg_scr[slot], cwg_ref, cbg_ref)
        act = (_gelu_tanh(a) * gate).astype(BF16)
        o_ref[...] += _dot(act, wd_ref[...])

    @pl.when(c == 0)
    def _():
        o_ref[...] = jnp.zeros(o_ref.shape, o_ref.dtype)
        up(0)

    @pl.when((c > 0) & (c < n_c))
    def _():
        up(c % 2)
        down((c - 1) % 2)

    @pl.when(c == n_c)
    def _():
        down((c - 1) % 2)
        o_ref[...] = _layer_norm(ALPHA * hres_ref[...] + o_ref[...], g_ref[...], b_ref[...])


def _ffn_ln2(h1, h1b, w_up, conv_w, conv_b, w_down, g, b, seq):
    B, tp, d = h1.shape
    d_ff = w_down.shape[0]
    n_c = d_ff // FFN_COLS
    up_c = lambda c: jnp.minimum(c, n_c - 1)
    dn_c = lambda c: jnp.maximum(c - 1, 0)
    vec = lambda off: pl.BlockSpec((1, FFN_COLS), lambda bi, r, c: (0, dn_c(c) + off))
    return pl.pallas_call(
        _ffn_kernel,
        grid=(B, seq // FFN_ROWS, n_c + 1),
        in_specs=[
            pl.BlockSpec((None, pl.Element(FFN_ROWS + HALO), pl.Element(d)),
                         lambda bi, r, c: (bi, r * FFN_ROWS + N_META - HALO, 0)),
            pl.BlockSpec((None, pl.Element(FFN_ROWS), pl.Element(d)),
                         lambda bi, r, c: (bi, pl.multiple_of(r * FFN_ROWS + N_META, N_META), 0)),
            pl.BlockSpec((d, FFN_COLS), lambda bi, r, c: (0, up_c(c))),
            pl.BlockSpec((d, FFN_COLS), lambda bi, r, c: (0, up_c(c) + n_c)),
            pl.BlockSpec((CONV_WIDTH, FFN_COLS), lambda bi, r, c: (0, dn_c(c))),
            pl.BlockSpec((CONV_WIDTH, FFN_COLS), lambda bi, r, c: (0, dn_c(c) + n_c)),
            vec(0), vec(n_c),
            pl.BlockSpec((FFN_COLS, d), lambda bi, r, c: (dn_c(c), 0)),
            pl.BlockSpec((1, d), lambda bi, r, c: (0, 0)),
            pl.BlockSpec((1, d), lambda bi, r, c: (0, 0)),
        ],
        out_specs=pl.BlockSpec((None, FFN_ROWS, d), lambda bi, r, c: (bi, r, 0), pipeline_mode=pl.Buffered(1)),
        out_shape=jax.ShapeDtypeStruct((B, seq, d), F32),
        scratch_shapes=[pltpu.VMEM((2, FFN_ROWS + HALO, FFN_COLS), F32),
                        pltpu.VMEM((2, FFN_ROWS + HALO, FFN_COLS), F32)],
        compiler_params=_compiler_params(("parallel", "parallel", "arbitrary"), BIG_VMEM_LIMIT_BYTES),
        name="ffn_ln2",
    )(h1b, h1, w_up, w_up, conv_w, conv_w, conv_b, conv_b, w_down, g, b)


def _t5_bucket_table(n):
    dist = np.arange(n, dtype=np.int32)
    max_exact = REL_BUCKETS // 2
    d_f = np.maximum(dist, 1).astype(np.float32)
    large = max_exact + (np.log(d_f / np.float32(max_exact)) / np.float32(math.log(REL_MAX_DIST / max_exact))
                         * np.float32(REL_BUCKETS - max_exact)).astype(np.int32)
    return np.where(dist < max_exact, dist, np.minimum(large, REL_BUCKETS - 1))


def _near_bias_diagonals(rel_bias, qt):
    assert qt <= K_CHUNK
    probe = _t5_bucket_table(4 * REL_MAX_DIST)
    first_far = int(np.argmax(probe == REL_BUCKETS - 1))
    assert np.all(probe[first_far:] == REL_BUCKETS - 1)
    n_real = -(-(first_far + K_CHUNK - 1) // K_CHUNK)
    buckets = _t5_bucket_table((n_real + 1) * K_CHUNK)
    period = 2 * K_CHUNK
    u = np.arange(period)
    k = np.arange(n_real)[:, None]
    dist = np.where(u < K_CHUNK, k * K_CHUNK - u, k * K_CHUNK + period - u)
    idx = buckets[np.clip(dist, 0, len(buckets) - 1)]
    rel = rel_bias.astype(F32) - rel_bias[REL_BUCKETS - 1:].astype(F32)
    return jnp.transpose(rel[idx], (0, 2, 1))


def kernel(x, meta, rel_bias, w_in, kv_norm_g, w_uk, w_uv, w_pool, pool_scale, w_o, ln1_g, ln1_b, w_up, conv_w,
           conv_b, w_down, ln2_g, ln2_b):
    B, S, D = x.shape
    assert w_in.shape[0] == DEPTH and S % FFN_ROWS == 0
    T = S + N_META
    tp = -(-T // ROW_ALIGN) * ROW_ALIGN
    assert tp - T >= max(POOL_WINDOWS)
    top_k = min(TOPK_MAX, S // 4)
    assert top_k <= K_CHUNK

    attn_w = N_HEADS * HEAD_DIM
    idx_w = IDX_HEADS * IDX_DIM
    o_c, o_qi = attn_w, attn_w + KV_RANK
    o_ki = o_qi + idx_w
    o_wi = o_ki + IDX_DIM
    o_u = o_wi + IDX_HEADS
    w = jnp.transpose(w_in[0])
    w_qq = jnp.concatenate([w[:o_c], w[o_qi:o_ki] * (IDX_DIM ** -0.5)], axis=0).astype(BF16)
    w_u = w[o_u:].astype(BF16)
    w_small = jnp.concatenate([w[o_c:o_qi], w[o_ki:o_wi], w[o_ki:o_wi],
                               w[o_wi:o_u] * (IDX_HEADS ** -0.5), jnp.zeros((128 - IDX_HEADS, D), w.dtype)],
                              axis=0).astype(BF16)

    qq, w_up_b = _proj_qq(x, meta, w_qq, tp, w_up[0])
    pool_diff, w_down_b = _proj_pool(x, meta, w_u, tp, w_down[0])
    c_kv, kk, wi, w_o_b = _proj_small(x, meta, w_small, kv_norm_g[0].reshape(1, KV_RANK), tp, w_o[0])

    wuk = jnp.transpose(w_uk[0], (1, 2, 0)).astype(BF16)
    o_lat = _dsa_attention(qq, wi, kk, c_kv, wuk, rel_bias, top_k, S)

    wuv = jnp.transpose(w_uv[0], (1, 0, 2)).astype(BF16)
    h1, h1b = _mix_ln1(o_lat, pool_diff, x, meta, wuv, w_pool[0].astype(BF16), pool_scale[0].reshape(1, -1),
                       w_o_b, ln1_g[0].reshape(1, D), ln1_b[0].reshape(1, D))

    return _ffn_ln2(h1, h1b, w_up_b, conv_w[0], conv_b[0].reshape(1, -1), w_down_b,
                    ln2_g[0].reshape(1, D), ln2_b[0].reshape(1, D), S)
```

```python
import functools
import math

import numpy as np
import jax
import jax.numpy as jnp
from jax import lax
from jax.experimental import pallas as pl
from jax.experimental.pallas import tpu as pltpu

F32 = jnp.float32
BF16 = jnp.bfloat16

N_META = 16
N_HEADS = 8
HEAD_DIM = 128
KV_RANK = 256
IDX_HEADS = 16
IDX_DIM = 64
TOPK_MAX = 256
POOL_WINDOWS = (2, 4, 8, 16)
POOL_GROUP = 256
CONV_WIDTH = 3
REL_BUCKETS = 32
REL_MAX_DIST = 128
DEPTH = 1
ALPHA = (2.0 * DEPTH) ** 0.25
LN_EPS = 1e-5
NEG_INF = -1e30

VMEM_LIMIT_BYTES = 56 * 1024 * 1024
BIG_VMEM_LIMIT_BYTES = 60 * 1024 * 1024
SUBLANES = 8
LANES = 128
ROW_ALIGN = 256
PROJ_ROWS = 1024
Q_TILE = 256
TAIL_TILE = 16
K_CHUNK = 256
MIX_ROWS = 688
FFN_ROWS = 1024
FFN_COLS = 512
HALO = 16
INT_MIN = -(2 ** 31)
PACKED_SUBLANES = 16
TIE_ROUNDS = 8


def _dot(a, b):
    return jnp.dot(a, b, preferred_element_type=F32)


def _dot_nt(a, b):
    return lax.dot_general(a, b, (((1,), (1,)), ((), ())), preferred_element_type=F32)


def _layer_norm(y, g, b):
    mu = jnp.mean(y, axis=-1, keepdims=True)
    yc = y - mu
    var = jnp.mean(yc * yc, axis=-1, keepdims=True)
    return yc * lax.rsqrt(var + LN_EPS) * g + b


def _proj_rows(x_ref, meta_ref, w_ref, tp, store):
    seq = x_ref.shape[0]
    store(pl.ds(0, N_META), _dot_nt(meta_ref[...].astype(BF16), w_ref[...]))
    for r in range(seq // PROJ_ROWS):
        acc = _dot_nt(x_ref[r * PROJ_ROWS:(r + 1) * PROJ_ROWS, :].astype(BF16), w_ref[...])
        store(pl.ds(N_META + r * PROJ_ROWS, PROJ_ROWS), acc)
    n_pad = tp - seq - N_META
    store(pl.ds(seq + N_META, n_pad), jnp.zeros((n_pad, w_ref.shape[0]), F32))


def _proj_cast_kernel(x_ref, meta_ref, w_ref, o_ref):
    def store(rows, acc):
        o_ref[rows, :] = acc.astype(o_ref.dtype)

    _proj_rows(x_ref, meta_ref, w_ref, o_ref.shape[0], store)


def _proj_pool_kernel(x_ref, meta_ref, w_ref, o_ref, u_scr):
    def store(rows, acc):
        u_scr[rows, :] = acc

    _proj_rows(x_ref, meta_ref, w_ref, u_scr.shape[0], store)
    group = pl.program_id(1)
    tp = u_scr.shape[0]
    pos = lax.broadcasted_iota(jnp.int32, (tp, 1), 0)
    for g, window in enumerate(POOL_WINDOWS):

        @pl.when(group == g)
        def _():
            u = u_scr[...]
            s = u
            shift = 1
            while shift < window:
                s = s + pltpu.roll(s, shift, axis=0)
                shift *= 2
            count = jnp.minimum(pos + 1, window).astype(F32)
            o_ref[...] = (s / count - u).astype(o_ref.dtype)


def _proj_small_kernel(x_ref, meta_ref, w_ref, g_ref, c_ref, kk_ref, wi_ref):
    def store(rows, acc):
        c = acc[:, :KV_RANK]
        ms = jnp.mean(c * c, axis=-1, keepdims=True)
        c_ref[rows, :] = (c * lax.rsqrt(ms + LN_EPS) * g_ref[...]).astype(c_ref.dtype)
        kk_ref[rows, :] = acc[:, KV_RANK:KV_RANK + LANES].astype(kk_ref.dtype)
        wi_ref[rows, :] = acc[:, KV_RANK + LANES:]

    _proj_rows(x_ref, meta_ref, w_ref, c_ref.shape[0], store)


def _batch_block(tp, n):
    return pl.BlockSpec((None, tp, n), lambda b, j: (b, 0, 0))


def _compiler_params(semantics, vmem_limit_bytes=VMEM_LIMIT_BYTES):
    return pltpu.CompilerParams(dimension_semantics=semantics, vmem_limit_bytes=vmem_limit_bytes)


def _with_side_cast(body, n_in, n_out):
    def kernel(*refs):
        side_in, side_out = refs[n_in], refs[n_in + 1 + n_out]
        side_out[...] = side_in[...].astype(side_out.dtype)
        body(*refs[:n_in], *refs[n_in + 1:n_in + 1 + n_out], *refs[n_in + 2 + n_out:])

    return kernel


def _slab_block(side, n_slabs, slab_of_step):
    slab = side.shape[0] // n_slabs
    assert slab * n_slabs == side.shape[0] and slab % PACKED_SUBLANES == 0
    return pl.BlockSpec((slab, side.shape[1]), lambda *step: (slab_of_step(*step), 0))


def _proj_call(body, x, meta, w, tn, extra_in, extra_specs, out_blocks, out_shapes, scratch, name, side):
    B, seq, d = x.shape
    n_j = w.shape[0] // tn
    out_blocks = list(out_blocks) if isinstance(out_blocks, (list, tuple)) else [out_blocks]
    out_shapes = list(out_shapes) if isinstance(out_shapes, (list, tuple)) else [out_shapes]
    in_specs = [_batch_block(seq, d), pl.BlockSpec(meta.shape, lambda b, j: (0, 0)),
                pl.BlockSpec((tn, d), lambda b, j: (j, 0))] + extra_specs
    args = [x, meta, w, *extra_in]
    if side is not None:
        side_block = _slab_block(side, B * n_j, lambda b, j: b * n_j + j)
        body = _with_side_cast(body, len(args), len(out_blocks))
        in_specs, args = in_specs + [side_block], args + [side]
        out_blocks, out_shapes = out_blocks + [side_block], out_shapes + [jax.ShapeDtypeStruct(side.shape, BF16)]
    return pl.pallas_call(
        body,
        grid=(B, n_j),
        in_specs=in_specs,
        out_specs=out_blocks,
        out_shape=out_shapes,
        scratch_shapes=scratch,
        compiler_params=_compiler_params(("parallel", "arbitrary"), BIG_VMEM_LIMIT_BYTES),
        name=name,
    )(*args)


def _proj_qq(x, meta, w, tp, side):
    B, n, tn = x.shape[0], w.shape[0], 512
    return _proj_call(_proj_cast_kernel, x, meta, w, tn, [], [],
                      pl.BlockSpec((None, tp, tn), lambda b, j: (b, 0, j)),
                      jax.ShapeDtypeStruct((B, tp, n), BF16), [], "proj_qq", side)


def _proj_pool(x, meta, w, tp, side):
    B, n = x.shape[0], w.shape[0]
    return _proj_call(_proj_pool_kernel, x, meta, w, POOL_GROUP, [], [],
                      pl.BlockSpec((None, tp, POOL_GROUP), lambda b, j: (b, 0, j)),
                      jax.ShapeDtypeStruct((B, tp, n), BF16), [pltpu.VMEM((tp, POOL_GROUP), F32)], "proj_pool", side)


def _proj_small(x, meta, w, kv_g, tp, side):
    B = x.shape[0]
    return _proj_call(_proj_small_kernel, x, meta, w, w.shape[0], [kv_g],
                      [pl.BlockSpec((1, KV_RANK), lambda b, j: (0, 0))],
                      [_batch_block(tp, KV_RANK), _batch_block(tp, LANES), _batch_block(tp, 128)],
                      [jax.ShapeDtypeStruct((B, tp, KV_RANK), BF16), jax.ShapeDtypeStruct((B, tp, LANES), BF16),
                       jax.ShapeDtypeStruct((B, tp, 128), F32)], [], "proj_small", side)


def _attn_kernel(*refs, n_tiles, aliased, side, **static):
    n_in = 6
    side_in = refs[n_in + int(aliased)] if side else None
    refs = refs[:n_in] + refs[n_in + int(aliased) + int(side):]
    o_ref = refs[n_in]
    if side:
        side_out = refs[n_in + 1]
        refs = refs[:n_in + 1] + refs[n_in + 2:]

    @pl.when(pl.program_id(1) < n_tiles)
    def _():
        if side:
            side_out[...] = side_in[...].astype(side_out.dtype)
        _attn_tile(*refs, **static)

    @pl.when(pl.program_id(1) >= n_tiles)
    def _():
        o_ref[...] = jnp.zeros(o_ref.shape, o_ref.dtype)


def _attn_tile(qq_ref, wi_ref, kk_ref, c_ref, wuk_ref, diag_ref, o_ref, nb_scr, sc_scr, sct_scr, scf_scr, mb_scr,
               wt_scr, qa_scr, s_scr, m_scr, l_scr, acc_scr, *, top_k, qt, first_tile):
    i = first_tile + pl.program_id(1)
    n_chunks = ((i + 1) * qt - 1) // K_CHUNK + 1
    attn_w = N_HEADS * HEAD_DIM
    scale = HEAD_DIM ** -0.5
    n_pairs = IDX_HEADS // 2
    lanes_are_queries = qt % LANES == 0

    t_col = i * qt + lax.broadcasted_iota(jnp.int32, (qt, 1), 0)
    t_row = i * qt + lax.broadcasted_iota(jnp.int32, (1, qt), 1)
    s_row = lax.broadcasted_iota(jnp.int32, (1, K_CHUNK), 1)
    s_col = lax.broadcasted_iota(jnp.int32, (K_CHUNK, 1), 0)
    lane_half = lax.broadcasted_iota(jnp.int32, (K_CHUNK, LANES), 1) // IDX_DIM

    n_near = nb_scr.shape[0]

    @pl.when((pl.program_id(0) == 0) & (pl.program_id(1) == 0))
    def _():
        for k in range(n_near - 1):
            for h in range(N_HEADS):
                v = jnp.broadcast_to(diag_ref[k, h:h + 1, :], (qt, 2 * K_CHUNK))
                t = pltpu.roll(v, 0, 1, stride=1, stride_axis=0)
                nb_scr[k, h * qt:(h + 1) * qt, :] = t[:, :K_CHUNK]
        nb_scr[n_near - 1] = jnp.zeros(nb_scr.shape[1:], F32)

    for h in range(N_HEADS):
        qa_scr[h * qt:(h + 1) * qt, :] = _dot(
            qq_ref[:, h * HEAD_DIM:(h + 1) * HEAD_DIM], wuk_ref[h]).astype(BF16)
    if lanes_are_queries:
        wt_scr[...] = wi_ref[...].T

    def idx_chunk(j, carry):
        ks = kk_ref[pl.ds(pl.multiple_of(j * K_CHUNK, K_CHUNK), K_CHUNK), :]
        zero = jnp.zeros(ks.shape, ks.dtype)
        k_even = jnp.where(lane_half == 0, ks, zero)
        k_odd = jnp.where(lane_half == 1, ks, zero)
        score = jnp.zeros((K_CHUNK, qt) if lanes_are_queries else (qt, K_CHUNK), F32)
        if lanes_are_queries:
            for p in range(n_pairs):
                q_pair = qq_ref[:, attn_w + p * LANES:attn_w + (p + 1) * LANES]
                for hh, k_half in ((2 * p, k_even), (2 * p + 1, k_odd)):
                    score = score + jnp.maximum(_dot_nt(k_half, q_pair), 0.0) * wt_scr[hh:hh + 1, :]
        else:
            q_pairs = jnp.concatenate(
                [qq_ref[:, attn_w + p * LANES:attn_w + (p + 1) * LANES] for p in range(n_pairs)], axis=0)
            for half, k_half in enumerate((k_even, k_odd)):
                dots = jnp.maximum(_dot_nt(q_pairs, k_half), 0.0)
                for p in range(n_pairs):
                    hh = 2 * p + half
                    score = score + dots[p * qt:(p + 1) * qt] * wi_ref[:, hh:hh + 1]
        s_pos = j * K_CHUNK + (s_col if lanes_are_queries else s_row)
        score = jnp.where(s_pos <= (t_row if lanes_are_queries else t_col), score, NEG_INF)
        if lanes_are_queries:
            sct_scr[j] = score
            sc_scr[j] = score.T
            near = score.astype(BF16)
            bits = lax.bitcast_convert_type(near, jnp.int16)
            below = lax.bitcast_convert_type(bits + jnp.where(bits < 0, jnp.int16(1), jnp.int16(-1)), BF16)
            scf_scr[j] = jnp.where(near.astype(F32) > score, below, near)
        else:
            sc_scr[j] = score
        return carry

    lax.fori_loop(0, n_chunks, idx_chunk, 0)

    k_f = float(top_k)
    idx_bits = int(math.ceil(math.log2(sc_scr.shape[0] * K_CHUNK)))
    per_query = (1, qt) if lanes_are_queries else (qt, 1)
    key_axis = 0 if lanes_are_queries else 1
    s_idx = s_col if lanes_are_queries else s_row
    search_scr = sct_scr if lanes_are_queries else sc_scr

    def fold_chunks(chunk_fn, combine, init):
        if lanes_are_queries:
            acc_rows = 4 * SUBLANES

            def body(j, acc):
                v = chunk_fn(j, search_scr[j]).reshape(K_CHUNK // acc_rows, acc_rows, qt)
                return combine(acc, functools.reduce(combine, [v[t] for t in range(K_CHUNK // acc_rows)]))

            acc = lax.fori_loop(0, n_chunks, body, jnp.full((acc_rows, qt), init, F32))
        else:
            acc = lax.fori_loop(0, n_chunks, lambda j, acc: combine(acc, chunk_fn(j, search_scr[j])),
                                jnp.full((qt, K_CHUNK), init, F32))
        reduce = jnp.sum if combine is jnp.add else jnp.min
        return reduce(acc, axis=key_axis, keepdims=True)

    def count(pred):
        return fold_chunks(lambda j, sc: jnp.where(pred(j, sc), 1.0, 0.0), jnp.add, 0.0)

    def key_to_float(key):
        return lax.bitcast_convert_type(jnp.where(key < 0, key ^ jnp.int32(0x7FFFFFFF), key), F32)

    def count_coarse(cand_f):
        acc_rows = 4 * PACKED_SUBLANES
        cand_b = jnp.broadcast_to(cand_f, (acc_rows, qt)).astype(BF16)

        def body(j, acc):
            hit = jnp.where(scf_scr[j].reshape(K_CHUNK // acc_rows, acc_rows, qt) >= cand_b[None],
                            jnp.ones((), BF16), jnp.zeros((), BF16))
            return acc + functools.reduce(jnp.add, [hit[t] for t in range(K_CHUNK // acc_rows)])

        acc = lax.fori_loop(0, n_chunks, body, jnp.zeros((acc_rows, qt), BF16))
        return jnp.sum(acc.astype(F32), axis=0, keepdims=True)

    def bit_step(coarse, carry):
        key, bit = carry
        cand = key + bit
        cand_f = key_to_float(cand)
        n = count_coarse(cand_f) if coarse else count(lambda j, sc: sc >= cand_f)
        return jnp.where(n >= k_f, cand, key), lax.shift_right_logical(bit, jnp.int32(1))

    state = (jnp.full(per_query, INT_MIN, jnp.int32), jnp.int32(INT_MIN))
    n_coarse = 0
    if lanes_are_queries:
        n_coarse = 16
        state = lax.fori_loop(0, n_coarse, lambda _, c: bit_step(True, c), state)
    thr_key, _ = lax.fori_loop(n_coarse, 32, lambda _, c: bit_step(False, c), state)
    thr = key_to_float(thr_key)

    n_ge = count(lambda j, sc: sc >= thr)
    has_tie = jnp.max(jnp.where((n_ge > k_f) & (thr > NEG_INF), 1.0, 0.0)) > 0.0

    def tie_break():
        def next_value(_, m):
            n_gt = count(lambda j, sc: sc > m)
            above = fold_chunks(lambda j, sc: jnp.where(sc > m, sc, jnp.inf), jnp.minimum, jnp.inf)
            return jnp.where(n_gt >= k_f, above, m)

        m = lax.fori_loop(0, TIE_ROUNDS, next_value, thr)
        need = k_f - count(lambda j, sc: sc > m)

        def step(_, carry):
            cut, bit = carry
            cand = cut + bit
            n_before = count(lambda j, sc: (sc == m) & ((j * K_CHUNK + s_idx) < cand))
            return jnp.where(n_before < need, cand, cut), lax.shift_right_logical(bit, jnp.int32(1))

        cut, _ = lax.fori_loop(0, idx_bits, step,
                               (jnp.zeros(per_query, jnp.int32), jnp.int32(2 ** (idx_bits - 1))))
        return m, cut

    thr, cut = lax.cond(has_tie, tie_break, lambda: (thr, jnp.full(per_query, 2 ** 30, jnp.int32)))

    def to_rows(v):
        if not lanes_are_queries:
            return jnp.broadcast_to(v, (qt, K_CHUNK))
        t = jnp.broadcast_to(v, (qt, qt)).T
        return jnp.concatenate([t] * (K_CHUNK // qt), axis=1)

    thr_b, cut_b = to_rows(thr), to_rows(cut)

    def mask_chunk(j, carry):
        sc = sc_scr[j]
        s_pos = j * K_CHUNK + s_row
        keep = ((sc > thr_b) | ((sc == thr_b) & (s_pos <= cut_b))) & (s_pos <= t_col)
        mb_scr[j] = jnp.where(keep, 0.0, NEG_INF)
        return carry

    lax.fori_loop(0, n_chunks, mask_chunk, 0)

    rows_h = N_HEADS * qt
    lane_fold = lambda v, op: functools.reduce(op, [v[:, k * LANES:(k + 1) * LANES] for k in range(K_CHUNK // LANES)])

    def key_rows(j):
        start = j * K_CHUNK
        return pl.ds(start if isinstance(j, int) else pl.multiple_of(start, K_CHUNK), K_CHUNK)

    def over_chunks(chunk_fn):
        chunk_fn(0, True)
        lax.fori_loop(1, n_chunks, lambda j, carry: (chunk_fn(j, False), carry)[1], 0)

    def logit_chunk(j, first):
        near = jnp.minimum((i * qt) // K_CHUNK - j, n_near - 1)
        s = _dot_nt(qa_scr[...], c_ref[key_rows(j), :]) * scale + nb_scr[near]
        s = s + jnp.concatenate([mb_scr[j]] * N_HEADS, axis=0)
        s_scr[j] = s
        fold = lane_fold(s, jnp.maximum)
        m_scr[...] = fold if first else jnp.maximum(m_scr[...], fold)

    over_chunks(logit_chunk)
    m_b = jnp.broadcast_to(jnp.max(m_scr[...], axis=-1, keepdims=True), (rows_h, LANES))
    m_scr[...] = m_b

    def value_chunk(j, first):
        p = jnp.exp(s_scr[j] - jnp.concatenate([m_scr[...]] * (K_CHUNK // LANES), axis=1))
        pv = _dot(p.astype(BF16), c_ref[key_rows(j), :])
        l_scr[...] = lane_fold(p, jnp.add) if first else l_scr[...] + lane_fold(p, jnp.add)
        acc_scr[...] = pv if first else acc_scr[...] + pv

    over_chunks(value_chunk)
    out = acc_scr[...] * (1.0 / jnp.sum(l_scr[...], axis=-1, keepdims=True))
    for h in range(N_HEADS):
        o_ref[:, h * KV_RANK:(h + 1) * KV_RANK] = out[h * qt:(h + 1) * qt].astype(o_ref.dtype)


def _dsa_attention(qq, wi, kk, c_kv, wuk, rel_bias, top_k, seq, side):
    B, tp, _ = qq.shape
    n_main = (seq + N_META) // Q_TILE
    assert (seq + N_META) - n_main * Q_TILE <= TAIL_TILE and Q_TILE % K_CHUNK == 0 and K_CHUNK % TAIL_TILE == 0
    n_chunks_max = tp // K_CHUNK
    width = N_HEADS * KV_RANK

    def call(qt, first_tile, n_tiles, n_fill, prev, side=None):
        rows_h = N_HEADS * qt
        diag = _near_bias_diagonals(rel_bias, qt)
        tile = lambda i: first_tile + jnp.minimum(i, n_tiles - 1)
        in_specs = [
            pl.BlockSpec((None, qt, qq.shape[2]), lambda b, i: (b, tile(i), 0)),
            pl.BlockSpec((None, qt, 128), lambda b, i: (b, tile(i), 0)),
            pl.BlockSpec((None, tp, LANES), lambda b, i: (b, 0, 0)),
            pl.BlockSpec((None, tp, KV_RANK), lambda b, i: (b, 0, 0)),
            pl.BlockSpec(wuk.shape, lambda b, i: (0, 0, 0)),
            pl.BlockSpec(diag.shape, lambda b, i: (0, 0, 0)),
        ]
        args = [qq, wi, kk, c_kv, wuk, diag]
        out_specs = [pl.BlockSpec((None, qt, width), lambda b, i: (b, first_tile + i, 0))]
        out_shape = [jax.ShapeDtypeStruct((B, tp, width), BF16)]
        aliases = {}
        if prev is not None:
            in_specs.append(pl.BlockSpec(memory_space=pl.ANY))
            args.append(prev)
            aliases = {len(args) - 1: 0}
        if side is not None:
            side_block = _slab_block(side, B * n_tiles, lambda b, i: b * n_tiles + jnp.minimum(i, n_tiles - 1))
            in_specs.append(side_block)
            args.append(side)
            out_specs.append(side_block)
            out_shape.append(jax.ShapeDtypeStruct(side.shape, BF16))
        return pl.pallas_call(
            functools.partial(_attn_kernel, n_tiles=n_tiles, aliased=prev is not None, side=side is not None,
                              top_k=top_k, qt=qt, first_tile=first_tile),
            grid=(B, n_tiles + n_fill),
            in_specs=in_specs,
            out_specs=out_specs,
            out_shape=out_shape,
            input_output_aliases=aliases,
            scratch_shapes=[pltpu.VMEM((diag.shape[0] + 1, rows_h, K_CHUNK), F32),
                            pltpu.VMEM((n_chunks_max, qt, K_CHUNK), F32),
                            pltpu.VMEM((n_chunks_max, K_CHUNK, qt), F32),
                            pltpu.VMEM((n_chunks_max, K_CHUNK, qt), BF16),
                            pltpu.VMEM((n_chunks_max, qt, K_CHUNK), F32),
                            pltpu.VMEM((LANES, qt), F32),
                            pltpu.VMEM((rows_h, KV_RANK), BF16),
                            pltpu.VMEM((n_chunks_max, rows_h, K_CHUNK), F32),
                            pltpu.VMEM((rows_h, LANES), F32),
                            pltpu.VMEM((rows_h, LANES), F32),
                            pltpu.VMEM((rows_h, KV_RANK), F32)],
            compiler_params=_compiler_params(("arbitrary", "arbitrary")),
            name="dsa_attention" if prev is None else "dsa_attention_tail",
        )(*args)

    o_lat, side_b = call(Q_TILE, 0, n_main, tp // Q_TILE - n_main, None, side)
    return call(TAIL_TILE, n_main * Q_TILE // TAIL_TILE, 1, 0, o_lat)[0], side_b


def _mix_window_start(r, seq):
    return min(max(r * MIX_ROWS - N_META, 0), seq - MIX_ROWS)


def _residual_rows(x_ref, meta_ref, h_scr, seq):
    r = pl.program_id(1)
    n_real = -(-(seq + N_META) // MIX_ROWS)
    for rv in range(n_real):

        @pl.when(r == rv)
        def _(rv=rv):
            skip = rv * MIX_ROWS - N_META - _mix_window_start(rv, seq)
            if rv == 0:
                h_scr[:N_META, :] = meta_ref[...]
                h_scr[N_META:, :] = x_ref[:MIX_ROWS - N_META, :]
            elif skip == 0:
                h_scr[...] = x_ref[...]
            else:
                h_scr[:MIX_ROWS - skip, :] = x_ref[skip:, :]
                h_scr[MIX_ROWS - skip:, :] = jnp.zeros((skip, h_scr.shape[1]), F32)

    @pl.when(r >= n_real)
    def _():
        h_scr[...] = jnp.zeros(h_scr.shape, F32)


def _mix_kernel(ol_ref, pd_ref, x_ref, meta_ref, wuv_ref, wp_ref, ps_ref, wo_ref, g_ref, b_ref, h1_ref, h1b_ref,
                h_scr, *, seq):
    _residual_rows(x_ref, meta_ref, h_scr, seq)
    attn = [_dot(ol_ref[:, h * KV_RANK:(h + 1) * KV_RANK], wuv_ref[h]) for h in range(N_HEADS)]
    pool = [_dot(pd_ref[:, g * POOL_GROUP:(g + 1) * POOL_GROUP], wp_ref[g]) for g in range(len(POOL_WINDOWS))]
    pool = jnp.concatenate(pool, axis=-1) * ps_ref[...]
    cat = jnp.concatenate(attn + [pool], axis=-1).astype(BF16)
    y = ALPHA * h_scr[...] + _dot(cat, wo_ref[...])
    h1 = _layer_norm(y, g_ref[...], b_ref[...])
    h1_ref[...] = h1
    h1b_ref[...] = h1.astype(h1b_ref.dtype)


def _mix_ln1(o_lat, pool_diff, x, meta, wuv, wpool, pool_scale, wo, g, b):
    B, seq, d = x.shape
    n_rows = seq + N_META
    assert n_rows % MIX_ROWS == 0
    row = lambda n: pl.BlockSpec((None, MIX_ROWS, n), lambda bi, r: (bi, r, 0))
    full = lambda a: pl.BlockSpec(a.shape, lambda bi, r: (0,) * a.ndim, pipeline_mode=pl.Buffered(1))
    window = pl.BlockSpec(
        (None, pl.Element(MIX_ROWS), pl.Element(d)),
        lambda bi, r: (bi, pl.multiple_of(jnp.clip(r * MIX_ROWS - N_META, 0, seq - MIX_ROWS), N_META), 0))
    return pl.pallas_call(
        functools.partial(_mix_kernel, seq=seq),
        grid=(B, n_rows // MIX_ROWS),
        in_specs=[row(o_lat.shape[2]), row(pool_diff.shape[2]), window, full(meta),
                  full(wuv), full(wpool), full(pool_scale), full(wo), full(g), full(b)],
        out_specs=[row(d), row(d)],
        out_shape=[jax.ShapeDtypeStruct((B, n_rows, d), F32), jax.ShapeDtypeStruct((B, n_rows, d), BF16)],
        scratch_shapes=[pltpu.VMEM((MIX_ROWS, d), F32)],
        compiler_params=_compiler_params(("parallel", "arbitrary"), BIG_VMEM_LIMIT_BYTES),
        name="mix_ln1",
    )(o_lat, pool_diff, x, meta, wuv, wpool, pool_scale, wo, g, b)


def _gelu_tanh(x):
    return 0.5 * x * (1.0 + jnp.tanh(math.sqrt(2.0 / math.pi) * (x + 0.044715 * (x * x * x))))


def _ffn_kernel(hw_ref, hres_ref, wa_ref, wg_ref, cwa_ref, cwg_ref, cba_ref, cbg_ref, wd_ref, g_ref, b_ref, o_ref,
                za_scr, zg_scr):
    c = pl.program_id(2)
    n_c = pl.num_programs(2) - 1

    def up(slot):
        x = hw_ref[...]
        za_scr[slot] = _dot(x, wa_ref[...])
        zg_scr[slot] = _dot(x, wg_ref[...])

    def conv(z, cw_ref, cb_ref):
        cw = cw_ref[...]
        n = z.shape[0]
        out = z[HALO - 2:n - 2] * cw[0:1] + z[HALO - 1:n - 1] * cw[1:2] + z[HALO:] * cw[2:3]
        return out + cb_ref[...]

    def down(slot):
        a = conv(za_scr[slot], cwa_ref, cba_ref)
        gate = conv(zg_scr[slot], cwg_ref, cbg_ref)
        act = (_gelu_tanh(a) * gate).astype(BF16)
        o_ref[...] += _dot(act, wd_ref[...])

    @pl.when(c == 0)
    def _():
        o_ref[...] = jnp.zeros(o_ref.shape, o_ref.dtype)
        up(0)

    @pl.when((c > 0) & (c < n_c))
    def _():
        up(c % 2)
        down((c - 1) % 2)

    @pl.when(c == n_c)
    def _():
        down((c - 1) % 2)
        o_ref[...] = _layer_norm(ALPHA * hres_ref[...] + o_ref[...], g_ref[...], b_ref[...])


def _ffn_ln2(h1, h1b, w_up, conv_w, conv_b, w_down, g, b, seq):
    B, tp, d = h1.shape
    d_ff = w_down.shape[0]
    n_c = d_ff // FFN_COLS
    up_c = lambda c: jnp.minimum(c, n_c - 1)
    dn_c = lambda c: jnp.maximum(c - 1, 0)
    vec = lambda off: pl.BlockSpec((1, FFN_COLS), lambda bi, r, c: (0, dn_c(c) + off))
    return pl.pallas_call(
        _ffn_kernel,
        grid=(B, seq // FFN_ROWS, n_c + 1),
        in_specs=[
            pl.BlockSpec((None, pl.Element(FFN_ROWS + HALO), pl.Element(d)),
                         lambda bi, r, c: (bi, r * FFN_ROWS + N_META - HALO, 0)),
            pl.BlockSpec((None, pl.Element(FFN_ROWS), pl.Element(d)),
                         lambda bi, r, c: (bi, pl.multiple_of(r * FFN_ROWS + N_META, N_META), 0)),
            pl.BlockSpec((d, FFN_COLS), lambda bi, r, c: (0, up_c(c))),
            pl.BlockSpec((d, FFN_COLS), lambda bi, r, c: (0, up_c(c) + n_c)),
            pl.BlockSpec((CONV_WIDTH, FFN_COLS), lambda bi, r, c: (0, dn_c(c))),
            pl.BlockSpec((CONV_WIDTH, FFN_COLS), lambda bi, r, c: (0, dn_c(c) + n_c)),
            vec(0), vec(n_c),
            pl.BlockSpec((FFN_COLS, d), lambda bi, r, c: (dn_c(c), 0)),
            pl.BlockSpec((1, d), lambda bi, r, c: (0, 0)),
            pl.BlockSpec((1, d), lambda bi, r, c: (0, 0)),
        ],
        out_specs=pl.BlockSpec((None, FFN_ROWS, d), lambda bi, r, c: (bi, r, 0), pipeline_mode=pl.Buffered(1)),
        out_shape=jax.ShapeDtypeStruct((B, seq, d), F32),
        scratch_shapes=[pltpu.VMEM((2, FFN_ROWS + HALO, FFN_COLS), F32),
                        pltpu.VMEM((2, FFN_ROWS + HALO, FFN_COLS), F32)],
        compiler_params=_compiler_params(("parallel", "parallel", "arbitrary"), BIG_VMEM_LIMIT_BYTES),
        name="ffn_ln2",
    )(h1b, h1, w_up, w_up, conv_w, conv_w, conv_b, conv_b, w_down, g, b)


def _t5_bucket_table(n):
    dist = np.arange(n, dtype=np.int32)
    max_exact = REL_BUCKETS // 2
    d_f = np.maximum(dist, 1).astype(np.float32)
    large = max_exact + (np.log(d_f / np.float32(max_exact)) / np.float32(math.log(REL_MAX_DIST / max_exact))
                         * np.float32(REL_BUCKETS - max_exact)).astype(np.int32)
    return np.where(dist < max_exact, dist, np.minimum(large, REL_BUCKETS - 1))


def _near_bias_diagonals(rel_bias, qt):
    assert qt <= K_CHUNK
    probe = _t5_bucket_table(4 * REL_MAX_DIST)
    first_far = int(np.argmax(probe == REL_BUCKETS - 1))
    assert np.all(probe[first_far:] == REL_BUCKETS - 1)
    n_real = -(-(first_far + K_CHUNK - 1) // K_CHUNK)
    buckets = _t5_bucket_table((n_real + 1) * K_CHUNK)
    period = 2 * K_CHUNK
    u = np.arange(period)
    k = np.arange(n_real)[:, None]
    dist = np.where(u < K_CHUNK, k * K_CHUNK - u, k * K_CHUNK + period - u)
    idx = buckets[np.clip(dist, 0, len(buckets) - 1)]
    rel = rel_bias.astype(F32) - rel_bias[REL_BUCKETS - 1:].astype(F32)
    return jnp.transpose(rel[idx], (0, 2, 1))


def kernel(x, meta, rel_bias, w_in, kv_norm_g, w_uk, w_uv, w_pool, pool_scale, w_o, ln1_g, ln1_b, w_up, conv_w,
           conv_b, w_down, ln2_g, ln2_b):
    B, S, D = x.shape
    assert w_in.shape[0] == DEPTH and S % FFN_ROWS == 0
    T = S + N_META
    tp = -(-T // ROW_ALIGN) * ROW_ALIGN
    assert tp - T >= max(POOL_WINDOWS)
    top_k = min(TOPK_MAX, S // 4)
    assert top_k <= K_CHUNK

    attn_w = N_HEADS * HEAD_DIM
    idx_w = IDX_HEADS * IDX_DIM
    o_c, o_qi = attn_w, attn_w + KV_RANK
    o_ki = o_qi + idx_w
    o_wi = o_ki + IDX_DIM
    o_u = o_wi + IDX_HEADS
    w = jnp.transpose(w_in[0])
    w_qq = jnp.concatenate([w[:o_c], w[o_qi:o_ki] * (IDX_DIM ** -0.5)], axis=0).astype(BF16)
    w_u = w[o_u:].astype(BF16)
    w_small = jnp.concatenate([w[o_c:o_qi], w[o_ki:o_wi], w[o_ki:o_wi],
                               w[o_wi:o_u] * (IDX_HEADS ** -0.5), jnp.zeros((128 - IDX_HEADS, D), w.dtype)],
                              axis=0).astype(BF16)

    qq, w_down_b = _proj_qq(x, meta, w_qq, tp, w_down[0])
    pool_diff, w_o_b = _proj_pool(x, meta, w_u, tp, w_o[0])
    c_kv, kk, wi = _proj_small(x, meta, w_small, kv_norm_g[0].reshape(1, KV_RANK), tp, None)

    wuk = jnp.transpose(w_uk[0], (1, 2, 0)).astype(BF16)
    o_lat, w_up_b = _dsa_attention(qq, wi, kk, c_kv, wuk, rel_bias, top_k, S, w_up[0])

    wuv = jnp.transpose(w_uv[0], (1, 0, 2)).astype(BF16)
    h1, h1b = _mix_ln1(o_lat, pool_diff, x, meta, wuv, w_pool[0].astype(BF16), pool_scale[0].reshape(1, -1),
                       w_o_b, ln1_g[0].reshape(1, D), ln1_b[0].reshape(1, D))

    return _ffn_ln2(h1, h1b, w_up_b, conv_w[0], conv_b[0].reshape(1, -1), w_down_b,
                    ln2_g[0].reshape(1, D), ln2_b[0].reshape(1, D), S)
```

```python
import functools
import math

import numpy as np
import jax
import jax.numpy as jnp
from jax import lax
from jax.experimental import pallas as pl
from jax.experimental.pallas import tpu as pltpu

F32 = jnp.float32
BF16 = jnp.bfloat16

N_META = 16
N_HEADS = 8
HEAD_DIM = 128
KV_RANK = 256
IDX_HEADS = 16
IDX_DIM = 64
TOPK_MAX = 256
POOL_WINDOWS = (2, 4, 8, 16)
POOL_GROUP = 256
CONV_WIDTH = 3
REL_BUCKETS = 32
REL_MAX_DIST = 128
DEPTH = 1
ALPHA = (2.0 * DEPTH) ** 0.25
LN_EPS = 1e-5
NEG_INF = -1e30

VMEM_LIMIT_BYTES = 56 * 1024 * 1024
BIG_VMEM_LIMIT_BYTES = 60 * 1024 * 1024
SUBLANES = 8
LANES = 128
ROW_ALIGN = 256
PROJ_ROWS = 1024
Q_TILE = 256
TAIL_TILE = 16
K_CHUNK = 256
MIX_ROWS = 688
FFN_ROWS = 1024
FFN_COLS = 512
HALO = 16
INT_MIN = -(2 ** 31)
PACKED_SUBLANES = 16
TIE_ROUNDS = 8


def _dot(a, b):
    return jnp.dot(a, b, preferred_element_type=F32)


def _dot_nt(a, b):
    return lax.dot_general(a, b, (((1,), (1,)), ((), ())), preferred_element_type=F32)


def _layer_norm(y, g, b):
    mu = jnp.mean(y, axis=-1, keepdims=True)
    yc = y - mu
    var = jnp.mean(yc * yc, axis=-1, keepdims=True)
    return yc * lax.rsqrt(var + LN_EPS) * g + b


def _proj_rows(x_ref, meta_ref, w_ref, tp, store):
    seq = x_ref.shape[0]
    store(pl.ds(0, N_META), _dot_nt(meta_ref[...].astype(BF16), w_ref[...]))
    for r in range(seq // PROJ_ROWS):
        acc = _dot_nt(x_ref[r * PROJ_ROWS:(r + 1) * PROJ_ROWS, :].astype(BF16), w_ref[...])
        store(pl.ds(N_META + r * PROJ_ROWS, PROJ_ROWS), acc)
    n_pad = tp - seq - N_META
    store(pl.ds(seq + N_META, n_pad), jnp.zeros((n_pad, w_ref.shape[0]), F32))


def _proj_cast_kernel(x_ref, meta_ref, w_ref, o_ref):
    def store(rows, acc):
        o_ref[rows, :] = acc.astype(o_ref.dtype)

    _proj_rows(x_ref, meta_ref, w_ref, o_ref.shape[0], store)


def _proj_pool_kernel(x_ref, meta_ref, w_ref, o_ref, u_scr):
    def store(rows, acc):
        u_scr[rows, :] = acc

    _proj_rows(x_ref, meta_ref, w_ref, u_scr.shape[0], store)
    group = pl.program_id(1)
    tp = u_scr.shape[0]
    pos = lax.broadcasted_iota(jnp.int32, (tp, 1), 0)
    for g, window in enumerate(POOL_WINDOWS):

        @pl.when(group == g)
        def _():
            u = u_scr[...]
            s = u
            shift = 1
            while shift < window:
                s = s + pltpu.roll(s, shift, axis=0)
                shift *= 2
            count = jnp.minimum(pos + 1, window).astype(F32)
            o_ref[...] = (s / count - u).astype(o_ref.dtype)


def _proj_small_kernel(x_ref, meta_ref, w_ref, g_ref, c_ref, kk_ref, wi_ref):
    def store(rows, acc):
        c = acc[:, :KV_RANK]
        ms = jnp.mean(c * c, axis=-1, keepdims=True)
        c_ref[rows, :] = (c * lax.rsqrt(ms + LN_EPS) * g_ref[...]).astype(c_ref.dtype)
        kk_ref[rows, :] = acc[:, KV_RANK:KV_RANK + LANES].astype(kk_ref.dtype)
        wi_ref[rows, :] = acc[:, KV_RANK + LANES:]

    _proj_rows(x_ref, meta_ref, w_ref, c_ref.shape[0], store)


def _batch_block(tp, n):
    return pl.BlockSpec((None, tp, n), lambda b, j: (b, 0, 0))


def _compiler_params(semantics, vmem_limit_bytes=VMEM_LIMIT_BYTES):
    return pltpu.CompilerParams(dimension_semantics=semantics, vmem_limit_bytes=vmem_limit_bytes)


def _with_side_cast(body, n_in, n_out):
    def kernel(*refs):
        side_in, side_out = refs[n_in], refs[n_in + 1 + n_out]
        side_out[...] = side_in[...].astype(side_out.dtype)
        body(*refs[:n_in], *refs[n_in + 1:n_in + 1 + n_out], *refs[n_in + 2 + n_out:])

    return kernel


def _slab_block(side, n_slabs, slab_of_step):
    slab = side.shape[0] // n_slabs
    assert slab * n_slabs == side.shape[0] and slab % PACKED_SUBLANES == 0
    return pl.BlockSpec((slab, side.shape[1]), lambda *step: (slab_of_step(*step), 0))


def _proj_call(body, x, meta, w, tn, extra_in, extra_specs, out_blocks, out_shapes, scratch, name, side):
    B, seq, d = x.shape
    n_j = w.shape[0] // tn
    out_blocks = list(out_blocks) if isinstance(out_blocks, (list, tuple)) else [out_blocks]
    out_shapes = list(out_shapes) if isinstance(out_shapes, (list, tuple)) else [out_shapes]
    in_specs = [_batch_block(seq, d), pl.BlockSpec(meta.shape, lambda b, j: (0, 0)),
                pl.BlockSpec((tn, d), lambda b, j: (j, 0))] + extra_specs
    args = [x, meta, w, *extra_in]
    if side is not None:
        side_block = _slab_block(side, B * n_j, lambda b, j: b * n_j + j)
        body = _with_side_cast(body, len(args), len(out_blocks))
        in_specs, args = in_specs + [side_block], args + [side]
        out_blocks, out_shapes = out_blocks + [side_block], out_shapes + [jax.ShapeDtypeStruct(side.shape, BF16)]
    return pl.pallas_call(
        body,
        grid=(B, n_j),
        in_specs=in_specs,
        out_specs=out_blocks,
        out_shape=out_shapes,
        scratch_shapes=scratch,
        compiler_params=_compiler_params(("parallel", "arbitrary"), BIG_VMEM_LIMIT_BYTES),
        name=name,
    )(*args)


def _proj_qq(x, meta, w, tp, side):
    B, n, tn = x.shape[0], w.shape[0], 512
    return _proj_call(_proj_cast_kernel, x, meta, w, tn, [], [],
                      pl.BlockSpec((None, tp, tn), lambda b, j: (b, 0, j)),
                      jax.ShapeDtypeStruct((B, tp, n), BF16), [], "proj_qq", side)


def _proj_pool(x, meta, w, tp, side):
    B, n = x.shape[0], w.shape[0]
    return _proj_call(_proj_pool_kernel, x, meta, w, POOL_GROUP, [], [],
                      pl.BlockSpec((None, tp, POOL_GROUP), lambda b, j: (b, 0, j)),
                      jax.ShapeDtypeStruct((B, tp, n), BF16), [pltpu.VMEM((tp, POOL_GROUP), F32)], "proj_pool", side)


def _proj_small(x, meta, w, kv_g, tp, side):
    B = x.shape[0]
    return _proj_call(_proj_small_kernel, x, meta, w, w.shape[0], [kv_g],
                      [pl.BlockSpec((1, KV_RANK), lambda b, j: (0, 0))],
                      [_batch_block(tp, KV_RANK), _batch_block(tp, LANES), _batch_block(tp, 128)],
                      [jax.ShapeDtypeStruct((B, tp, KV_RANK), BF16), jax.ShapeDtypeStruct((B, tp, LANES), BF16),
                       jax.ShapeDtypeStruct((B, tp, 128), F32)], [], "proj_small", side)


def _attn_kernel(*refs, n_tiles, aliased, side, **static):
    n_in = 6
    side_in = refs[n_in + int(aliased)] if side else None
    refs = refs[:n_in] + refs[n_in + int(aliased) + int(side):]
    o_ref = refs[n_in]
    if side:
        side_out = refs[n_in + 1]
        refs = refs[:n_in + 1] + refs[n_in + 2:]

    @pl.when(pl.program_id(1) < n_tiles)
    def _():
        if side:
            side_out[...] = side_in[...].astype(side_out.dtype)
        _attn_tile(*refs, **static)

    @pl.when(pl.program_id(1) >= n_tiles)
    def _():
        o_ref[...] = jnp.zeros(o_ref.shape, o_ref.dtype)


def _attn_tile(qq_ref, wi_ref, kk_ref, c_ref, wuk_ref, diag_ref, o_ref, nb_scr, sc_scr, sct_scr, scf_scr, mb_scr,
               wt_scr, qa_scr, s_scr, m_scr, l_scr, acc_scr, *, top_k, qt, first_tile):
    i = first_tile + pl.program_id(1)
    n_chunks = ((i + 1) * qt - 1) // K_CHUNK + 1
    attn_w = N_HEADS * HEAD_DIM
    scale = HEAD_DIM ** -0.5
    n_pairs = IDX_HEADS // 2
    lanes_are_queries = qt % LANES == 0

    t_col = i * qt + lax.broadcasted_iota(jnp.int32, (qt, 1), 0)
    t_row = i * qt + lax.broadcasted_iota(jnp.int32, (1, qt), 1)
    s_row = lax.broadcasted_iota(jnp.int32, (1, K_CHUNK), 1)
    s_col = lax.broadcasted_iota(jnp.int32, (K_CHUNK, 1), 0)
    lane_half = lax.broadcasted_iota(jnp.int32, (K_CHUNK, LANES), 1) // IDX_DIM

    n_near = nb_scr.shape[0]

    @pl.when((pl.program_id(0) == 0) & (pl.program_id(1) == 0))
    def _():
        for k in range(n_near - 1):
            for h in range(N_HEADS):
                v = jnp.broadcast_to(diag_ref[k, h:h + 1, :], (qt, 2 * K_CHUNK))
                t = pltpu.roll(v, 0, 1, stride=1, stride_axis=0)
                nb_scr[k, h * qt:(h + 1) * qt, :] = t[:, :K_CHUNK]
        nb_scr[n_near - 1] = jnp.zeros(nb_scr.shape[1:], F32)

    for h in range(N_HEADS):
        qa_scr[h * qt:(h + 1) * qt, :] = _dot(
            qq_ref[:, h * HEAD_DIM:(h + 1) * HEAD_DIM], wuk_ref[h]).astype(BF16)
    if lanes_are_queries:
        wt_scr[...] = wi_ref[...].T

    def bf16_floor(v):
        near = v.astype(BF16)
        bits = lax.bitcast_convert_type(near, jnp.int16)
        below = lax.bitcast_convert_type(bits + jnp.where(bits < 0, jnp.int16(1), jnp.int16(-1)), BF16)
        return jnp.where(near.astype(F32) > v, below, near)

    def idx_chunk(j, carry):
        ks = kk_ref[pl.ds(pl.multiple_of(j * K_CHUNK, K_CHUNK), K_CHUNK), :]
        zero = jnp.zeros(ks.shape, ks.dtype)
        k_even = jnp.where(lane_half == 0, ks, zero)
        k_odd = jnp.where(lane_half == 1, ks, zero)
        score = jnp.zeros((K_CHUNK, qt) if lanes_are_queries else (qt, K_CHUNK), F32)
        if lanes_are_queries:
            for p in range(n_pairs):
                q_pair = qq_ref[:, attn_w + p * LANES:attn_w + (p + 1) * LANES]
                for hh, k_half in ((2 * p, k_even), (2 * p + 1, k_odd)):
                    score = score + jnp.maximum(_dot_nt(k_half, q_pair), 0.0) * wt_scr[hh:hh + 1, :]
        else:
            q_pairs = jnp.concatenate(
                [qq_ref[:, attn_w + p * LANES:attn_w + (p + 1) * LANES] for p in range(n_pairs)], axis=0)
            for half, k_half in enumerate((k_even, k_odd)):
                dots = jnp.maximum(_dot_nt(q_pairs, k_half), 0.0)
                for p in range(n_pairs):
                    hh = 2 * p + half
                    score = score + dots[p * qt:(p + 1) * qt] * wi_ref[:, hh:hh + 1]
        s_pos = j * K_CHUNK + (s_col if lanes_are_queries else s_row)
        score = jnp.where(s_pos <= (t_row if lanes_are_queries else t_col), score, NEG_INF)
        if lanes_are_queries:
            sct_scr[j] = score
            sc_scr[j] = score.T
            scf_scr[j] = bf16_floor(score)
        else:
            sc_scr[j] = score
        return carry

    lax.fori_loop(0, n_chunks, idx_chunk, 0)

    k_f = float(top_k)
    idx_bits = int(math.ceil(math.log2(sc_scr.shape[0] * K_CHUNK)))
    per_query = (1, qt) if lanes_are_queries else (qt, 1)
    key_axis = 0 if lanes_are_queries else 1
    s_idx = s_col if lanes_are_queries else s_row
    search_scr = sct_scr if lanes_are_queries else sc_scr

    def fold_chunks(chunk_fn, combine, init):
        if lanes_are_queries:
            acc_rows = 4 * SUBLANES

            def body(j, acc):
                v = chunk_fn(j, search_scr[j]).reshape(K_CHUNK // acc_rows, acc_rows, qt)
                return combine(acc, functools.reduce(combine, [v[t] for t in range(K_CHUNK // acc_rows)]))

            acc = lax.fori_loop(0, n_chunks, body, jnp.full((acc_rows, qt), init, F32))
        else:
            acc = lax.fori_loop(0, n_chunks, lambda j, acc: combine(acc, chunk_fn(j, search_scr[j])),
                                jnp.full((qt, K_CHUNK), init, F32))
        reduce = jnp.sum if combine is jnp.add else jnp.min
        return reduce(acc, axis=key_axis, keepdims=True)

    def count(pred):
        return fold_chunks(lambda j, sc: jnp.where(pred(j, sc), 1.0, 0.0), jnp.add, 0.0)

    def key_to_float(key):
        return lax.bitcast_convert_type(jnp.where(key < 0, key ^ jnp.int32(0x7FFFFFFF), key), F32)

    def count_coarse(cand_f):
        acc_rows = 4 * PACKED_SUBLANES
        cand_b = jnp.broadcast_to(cand_f, (acc_rows, qt)).astype(BF16)

        def body(j, acc):
            hit = jnp.where(scf_scr[j].reshape(K_CHUNK // acc_rows, acc_rows, qt) >= cand_b[None],
                            jnp.ones((), BF16), jnp.zeros((), BF16))
            return acc + functools.reduce(jnp.add, [hit[t] for t in range(K_CHUNK // acc_rows)])

        acc = lax.fori_loop(0, n_chunks, body, jnp.zeros((acc_rows, qt), BF16))
        return jnp.sum(acc.astype(F32), axis=0, keepdims=True)

    def bit_step(base, carry):
        key, bit = carry
        cand = key + bit
        cand_f = key_to_float(cand)
        n = count(lambda j, sc: sc >= cand_f) if base is None else count_coarse(cand_f - base)
        return jnp.where(n >= k_f, cand, key), lax.shift_right_logical(bit, jnp.int32(1))

    state = (jnp.full(per_query, INT_MIN, jnp.int32), jnp.int32(INT_MIN))
    if lanes_are_queries:
        state = lax.fori_loop(0, 16, lambda _, c: bit_step(0.0, c), state)
        for n_bits in (8, 8):
            base = key_to_float(state[0])

            def rebase(j, carry, base=base):
                scf_scr[j] = bf16_floor(sct_scr[j] - base)
                return carry

            lax.fori_loop(0, n_chunks, rebase, 0)
            state = lax.fori_loop(0, n_bits, lambda _, c, base=base: bit_step(base, c), state)
    else:
        state = lax.fori_loop(0, 32, lambda _, c: bit_step(None, c), state)
    thr = key_to_float(state[0])

    n_ge = count(lambda j, sc: sc >= thr)
    has_tie = jnp.max(jnp.where((n_ge > k_f) & (thr > NEG_INF), 1.0, 0.0)) > 0.0

    def tie_break():
        def next_value(_, m):
            n_gt = count(lambda j, sc: sc > m)
            above = fold_chunks(lambda j, sc: jnp.where(sc > m, sc, jnp.inf), jnp.minimum, jnp.inf)
            return jnp.where(n_gt >= k_f, above, m)

        m = lax.fori_loop(0, TIE_ROUNDS, next_value, thr)
        need = k_f - count(lambda j, sc: sc > m)

        def step(_, carry):
            cut, bit = carry
            cand = cut + bit
            n_before = count(lambda j, sc: (sc == m) & ((j * K_CHUNK + s_idx) < cand))
            return jnp.where(n_before < need, cand, cut), lax.shift_right_logical(bit, jnp.int32(1))

        cut, _ = lax.fori_loop(0, idx_bits, step,
                               (jnp.zeros(per_query, jnp.int32), jnp.int32(2 ** (idx_bits - 1))))
        return m, cut

    thr, cut = lax.cond(has_tie, tie_break, lambda: (thr, jnp.full(per_query, 2 ** 30, jnp.int32)))

    def to_rows(v):
        if not lanes_are_queries:
            return jnp.broadcast_to(v, (qt, K_CHUNK))
        t = jnp.broadcast_to(v, (qt, qt)).T
        return jnp.concatenate([t] * (K_CHUNK // qt), axis=1)

    thr_b, cut_b = to_rows(thr), to_rows(cut)

    def mask_chunk(j, carry):
        sc = sc_scr[j]
        s_pos = j * K_CHUNK + s_row
        keep = ((sc > thr_b) | ((sc == thr_b) & (s_pos <= cut_b))) & (s_pos <= t_col)
        mb_scr[j] = jnp.where(keep, 0.0, NEG_INF)
        return carry

    lax.fori_loop(0, n_chunks, mask_chunk, 0)

    rows_h = N_HEADS * qt
    lane_fold = lambda v, op: functools.reduce(op, [v[:, k * LANES:(k + 1) * LANES] for k in range(K_CHUNK // LANES)])

    def key_rows(j):
        start = j * K_CHUNK
        return pl.ds(start if isinstance(j, int) else pl.multiple_of(start, K_CHUNK), K_CHUNK)

    def over_chunks(chunk_fn):
        chunk_fn(0, True)
        lax.fori_loop(1, n_chunks, lambda j, carry: (chunk_fn(j, False), carry)[1], 0)

    def logit_chunk(j, first):
        near = jnp.minimum((i * qt) // K_CHUNK - j, n_near - 1)
        s = _dot_nt(qa_scr[...], c_ref[key_rows(j), :]) * scale + nb_scr[near]
        s = s + jnp.concatenate([mb_scr[j]] * N_HEADS, axis=0)
        s_scr[j] = s
        fold = lane_fold(s, jnp.maximum)
        m_scr[...] = fold if first else jnp.maximum(m_scr[...], fold)

    over_chunks(logit_chunk)
    m_b = jnp.broadcast_to(jnp.max(m_scr[...], axis=-1, keepdims=True), (rows_h, LANES))
    m_scr[...] = m_b

    def value_chunk(j, first):
        p = jnp.exp(s_scr[j] - jnp.concatenate([m_scr[...]] * (K_CHUNK // LANES), axis=1))
        pv = _dot(p.astype(BF16), c_ref[key_rows(j), :])
        l_scr[...] = lane_fold(p, jnp.add) if first else l_scr[...] + lane_fold(p, jnp.add)
        acc_scr[...] = pv if first else acc_scr[...] + pv

    over_chunks(value_chunk)
    out = acc_scr[...] * (1.0 / jnp.sum(l_scr[...], axis=-1, keepdims=True))
    for h in range(N_HEADS):
        o_ref[:, h * KV_RANK:(h + 1) * KV_RANK] = out[h * qt:(h + 1) * qt].astype(o_ref.dtype)


def _dsa_attention(qq, wi, kk, c_kv, wuk, rel_bias, top_k, seq, side):
    B, tp, _ = qq.shape
    n_main = (seq + N_META) // Q_TILE
    assert (seq + N_META) - n_main * Q_TILE <= TAIL_TILE and Q_TILE % K_CHUNK == 0 and K_CHUNK % TAIL_TILE == 0
    n_chunks_max = tp // K_CHUNK
    width = N_HEADS * KV_RANK

    def call(qt, first_tile, n_tiles, n_fill, prev, side=None):
        rows_h = N_HEADS * qt
        diag = _near_bias_diagonals(rel_bias, qt)
        tile = lambda i: first_tile + jnp.minimum(i, n_tiles - 1)
        in_specs = [
            pl.BlockSpec((None, qt, qq.shape[2]), lambda b, i: (b, tile(i), 0)),
            pl.BlockSpec((None, qt, 128), lambda b, i: (b, tile(i), 0)),
            pl.BlockSpec((None, tp, LANES), lambda b, i: (b, 0, 0)),
            pl.BlockSpec((None, tp, KV_RANK), lambda b, i: (b, 0, 0)),
            pl.BlockSpec(wuk.shape, lambda b, i: (0, 0, 0)),
            pl.BlockSpec(diag.shape, lambda b, i: (0, 0, 0)),
        ]
        args = [qq, wi, kk, c_kv, wuk, diag]
        out_specs = [pl.BlockSpec((None, qt, width), lambda b, i: (b, first_tile + i, 0))]
        out_shape = [jax.ShapeDtypeStruct((B, tp, width), BF16)]
        aliases = {}
        if prev is not None:
            in_specs.append(pl.BlockSpec(memory_space=pl.ANY))
            args.append(prev)
            aliases = {len(args) - 1: 0}
        if side is not None:
            side_block = _slab_block(side, B * n_tiles, lambda b, i: b * n_tiles + jnp.minimum(i, n_tiles - 1))
            in_specs.append(side_block)
            args.append(side)
            out_specs.append(side_block)
            out_shape.append(jax.ShapeDtypeStruct(side.shape, BF16))
        return pl.pallas_call(
            functools.partial(_attn_kernel, n_tiles=n_tiles, aliased=prev is not None, side=side is not None,
                              top_k=top_k, qt=qt, first_tile=first_tile),
            grid=(B, n_tiles + n_fill),
            in_specs=in_specs,
            out_specs=out_specs,
            out_shape=out_shape,
            input_output_aliases=aliases,
            scratch_shapes=[pltpu.VMEM((diag.shape[0] + 1, rows_h, K_CHUNK), F32),
                            pltpu.VMEM((n_chunks_max, qt, K_CHUNK), F32),
                            pltpu.VMEM((n_chunks_max, K_CHUNK, qt), F32),
                            pltpu.VMEM((n_chunks_max, K_CHUNK, qt), BF16),
                            pltpu.VMEM((n_chunks_max, qt, K_CHUNK), F32),
                            pltpu.VMEM((LANES, qt), F32),
                            pltpu.VMEM((rows_h, KV_RANK), BF16),
                            pltpu.VMEM((n_chunks_max, rows_h, K_CHUNK), F32),
                            pltpu.VMEM((rows_h, LANES), F32),
                            pltpu.VMEM((rows_h, LANES), F32),
                            pltpu.VMEM((rows_h, KV_RANK), F32)],
            compiler_params=_compiler_params(("arbitrary", "arbitrary")),
            name="dsa_attention" if prev is None else "dsa_attention_tail",
        )(*args)

    o_lat, side_b = call(Q_TILE, 0, n_main, tp // Q_TILE - n_main, None, side)
    return call(TAIL_TILE, n_main * Q_TILE // TAIL_TILE, 1, 0, o_lat)[0], side_b


def _mix_window_start(r, seq):
    return min(max(r * MIX_ROWS - N_META, 0), seq - MIX_ROWS)


def _residual_rows(x_ref, meta_ref, h_scr, seq):
    r = pl.program_id(1)
    n_real = -(-(seq + N_META) // MIX_ROWS)
    for rv in range(n_real):

        @pl.when(r == rv)
        def _(rv=rv):
            skip = rv * MIX_ROWS - N_META - _mix_window_start(rv, seq)
            if rv == 0:
                h_scr[:N_META, :] = meta_ref[...]
                h_scr[N_META:, :] = x_ref[:MIX_ROWS - N_META, :]
            elif skip == 0:
                h_scr[...] = x_ref[...]
            else:
                h_scr[:MIX_ROWS - skip, :] = x_ref[skip:, :]
                h_scr[MIX_ROWS - skip:, :] = jnp.zeros((skip, h_scr.shape[1]), F32)

    @pl.when(r >= n_real)
    def _():
        h_scr[...] = jnp.zeros(h_scr.shape, F32)


def _mix_kernel(ol_ref, pd_ref, x_ref, meta_ref, wuv_ref, wp_ref, ps_ref, wo_ref, g_ref, b_ref, h1_ref, h1b_ref,
                h_scr, *, seq):
    _residual_rows(x_ref, meta_ref, h_scr, seq)
    attn = [_dot(ol_ref[:, h * KV_RANK:(h + 1) * KV_RANK], wuv_ref[h]) for h in range(N_HEADS)]
    pool = [_dot(pd_ref[:, g * POOL_GROUP:(g + 1) * POOL_GROUP], wp_ref[g]) for g in range(len(POOL_WINDOWS))]
    pool = jnp.concatenate(pool, axis=-1) * ps_ref[...]
    cat = jnp.concatenate(attn + [pool], axis=-1).astype(BF16)
    y = ALPHA * h_scr[...] + _dot(cat, wo_ref[...])
    h1 = _layer_norm(y, g_ref[...], b_ref[...])
    h1_ref[...] = h1
    h1b_ref[...] = h1.astype(h1b_ref.dtype)


def _mix_ln1(o_lat, pool_diff, x, meta, wuv, wpool, pool_scale, wo, g, b):
    B, seq, d = x.shape
    n_rows = seq + N_META
    assert n_rows % MIX_ROWS == 0
    row = lambda n: pl.BlockSpec((None, MIX_ROWS, n), lambda bi, r: (bi, r, 0))
    full = lambda a: pl.BlockSpec(a.shape, lambda bi, r: (0,) * a.ndim, pipeline_mode=pl.Buffered(1))
    window = pl.BlockSpec(
        (None, pl.Element(MIX_ROWS), pl.Element(d)),
        lambda bi, r: (bi, pl.multiple_of(jnp.clip(r * MIX_ROWS - N_META, 0, seq - MIX_ROWS), N_META), 0))
    return pl.pallas_call(
        functools.partial(_mix_kernel, seq=seq),
        grid=(B, n_rows // MIX_ROWS),
        in_specs=[row(o_lat.shape[2]), row(pool_diff.shape[2]), window, full(meta),
                  full(wuv), full(wpool), full(pool_scale), full(wo), full(g), full(b)],
        out_specs=[row(d), row(d)],
        out_shape=[jax.ShapeDtypeStruct((B, n_rows, d), F32), jax.ShapeDtypeStruct((B, n_rows, d), BF16)],
        scratch_shapes=[pltpu.VMEM((MIX_ROWS, d), F32)],
        compiler_params=_compiler_params(("parallel", "arbitrary"), BIG_VMEM_LIMIT_BYTES),
        name="mix_ln1",
    )(o_lat, pool_diff, x, meta, wuv, wpool, pool_scale, wo, g, b)


def _gelu_tanh(x):
    return 0.5 * x * (1.0 + jnp.tanh(math.sqrt(2.0 / math.pi) * (x + 0.044715 * (x * x * x))))


def _ffn_kernel(hw_ref, hres_ref, wa_ref, wg_ref, cwa_ref, cwg_ref, cba_ref, cbg_ref, wd_ref, g_ref, b_ref, o_ref,
                za_scr, zg_scr):
    c = pl.program_id(2)
    n_c = pl.num_programs(2) - 1

    def up(slot):
        x = hw_ref[...]
        za_scr[slot] = _dot(x, wa_ref[...])
        zg_scr[slot] = _dot(x, wg_ref[...])

    def conv(z, cw_ref, cb_ref):
        cw = cw_ref[...]
        n = z.shape[0]
        out = z[HALO - 2:n - 2] * cw[0:1] + z[HALO - 1:n - 1] * cw[1:2] + z[HALO:] * cw[2:3]
        return out + cb_ref[...]

    def down(slot):
        a = conv(za_scr[slot], cwa_ref, cba_ref)
        gate = conv(zg_scr[slot], cwg_ref, cbg_ref)
        act = (_gelu_tanh(a) * gate).astype(BF16)
        o_ref[...] += _dot(act, wd_ref[...])

    @pl.when(c == 0)
    def _():
        o_ref[...] = jnp.zeros(o_ref.shape, o_ref.dtype)
        up(0)

    @pl.when((c > 0) & (c < n_c))
    def _():
        up(c % 2)
        down((c - 1) % 2)

    @pl.when(c == n_c)
    def _():
        down((c - 1) % 2)
        o_ref[...] = _layer_norm(ALPHA * hres_ref[...] + o_ref[...], g_ref[...], b_ref[...])


def _ffn_ln2(h1, h1b, w_up, conv_w, conv_b, w_down, g, b, seq):
    B, tp, d = h1.shape
    d_ff = w_down.shape[0]
    n_c = d_ff // FFN_COLS
    up_c = lambda c: jnp.minimum(c, n_c - 1)
    dn_c = lambda c: jnp.maximum(c - 1, 0)
    vec = lambda off: pl.BlockSpec((1, FFN_COLS), lambda bi, r, c: (0, dn_c(c) + off))
    return pl.pallas_call(
        _ffn_kernel,
        grid=(B, seq // FFN_ROWS, n_c + 1),
        in_specs=[
            pl.BlockSpec((None, pl.Element(FFN_ROWS + HALO), pl.Element(d)),
                         lambda bi, r, c: (bi, r * FFN_ROWS + N_META - HALO, 0)),
            pl.BlockSpec((None, pl.Element(FFN_ROWS), pl.Element(d)),
                         lambda bi, r, c: (bi, pl.multiple_of(r * FFN_ROWS + N_META, N_META), 0)),
            pl.BlockSpec((d, FFN_COLS), lambda bi, r, c: (0, up_c(c))),
            pl.BlockSpec((d, FFN_COLS), lambda bi, r, c: (0, up_c(c) + n_c)),
            pl.BlockSpec((CONV_WIDTH, FFN_COLS), lambda bi, r, c: (0, dn_c(c))),
            pl.BlockSpec((CONV_WIDTH, FFN_COLS), lambda bi, r, c: (0, dn_c(c) + n_c)),
            vec(0), vec(n_c),
            pl.BlockSpec((FFN_COLS, d), lambda bi, r, c: (dn_c(c), 0)),
            pl.BlockSpec((1, d), lambda bi, r, c: (0, 0)),
            pl.BlockSpec((1, d), lambda bi, r, c: (0, 0)),
        ],
        out_specs=pl.BlockSpec((None, FFN_ROWS, d), lambda bi, r, c: (bi, r, 0), pipeline_mode=pl.Buffered(1)),
        out_shape=jax.ShapeDtypeStruct((B, seq, d), F32),
        scratch_shapes=[pltpu.VMEM((2, FFN_ROWS + HALO, FFN_COLS), F32),
                        pltpu.VMEM((2, FFN_ROWS + HALO, FFN_COLS), F32)],
        compiler_params=_compiler_params(("parallel", "parallel", "arbitrary"), BIG_VMEM_LIMIT_BYTES),
        name="ffn_ln2",
    )(h1b, h1, w_up, w_up, conv_w, conv_w, conv_b, conv_b, w_down, g, b)


def _t5_bucket_table(n):
    dist = np.arange(n, dtype=np.int32)
    max_exact = REL_BUCKETS // 2
    d_f = np.maximum(dist, 1).astype(np.float32)
    large = max_exact + (np.log(d_f / np.float32(max_exact)) / np.float32(math.log(REL_MAX_DIST / max_exact))
                         * np.float32(REL_BUCKETS - max_exact)).astype(np.int32)
    return np.where(dist < max_exact, dist, np.minimum(large, REL_BUCKETS - 1))


def _near_bias_diagonals(rel_bias, qt):
    assert qt <= K_CHUNK
    probe = _t5_bucket_table(4 * REL_MAX_DIST)
    first_far = int(np.argmax(probe == REL_BUCKETS - 1))
    assert np.all(probe[first_far:] == REL_BUCKETS - 1)
    n_real = -(-(first_far + K_CHUNK - 1) // K_CHUNK)
    buckets = _t5_bucket_table((n_real + 1) * K_CHUNK)
    period = 2 * K_CHUNK
    u = np.arange(period)
    k = np.arange(n_real)[:, None]
    dist = np.where(u < K_CHUNK, k * K_CHUNK - u, k * K_CHUNK + period - u)
    idx = buckets[np.clip(dist, 0, len(buckets) - 1)]
    rel = rel_bias.astype(F32) - rel_bias[REL_BUCKETS - 1:].astype(F32)
    return jnp.transpose(rel[idx], (0, 2, 1))


def kernel(x, meta, rel_bias, w_in, kv_norm_g, w_uk, w_uv, w_pool, pool_scale, w_o, ln1_g, ln1_b, w_up, conv_w,
           conv_b, w_down, ln2_g, ln2_b):
    B, S, D = x.shape
    assert w_in.shape[0] == DEPTH and S % FFN_ROWS == 0
    T = S + N_META
    tp = -(-T // ROW_ALIGN) * ROW_ALIGN
    assert tp - T >= max(POOL_WINDOWS)
    top_k = min(TOPK_MAX, S // 4)
    assert top_k <= K_CHUNK

    attn_w = N_HEADS * HEAD_DIM
    idx_w = IDX_HEADS * IDX_DIM
    o_c, o_qi = attn_w, attn_w + KV_RANK
    o_ki = o_qi + idx_w
    o_wi = o_ki + IDX_DIM
    o_u = o_wi + IDX_HEADS
    w = jnp.transpose(w_in[0])
    w_qq = jnp.concatenate([w[:o_c], w[o_qi:o_ki] * (IDX_DIM ** -0.5)], axis=0).astype(BF16)
    w_u = w[o_u:].astype(BF16)
    w_small = jnp.concatenate([w[o_c:o_qi], w[o_ki:o_wi], w[o_ki:o_wi],
                               w[o_wi:o_u] * (IDX_HEADS ** -0.5), jnp.zeros((128 - IDX_HEADS, D), w.dtype)],
                              axis=0).astype(BF16)

    qq, w_down_b = _proj_qq(x, meta, w_qq, tp, w_down[0])
    pool_diff, w_o_b = _proj_pool(x, meta, w_u, tp, w_o[0])
    c_kv, kk, wi = _proj_small(x, meta, w_small, kv_norm_g[0].reshape(1, KV_RANK), tp, None)

    wuk = jnp.transpose(w_uk[0], (1, 2, 0)).astype(BF16)
    o_lat, w_up_b = _dsa_attention(qq, wi, kk, c_kv, wuk, rel_bias, top_k, S, w_up[0])

    wuv = jnp.transpose(w_uv[0], (1, 0, 2)).astype(BF16)
    h1, h1b = _mix_ln1(o_lat, pool_diff, x, meta, wuv, w_pool[0].astype(BF16), pool_scale[0].reshape(1, -1),
                       w_o_b, ln1_g[0].reshape(1, D), ln1_b[0].reshape(1, D))

    return _ffn_ln2(h1, h1b, w_up_b, conv_w[0], conv_b[0].reshape(1, -1), w_down_b,
                    ln2_g[0].reshape(1, D), ln2_b[0].reshape(1, D), S)
```

```python
import functools
import math

import numpy as np
import jax
import jax.numpy as jnp
from jax import lax
from jax.experimental import pallas as pl
from jax.experimental.pallas import tpu as pltpu

F32 = jnp.float32
BF16 = jnp.bfloat16

N_META = 16
N_HEADS = 8
HEAD_DIM = 128
KV_RANK = 256
IDX_HEADS = 16
IDX_DIM = 64
TOPK_MAX = 256
POOL_WINDOWS = (2, 4, 8, 16)
POOL_GROUP = 256
CONV_WIDTH = 3
REL_BUCKETS = 32
REL_MAX_DIST = 128
DEPTH = 1
ALPHA = (2.0 * DEPTH) ** 0.25
LN_EPS = 1e-5
NEG_INF = -1e30

VMEM_LIMIT_BYTES = 56 * 1024 * 1024
BIG_VMEM_LIMIT_BYTES = 60 * 1024 * 1024
SUBLANES = 8
LANES = 128
ROW_ALIGN = 256
PROJ_ROWS = 1024
Q_TILE = 256
TAIL_TILE = 16
K_CHUNK = 256
MIX_ROWS = 688
FFN_ROWS = 1024
FFN_COLS = 512
HALO = 16
INT_MIN = -(2 ** 31)
PACKED_SUBLANES = 16
TIE_ROUNDS = 8


def _dot(a, b):
    return jnp.dot(a, b, preferred_element_type=F32)


def _dot_nt(a, b):
    return lax.dot_general(a, b, (((1,), (1,)), ((), ())), preferred_element_type=F32)


def _layer_norm(y, g, b):
    mu = jnp.mean(y, axis=-1, keepdims=True)
    yc = y - mu
    var = jnp.mean(yc * yc, axis=-1, keepdims=True)
    return yc * lax.rsqrt(var + LN_EPS) * g + b


def _proj_rows(x_ref, meta_ref, w_ref, tp, store):
    seq = x_ref.shape[0]
    store(pl.ds(0, N_META), _dot_nt(meta_ref[...].astype(BF16), w_ref[...]))
    for r in range(seq // PROJ_ROWS):
        acc = _dot_nt(x_ref[r * PROJ_ROWS:(r + 1) * PROJ_ROWS, :].astype(BF16), w_ref[...])
        store(pl.ds(N_META + r * PROJ_ROWS, PROJ_ROWS), acc)
    n_pad = tp - seq - N_META
    store(pl.ds(seq + N_META, n_pad), jnp.zeros((n_pad, w_ref.shape[0]), F32))


def _proj_cast_kernel(x_ref, meta_ref, w_ref, o_ref):
    def store(rows, acc):
        o_ref[rows, :] = acc.astype(o_ref.dtype)

    _proj_rows(x_ref, meta_ref, w_ref, o_ref.shape[0], store)


def _proj_pool_kernel(x_ref, meta_ref, w_ref, o_ref, u_scr):
    def store(rows, acc):
        u_scr[rows, :] = acc

    _proj_rows(x_ref, meta_ref, w_ref, u_scr.shape[0], store)
    group = pl.program_id(1)
    tp = u_scr.shape[0]
    pos = lax.broadcasted_iota(jnp.int32, (tp, 1), 0)
    for g, window in enumerate(POOL_WINDOWS):

        @pl.when(group == g)
        def _():
            u = u_scr[...]
            s = u
            shift = 1
            while shift < window:
                s = s + pltpu.roll(s, shift, axis=0)
                shift *= 2
            count = jnp.minimum(pos + 1, window).astype(F32)
            o_ref[...] = (s / count - u).astype(o_ref.dtype)


def _proj_small_kernel(x_ref, meta_ref, w_ref, g_ref, c_ref, kk_ref, wi_ref):
    def store(rows, acc):
        c = acc[:, :KV_RANK]
        ms = jnp.mean(c * c, axis=-1, keepdims=True)
        c_ref[rows, :] = (c * lax.rsqrt(ms + LN_EPS) * g_ref[...]).astype(c_ref.dtype)
        kk_ref[rows, :] = acc[:, KV_RANK:KV_RANK + LANES].astype(kk_ref.dtype)
        wi_ref[rows, :] = acc[:, KV_RANK + LANES:]

    _proj_rows(x_ref, meta_ref, w_ref, c_ref.shape[0], store)


def _batch_block(tp, n):
    return pl.BlockSpec((None, tp, n), lambda b, j: (b, 0, 0))


def _compiler_params(semantics, vmem_limit_bytes=VMEM_LIMIT_BYTES):
    return pltpu.CompilerParams(dimension_semantics=semantics, vmem_limit_bytes=vmem_limit_bytes)


def _with_side_cast(body, n_in, n_out):
    def kernel(*refs):
        side_in, side_out = refs[n_in], refs[n_in + 1 + n_out]
        side_out[...] = side_in[...].astype(side_out.dtype)
        body(*refs[:n_in], *refs[n_in + 1:n_in + 1 + n_out], *refs[n_in + 2 + n_out:])

    return kernel


def _slab_block(side, n_slabs, slab_of_step):
    slab = side.shape[0] // n_slabs
    assert slab * n_slabs == side.shape[0] and slab % PACKED_SUBLANES == 0
    return pl.BlockSpec((slab, side.shape[1]), lambda *step: (slab_of_step(*step), 0))


def _proj_call(body, x, meta, w, tn, extra_in, extra_specs, out_blocks, out_shapes, scratch, name, side):
    B, seq, d = x.shape
    n_j = w.shape[0] // tn
    out_blocks = list(out_blocks) if isinstance(out_blocks, (list, tuple)) else [out_blocks]
    out_shapes = list(out_shapes) if isinstance(out_shapes, (list, tuple)) else [out_shapes]
    in_specs = [_batch_block(seq, d), pl.BlockSpec(meta.shape, lambda b, j: (0, 0)),
                pl.BlockSpec((tn, d), lambda b, j: (j, 0))] + extra_specs
    args = [x, meta, w, *extra_in]
    if side is not None:
        side_block = _slab_block(side, B * n_j, lambda b, j: b * n_j + j)
        body = _with_side_cast(body, len(args), len(out_blocks))
        in_specs, args = in_specs + [side_block], args + [side]
        out_blocks, out_shapes = out_blocks + [side_block], out_shapes + [jax.ShapeDtypeStruct(side.shape, BF16)]
    return pl.pallas_call(
        body,
        grid=(B, n_j),
        in_specs=in_specs,
        out_specs=out_blocks,
        out_shape=out_shapes,
        scratch_shapes=scratch,
        compiler_params=_compiler_params(("parallel", "arbitrary"), BIG_VMEM_LIMIT_BYTES),
        name=name,
    )(*args)


def _proj_qq(x, meta, w, tp, side):
    B, n, tn = x.shape[0], w.shape[0], 512
    return _proj_call(_proj_cast_kernel, x, meta, w, tn, [], [],
                      pl.BlockSpec((None, tp, tn), lambda b, j: (b, 0, j)),
                      jax.ShapeDtypeStruct((B, tp, n), BF16), [], "proj_qq", side)


def _proj_pool(x, meta, w, tp, side):
    B, n = x.shape[0], w.shape[0]
    return _proj_call(_proj_pool_kernel, x, meta, w, POOL_GROUP, [], [],
                      pl.BlockSpec((None, tp, POOL_GROUP), lambda b, j: (b, 0, j)),
                      jax.ShapeDtypeStruct((B, tp, n), BF16), [pltpu.VMEM((tp, POOL_GROUP), F32)], "proj_pool", side)


def _proj_small(x, meta, w, kv_g, tp, side):
    B = x.shape[0]
    return _proj_call(_proj_small_kernel, x, meta, w, w.shape[0], [kv_g],
                      [pl.BlockSpec((1, KV_RANK), lambda b, j: (0, 0))],
                      [_batch_block(tp, KV_RANK), _batch_block(tp, LANES), _batch_block(tp, 128)],
                      [jax.ShapeDtypeStruct((B, tp, KV_RANK), BF16), jax.ShapeDtypeStruct((B, tp, LANES), BF16),
                       jax.ShapeDtypeStruct((B, tp, 128), F32)], [], "proj_small", side)


def _attn_kernel(*refs, n_tiles, aliased, n_sides, **static):
    n_in = 6
    sides_in = refs[n_in + int(aliased):n_in + int(aliased) + n_sides]
    refs = refs[:n_in] + refs[n_in + int(aliased) + n_sides:]
    o_ref = refs[n_in]
    sides_out = refs[n_in + 1:n_in + 1 + n_sides]
    refs = refs[:n_in + 1] + refs[n_in + 1 + n_sides:]

    @pl.when(pl.program_id(1) < n_tiles)
    def _():
        for side_in, side_out in zip(sides_in, sides_out):
            side_out[...] = side_in[...].astype(side_out.dtype)
        _attn_tile(*refs, **static)

    @pl.when(pl.program_id(1) >= n_tiles)
    def _():
        o_ref[...] = jnp.zeros(o_ref.shape, o_ref.dtype)


def _attn_tile(qq_ref, wi_ref, kk_ref, c_ref, wuk_ref, diag_ref, o_ref, nb_scr, sc_scr, sct_scr, scf_scr, mb_scr,
               wt_scr, qa_scr, s_scr, m_scr, l_scr, acc_scr, *, top_k, qt, first_tile):
    i = first_tile + pl.program_id(1)
    n_chunks = ((i + 1) * qt - 1) // K_CHUNK + 1
    attn_w = N_HEADS * HEAD_DIM
    scale = HEAD_DIM ** -0.5
    n_pairs = IDX_HEADS // 2
    lanes_are_queries = qt % LANES == 0

    t_col = i * qt + lax.broadcasted_iota(jnp.int32, (qt, 1), 0)
    t_row = i * qt + lax.broadcasted_iota(jnp.int32, (1, qt), 1)
    s_row = lax.broadcasted_iota(jnp.int32, (1, K_CHUNK), 1)
    s_col = lax.broadcasted_iota(jnp.int32, (K_CHUNK, 1), 0)
    lane_half = lax.broadcasted_iota(jnp.int32, (K_CHUNK, LANES), 1) // IDX_DIM

    n_near = nb_scr.shape[0]

    @pl.when((pl.program_id(0) == 0) & (pl.program_id(1) == 0))
    def _():
        for k in range(n_near - 1):
            for h in range(N_HEADS):
                v = jnp.broadcast_to(diag_ref[k, h:h + 1, :], (qt, 2 * K_CHUNK))
                t = pltpu.roll(v, 0, 1, stride=1, stride_axis=0)
                nb_scr[k, h * qt:(h + 1) * qt, :] = t[:, :K_CHUNK]
        nb_scr[n_near - 1] = jnp.zeros(nb_scr.shape[1:], F32)

    for h in range(N_HEADS):
        qa_scr[h * qt:(h + 1) * qt, :] = _dot(
            qq_ref[:, h * HEAD_DIM:(h + 1) * HEAD_DIM], wuk_ref[h]).astype(BF16)
    if lanes_are_queries:
        wt_scr[...] = wi_ref[...].T

    def bf16_floor(v):
        near = v.astype(BF16)
        bits = lax.bitcast_convert_type(near, jnp.int16)
        below = lax.bitcast_convert_type(bits + jnp.where(bits < 0, jnp.int16(1), jnp.int16(-1)), BF16)
        return jnp.where(near.astype(F32) > v, below, near)

    def idx_chunk(j, carry):
        ks = kk_ref[pl.ds(pl.multiple_of(j * K_CHUNK, K_CHUNK), K_CHUNK), :]
        zero = jnp.zeros(ks.shape, ks.dtype)
        k_even = jnp.where(lane_half == 0, ks, zero)
        k_odd = jnp.where(lane_half == 1, ks, zero)
        score = jnp.zeros((K_CHUNK, qt) if lanes_are_queries else (qt, K_CHUNK), F32)
        if lanes_are_queries:
            for p in range(n_pairs):
                q_pair = qq_ref[:, attn_w + p * LANES:attn_w + (p + 1) * LANES]
                for hh, k_half in ((2 * p, k_even), (2 * p + 1, k_odd)):
                    score = score + jnp.maximum(_dot_nt(k_half, q_pair), 0.0) * wt_scr[hh:hh + 1, :]
        else:
            q_pairs = jnp.concatenate(
                [qq_ref[:, attn_w + p * LANES:attn_w + (p + 1) * LANES] for p in range(n_pairs)], axis=0)
            for half, k_half in enumerate((k_even, k_odd)):
                dots = jnp.maximum(_dot_nt(q_pairs, k_half), 0.0)
                for p in range(n_pairs):
                    hh = 2 * p + half
                    score = score + dots[p * qt:(p + 1) * qt] * wi_ref[:, hh:hh + 1]
        s_pos = j * K_CHUNK + (s_col if lanes_are_queries else s_row)
        score = jnp.where(s_pos <= (t_row if lanes_are_queries else t_col), score, NEG_INF)
        if lanes_are_queries:
            sct_scr[j] = score
            sc_scr[j] = score.T
            scf_scr[j] = bf16_floor(score)
        else:
            sc_scr[j] = score
        return carry

    lax.fori_loop(0, n_chunks, idx_chunk, 0)

    k_f = float(top_k)
    idx_bits = int(math.ceil(math.log2(sc_scr.shape[0] * K_CHUNK)))
    per_query = (1, qt) if lanes_are_queries else (qt, 1)
    key_axis = 0 if lanes_are_queries else 1
    s_idx = s_col if lanes_are_queries else s_row
    search_scr = sct_scr if lanes_are_queries else sc_scr

    def fold_chunks(chunk_fn, combine, init):
        if lanes_are_queries:
            acc_rows = 4 * SUBLANES

            def body(j, acc):
                v = chunk_fn(j, search_scr[j]).reshape(K_CHUNK // acc_rows, acc_rows, qt)
                return combine(acc, functools.reduce(combine, [v[t] for t in range(K_CHUNK // acc_rows)]))

            acc = lax.fori_loop(0, n_chunks, body, jnp.full((acc_rows, qt), init, F32))
        else:
            acc = lax.fori_loop(0, n_chunks, lambda j, acc: combine(acc, chunk_fn(j, search_scr[j])),
                                jnp.full((qt, K_CHUNK), init, F32))
        reduce = jnp.sum if combine is jnp.add else jnp.min
        return reduce(acc, axis=key_axis, keepdims=True)

    def count(pred):
        return fold_chunks(lambda j, sc: jnp.where(pred(j, sc), 1.0, 0.0), jnp.add, 0.0)

    def key_to_float(key):
        return lax.bitcast_convert_type(jnp.where(key < 0, key ^ jnp.int32(0x7FFFFFFF), key), F32)

    def count_coarse(cand_f):
        acc_rows = 4 * PACKED_SUBLANES
        cand_b = jnp.broadcast_to(cand_f, (acc_rows, qt)).astype(BF16)

        def body(j, acc):
            hit = jnp.where(scf_scr[j].reshape(K_CHUNK // acc_rows, acc_rows, qt) >= cand_b[None],
                            jnp.ones((), BF16), jnp.zeros((), BF16))
            return acc + functools.reduce(jnp.add, [hit[t] for t in range(K_CHUNK // acc_rows)])

        acc = lax.fori_loop(0, n_chunks, body, jnp.zeros((acc_rows, qt), BF16))
        return jnp.sum(acc.astype(F32), axis=0, keepdims=True)

    def bit_step(base, carry):
        key, bit = carry
        cand = key + bit
        cand_f = key_to_float(cand)
        n = count(lambda j, sc: sc >= cand_f) if base is None else count_coarse(cand_f - base)
        return jnp.where(n >= k_f, cand, key), lax.shift_right_logical(bit, jnp.int32(1))

    state = (jnp.full(per_query, INT_MIN, jnp.int32), jnp.int32(INT_MIN))
    if lanes_are_queries:
        state = lax.fori_loop(0, 16, lambda _, c: bit_step(0.0, c), state)
        for n_bits in (8, 8):
            base = key_to_float(state[0])

            def rebase(j, carry, base=base):
                scf_scr[j] = bf16_floor(sct_scr[j] - base)
                return carry

            lax.fori_loop(0, n_chunks, rebase, 0)
            state = lax.fori_loop(0, n_bits, lambda _, c, base=base: bit_step(base, c), state)
    else:
        state = lax.fori_loop(0, 32, lambda _, c: bit_step(None, c), state)
    thr = key_to_float(state[0])

    n_ge = count(lambda j, sc: sc >= thr)
    has_tie = jnp.max(jnp.where((n_ge > k_f) & (thr > NEG_INF), 1.0, 0.0)) > 0.0

    def tie_break():
        def next_value(_, m):
            n_gt = count(lambda j, sc: sc > m)
            above = fold_chunks(lambda j, sc: jnp.where(sc > m, sc, jnp.inf), jnp.minimum, jnp.inf)
            return jnp.where(n_gt >= k_f, above, m)

        m = lax.fori_loop(0, TIE_ROUNDS, next_value, thr)
        need = k_f - count(lambda j, sc: sc > m)

        def step(_, carry):
            cut, bit = carry
            cand = cut + bit
            n_before = count(lambda j, sc: (sc == m) & ((j * K_CHUNK + s_idx) < cand))
            return jnp.where(n_before < need, cand, cut), lax.shift_right_logical(bit, jnp.int32(1))

        cut, _ = lax.fori_loop(0, idx_bits, step,
                               (jnp.zeros(per_query, jnp.int32), jnp.int32(2 ** (idx_bits - 1))))
        return m, cut

    thr, cut = lax.cond(has_tie, tie_break, lambda: (thr, jnp.full(per_query, 2 ** 30, jnp.int32)))

    def to_rows(v):
        if not lanes_are_queries:
            return jnp.broadcast_to(v, (qt, K_CHUNK))
        t = jnp.broadcast_to(v, (qt, qt)).T
        return jnp.concatenate([t] * (K_CHUNK // qt), axis=1)

    thr_b, cut_b = to_rows(thr), to_rows(cut)

    def mask_chunk(j, carry):
        sc = sc_scr[j]
        s_pos = j * K_CHUNK + s_row
        keep = ((sc > thr_b) | ((sc == thr_b) & (s_pos <= cut_b))) & (s_pos <= t_col)
        mb_scr[j] = jnp.where(keep, 0.0, NEG_INF)
        return carry

    lax.fori_loop(0, n_chunks, mask_chunk, 0)

    rows_h = N_HEADS * qt
    lane_fold = lambda v, op: functools.reduce(op, [v[:, k * LANES:(k + 1) * LANES] for k in range(K_CHUNK // LANES)])

    def key_rows(j):
        start = j * K_CHUNK
        return pl.ds(start if isinstance(j, int) else pl.multiple_of(start, K_CHUNK), K_CHUNK)

    def over_chunks(chunk_fn):
        chunk_fn(0, True)
        lax.fori_loop(1, n_chunks, lambda j, carry: (chunk_fn(j, False), carry)[1], 0)

    def logit_chunk(j, first):
        near = jnp.minimum((i * qt) // K_CHUNK - j, n_near - 1)
        s = _dot_nt(qa_scr[...], c_ref[key_rows(j), :]) * scale + nb_scr[near]
        s = s + jnp.concatenate([mb_scr[j]] * N_HEADS, axis=0)
        s_scr[j] = s
        fold = lane_fold(s, jnp.maximum)
        m_scr[...] = fold if first else jnp.maximum(m_scr[...], fold)

    over_chunks(logit_chunk)
    m_b = jnp.broadcast_to(jnp.max(m_scr[...], axis=-1, keepdims=True), (rows_h, LANES))
    m_scr[...] = m_b

    def value_chunk(j, first):
        p = jnp.exp(s_scr[j] - jnp.concatenate([m_scr[...]] * (K_CHUNK // LANES), axis=1))
        pv = _dot(p.astype(BF16), c_ref[key_rows(j), :])
        l_scr[...] = lane_fold(p, jnp.add) if first else l_scr[...] + lane_fold(p, jnp.add)
        acc_scr[...] = pv if first else acc_scr[...] + pv

    over_chunks(value_chunk)
    out = acc_scr[...] * (1.0 / jnp.sum(l_scr[...], axis=-1, keepdims=True))
    for h in range(N_HEADS):
        o_ref[:, h * KV_RANK:(h + 1) * KV_RANK] = out[h * qt:(h + 1) * qt].astype(o_ref.dtype)


def _dsa_attention(qq, wi, kk, c_kv, wuk, rel_bias, top_k, seq, sides):
    B, tp, _ = qq.shape
    n_main = (seq + N_META) // Q_TILE
    assert (seq + N_META) - n_main * Q_TILE <= TAIL_TILE and Q_TILE % K_CHUNK == 0 and K_CHUNK % TAIL_TILE == 0
    n_chunks_max = tp // K_CHUNK
    width = N_HEADS * KV_RANK

    def call(qt, first_tile, n_tiles, n_fill, prev, sides=()):
        rows_h = N_HEADS * qt
        diag = _near_bias_diagonals(rel_bias, qt)
        tile = lambda i: first_tile + jnp.minimum(i, n_tiles - 1)
        in_specs = [
            pl.BlockSpec((None, qt, qq.shape[2]), lambda b, i: (b, tile(i), 0)),
            pl.BlockSpec((None, qt, 128), lambda b, i: (b, tile(i), 0)),
            pl.BlockSpec((None, tp, LANES), lambda b, i: (b, 0, 0)),
            pl.BlockSpec((None, tp, KV_RANK), lambda b, i: (b, 0, 0)),
            pl.BlockSpec(wuk.shape, lambda b, i: (0, 0, 0)),
            pl.BlockSpec(diag.shape, lambda b, i: (0, 0, 0)),
        ]
        args = [qq, wi, kk, c_kv, wuk, diag]
        out_specs = [pl.BlockSpec((None, qt, width), lambda b, i: (b, first_tile + i, 0))]
        out_shape = [jax.ShapeDtypeStruct((B, tp, width), BF16)]
        aliases = {}
        if prev is not None:
            in_specs.append(pl.BlockSpec(memory_space=pl.ANY))
            args.append(prev)
            aliases = {len(args) - 1: 0}
        for side in sides:
            side_block = _slab_block(side, B * n_tiles, lambda b, i: b * n_tiles + jnp.minimum(i, n_tiles - 1))
            in_specs.append(side_block)
            args.append(side)
            out_specs.append(side_block)
            out_shape.append(jax.ShapeDtypeStruct(side.shape, BF16))
        return pl.pallas_call(
            functools.partial(_attn_kernel, n_tiles=n_tiles, aliased=prev is not None, n_sides=len(sides),
                              top_k=top_k, qt=qt, first_tile=first_tile),
            grid=(B, n_tiles + n_fill),
            in_specs=in_specs,
            out_specs=out_specs,
            out_shape=out_shape,
            input_output_aliases=aliases,
            scratch_shapes=[pltpu.VMEM((diag.shape[0] + 1, rows_h, K_CHUNK), F32),
                            pltpu.VMEM((n_chunks_max, qt, K_CHUNK), F32),
                            pltpu.VMEM((n_chunks_max, K_CHUNK, qt), F32),
                            pltpu.VMEM((n_chunks_max, K_CHUNK, qt), BF16),
                            pltpu.VMEM((n_chunks_max, qt, K_CHUNK), F32),
                            pltpu.VMEM((LANES, qt), F32),
                            pltpu.VMEM((rows_h, KV_RANK), BF16),
                            pltpu.VMEM((n_chunks_max, rows_h, K_CHUNK), F32),
                            pltpu.VMEM((rows_h, LANES), F32),
                            pltpu.VMEM((rows_h, LANES), F32),
                            pltpu.VMEM((rows_h, KV_RANK), F32)],
            compiler_params=_compiler_params(("arbitrary", "arbitrary"), BIG_VMEM_LIMIT_BYTES),
            name="dsa_attention" if prev is None else "dsa_attention_tail",
        )(*args)

    o_lat, *sides_b = call(Q_TILE, 0, n_main, tp // Q_TILE - n_main, None, sides)
    return (call(TAIL_TILE, n_main * Q_TILE // TAIL_TILE, 1, 0, o_lat)[0], *sides_b)


def _mix_window_start(r, seq):
    return min(max(r * MIX_ROWS - N_META, 0), seq - MIX_ROWS)


def _residual_rows(x_ref, meta_ref, h_scr, seq):
    r = pl.program_id(1)
    n_real = -(-(seq + N_META) // MIX_ROWS)
    for rv in range(n_real):

        @pl.when(r == rv)
        def _(rv=rv):
            skip = rv * MIX_ROWS - N_META - _mix_window_start(rv, seq)
            if rv == 0:
                h_scr[:N_META, :] = meta_ref[...]
                h_scr[N_META:, :] = x_ref[:MIX_ROWS - N_META, :]
            elif skip == 0:
                h_scr[...] = x_ref[...]
            else:
                h_scr[:MIX_ROWS - skip, :] = x_ref[skip:, :]
                h_scr[MIX_ROWS - skip:, :] = jnp.zeros((skip, h_scr.shape[1]), F32)

    @pl.when(r >= n_real)
    def _():
        h_scr[...] = jnp.zeros(h_scr.shape, F32)


def _mix_kernel(ol_ref, pd_ref, x_ref, meta_ref, wuv_ref, wp_ref, ps_ref, wo_ref, g_ref, b_ref, h1_ref, h1b_ref,
                h_scr, *, seq):
    _residual_rows(x_ref, meta_ref, h_scr, seq)
    attn = [_dot(ol_ref[:, h * KV_RANK:(h + 1) * KV_RANK], wuv_ref[h]) for h in range(N_HEADS)]
    pool = [_dot(pd_ref[:, g * POOL_GROUP:(g + 1) * POOL_GROUP], wp_ref[g]) for g in range(len(POOL_WINDOWS))]
    pool = jnp.concatenate(pool, axis=-1) * ps_ref[...]
    cat = jnp.concatenate(attn + [pool], axis=-1).astype(BF16)
    y = ALPHA * h_scr[...] + _dot(cat, wo_ref[...])
    h1 = _layer_norm(y, g_ref[...], b_ref[...])
    h1_ref[...] = h1
    h1b_ref[...] = h1.astype(h1b_ref.dtype)


def _mix_ln1(o_lat, pool_diff, x, meta, wuv, wpool, pool_scale, wo, g, b):
    B, seq, d = x.shape
    n_rows = seq + N_META
    assert n_rows % MIX_ROWS == 0
    row = lambda n: pl.BlockSpec((None, MIX_ROWS, n), lambda bi, r: (bi, r, 0))
    full = lambda a: pl.BlockSpec(a.shape, lambda bi, r: (0,) * a.ndim, pipeline_mode=pl.Buffered(1))
    window = pl.BlockSpec(
        (None, pl.Element(MIX_ROWS), pl.Element(d)),
        lambda bi, r: (bi, pl.multiple_of(jnp.clip(r * MIX_ROWS - N_META, 0, seq - MIX_ROWS), N_META), 0))
    return pl.pallas_call(
        functools.partial(_mix_kernel, seq=seq),
        grid=(B, n_rows // MIX_ROWS),
        in_specs=[row(o_lat.shape[2]), row(pool_diff.shape[2]), window, full(meta),
                  full(wuv), full(wpool), full(pool_scale), full(wo), full(g), full(b)],
        out_specs=[row(d), row(d)],
        out_shape=[jax.ShapeDtypeStruct((B, n_rows, d), F32), jax.ShapeDtypeStruct((B, n_rows, d), BF16)],
        scratch_shapes=[pltpu.VMEM((MIX_ROWS, d), F32)],
        compiler_params=_compiler_params(("parallel", "arbitrary"), BIG_VMEM_LIMIT_BYTES),
        name="mix_ln1",
    )(o_lat, pool_diff, x, meta, wuv, wpool, pool_scale, wo, g, b)


def _gelu_tanh(x):
    return 0.5 * x * (1.0 + jnp.tanh(math.sqrt(2.0 / math.pi) * (x + 0.044715 * (x * x * x))))


def _ffn_kernel(hw_ref, hres_ref, wa_ref, wg_ref, cwa_ref, cwg_ref, cba_ref, cbg_ref, wd_ref, g_ref, b_ref, o_ref,
                za_scr, zg_scr):
    c = pl.program_id(2)
    n_c = pl.num_programs(2) - 1

    def up(slot):
        x = hw_ref[...]
        za_scr[slot] = _dot(x, wa_ref[...])
        zg_scr[slot] = _dot(x, wg_ref[...])

    def conv(z, cw_ref, cb_ref):
        cw = cw_ref[...]
        n = z.shape[0]
        out = z[HALO - 2:n - 2] * cw[0:1] + z[HALO - 1:n - 1] * cw[1:2] + z[HALO:] * cw[2:3]
        return out + cb_ref[...]

    def down(slot):
        a = conv(za_scr[slot], cwa_ref, cba_ref)
        gate = conv(zg_scr[slot], cwg_ref, cbg_ref)
        act = (_gelu_tanh(a) * gate).astype(BF16)
        o_ref[...] += _dot(act, wd_ref[...])

    @pl.when(c == 0)
    def _():
        o_ref[...] = jnp.zeros(o_ref.shape, o_ref.dtype)
        up(0)

    @pl.when((c > 0) & (c < n_c))
    def _():
        up(c % 2)
        down((c - 1) % 2)

    @pl.when(c == n_c)
    def _():
        down((c - 1) % 2)
        o_ref[...] = _layer_norm(ALPHA * hres_ref[...] + o_ref[...], g_ref[...], b_ref[...])


def _ffn_ln2(h1, h1b, w_up, conv_w, conv_b, w_down, g, b, seq):
    B, tp, d = h1.shape
    d_ff = w_down.shape[0]
    n_c = d_ff // FFN_COLS
    up_c = lambda c: jnp.minimum(c, n_c - 1)
    dn_c = lambda c: jnp.maximum(c - 1, 0)
    vec = lambda off: pl.BlockSpec((1, FFN_COLS), lambda bi, r, c: (0, dn_c(c) + off))
    return pl.pallas_call(
        _ffn_kernel,
        grid=(B, seq // FFN_ROWS, n_c + 1),
        in_specs=[
            pl.BlockSpec((None, pl.Element(FFN_ROWS + HALO), pl.Element(d)),
                         lambda bi, r, c: (bi, r * FFN_ROWS + N_META - HALO, 0)),
            pl.BlockSpec((None, pl.Element(FFN_ROWS), pl.Element(d)),
                         lambda bi, r, c: (bi, pl.multiple_of(r * FFN_ROWS + N_META, N_META), 0)),
            pl.BlockSpec((d, FFN_COLS), lambda bi, r, c: (0, up_c(c))),
            pl.BlockSpec((d, FFN_COLS), lambda bi, r, c: (0, up_c(c) + n_c)),
            pl.BlockSpec((CONV_WIDTH, FFN_COLS), lambda bi, r, c: (0, dn_c(c))),
            pl.BlockSpec((CONV_WIDTH, FFN_COLS), lambda bi, r, c: (0, dn_c(c) + n_c)),
            vec(0), vec(n_c),
            pl.BlockSpec((FFN_COLS, d), lambda bi, r, c: (dn_c(c), 0)),
            pl.BlockSpec((1, d), lambda bi, r, c: (0, 0)),
            pl.BlockSpec((1, d), lambda bi, r, c: (0, 0)),
        ],
        out_specs=pl.BlockSpec((None, FFN_ROWS, d), lambda bi, r, c: (bi, r, 0), pipeline_mode=pl.Buffered(1)),
        out_shape=jax.ShapeDtypeStruct((B, seq, d), F32),
        scratch_shapes=[pltpu.VMEM((2, FFN_ROWS + HALO, FFN_COLS), F32),
                        pltpu.VMEM((2, FFN_ROWS + HALO, FFN_COLS), F32)],
        compiler_params=_compiler_params(("parallel", "parallel", "arbitrary"), BIG_VMEM_LIMIT_BYTES),
        name="ffn_ln2",
    )(h1b, h1, w_up, w_up, conv_w, conv_w, conv_b, conv_b, w_down, g, b)


def _t5_bucket_table(n):
    dist = np.arange(n, dtype=np.int32)
    max_exact = REL_BUCKETS // 2
    d_f = np.maximum(dist, 1).astype(np.float32)
    large = max_exact + (np.log(d_f / np.float32(max_exact)) / np.float32(math.log(REL_MAX_DIST / max_exact))
                         * np.float32(REL_BUCKETS - max_exact)).astype(np.int32)
    return np.where(dist < max_exact, dist, np.minimum(large, REL_BUCKETS - 1))


def _near_bias_diagonals(rel_bias, qt):
    assert qt <= K_CHUNK
    probe = _t5_bucket_table(4 * REL_MAX_DIST)
    first_far = int(np.argmax(probe == REL_BUCKETS - 1))
    assert np.all(probe[first_far:] == REL_BUCKETS - 1)
    n_real = -(-(first_far + K_CHUNK - 1) // K_CHUNK)
    buckets = _t5_bucket_table((n_real + 1) * K_CHUNK)
    period = 2 * K_CHUNK
    u = np.arange(period)
    k = np.arange(n_real)[:, None]
    dist = np.where(u < K_CHUNK, k * K_CHUNK - u, k * K_CHUNK + period - u)
    idx = buckets[np.clip(dist, 0, len(buckets) - 1)]
    rel = rel_bias.astype(F32) - rel_bias[REL_BUCKETS - 1:].astype(F32)
    return jnp.transpose(rel[idx], (0, 2, 1))


def kernel(x, meta, rel_bias, w_in, kv_norm_g, w_uk, w_uv, w_pool, pool_scale, w_o, ln1_g, ln1_b, w_up, conv_w,
           conv_b, w_down, ln2_g, ln2_b):
    B, S, D = x.shape
    assert w_in.shape[0] == DEPTH and S % FFN_ROWS == 0
    T = S + N_META
    tp = -(-T // ROW_ALIGN) * ROW_ALIGN
    assert tp - T >= max(POOL_WINDOWS)
    top_k = min(TOPK_MAX, S // 4)
    assert top_k <= K_CHUNK

    attn_w = N_HEADS * HEAD_DIM
    idx_w = IDX_HEADS * IDX_DIM
    o_c, o_qi = attn_w, attn_w + KV_RANK
    o_ki = o_qi + idx_w
    o_wi = o_ki + IDX_DIM
    o_u = o_wi + IDX_HEADS
    w = jnp.transpose(w_in[0])
    w_qq = jnp.concatenate([w[:o_c], w[o_qi:o_ki] * (IDX_DIM ** -0.5)], axis=0).astype(BF16)
    w_u = w[o_u:].astype(BF16)
    w_small = jnp.concatenate([w[o_c:o_qi], w[o_ki:o_wi], w[o_ki:o_wi],
                               w[o_wi:o_u] * (IDX_HEADS ** -0.5), jnp.zeros((128 - IDX_HEADS, D), w.dtype)],
                              axis=0).astype(BF16)

    qq, = _proj_qq(x, meta, w_qq, tp, None)
    pool_diff, w_o_b = _proj_pool(x, meta, w_u, tp, w_o[0])
    c_kv, kk, wi = _proj_small(x, meta, w_small, kv_norm_g[0].reshape(1, KV_RANK), tp, None)

    wuk = jnp.transpose(w_uk[0], (1, 2, 0)).astype(BF16)
    o_lat, w_up_b, w_down_b = _dsa_attention(qq, wi, kk, c_kv, wuk, rel_bias, top_k, S, (w_up[0], w_down[0]))

    wuv = jnp.transpose(w_uv[0], (1, 0, 2)).astype(BF16)
    h1, h1b = _mix_ln1(o_lat, pool_diff, x, meta, wuv, w_pool[0].astype(BF16), pool_scale[0].reshape(1, -1),
                       w_o_b, ln1_g[0].reshape(1, D), ln1_b[0].reshape(1, D))

    return _ffn_ln2(h1, h1b, w_up_b, conv_w[0], conv_b[0].reshape(1, -1), w_down_b,
                    ln2_g[0].reshape(1, D), ln2_b[0].reshape(1, D), S)
```

```python
import functools
import math

import numpy as np
import jax
import jax.numpy as jnp
from jax import lax
from jax.experimental import pallas as pl
from jax.experimental.pallas import tpu as pltpu

F32 = jnp.float32
BF16 = jnp.bfloat16

N_META = 16
N_HEADS = 8
HEAD_DIM = 128
KV_RANK = 256
IDX_HEADS = 16
IDX_DIM = 64
TOPK_MAX = 256
POOL_WINDOWS = (2, 4, 8, 16)
POOL_GROUP = 256
CONV_WIDTH = 3
REL_BUCKETS = 32
REL_MAX_DIST = 128
DEPTH = 1
ALPHA = (2.0 * DEPTH) ** 0.25
LN_EPS = 1e-5
NEG_INF = -1e30

VMEM_LIMIT_BYTES = 56 * 1024 * 1024
BIG_VMEM_LIMIT_BYTES = 60 * 1024 * 1024
SUBLANES = 8
LANES = 128
ROW_ALIGN = 256
PROJ_ROWS = 1024
Q_TILE = 256
TAIL_TILE = 16
K_CHUNK = 256
MIX_ROWS = 688
FFN_ROWS = 1024
FFN_COLS = 512
HALO = 16
INT_MIN = -(2 ** 31)
PACKED_SUBLANES = 16
TIE_ROUNDS = 8


def _dot(a, b):
    return jnp.dot(a, b, preferred_element_type=F32)


def _dot_nt(a, b):
    return lax.dot_general(a, b, (((1,), (1,)), ((), ())), preferred_element_type=F32)


def _layer_norm(y, g, b):
    mu = jnp.mean(y, axis=-1, keepdims=True)
    yc = y - mu
    var = jnp.mean(yc * yc, axis=-1, keepdims=True)
    return yc * lax.rsqrt(var + LN_EPS) * g + b


def _proj_rows(x_ref, meta_ref, w, tp, store):
    seq = x_ref.shape[0]
    store(pl.ds(0, N_META), _dot_nt(meta_ref[...].astype(BF16), w))
    for r in range(seq // PROJ_ROWS):
        acc = _dot_nt(x_ref[r * PROJ_ROWS:(r + 1) * PROJ_ROWS, :].astype(BF16), w)
        store(pl.ds(N_META + r * PROJ_ROWS, PROJ_ROWS), acc)
    n_pad = tp - seq - N_META
    store(pl.ds(seq + N_META, n_pad), jnp.zeros((n_pad, w.shape[0]), F32))


def _proj_cast_kernel(x_ref, meta_ref, w_ref, o_ref, *, scale_from):
    def store(rows, acc):
        o_ref[rows, :] = acc.astype(o_ref.dtype)

    scale = jnp.where(pl.program_id(1) >= scale_from, IDX_DIM ** -0.5, 1.0)
    _proj_rows(x_ref, meta_ref, (w_ref[...] * scale).astype(BF16), o_ref.shape[0], store)


def _proj_pool_kernel(x_ref, meta_ref, w_ref, o_ref, u_scr):
    def store(rows, acc):
        u_scr[rows, :] = acc

    _proj_rows(x_ref, meta_ref, w_ref[...].astype(BF16), u_scr.shape[0], store)
    group = pl.program_id(1)
    tp = u_scr.shape[0]
    pos = lax.broadcasted_iota(jnp.int32, (tp, 1), 0)
    for g, window in enumerate(POOL_WINDOWS):

        @pl.when(group == g)
        def _():
            u = u_scr[...]
            s = u
            shift = 1
            while shift < window:
                s = s + pltpu.roll(s, shift, axis=0)
                shift *= 2
            count = jnp.minimum(pos + 1, window).astype(F32)
            o_ref[...] = (s / count - u).astype(o_ref.dtype)


def _proj_small_kernel(x_ref, meta_ref, wc_ref, wk_ref, ww_ref, g_ref, c_ref, kk_ref, wi_ref):
    w = jnp.concatenate([wc_ref[...], wk_ref[...], wk_ref[...], ww_ref[...] * (IDX_HEADS ** -0.5),
                         jnp.zeros((LANES - IDX_HEADS, wc_ref.shape[1]), F32)], axis=0).astype(BF16)

    def store(rows, acc):
        c = acc[:, :KV_RANK]
        ms = jnp.mean(c * c, axis=-1, keepdims=True)
        c_ref[rows, :] = (c * lax.rsqrt(ms + LN_EPS) * g_ref[...]).astype(c_ref.dtype)
        kk_ref[rows, :] = acc[:, KV_RANK:KV_RANK + LANES].astype(kk_ref.dtype)
        wi_ref[rows, :] = acc[:, KV_RANK + LANES:]

    _proj_rows(x_ref, meta_ref, w, c_ref.shape[0], store)


def _batch_block(tp, n):
    return pl.BlockSpec((None, tp, n), lambda b, j: (b, 0, 0))


def _compiler_params(semantics, vmem_limit_bytes=VMEM_LIMIT_BYTES):
    return pltpu.CompilerParams(dimension_semantics=semantics, vmem_limit_bytes=vmem_limit_bytes)


def _with_side_cast(body, n_in, n_out):
    def kernel(*refs):
        side_in, side_out = refs[n_in], refs[n_in + 1 + n_out]
        side_out[...] = side_in[...].astype(side_out.dtype)
        body(*refs[:n_in], *refs[n_in + 1:n_in + 1 + n_out], *refs[n_in + 2 + n_out:])

    return kernel


def _slab_block(side, n_slabs, slab_of_step):
    slab = side.shape[0] // n_slabs
    assert slab * n_slabs == side.shape[0] and slab % PACKED_SUBLANES == 0
    return pl.BlockSpec((slab, side.shape[1]), lambda *step: (slab_of_step(*step), 0))


def _w_rows(d, n_rows, first_row):
    return pl.BlockSpec((pl.Element(n_rows), pl.Element(d)), lambda b, j: (first_row(j), 0))


def _proj_call(body, x, meta, wt, w_specs, n_j, extra_in, extra_specs, out_blocks, out_shapes, scratch, name, side):
    B, seq, d = x.shape
    out_blocks = list(out_blocks) if isinstance(out_blocks, (list, tuple)) else [out_blocks]
    out_shapes = list(out_shapes) if isinstance(out_shapes, (list, tuple)) else [out_shapes]
    in_specs = [_batch_block(seq, d), pl.BlockSpec(meta.shape, lambda b, j: (0, 0))] + w_specs + extra_specs
    args = [x, meta] + [wt] * len(w_specs) + list(extra_in)
    if side is not None:
        side_block = _slab_block(side, B * n_j, lambda b, j: b * n_j + j)
        body = _with_side_cast(body, len(args), len(out_blocks))
        in_specs, args = in_specs + [side_block], args + [side]
        out_blocks, out_shapes = out_blocks + [side_block], out_shapes + [jax.ShapeDtypeStruct(side.shape, BF16)]
    return pl.pallas_call(
        body,
        grid=(B, n_j),
        in_specs=in_specs,
        out_specs=out_blocks,
        out_shape=out_shapes,
        scratch_shapes=scratch,
        compiler_params=_compiler_params(("parallel", "arbitrary"), BIG_VMEM_LIMIT_BYTES),
        name=name,
    )(*args)


def _proj_qq(x, meta, wt, cols, tp, side):
    B, d, tn = x.shape[0], x.shape[2], 512
    n_q, n = cols["q"][1] // tn, cols["q"][1] + cols["q_idx"][1]
    first_row = lambda j: pl.multiple_of(
        jnp.where(j < n_q, cols["q"][0] + j * tn, cols["q_idx"][0] + (j - n_q) * tn), PACKED_SUBLANES)
    return _proj_call(functools.partial(_proj_cast_kernel, scale_from=n_q), x, meta, wt,
                      [_w_rows(d, tn, first_row)], n // tn, [], [],
                      pl.BlockSpec((None, tp, tn), lambda b, j: (b, 0, j)),
                      jax.ShapeDtypeStruct((B, tp, n), BF16), [], "proj_qq", side)


def _proj_pool(x, meta, wt, cols, tp, side):
    B, d, n = x.shape[0], x.shape[2], cols["u"][1]
    first_row = lambda j: pl.multiple_of(cols["u"][0] + j * POOL_GROUP, PACKED_SUBLANES)
    return _proj_call(_proj_pool_kernel, x, meta, wt, [_w_rows(d, POOL_GROUP, first_row)], n // POOL_GROUP, [], [],
                      pl.BlockSpec((None, tp, POOL_GROUP), lambda b, j: (b, 0, j)),
                      jax.ShapeDtypeStruct((B, tp, n), BF16), [pltpu.VMEM((tp, POOL_GROUP), F32)], "proj_pool", side)


def _proj_small(x, meta, wt, cols, kv_g, tp, side):
    B, d = x.shape[0], x.shape[2]
    w_specs = [_w_rows(d, cols[name][1], lambda j, name=name: cols[name][0]) for name in ("c_kv", "k_idx", "w_idx")]
    return _proj_call(_proj_small_kernel, x, meta, wt, w_specs, 1, [kv_g],
                      [pl.BlockSpec((1, KV_RANK), lambda b, j: (0, 0))],
                      [_batch_block(tp, KV_RANK), _batch_block(tp, LANES), _batch_block(tp, 128)],
                      [jax.ShapeDtypeStruct((B, tp, KV_RANK), BF16), jax.ShapeDtypeStruct((B, tp, LANES), BF16),
                       jax.ShapeDtypeStruct((B, tp, 128), F32)], [], "proj_small", side)


def _attn_kernel(*refs, n_tiles, aliased, n_sides, **static):
    n_in = 6
    sides_in = refs[n_in + int(aliased):n_in + int(aliased) + n_sides]
    refs = refs[:n_in] + refs[n_in + int(aliased) + n_sides:]
    o_ref = refs[n_in]
    sides_out = refs[n_in + 1:n_in + 1 + n_sides]
    refs = refs[:n_in + 1] + refs[n_in + 1 + n_sides:]

    @pl.when(pl.program_id(1) < n_tiles)
    def _():
        for side_in, side_out in zip(sides_in, sides_out):
            side_out[...] = side_in[...].astype(side_out.dtype)
        _attn_tile(*refs, **static)

    @pl.when(pl.program_id(1) >= n_tiles)
    def _():
        o_ref[...] = jnp.zeros(o_ref.shape, o_ref.dtype)


def _attn_tile(qq_ref, wi_ref, kk_ref, c_ref, wuk_ref, diag_ref, o_ref, nb_scr, sc_scr, sct_scr, scf_scr, mb_scr,
               wt_scr, qa_scr, s_scr, m_scr, l_scr, acc_scr, *, top_k, qt, first_tile):
    i = first_tile + pl.program_id(1)
    n_chunks = ((i + 1) * qt - 1) // K_CHUNK + 1
    attn_w = N_HEADS * HEAD_DIM
    scale = HEAD_DIM ** -0.5
    n_pairs = IDX_HEADS // 2
    lanes_are_queries = qt % LANES == 0

    t_col = i * qt + lax.broadcasted_iota(jnp.int32, (qt, 1), 0)
    t_row = i * qt + lax.broadcasted_iota(jnp.int32, (1, qt), 1)
    s_row = lax.broadcasted_iota(jnp.int32, (1, K_CHUNK), 1)
    s_col = lax.broadcasted_iota(jnp.int32, (K_CHUNK, 1), 0)
    lane_half = lax.broadcasted_iota(jnp.int32, (K_CHUNK, LANES), 1) // IDX_DIM

    n_near = nb_scr.shape[0]

    @pl.when((pl.program_id(0) == 0) & (pl.program_id(1) == 0))
    def _():
        for k in range(n_near - 1):
            for h in range(N_HEADS):
                v = jnp.broadcast_to(diag_ref[k, h:h + 1, :], (qt, 2 * K_CHUNK))
                t = pltpu.roll(v, 0, 1, stride=1, stride_axis=0)
                nb_scr[k, h * qt:(h + 1) * qt, :] = t[:, :K_CHUNK]
        nb_scr[n_near - 1] = jnp.zeros(nb_scr.shape[1:], F32)

    for h in range(N_HEADS):
        qa_scr[h * qt:(h + 1) * qt, :] = _dot(
            qq_ref[:, h * HEAD_DIM:(h + 1) * HEAD_DIM], wuk_ref[h]).astype(BF16)
    if lanes_are_queries:
        wt_scr[...] = wi_ref[...].T

    def bf16_floor(v):
        near = v.astype(BF16)
        bits = lax.bitcast_convert_type(near, jnp.int16)
        below = lax.bitcast_convert_type(bits + jnp.where(bits < 0, jnp.int16(1), jnp.int16(-1)), BF16)
        return jnp.where(near.astype(F32) > v, below, near)

    def idx_chunk(j, carry):
        ks = kk_ref[pl.ds(pl.multiple_of(j * K_CHUNK, K_CHUNK), K_CHUNK), :]
        zero = jnp.zeros(ks.shape, ks.dtype)
        k_even = jnp.where(lane_half == 0, ks, zero)
        k_odd = jnp.where(lane_half == 1, ks, zero)
        score = jnp.zeros((K_CHUNK, qt) if lanes_are_queries else (qt, K_CHUNK), F32)
        if lanes_are_queries:
            for p in range(n_pairs):
                q_pair = qq_ref[:, attn_w + p * LANES:attn_w + (p + 1) * LANES]
                for hh, k_half in ((2 * p, k_even), (2 * p + 1, k_odd)):
                    score = score + jnp.maximum(_dot_nt(k_half, q_pair), 0.0) * wt_scr[hh:hh + 1, :]
        else:
            q_pairs = jnp.concatenate(
                [qq_ref[:, attn_w + p * LANES:attn_w + (p + 1) * LANES] for p in range(n_pairs)], axis=0)
            for half, k_half in enumerate((k_even, k_odd)):
                dots = jnp.maximum(_dot_nt(q_pairs, k_half), 0.0)
                for p in range(n_pairs):
                    hh = 2 * p + half
                    score = score + dots[p * qt:(p + 1) * qt] * wi_ref[:, hh:hh + 1]
        s_pos = j * K_CHUNK + (s_col if lanes_are_queries else s_row)
        score = jnp.where(s_pos <= (t_row if lanes_are_queries else t_col), score, NEG_INF)
        if lanes_are_queries:
            sct_scr[j] = score
            sc_scr[j] = score.T
            scf_scr[j] = bf16_floor(score)
        else:
            sc_scr[j] = score
        return carry

    lax.fori_loop(0, n_chunks, idx_chunk, 0)

    k_f = float(top_k)
    idx_bits = int(math.ceil(math.log2(sc_scr.shape[0] * K_CHUNK)))
    per_query = (1, qt) if lanes_are_queries else (qt, 1)
    key_axis = 0 if lanes_are_queries else 1
    s_idx = s_col if lanes_are_queries else s_row
    search_scr = sct_scr if lanes_are_queries else sc_scr

    def fold_chunks(chunk_fn, combine, init):
        if lanes_are_queries:
            acc_rows = 4 * SUBLANES

            def body(j, acc):
                v = chunk_fn(j, search_scr[j]).reshape(K_CHUNK // acc_rows, acc_rows, qt)
                return combine(acc, functools.reduce(combine, [v[t] for t in range(K_CHUNK // acc_rows)]))

            acc = lax.fori_loop(0, n_chunks, body, jnp.full((acc_rows, qt), init, F32))
        else:
            acc = lax.fori_loop(0, n_chunks, lambda j, acc: combine(acc, chunk_fn(j, search_scr[j])),
                                jnp.full((qt, K_CHUNK), init, F32))
        reduce = jnp.sum if combine is jnp.add else jnp.min
        return reduce(acc, axis=key_axis, keepdims=True)

    def count(pred):
        return fold_chunks(lambda j, sc: jnp.where(pred(j, sc), 1.0, 0.0), jnp.add, 0.0)

    def key_to_float(key):
        return lax.bitcast_convert_type(jnp.where(key < 0, key ^ jnp.int32(0x7FFFFFFF), key), F32)

    def count_coarse(cand_f):
        acc_rows = 4 * PACKED_SUBLANES
        cand_b = jnp.broadcast_to(cand_f, (acc_rows, qt)).astype(BF16)

        def body(j, acc):
            hit = jnp.where(scf_scr[j].reshape(K_CHUNK // acc_rows, acc_rows, qt) >= cand_b[None],
                            jnp.ones((), BF16), jnp.zeros((), BF16))
            return acc + functools.reduce(jnp.add, [hit[t] for t in range(K_CHUNK // acc_rows)])

        acc = lax.fori_loop(0, n_chunks, body, jnp.zeros((acc_rows, qt), BF16))
        return jnp.sum(acc.astype(F32), axis=0, keepdims=True)

    def bit_step(base, carry):
        key, bit = carry
        cand = key + bit
        cand_f = key_to_float(cand)
        n = count(lambda j, sc: sc >= cand_f) if base is None else count_coarse(cand_f - base)
        return jnp.where(n >= k_f, cand, key), lax.shift_right_logical(bit, jnp.int32(1))

    state = (jnp.full(per_query, INT_MIN, jnp.int32), jnp.int32(INT_MIN))
    if lanes_are_queries:
        state = lax.fori_loop(0, 16, lambda _, c: bit_step(0.0, c), state)
        for n_bits in (8, 8):
            base = key_to_float(state[0])

            def rebase(j, carry, base=base):
                scf_scr[j] = bf16_floor(sct_scr[j] - base)
                return carry

            lax.fori_loop(0, n_chunks, rebase, 0)
            state = lax.fori_loop(0, n_bits, lambda _, c, base=base: bit_step(base, c), state)
    else:
        state = lax.fori_loop(0, 32, lambda _, c: bit_step(None, c), state)
    thr = key_to_float(state[0])

    n_ge = count(lambda j, sc: sc >= thr)
    has_tie = jnp.max(jnp.where((n_ge > k_f) & (thr > NEG_INF), 1.0, 0.0)) > 0.0

    def tie_break():
        def next_value(_, m):
            n_gt = count(lambda j, sc: sc > m)
            above = fold_chunks(lambda j, sc: jnp.where(sc > m, sc, jnp.inf), jnp.minimum, jnp.inf)
            return jnp.where(n_gt >= k_f, above, m)

        m = lax.fori_loop(0, TIE_ROUNDS, next_value, thr)
        need = k_f - count(lambda j, sc: sc > m)

        def step(_, carry):
            cut, bit = carry
            cand = cut + bit
            n_before = count(lambda j, sc: (sc == m) & ((j * K_CHUNK + s_idx) < cand))
            return jnp.where(n_before < need, cand, cut), lax.shift_right_logical(bit, jnp.int32(1))

        cut, _ = lax.fori_loop(0, idx_bits, step,
                               (jnp.zeros(per_query, jnp.int32), jnp.int32(2 ** (idx_bits - 1))))
        return m, cut

    thr, cut = lax.cond(has_tie, tie_break, lambda: (thr, jnp.full(per_query, 2 ** 30, jnp.int32)))

    def to_rows(v):
        if not lanes_are_queries:
            return jnp.broadcast_to(v, (qt, K_CHUNK))
        t = jnp.broadcast_to(v, (qt, qt)).T
        return jnp.concatenate([t] * (K_CHUNK // qt), axis=1)

    thr_b, cut_b = to_rows(thr), to_rows(cut)

    def mask_chunk(j, carry):
        sc = sc_scr[j]
        s_pos = j * K_CHUNK + s_row
        keep = ((sc > thr_b) | ((sc == thr_b) & (s_pos <= cut_b))) & (s_pos <= t_col)
        mb_scr[j] = jnp.where(keep, 0.0, NEG_INF)
        return carry

    lax.fori_loop(0, n_chunks, mask_chunk, 0)

    rows_h = N_HEADS * qt
    lane_fold = lambda v, op: functools.reduce(op, [v[:, k * LANES:(k + 1) * LANES] for k in range(K_CHUNK // LANES)])

    def key_rows(j):
        start = j * K_CHUNK
        return pl.ds(start if isinstance(j, int) else pl.multiple_of(start, K_CHUNK), K_CHUNK)

    def over_chunks(chunk_fn):
        chunk_fn(0, True)
        lax.fori_loop(1, n_chunks, lambda j, carry: (chunk_fn(j, False), carry)[1], 0)

    def logit_chunk(j, first):
        near = jnp.minimum((i * qt) // K_CHUNK - j, n_near - 1)
        s = _dot_nt(qa_scr[...], c_ref[key_rows(j), :]) * scale + nb_scr[near]
        s = s + jnp.concatenate([mb_scr[j]] * N_HEADS, axis=0)
        s_scr[j] = s
        fold = lane_fold(s, jnp.maximum)
        m_scr[...] = fold if first else jnp.maximum(m_scr[...], fold)

    over_chunks(logit_chunk)
    m_b = jnp.broadcast_to(jnp.max(m_scr[...], axis=-1, keepdims=True), (rows_h, LANES))
    m_scr[...] = m_b

    def value_chunk(j, first):
        p = jnp.exp(s_scr[j] - jnp.concatenate([m_scr[...]] * (K_CHUNK // LANES), axis=1))
        pv = _dot(p.astype(BF16), c_ref[key_rows(j), :])
        l_scr[...] = lane_fold(p, jnp.add) if first else l_scr[...] + lane_fold(p, jnp.add)
        acc_scr[...] = pv if first else acc_scr[...] + pv

    over_chunks(value_chunk)
    out = acc_scr[...] * (1.0 / jnp.sum(l_scr[...], axis=-1, keepdims=True))
    for h in range(N_HEADS):
        o_ref[:, h * KV_RANK:(h + 1) * KV_RANK] = out[h * qt:(h + 1) * qt].astype(o_ref.dtype)


def _dsa_attention(qq, wi, kk, c_kv, wuk, rel_bias, top_k, seq, sides):
    B, tp, _ = qq.shape
    n_main = (seq + N_META) // Q_TILE
    assert (seq + N_META) - n_main * Q_TILE <= TAIL_TILE and Q_TILE % K_CHUNK == 0 and K_CHUNK % TAIL_TILE == 0
    n_chunks_max = tp // K_CHUNK
    width = N_HEADS * KV_RANK

    def call(qt, first_tile, n_tiles, n_fill, prev, sides=()):
        rows_h = N_HEADS * qt
        diag = _near_bias_diagonals(rel_bias, qt)
        tile = lambda i: first_tile + jnp.minimum(i, n_tiles - 1)
        in_specs = [
            pl.BlockSpec((None, qt, qq.shape[2]), lambda b, i: (b, tile(i), 0)),
            pl.BlockSpec((None, qt, 128), lambda b, i: (b, tile(i), 0)),
            pl.BlockSpec((None, tp, LANES), lambda b, i: (b, 0, 0)),
            pl.BlockSpec((None, tp, KV_RANK), lambda b, i: (b, 0, 0)),
            pl.BlockSpec(wuk.shape, lambda b, i: (0, 0, 0)),
            pl.BlockSpec(diag.shape, lambda b, i: (0, 0, 0)),
        ]
        args = [qq, wi, kk, c_kv, wuk, diag]
        out_specs = [pl.BlockSpec((None, qt, width), lambda b, i: (b, first_tile + i, 0))]
        out_shape = [jax.ShapeDtypeStruct((B, tp, width), BF16)]
        aliases = {}
        if prev is not None:
            in_specs.append(pl.BlockSpec(memory_space=pl.ANY))
            args.append(prev)
            aliases = {len(args) - 1: 0}
        for side in sides:
            side_block = _slab_block(side, B * n_tiles, lambda b, i: b * n_tiles + jnp.minimum(i, n_tiles - 1))
            in_specs.append(side_block)
            args.append(side)
            out_specs.append(side_block)
            out_shape.append(jax.ShapeDtypeStruct(side.shape, BF16))
        return pl.pallas_call(
            functools.partial(_attn_kernel, n_tiles=n_tiles, aliased=prev is not None, n_sides=len(sides),
                              top_k=top_k, qt=qt, first_tile=first_tile),
            grid=(B, n_tiles + n_fill),
            in_specs=in_specs,
            out_specs=out_specs,
            out_shape=out_shape,
            input_output_aliases=aliases,
            scratch_shapes=[pltpu.VMEM((diag.shape[0] + 1, rows_h, K_CHUNK), F32),
                            pltpu.VMEM((n_chunks_max, qt, K_CHUNK), F32),
                            pltpu.VMEM((n_chunks_max, K_CHUNK, qt), F32),
                            pltpu.VMEM((n_chunks_max, K_CHUNK, qt), BF16),
                            pltpu.VMEM((n_chunks_max, qt, K_CHUNK), F32),
                            pltpu.VMEM((LANES, qt), F32),
                            pltpu.VMEM((rows_h, KV_RANK), BF16),
                            pltpu.VMEM((n_chunks_max, rows_h, K_CHUNK), F32),
                            pltpu.VMEM((rows_h, LANES), F32),
                            pltpu.VMEM((rows_h, LANES), F32),
                            pltpu.VMEM((rows_h, KV_RANK), F32)],
            compiler_params=_compiler_params(("arbitrary", "arbitrary"), BIG_VMEM_LIMIT_BYTES),
            name="dsa_attention" if prev is None else "dsa_attention_tail",
        )(*args)

    o_lat, *sides_b = call(Q_TILE, 0, n_main, tp // Q_TILE - n_main, None, sides)
    return (call(TAIL_TILE, n_main * Q_TILE // TAIL_TILE, 1, 0, o_lat)[0], *sides_b)


def _mix_window_start(r, seq):
    return min(max(r * MIX_ROWS - N_META, 0), seq - MIX_ROWS)


def _residual_rows(x_ref, meta_ref, h_scr, seq):
    r = pl.program_id(1)
    n_real = -(-(seq + N_META) // MIX_ROWS)
    for rv in range(n_real):

        @pl.when(r == rv)
        def _(rv=rv):
            skip = rv * MIX_ROWS - N_META - _mix_window_start(rv, seq)
            if rv == 0:
                h_scr[:N_META, :] = meta_ref[...]
                h_scr[N_META:, :] = x_ref[:MIX_ROWS - N_META, :]
            elif skip == 0:
                h_scr[...] = x_ref[...]
            else:
                h_scr[:MIX_ROWS - skip, :] = x_ref[skip:, :]
                h_scr[MIX_ROWS - skip:, :] = jnp.zeros((skip, h_scr.shape[1]), F32)

    @pl.when(r >= n_real)
    def _():
        h_scr[...] = jnp.zeros(h_scr.shape, F32)


def _mix_kernel(ol_ref, pd_ref, x_ref, meta_ref, wuv_ref, wp_ref, ps_ref, wo_ref, g_ref, b_ref, h1_ref, h1b_ref,
                h_scr, *, seq):
    _residual_rows(x_ref, meta_ref, h_scr, seq)
    attn = [_dot(ol_ref[:, h * KV_RANK:(h + 1) * KV_RANK], wuv_ref[h]) for h in range(N_HEADS)]
    pool = [_dot(pd_ref[:, g * POOL_GROUP:(g + 1) * POOL_GROUP], wp_ref[g]) for g in range(len(POOL_WINDOWS))]
    pool = jnp.concatenate(pool, axis=-1) * ps_ref[...]
    cat = jnp.concatenate(attn + [pool], axis=-1).astype(BF16)
    y = ALPHA * h_scr[...] + _dot(cat, wo_ref[...])
    h1 = _layer_norm(y, g_ref[...], b_ref[...])
    h1_ref[...] = h1
    h1b_ref[...] = h1.astype(h1b_ref.dtype)


def _mix_ln1(o_lat, pool_diff, x, meta, wuv, wpool, pool_scale, wo, g, b):
    B, seq, d = x.shape
    n_rows = seq + N_META
    assert n_rows % MIX_ROWS == 0
    row = lambda n: pl.BlockSpec((None, MIX_ROWS, n), lambda bi, r: (bi, r, 0))
    full = lambda a: pl.BlockSpec(a.shape, lambda bi, r: (0,) * a.ndim, pipeline_mode=pl.Buffered(1))
    window = pl.BlockSpec(
        (None, pl.Element(MIX_ROWS), pl.Element(d)),
        lambda bi, r: (bi, pl.multiple_of(jnp.clip(r * MIX_ROWS - N_META, 0, seq - MIX_ROWS), N_META), 0))
    return pl.pallas_call(
        functools.partial(_mix_kernel, seq=seq),
        grid=(B, n_rows // MIX_ROWS),
        in_specs=[row(o_lat.shape[2]), row(pool_diff.shape[2]), window, full(meta),
                  full(wuv), full(wpool), full(pool_scale), full(wo), full(g), full(b)],
        out_specs=[row(d), row(d)],
        out_shape=[jax.ShapeDtypeStruct((B, n_rows, d), F32), jax.ShapeDtypeStruct((B, n_rows, d), BF16)],
        scratch_shapes=[pltpu.VMEM((MIX_ROWS, d), F32)],
        compiler_params=_compiler_params(("parallel", "arbitrary"), BIG_VMEM_LIMIT_BYTES),
        name="mix_ln1",
    )(o_lat, pool_diff, x, meta, wuv, wpool, pool_scale, wo, g, b)


def _gelu_tanh(x):
    return 0.5 * x * (1.0 + jnp.tanh(math.sqrt(2.0 / math.pi) * (x + 0.044715 * (x * x * x))))


def _ffn_kernel(hw_ref, hres_ref, wa_ref, wg_ref, cwa_ref, cwg_ref, cba_ref, cbg_ref, wd_ref, g_ref, b_ref, o_ref,
                za_scr, zg_scr):
    c = pl.program_id(2)
    n_c = pl.num_programs(2) - 1

    def up(slot):
        x = hw_ref[...]
        za_scr[slot] = _dot(x, wa_ref[...])
        zg_scr[slot] = _dot(x, wg_ref[...])

    def conv(z, cw_ref, cb_ref):
        cw = cw_ref[...]
        n = z.shape[0]
        out = z[HALO - 2:n - 2] * cw[0:1] + z[HALO - 1:n - 1] * cw[1:2] + z[HALO:] * cw[2:3]
        return out + cb_ref[...]

    def down(slot):
        a = conv(za_scr[slot], cwa_ref, cba_ref)
        gate = conv(zg_scr[slot], cwg_ref, cbg_ref)
        act = (_gelu_tanh(a) * gate).astype(BF16)
        o_ref[...] += _dot(act, wd_ref[...])

    @pl.when(c == 0)
    def _():
        o_ref[...] = jnp.zeros(o_ref.shape, o_ref.dtype)
        up(0)

    @pl.when((c > 0) & (c < n_c))
    def _():
        up(c % 2)
        down((c - 1) % 2)

    @pl.when(c == n_c)
    def _():
        down((c - 1) % 2)
        o_ref[...] = _layer_norm(ALPHA * hres_ref[...] + o_ref[...], g_ref[...], b_ref[...])


def _ffn_ln2(h1, h1b, w_up, conv_w, conv_b, w_down, g, b, seq):
    B, tp, d = h1.shape
    d_ff = w_down.shape[0]
    n_c = d_ff // FFN_COLS
    up_c = lambda c: jnp.minimum(c, n_c - 1)
    dn_c = lambda c: jnp.maximum(c - 1, 0)
    vec = lambda off: pl.BlockSpec((1, FFN_COLS), lambda bi, r, c: (0, dn_c(c) + off))
    return pl.pallas_call(
        _ffn_kernel,
        grid=(B, seq // FFN_ROWS, n_c + 1),
        in_specs=[
            pl.BlockSpec((None, pl.Element(FFN_ROWS + HALO), pl.Element(d)),
                         lambda bi, r, c: (bi, r * FFN_ROWS + N_META - HALO, 0)),
            pl.BlockSpec((None, pl.Element(FFN_ROWS), pl.Element(d)),
                         lambda bi, r, c: (bi, pl.multiple_of(r * FFN_ROWS + N_META, N_META), 0)),
            pl.BlockSpec((d, FFN_COLS), lambda bi, r, c: (0, up_c(c))),
            pl.BlockSpec((d, FFN_COLS), lambda bi, r, c: (0, up_c(c) + n_c)),
            pl.BlockSpec((CONV_WIDTH, FFN_COLS), lambda bi, r, c: (0, dn_c(c))),
            pl.BlockSpec((CONV_WIDTH, FFN_COLS), lambda bi, r, c: (0, dn_c(c) + n_c)),
            vec(0), vec(n_c),
            pl.BlockSpec((FFN_COLS, d), lambda bi, r, c: (dn_c(c), 0)),
            pl.BlockSpec((1, d), lambda bi, r, c: (0, 0)),
            pl.BlockSpec((1, d), lambda bi, r, c: (0, 0)),
        ],
        out_specs=pl.BlockSpec((None, FFN_ROWS, d), lambda bi, r, c: (bi, r, 0), pipeline_mode=pl.Buffered(1)),
        out_shape=jax.ShapeDtypeStruct((B, seq, d), F32),
        scratch_shapes=[pltpu.VMEM((2, FFN_ROWS + HALO, FFN_COLS), F32),
                        pltpu.VMEM((2, FFN_ROWS + HALO, FFN_COLS), F32)],
        compiler_params=_compiler_params(("parallel", "parallel", "arbitrary"), BIG_VMEM_LIMIT_BYTES),
        name="ffn_ln2",
    )(h1b, h1, w_up, w_up, conv_w, conv_w, conv_b, conv_b, w_down, g, b)


def _t5_bucket_table(n):
    dist = np.arange(n, dtype=np.int32)
    max_exact = REL_BUCKETS // 2
    d_f = np.maximum(dist, 1).astype(np.float32)
    large = max_exact + (np.log(d_f / np.float32(max_exact)) / np.float32(math.log(REL_MAX_DIST / max_exact))
                         * np.float32(REL_BUCKETS - max_exact)).astype(np.int32)
    return np.where(dist < max_exact, dist, np.minimum(large, REL_BUCKETS - 1))


def _near_bias_diagonals(rel_bias, qt):
    assert qt <= K_CHUNK
    probe = _t5_bucket_table(4 * REL_MAX_DIST)
    first_far = int(np.argmax(probe == REL_BUCKETS - 1))
    assert np.all(probe[first_far:] == REL_BUCKETS - 1)
    n_real = -(-(first_far + K_CHUNK - 1) // K_CHUNK)
    buckets = _t5_bucket_table((n_real + 1) * K_CHUNK)
    period = 2 * K_CHUNK
    u = np.arange(period)
    k = np.arange(n_real)[:, None]
    dist = np.where(u < K_CHUNK, k * K_CHUNK - u, k * K_CHUNK + period - u)
    idx = buckets[np.clip(dist, 0, len(buckets) - 1)]
    rel = rel_bias.astype(F32) - rel_bias[REL_BUCKETS - 1:].astype(F32)
    return jnp.transpose(rel[idx], (0, 2, 1))


def kernel(x, meta, rel_bias, w_in, kv_norm_g, w_uk, w_uv, w_pool, pool_scale, w_o, ln1_g, ln1_b, w_up, conv_w,
           conv_b, w_down, ln2_g, ln2_b):
    B, S, D = x.shape
    assert w_in.shape[0] == DEPTH and S % FFN_ROWS == 0
    T = S + N_META
    tp = -(-T // ROW_ALIGN) * ROW_ALIGN
    assert tp - T >= max(POOL_WINDOWS)
    top_k = min(TOPK_MAX, S // 4)
    assert top_k <= K_CHUNK

    wt = jnp.transpose(w_in[0])
    sizes = (("q", N_HEADS * HEAD_DIM), ("c_kv", KV_RANK), ("q_idx", IDX_HEADS * IDX_DIM), ("k_idx", IDX_DIM),
             ("w_idx", IDX_HEADS), ("u", len(POOL_WINDOWS) * POOL_GROUP))
    cols, start = {}, 0
    for name, size in sizes:
        cols[name] = (start, size)
        start += size
    assert start == wt.shape[0]

    qq, = _proj_qq(x, meta, wt, cols, tp, None)
    pool_diff, w_o_b = _proj_pool(x, meta, wt, cols, tp, w_o[0])
    c_kv, kk, wi = _proj_small(x, meta, wt, cols, kv_norm_g[0].reshape(1, KV_RANK), tp, None)

    wuk = jnp.transpose(w_uk[0], (1, 2, 0)).astype(BF16)
    o_lat, w_up_b, w_down_b = _dsa_attention(qq, wi, kk, c_kv, wuk, rel_bias, top_k, S, (w_up[0], w_down[0]))

    wuv = jnp.transpose(w_uv[0], (1, 0, 2)).astype(BF16)
    h1, h1b = _mix_ln1(o_lat, pool_diff, x, meta, wuv, w_pool[0].astype(BF16), pool_scale[0].reshape(1, -1),
                       w_o_b, ln1_g[0].reshape(1, D), ln1_b[0].reshape(1, D))

    return _ffn_ln2(h1, h1b, w_up_b, conv_w[0], conv_b[0].reshape(1, -1), w_down_b,
                    ln2_g[0].reshape(1, D), ln2_b[0].reshape(1, D), S)
```

```python
import functools
import math

import numpy as np
import jax
import jax.numpy as jnp
from jax import lax
from jax.experimental import pallas as pl
from jax.experimental.pallas import tpu as pltpu

F32 = jnp.float32
BF16 = jnp.bfloat16

N_META = 16
N_HEADS = 8
HEAD_DIM = 128
KV_RANK = 256
IDX_HEADS = 16
IDX_DIM = 64
TOPK_MAX = 256
POOL_WINDOWS = (2, 4, 8, 16)
POOL_GROUP = 256
CONV_WIDTH = 3
REL_BUCKETS = 32
REL_MAX_DIST = 128
DEPTH = 1
ALPHA = (2.0 * DEPTH) ** 0.25
LN_EPS = 1e-5
NEG_INF = -1e30

VMEM_LIMIT_BYTES = 56 * 1024 * 1024
BIG_VMEM_LIMIT_BYTES = 60 * 1024 * 1024
SUBLANES = 8
LANES = 128
ROW_ALIGN = 256
PROJ_ROWS = 1024
Q_TILE = 256
TAIL_TILE = 16
K_CHUNK = 256
MIX_ROWS = 688
FFN_ROWS = 1024
FFN_COLS = 512
HALO = 16
INT_MIN = -(2 ** 31)
PACKED_SUBLANES = 16
TIE_ROUNDS = 8


def _dot(a, b):
    return jnp.dot(a, b, preferred_element_type=F32)


def _dot_nt(a, b):
    return lax.dot_general(a, b, (((1,), (1,)), ((), ())), preferred_element_type=F32)


def _layer_norm(y, g, b):
    mu = jnp.mean(y, axis=-1, keepdims=True)
    yc = y - mu
    var = jnp.mean(yc * yc, axis=-1, keepdims=True)
    return yc * lax.rsqrt(var + LN_EPS) * g + b


def _proj_rows(x_ref, meta_ref, w, tp, store):
    seq = x_ref.shape[0]
    store(pl.ds(0, N_META), _dot_nt(meta_ref[...].astype(BF16), w))
    for r in range(seq // PROJ_ROWS):
        acc = _dot_nt(x_ref[r * PROJ_ROWS:(r + 1) * PROJ_ROWS, :].astype(BF16), w)
        store(pl.ds(N_META + r * PROJ_ROWS, PROJ_ROWS), acc)
    n_pad = tp - seq - N_META
    store(pl.ds(seq + N_META, n_pad), jnp.zeros((n_pad, w.shape[0]), F32))


def _proj_cast_kernel(x_ref, meta_ref, w_ref, o_ref, *, scale_from):
    def store(rows, acc):
        o_ref[rows, :] = acc.astype(o_ref.dtype)

    scale = jnp.where(pl.program_id(1) >= scale_from, IDX_DIM ** -0.5, 1.0)
    _proj_rows(x_ref, meta_ref, (w_ref[...] * scale).astype(BF16), o_ref.shape[0], store)


def _proj_pool_kernel(x_ref, meta_ref, w_ref, o_ref, u_scr):
    def store(rows, acc):
        u_scr[rows, :] = acc

    _proj_rows(x_ref, meta_ref, w_ref[...].astype(BF16), u_scr.shape[0], store)
    group = pl.program_id(1)
    tp = u_scr.shape[0]
    pos = lax.broadcasted_iota(jnp.int32, (tp, 1), 0)
    for g, window in enumerate(POOL_WINDOWS):

        @pl.when(group == g)
        def _():
            u = u_scr[...]
            s = u
            shift = 1
            while shift < window:
                s = s + pltpu.roll(s, shift, axis=0)
                shift *= 2
            count = jnp.minimum(pos + 1, window).astype(F32)
            o_ref[...] = (s / count - u).astype(o_ref.dtype)


def _proj_small_kernel(x_ref, meta_ref, wc_ref, wk_ref, ww_ref, g_ref, c_ref, kk_ref, wi_ref):
    w = jnp.concatenate([wc_ref[...], wk_ref[...], wk_ref[...], ww_ref[...] * (IDX_HEADS ** -0.5),
                         jnp.zeros((LANES - IDX_HEADS, wc_ref.shape[1]), F32)], axis=0).astype(BF16)

    def store(rows, acc):
        c = acc[:, :KV_RANK]
        ms = jnp.mean(c * c, axis=-1, keepdims=True)
        c_ref[rows, :] = (c * lax.rsqrt(ms + LN_EPS) * g_ref[...]).astype(c_ref.dtype)
        kk_ref[rows, :] = acc[:, KV_RANK:KV_RANK + LANES].astype(kk_ref.dtype)
        wi_ref[rows, :] = acc[:, KV_RANK + LANES:]

    _proj_rows(x_ref, meta_ref, w, c_ref.shape[0], store)


def _batch_block(tp, n):
    return pl.BlockSpec((None, tp, n), lambda b, j: (b, 0, 0))


def _compiler_params(semantics, vmem_limit_bytes=VMEM_LIMIT_BYTES):
    return pltpu.CompilerParams(dimension_semantics=semantics, vmem_limit_bytes=vmem_limit_bytes)


def _with_side_cast(body, n_in, n_out):
    def kernel(*refs):
        side_in, side_out = refs[n_in], refs[n_in + 1 + n_out]
        side_out[...] = side_in[...].astype(side_out.dtype)
        body(*refs[:n_in], *refs[n_in + 1:n_in + 1 + n_out], *refs[n_in + 2 + n_out:])

    return kernel


def _slab_block(side, n_slabs, slab_of_step):
    slab = side.shape[0] // n_slabs
    assert slab * n_slabs == side.shape[0] and slab % PACKED_SUBLANES == 0
    return pl.BlockSpec((slab, side.shape[1]), lambda *step: (slab_of_step(*step), 0))


def _w_rows(d, n_rows, first_row):
    return pl.BlockSpec((pl.Element(n_rows), pl.Element(d)), lambda b, j: (first_row(j), 0))


def _proj_call(body, x, meta, wt, w_specs, n_j, extra_in, extra_specs, out_blocks, out_shapes, scratch, name, side):
    B, seq, d = x.shape
    out_blocks = list(out_blocks) if isinstance(out_blocks, (list, tuple)) else [out_blocks]
    out_shapes = list(out_shapes) if isinstance(out_shapes, (list, tuple)) else [out_shapes]
    in_specs = [_batch_block(seq, d), pl.BlockSpec(meta.shape, lambda b, j: (0, 0))] + w_specs + extra_specs
    args = [x, meta] + [wt] * len(w_specs) + list(extra_in)
    if side is not None:
        side_block = _slab_block(side, B * n_j, lambda b, j: b * n_j + j)
        body = _with_side_cast(body, len(args), len(out_blocks))
        in_specs, args = in_specs + [side_block], args + [side]
        out_blocks, out_shapes = out_blocks + [side_block], out_shapes + [jax.ShapeDtypeStruct(side.shape, BF16)]
    return pl.pallas_call(
        body,
        grid=(B, n_j),
        in_specs=in_specs,
        out_specs=out_blocks,
        out_shape=out_shapes,
        scratch_shapes=scratch,
        compiler_params=_compiler_params(("parallel", "arbitrary"), BIG_VMEM_LIMIT_BYTES),
        name=name,
    )(*args)


def _proj_qq(x, meta, wt, cols, tp, side):
    B, d, tn = x.shape[0], x.shape[2], 512
    n_q, n = cols["q"][1] // tn, cols["q"][1] + cols["q_idx"][1]
    first_row = lambda j: pl.multiple_of(
        jnp.where(j < n_q, cols["q"][0] + j * tn, cols["q_idx"][0] + (j - n_q) * tn), PACKED_SUBLANES)
    return _proj_call(functools.partial(_proj_cast_kernel, scale_from=n_q), x, meta, wt,
                      [_w_rows(d, tn, first_row)], n // tn, [], [],
                      pl.BlockSpec((None, tp, tn), lambda b, j: (b, 0, j)),
                      jax.ShapeDtypeStruct((B, tp, n), BF16), [], "proj_qq", side)


def _proj_pool(x, meta, wt, cols, tp, side):
    B, d, n = x.shape[0], x.shape[2], cols["u"][1]
    first_row = lambda j: pl.multiple_of(cols["u"][0] + j * POOL_GROUP, PACKED_SUBLANES)
    return _proj_call(_proj_pool_kernel, x, meta, wt, [_w_rows(d, POOL_GROUP, first_row)], n // POOL_GROUP, [], [],
                      pl.BlockSpec((None, tp, POOL_GROUP), lambda b, j: (b, 0, j)),
                      jax.ShapeDtypeStruct((B, tp, n), BF16), [pltpu.VMEM((tp, POOL_GROUP), F32)], "proj_pool", side)


def _proj_small(x, meta, wt, cols, kv_g, tp, side):
    B, d = x.shape[0], x.shape[2]
    w_specs = [_w_rows(d, cols[name][1], lambda j, name=name: cols[name][0]) for name in ("c_kv", "k_idx", "w_idx")]
    return _proj_call(_proj_small_kernel, x, meta, wt, w_specs, 1, [kv_g],
                      [pl.BlockSpec((1, KV_RANK), lambda b, j: (0, 0))],
                      [_batch_block(tp, KV_RANK), _batch_block(tp, LANES), _batch_block(tp, 128)],
                      [jax.ShapeDtypeStruct((B, tp, KV_RANK), BF16), jax.ShapeDtypeStruct((B, tp, LANES), BF16),
                       jax.ShapeDtypeStruct((B, tp, 128), F32)], [], "proj_small", side)


def _attn_kernel(*refs, n_tiles, aliased, n_sides, **static):
    n_in = 6
    sides_in = refs[n_in + int(aliased):n_in + int(aliased) + n_sides]
    refs = refs[:n_in] + refs[n_in + int(aliased) + n_sides:]
    o_ref = refs[n_in]
    sides_out = refs[n_in + 1:n_in + 1 + n_sides]
    refs = refs[:n_in + 1] + refs[n_in + 1 + n_sides:]

    @pl.when(pl.program_id(1) < n_tiles)
    def _():
        for side_in, side_out in zip(sides_in, sides_out):
            side_out[...] = side_in[...].astype(side_out.dtype)
        _attn_tile(*refs, **static)

    @pl.when(pl.program_id(1) >= n_tiles)
    def _():
        o_ref[...] = jnp.zeros(o_ref.shape, o_ref.dtype)


def _attn_tile(qq_ref, wi_ref, kk_ref, c_ref, wuk_ref, diag_ref, o_ref, nb_scr, sc_scr, sct_scr, scf_scr, mb_scr,
               wt_scr, qa_scr, s_scr, m_scr, l_scr, acc_scr, *, top_k, qt, first_tile):
    i = first_tile + pl.program_id(1)
    n_chunks = ((i + 1) * qt - 1) // K_CHUNK + 1
    attn_w = N_HEADS * HEAD_DIM
    scale = HEAD_DIM ** -0.5
    n_pairs = IDX_HEADS // 2
    lanes_are_queries = qt % LANES == 0

    t_col = i * qt + lax.broadcasted_iota(jnp.int32, (qt, 1), 0)
    t_row = i * qt + lax.broadcasted_iota(jnp.int32, (1, qt), 1)
    s_row = lax.broadcasted_iota(jnp.int32, (1, K_CHUNK), 1)
    s_col = lax.broadcasted_iota(jnp.int32, (K_CHUNK, 1), 0)
    lane_half = lax.broadcasted_iota(jnp.int32, (K_CHUNK, LANES), 1) // IDX_DIM

    n_near = nb_scr.shape[0]

    @pl.when((pl.program_id(0) == 0) & (pl.program_id(1) == 0))
    def _():
        for k in range(n_near - 1):
            for h in range(N_HEADS):
                v = jnp.broadcast_to(diag_ref[k, h:h + 1, :], (qt, 2 * K_CHUNK))
                t = pltpu.roll(v, 0, 1, stride=1, stride_axis=0)
                nb_scr[k, h * qt:(h + 1) * qt, :] = t[:, :K_CHUNK]
        nb_scr[n_near - 1] = jnp.zeros(nb_scr.shape[1:], F32)

    for h in range(N_HEADS):
        qa_scr[h * qt:(h + 1) * qt, :] = _dot(
            qq_ref[:, h * HEAD_DIM:(h + 1) * HEAD_DIM], wuk_ref[h]).astype(BF16)
    if lanes_are_queries:
        wt_scr[...] = wi_ref[...].T

    def bf16_floor(v):
        near = v.astype(BF16)
        bits = lax.bitcast_convert_type(near, jnp.int16)
        below = lax.bitcast_convert_type(bits + jnp.where(bits < 0, jnp.int16(1), jnp.int16(-1)), BF16)
        return jnp.where(near.astype(F32) > v, below, near)

    def idx_chunk(j, carry):
        ks = kk_ref[pl.ds(pl.multiple_of(j * K_CHUNK, K_CHUNK), K_CHUNK), :]
        zero = jnp.zeros(ks.shape, ks.dtype)
        k_even = jnp.where(lane_half == 0, ks, zero)
        k_odd = jnp.where(lane_half == 1, ks, zero)
        score = jnp.zeros((K_CHUNK, qt) if lanes_are_queries else (qt, K_CHUNK), F32)
        if lanes_are_queries:
            for p in range(n_pairs):
                q_pair = qq_ref[:, attn_w + p * LANES:attn_w + (p + 1) * LANES]
                for hh, k_half in ((2 * p, k_even), (2 * p + 1, k_odd)):
                    score = score + jnp.maximum(_dot_nt(k_half, q_pair), 0.0) * wt_scr[hh:hh + 1, :]
        else:
            q_pairs = jnp.concatenate(
                [qq_ref[:, attn_w + p * LANES:attn_w + (p + 1) * LANES] for p in range(n_pairs)], axis=0)
            for half, k_half in enumerate((k_even, k_odd)):
                dots = jnp.maximum(_dot_nt(q_pairs, k_half), 0.0)
                for p in range(n_pairs):
                    hh = 2 * p + half
                    score = score + dots[p * qt:(p + 1) * qt] * wi_ref[:, hh:hh + 1]
        s_pos = j * K_CHUNK + (s_col if lanes_are_queries else s_row)
        score = jnp.where(s_pos <= (t_row if lanes_are_queries else t_col), score, NEG_INF)
        if lanes_are_queries:
            sct_scr[j] = score
            sc_scr[j] = score.T
            scf_scr[j] = bf16_floor(score)
        else:
            sc_scr[j] = score
        return carry

    lax.fori_loop(0, n_chunks, idx_chunk, 0)

    k_f = float(top_k)
    idx_bits = int(math.ceil(math.log2(sc_scr.shape[0] * K_CHUNK)))
    per_query = (1, qt) if lanes_are_queries else (qt, 1)
    key_axis = 0 if lanes_are_queries else 1
    s_idx = s_col if lanes_are_queries else s_row
    search_scr = sct_scr if lanes_are_queries else sc_scr

    def fold_chunks(chunk_fn, combine, init):
        if lanes_are_queries:
            acc_rows = 4 * SUBLANES

            def body(j, acc):
                v = chunk_fn(j, search_scr[j]).reshape(K_CHUNK // acc_rows, acc_rows, qt)
                return combine(acc, functools.reduce(combine, [v[t] for t in range(K_CHUNK // acc_rows)]))

            acc = lax.fori_loop(0, n_chunks, body, jnp.full((acc_rows, qt), init, F32))
        else:
            acc = lax.fori_loop(0, n_chunks, lambda j, acc: combine(acc, chunk_fn(j, search_scr[j])),
                                jnp.full((qt, K_CHUNK), init, F32))
        reduce = jnp.sum if combine is jnp.add else jnp.min
        return reduce(acc, axis=key_axis, keepdims=True)

    def count(pred):
        return fold_chunks(lambda j, sc: jnp.where(pred(j, sc), 1.0, 0.0), jnp.add, 0.0)

    def key_to_float(key):
        return lax.bitcast_convert_type(jnp.where(key < 0, key ^ jnp.int32(0x7FFFFFFF), key), F32)

    def count_coarse(cand_f):
        acc_rows = 4 * PACKED_SUBLANES
        cand_b = jnp.broadcast_to(cand_f, (acc_rows, qt)).astype(BF16)

        def body(j, acc):
            hit = jnp.where(scf_scr[j].reshape(K_CHUNK // acc_rows, acc_rows, qt) >= cand_b[None],
                            jnp.ones((), BF16), jnp.zeros((), BF16))
            return acc + functools.reduce(jnp.add, [hit[t] for t in range(K_CHUNK // acc_rows)])

        acc = lax.fori_loop(0, n_chunks, body, jnp.zeros((acc_rows, qt), BF16))
        return jnp.sum(acc.astype(F32), axis=0, keepdims=True)

    def bit_step(base, carry):
        key, bit = carry
        cand = key + bit
        cand_f = key_to_float(cand)
        n = count(lambda j, sc: sc >= cand_f) if base is None else count_coarse(cand_f - base)
        return jnp.where(n >= k_f, cand, key), lax.shift_right_logical(bit, jnp.int32(1))

    state = (jnp.full(per_query, INT_MIN, jnp.int32), jnp.int32(INT_MIN))
    if lanes_are_queries:
        state = lax.fori_loop(0, 16, lambda _, c: bit_step(0.0, c), state)
        for n_bits in (8, 8):
            base = key_to_float(state[0])

            def rebase(j, carry, base=base):
                scf_scr[j] = bf16_floor(sct_scr[j] - base)
                return carry

            lax.fori_loop(0, n_chunks, rebase, 0)
            state = lax.fori_loop(0, n_bits, lambda _, c, base=base: bit_step(base, c), state)
    else:
        state = lax.fori_loop(0, 32, lambda _, c: bit_step(None, c), state)
    thr = key_to_float(state[0])

    n_ge = count(lambda j, sc: sc >= thr)
    has_tie = jnp.max(jnp.where((n_ge > k_f) & (thr > NEG_INF), 1.0, 0.0)) > 0.0

    def tie_break():
        def next_value(_, m):
            n_gt = count(lambda j, sc: sc > m)
            above = fold_chunks(lambda j, sc: jnp.where(sc > m, sc, jnp.inf), jnp.minimum, jnp.inf)
            return jnp.where(n_gt >= k_f, above, m)

        m = lax.fori_loop(0, TIE_ROUNDS, next_value, thr)
        need = k_f - count(lambda j, sc: sc > m)

        def step(_, carry):
            cut, bit = carry
            cand = cut + bit
            n_before = count(lambda j, sc: (sc == m) & ((j * K_CHUNK + s_idx) < cand))
            return jnp.where(n_before < need, cand, cut), lax.shift_right_logical(bit, jnp.int32(1))

        cut, _ = lax.fori_loop(0, idx_bits, step,
                               (jnp.zeros(per_query, jnp.int32), jnp.int32(2 ** (idx_bits - 1))))
        return m, cut

    def to_rows(v):
        if not lanes_are_queries:
            return jnp.broadcast_to(v, (qt, K_CHUNK))
        t = jnp.broadcast_to(v, (qt, qt)).T
        return jnp.concatenate([t] * (K_CHUNK // qt), axis=1)

    def write_masks(keep_fn):
        def mask_chunk(j, carry):
            s_pos = j * K_CHUNK + s_row
            mb_scr[j] = jnp.where(keep_fn(sc_scr[j], s_pos) & (s_pos <= t_col), 0.0, NEG_INF)
            return carry

        lax.fori_loop(0, n_chunks, mask_chunk, 0)

    def masks_with_ties():
        m, cut = tie_break()
        m_b, cut_b = to_rows(m), to_rows(cut)
        write_masks(lambda sc, s_pos: (sc > m_b) | ((sc == m_b) & (s_pos <= cut_b)))

    def masks_plain():
        thr_b = to_rows(thr)
        write_masks(lambda sc, s_pos: sc >= thr_b)

    lax.cond(has_tie, masks_with_ties, masks_plain)

    rows_h = N_HEADS * qt
    lane_fold = lambda v, op: functools.reduce(op, [v[:, k * LANES:(k + 1) * LANES] for k in range(K_CHUNK // LANES)])

    def key_rows(j):
        start = j * K_CHUNK
        return pl.ds(start if isinstance(j, int) else pl.multiple_of(start, K_CHUNK), K_CHUNK)

    def over_chunks(chunk_fn):
        chunk_fn(0, True)
        lax.fori_loop(1, n_chunks, lambda j, carry: (chunk_fn(j, False), carry)[1], 0)

    def logit_chunk(j, first):
        near = jnp.minimum((i * qt) // K_CHUNK - j, n_near - 1)
        s = _dot_nt(qa_scr[...], c_ref[key_rows(j), :]) * scale + nb_scr[near]
        s = s + jnp.concatenate([mb_scr[j]] * N_HEADS, axis=0)
        s_scr[j] = s
        fold = lane_fold(s, jnp.maximum)
        m_scr[...] = fold if first else jnp.maximum(m_scr[...], fold)

    over_chunks(logit_chunk)
    m_b = jnp.broadcast_to(jnp.max(m_scr[...], axis=-1, keepdims=True), (rows_h, LANES))
    m_scr[...] = m_b

    def value_chunk(j, first):
        p = jnp.exp(s_scr[j] - jnp.concatenate([m_scr[...]] * (K_CHUNK // LANES), axis=1))
        pv = _dot(p.astype(BF16), c_ref[key_rows(j), :])
        l_scr[...] = lane_fold(p, jnp.add) if first else l_scr[...] + lane_fold(p, jnp.add)
        acc_scr[...] = pv if first else acc_scr[...] + pv

    over_chunks(value_chunk)
    out = acc_scr[...] * (1.0 / jnp.sum(l_scr[...], axis=-1, keepdims=True))
    for h in range(N_HEADS):
        o_ref[:, h * KV_RANK:(h + 1) * KV_RANK] = out[h * qt:(h + 1) * qt].astype(o_ref.dtype)


def _dsa_attention(qq, wi, kk, c_kv, wuk, rel_bias, top_k, seq, sides):
    B, tp, _ = qq.shape
    n_main = (seq + N_META) // Q_TILE
    assert (seq + N_META) - n_main * Q_TILE <= TAIL_TILE and Q_TILE % K_CHUNK == 0 and K_CHUNK % TAIL_TILE == 0
    n_chunks_max = tp // K_CHUNK
    width = N_HEADS * KV_RANK

    def call(qt, first_tile, n_tiles, n_fill, prev, sides=()):
        rows_h = N_HEADS * qt
        diag = _near_bias_diagonals(rel_bias, qt)
        tile = lambda i: first_tile + jnp.minimum(i, n_tiles - 1)
        in_specs = [
            pl.BlockSpec((None, qt, qq.shape[2]), lambda b, i: (b, tile(i), 0)),
            pl.BlockSpec((None, qt, 128), lambda b, i: (b, tile(i), 0)),
            pl.BlockSpec((None, tp, LANES), lambda b, i: (b, 0, 0)),
            pl.BlockSpec((None, tp, KV_RANK), lambda b, i: (b, 0, 0)),
            pl.BlockSpec(wuk.shape, lambda b, i: (0, 0, 0)),
            pl.BlockSpec(diag.shape, lambda b, i: (0, 0, 0)),
        ]
        args = [qq, wi, kk, c_kv, wuk, diag]
        out_specs = [pl.BlockSpec((None, qt, width), lambda b, i: (b, first_tile + i, 0))]
        out_shape = [jax.ShapeDtypeStruct((B, tp, width), BF16)]
        aliases = {}
        if prev is not None:
            in_specs.append(pl.BlockSpec(memory_space=pl.ANY))
            args.append(prev)
            aliases = {len(args) - 1: 0}
        for side in sides:
            side_block = _slab_block(side, B * n_tiles, lambda b, i: b * n_tiles + jnp.minimum(i, n_tiles - 1))
            in_specs.append(side_block)
            args.append(side)
            out_specs.append(side_block)
            out_shape.append(jax.ShapeDtypeStruct(side.shape, BF16))
        return pl.pallas_call(
            functools.partial(_attn_kernel, n_tiles=n_tiles, aliased=prev is not None, n_sides=len(sides),
                              top_k=top_k, qt=qt, first_tile=first_tile),
            grid=(B, n_tiles + n_fill),
            in_specs=in_specs,
            out_specs=out_specs,
            out_shape=out_shape,
            input_output_aliases=aliases,
            scratch_shapes=[pltpu.VMEM((diag.shape[0] + 1, rows_h, K_CHUNK), F32),
                            pltpu.VMEM((n_chunks_max, qt, K_CHUNK), F32),
                            pltpu.VMEM((n_chunks_max, K_CHUNK, qt), F32),
                            pltpu.VMEM((n_chunks_max, K_CHUNK, qt), BF16),
                            pltpu.VMEM((n_chunks_max, qt, K_CHUNK), F32),
                            pltpu.VMEM((LANES, qt), F32),
                            pltpu.VMEM((rows_h, KV_RANK), BF16),
                            pltpu.VMEM((n_chunks_max, rows_h, K_CHUNK), F32),
                            pltpu.VMEM((rows_h, LANES), F32),
                            pltpu.VMEM((rows_h, LANES), F32),
                            pltpu.VMEM((rows_h, KV_RANK), F32)],
            compiler_params=_compiler_params(("arbitrary", "arbitrary"), BIG_VMEM_LIMIT_BYTES),
            name="dsa_attention" if prev is None else "dsa_attention_tail",
        )(*args)

    o_lat, *sides_b = call(Q_TILE, 0, n_main, tp // Q_TILE - n_main, None, sides)
    return (call(TAIL_TILE, n_main * Q_TILE // TAIL_TILE, 1, 0, o_lat)[0], *sides_b)


def _mix_window_start(r, seq):
    return min(max(r * MIX_ROWS - N_META, 0), seq - MIX_ROWS)


def _residual_rows(x_ref, meta_ref, h_scr, seq):
    r = pl.program_id(1)
    n_real = -(-(seq + N_META) // MIX_ROWS)
    for rv in range(n_real):

        @pl.when(r == rv)
        def _(rv=rv):
            skip = rv * MIX_ROWS - N_META - _mix_window_start(rv, seq)
            if rv == 0:
                h_scr[:N_META, :] = meta_ref[...]
                h_scr[N_META:, :] = x_ref[:MIX_ROWS - N_META, :]
            elif skip == 0:
                h_scr[...] = x_ref[...]
            else:
                h_scr[:MIX_ROWS - skip, :] = x_ref[skip:, :]
                h_scr[MIX_ROWS - skip:, :] = jnp.zeros((skip, h_scr.shape[1]), F32)

    @pl.when(r >= n_real)
    def _():
        h_scr[...] = jnp.zeros(h_scr.shape, F32)


def _mix_kernel(ol_ref, pd_ref, x_ref, meta_ref, wuv_ref, wp_ref, ps_ref, wo_ref, g_ref, b_ref, h1_ref, h1b_ref,
                h_scr, *, seq):
    _residual_rows(x_ref, meta_ref, h_scr, seq)
    attn = [_dot(ol_ref[:, h * KV_RANK:(h + 1) * KV_RANK], wuv_ref[h]) for h in range(N_HEADS)]
    pool = [_dot(pd_ref[:, g * POOL_GROUP:(g + 1) * POOL_GROUP], wp_ref[g]) for g in range(len(POOL_WINDOWS))]
    pool = jnp.concatenate(pool, axis=-1) * ps_ref[...]
    cat = jnp.concatenate(attn + [pool], axis=-1).astype(BF16)
    y = ALPHA * h_scr[...] + _dot(cat, wo_ref[...])
    h1 = _layer_norm(y, g_ref[...], b_ref[...])
    h1_ref[...] = h1
    h1b_ref[...] = h1.astype(h1b_ref.dtype)


def _mix_ln1(o_lat, pool_diff, x, meta, wuv, wpool, pool_scale, wo, g, b):
    B, seq, d = x.shape
    n_rows = seq + N_META
    assert n_rows % MIX_ROWS == 0
    row = lambda n: pl.BlockSpec((None, MIX_ROWS, n), lambda bi, r: (bi, r, 0))
    full = lambda a: pl.BlockSpec(a.shape, lambda bi, r: (0,) * a.ndim, pipeline_mode=pl.Buffered(1))
    window = pl.BlockSpec(
        (None, pl.Element(MIX_ROWS), pl.Element(d)),
        lambda bi, r: (bi, pl.multiple_of(jnp.clip(r * MIX_ROWS - N_META, 0, seq - MIX_ROWS), N_META), 0))
    return pl.pallas_call(
        functools.partial(_mix_kernel, seq=seq),
        grid=(B, n_rows // MIX_ROWS),
        in_specs=[row(o_lat.shape[2]), row(pool_diff.shape[2]), window, full(meta),
                  full(wuv), full(wpool), full(pool_scale), full(wo), full(g), full(b)],
        out_specs=[row(d), row(d)],
        out_shape=[jax.ShapeDtypeStruct((B, n_rows, d), F32), jax.ShapeDtypeStruct((B, n_rows, d), BF16)],
        scratch_shapes=[pltpu.VMEM((MIX_ROWS, d), F32)],
        compiler_params=_compiler_params(("parallel", "arbitrary"), BIG_VMEM_LIMIT_BYTES),
        name="mix_ln1",
    )(o_lat, pool_diff, x, meta, wuv, wpool, pool_scale, wo, g, b)


def _gelu_tanh(x):
    return 0.5 * x * (1.0 + jnp.tanh(math.sqrt(2.0 / math.pi) * (x + 0.044715 * (x * x * x))))


def _ffn_kernel(hw_ref, hres_ref, wa_ref, wg_ref, cwa_ref, cwg_ref, cba_ref, cbg_ref, wd_ref, g_ref, b_ref, o_ref,
                za_scr, zg_scr):
    c = pl.program_id(2)
    n_c = pl.num_programs(2) - 1

    def up(slot):
        x = hw_ref[...]
        za_scr[slot] = _dot(x, wa_ref[...])
        zg_scr[slot] = _dot(x, wg_ref[...])

    def conv(z, cw_ref, cb_ref):
        cw = cw_ref[...]
        n = z.shape[0]
        out = z[HALO - 2:n - 2] * cw[0:1] + z[HALO - 1:n - 1] * cw[1:2] + z[HALO:] * cw[2:3]
        return out + cb_ref[...]

    def down(slot):
        a = conv(za_scr[slot], cwa_ref, cba_ref)
        gate = conv(zg_scr[slot], cwg_ref, cbg_ref)
        act = (_gelu_tanh(a) * gate).astype(BF16)
        o_ref[...] += _dot(act, wd_ref[...])

    @pl.when(c == 0)
    def _():
        o_ref[...] = jnp.zeros(o_ref.shape, o_ref.dtype)
        up(0)

    @pl.when((c > 0) & (c < n_c))
    def _():
        up(c % 2)
        down((c - 1) % 2)

    @pl.when(c == n_c)
    def _():
        down((c - 1) % 2)
        o_ref[...] = _layer_norm(ALPHA * hres_ref[...] + o_ref[...], g_ref[...], b_ref[...])


def _ffn_ln2(h1, h1b, w_up, conv_w, conv_b, w_down, g, b, seq):
    B, tp, d = h1.shape
    d_ff = w_down.shape[0]
    n_c = d_ff // FFN_COLS
    up_c = lambda c: jnp.minimum(c, n_c - 1)
    dn_c = lambda c: jnp.maximum(c - 1, 0)
    vec = lambda off: pl.BlockSpec((1, FFN_COLS), lambda bi, r, c: (0, dn_c(c) + off))
    return pl.pallas_call(
        _ffn_kernel,
        grid=(B, seq // FFN_ROWS, n_c + 1),
        in_specs=[
            pl.BlockSpec((None, pl.Element(FFN_ROWS + HALO), pl.Element(d)),
                         lambda bi, r, c: (bi, r * FFN_ROWS + N_META - HALO, 0)),
            pl.BlockSpec((None, pl.Element(FFN_ROWS), pl.Element(d)),
                         lambda bi, r, c: (bi, pl.multiple_of(r * FFN_ROWS + N_META, N_META), 0)),
            pl.BlockSpec((d, FFN_COLS), lambda bi, r, c: (0, up_c(c))),
            pl.BlockSpec((d, FFN_COLS), lambda bi, r, c: (0, up_c(c) + n_c)),
            pl.BlockSpec((CONV_WIDTH, FFN_COLS), lambda bi, r, c: (0, dn_c(c))),
            pl.BlockSpec((CONV_WIDTH, FFN_COLS), lambda bi, r, c: (0, dn_c(c) + n_c)),
            vec(0), vec(n_c),
            pl.BlockSpec((FFN_COLS, d), lambda bi, r, c: (dn_c(c), 0)),
            pl.BlockSpec((1, d), lambda bi, r, c: (0, 0)),
            pl.BlockSpec((1, d), lambda bi, r, c: (0, 0)),
        ],
        out_specs=pl.BlockSpec((None, FFN_ROWS, d), lambda bi, r, c: (bi, r, 0), pipeline_mode=pl.Buffered(1)),
        out_shape=jax.ShapeDtypeStruct((B, seq, d), F32),
        scratch_shapes=[pltpu.VMEM((2, FFN_ROWS + HALO, FFN_COLS), F32),
                        pltpu.VMEM((2, FFN_ROWS + HALO, FFN_COLS), F32)],
        compiler_params=_compiler_params(("parallel", "parallel", "arbitrary"), BIG_VMEM_LIMIT_BYTES),
        name="ffn_ln2",
    )(h1b, h1, w_up, w_up, conv_w, conv_w, conv_b, conv_b, w_down, g, b)


def _t5_bucket_table(n):
    dist = np.arange(n, dtype=np.int32)
    max_exact = REL_BUCKETS // 2
    d_f = np.maximum(dist, 1).astype(np.float32)
    large = max_exact + (np.log(d_f / np.float32(max_exact)) / np.float32(math.log(REL_MAX_DIST / max_exact))
                         * np.float32(REL_BUCKETS - max_exact)).astype(np.int32)
    return np.where(dist < max_exact, dist, np.minimum(large, REL_BUCKETS - 1))


def _near_bias_diagonals(rel_bias, qt):
    assert qt <= K_CHUNK
    probe = _t5_bucket_table(4 * REL_MAX_DIST)
    first_far = int(np.argmax(probe == REL_BUCKETS - 1))
    assert np.all(probe[first_far:] == REL_BUCKETS - 1)
    n_real = -(-(first_far + K_CHUNK - 1) // K_CHUNK)
    buckets = _t5_bucket_table((n_real + 1) * K_CHUNK)
    period = 2 * K_CHUNK
    u = np.arange(period)
    k = np.arange(n_real)[:, None]
    dist = np.where(u < K_CHUNK, k * K_CHUNK - u, k * K_CHUNK + period - u)
    idx = buckets[np.clip(dist, 0, len(buckets) - 1)]
    rel = rel_bias.astype(F32) - rel_bias[REL_BUCKETS - 1:].astype(F32)
    return jnp.transpose(rel[idx], (0, 2, 1))


def kernel(x, meta, rel_bias, w_in, kv_norm_g, w_uk, w_uv, w_pool, pool_scale, w_o, ln1_g, ln1_b, w_up, conv_w,
           conv_b, w_down, ln2_g, ln2_b):
    B, S, D = x.shape
    assert w_in.shape[0] == DEPTH and S % FFN_ROWS == 0
    T = S + N_META
    tp = -(-T // ROW_ALIGN) * ROW_ALIGN
    assert tp - T >= max(POOL_WINDOWS)
    top_k = min(TOPK_MAX, S // 4)
    assert top_k <= K_CHUNK

    wt = jnp.transpose(w_in[0])
    sizes = (("q", N_HEADS * HEAD_DIM), ("c_kv", KV_RANK), ("q_idx", IDX_HEADS * IDX_DIM), ("k_idx", IDX_DIM),
             ("w_idx", IDX_HEADS), ("u", len(POOL_WINDOWS) * POOL_GROUP))
    cols, start = {}, 0
    for name, size in sizes:
        cols[name] = (start, size)
        start += size
    assert start == wt.shape[0]

    qq, = _proj_qq(x, meta, wt, cols, tp, None)
    pool_diff, w_o_b = _proj_pool(x, meta, wt, cols, tp, w_o[0])
    c_kv, kk, wi = _proj_small(x, meta, wt, cols, kv_norm_g[0].reshape(1, KV_RANK), tp, None)

    wuk = jnp.transpose(w_uk[0], (1, 2, 0)).astype(BF16)
    o_lat, w_up_b, w_down_b = _dsa_attention(qq, wi, kk, c_kv, wuk, rel_bias, top_k, S, (w_up[0], w_down[0]))

    wuv = jnp.transpose(w_uv[0], (1, 0, 2)).astype(BF16)
    h1, h1b = _mix_ln1(o_lat, pool_diff, x, meta, wuv, w_pool[0].astype(BF16), pool_scale[0].reshape(1, -1),
                       w_o_b, ln1_g[0].reshape(1, D), ln1_b[0].reshape(1, D))

    return _ffn_ln2(h1, h1b, w_up_b, conv_w[0], conv_b[0].reshape(1, -1), w_down_b,
                    ln2_g[0].reshape(1, D), ln2_b[0].reshape(1, D), S)
```

```python
import functools
import math

import numpy as np
import jax
import jax.numpy as jnp
from jax import lax
from jax.experimental import pallas as pl
from jax.experimental.pallas import tpu as pltpu

F32 = jnp.float32
BF16 = jnp.bfloat16

N_META = 16
N_HEADS = 8
HEAD_DIM = 128
KV_RANK = 256
IDX_HEADS = 16
IDX_DIM = 64
TOPK_MAX = 256
POOL_WINDOWS = (2, 4, 8, 16)
POOL_GROUP = 256
CONV_WIDTH = 3
REL_BUCKETS = 32
REL_MAX_DIST = 128
DEPTH = 1
ALPHA = (2.0 * DEPTH) ** 0.25
LN_EPS = 1e-5
NEG_INF = -1e30

VMEM_LIMIT_BYTES = 56 * 1024 * 1024
BIG_VMEM_LIMIT_BYTES = 60 * 1024 * 1024
SUBLANES = 8
LANES = 128
ROW_ALIGN = 256
PROJ_ROWS = 1024
Q_TILE = 256
TAIL_TILE = 16
K_CHUNK = 256
MIX_ROWS = 688
FFN_ROWS = 1024
FFN_COLS = 512
HALO = 16
INT_MIN = -(2 ** 31)
PACKED_SUBLANES = 16
TIE_ROUNDS = 8


def _dot(a, b):
    return jnp.dot(a, b, preferred_element_type=F32)


def _dot_nt(a, b):
    return lax.dot_general(a, b, (((1,), (1,)), ((), ())), preferred_element_type=F32)


def _layer_norm(y, g, b):
    mu = jnp.mean(y, axis=-1, keepdims=True)
    yc = y - mu
    var = jnp.mean(yc * yc, axis=-1, keepdims=True)
    return yc * lax.rsqrt(var + LN_EPS) * g + b


def _proj_rows(x_ref, meta_ref, w, tp, store):
    seq = x_ref.shape[0]
    store(pl.ds(0, N_META), _dot_nt(meta_ref[...].astype(BF16), w))
    for r in range(seq // PROJ_ROWS):
        acc = _dot_nt(x_ref[r * PROJ_ROWS:(r + 1) * PROJ_ROWS, :].astype(BF16), w)
        store(pl.ds(N_META + r * PROJ_ROWS, PROJ_ROWS), acc)
    n_pad = tp - seq - N_META
    store(pl.ds(seq + N_META, n_pad), jnp.zeros((n_pad, w.shape[0]), F32))


def _proj_cast_kernel(x_ref, meta_ref, w_ref, o_ref, *, scale_from):
    def store(rows, acc):
        o_ref[rows, :] = acc.astype(o_ref.dtype)

    scale = jnp.where(pl.program_id(1) >= scale_from, IDX_DIM ** -0.5, 1.0)
    _proj_rows(x_ref, meta_ref, (w_ref[...] * scale).astype(BF16), o_ref.shape[0], store)


def _proj_pool_kernel(x_ref, meta_ref, w_ref, o_ref, u_scr):
    def store(rows, acc):
        u_scr[rows, :] = acc

    _proj_rows(x_ref, meta_ref, w_ref[...].astype(BF16), u_scr.shape[0], store)
    group = pl.program_id(1)
    tp = u_scr.shape[0]
    pos = lax.broadcasted_iota(jnp.int32, (tp, 1), 0)
    for g, window in enumerate(POOL_WINDOWS):

        @pl.when(group == g)
        def _():
            u = u_scr[...]
            s = u
            shift = 1
            while shift < window:
                s = s + pltpu.roll(s, shift, axis=0)
                shift *= 2
            count = jnp.minimum(pos + 1, window).astype(F32)
            o_ref[...] = (s / count - u).astype(o_ref.dtype)


def _proj_small_kernel(x_ref, meta_ref, wc_ref, wk_ref, ww_ref, g_ref, c_ref, kk_ref, wi_ref):
    w = jnp.concatenate([wc_ref[...], wk_ref[...], wk_ref[...], ww_ref[...] * (IDX_HEADS ** -0.5),
                         jnp.zeros((LANES - IDX_HEADS, wc_ref.shape[1]), F32)], axis=0).astype(BF16)

    def store(rows, acc):
        c = acc[:, :KV_RANK]
        ms = jnp.mean(c * c, axis=-1, keepdims=True)
        c_ref[rows, :] = (c * lax.rsqrt(ms + LN_EPS) * g_ref[...]).astype(c_ref.dtype)
        kk_ref[rows, :] = acc[:, KV_RANK:KV_RANK + LANES].astype(kk_ref.dtype)
        wi_ref[rows, :] = acc[:, KV_RANK + LANES:]

    _proj_rows(x_ref, meta_ref, w, c_ref.shape[0], store)


def _batch_block(tp, n):
    return pl.BlockSpec((None, tp, n), lambda b, j: (b, 0, 0))


def _compiler_params(semantics, vmem_limit_bytes=VMEM_LIMIT_BYTES):
    return pltpu.CompilerParams(dimension_semantics=semantics, vmem_limit_bytes=vmem_limit_bytes)


def _with_side_cast(body, n_in, n_out):
    def kernel(*refs):
        side_in, side_out = refs[n_in], refs[n_in + 1 + n_out]
        side_out[...] = side_in[...].astype(side_out.dtype)
        body(*refs[:n_in], *refs[n_in + 1:n_in + 1 + n_out], *refs[n_in + 2 + n_out:])

    return kernel


def _slab_block(side, n_slabs, slab_of_step):
    slab = side.shape[0] // n_slabs
    assert slab * n_slabs == side.shape[0] and slab % PACKED_SUBLANES == 0
    return pl.BlockSpec((slab, side.shape[1]), lambda *step: (slab_of_step(*step), 0))


def _w_rows(d, n_rows, first_row):
    return pl.BlockSpec((pl.Element(n_rows), pl.Element(d)), lambda b, j: (first_row(j), 0))


def _proj_call(body, x, meta, wt, w_specs, n_j, extra_in, extra_specs, out_blocks, out_shapes, scratch, name, side):
    B, seq, d = x.shape
    out_blocks = list(out_blocks) if isinstance(out_blocks, (list, tuple)) else [out_blocks]
    out_shapes = list(out_shapes) if isinstance(out_shapes, (list, tuple)) else [out_shapes]
    in_specs = [_batch_block(seq, d), pl.BlockSpec(meta.shape, lambda b, j: (0, 0))] + w_specs + extra_specs
    args = [x, meta] + [wt] * len(w_specs) + list(extra_in)
    if side is not None:
        side_block = _slab_block(side, B * n_j, lambda b, j: b * n_j + j)
        body = _with_side_cast(body, len(args), len(out_blocks))
        in_specs, args = in_specs + [side_block], args + [side]
        out_blocks, out_shapes = out_blocks + [side_block], out_shapes + [jax.ShapeDtypeStruct(side.shape, BF16)]
    return pl.pallas_call(
        body,
        grid=(B, n_j),
        in_specs=in_specs,
        out_specs=out_blocks,
        out_shape=out_shapes,
        scratch_shapes=scratch,
        compiler_params=_compiler_params(("parallel", "arbitrary"), BIG_VMEM_LIMIT_BYTES),
        name=name,
    )(*args)


def _proj_qq(x, meta, wt, cols, tp, side):
    B, d, tn = x.shape[0], x.shape[2], 512
    n_q, n = cols["q"][1] // tn, cols["q"][1] + cols["q_idx"][1]
    first_row = lambda j: pl.multiple_of(
        jnp.where(j < n_q, cols["q"][0] + j * tn, cols["q_idx"][0] + (j - n_q) * tn), PACKED_SUBLANES)
    return _proj_call(functools.partial(_proj_cast_kernel, scale_from=n_q), x, meta, wt,
                      [_w_rows(d, tn, first_row)], n // tn, [], [],
                      pl.BlockSpec((None, tp, tn), lambda b, j: (b, 0, j)),
                      jax.ShapeDtypeStruct((B, tp, n), BF16), [], "proj_qq", side)


def _proj_pool(x, meta, wt, cols, tp, side):
    B, d, n = x.shape[0], x.shape[2], cols["u"][1]
    first_row = lambda j: pl.multiple_of(cols["u"][0] + j * POOL_GROUP, PACKED_SUBLANES)
    return _proj_call(_proj_pool_kernel, x, meta, wt, [_w_rows(d, POOL_GROUP, first_row)], n // POOL_GROUP, [], [],
                      pl.BlockSpec((None, tp, POOL_GROUP), lambda b, j: (b, 0, j)),
                      jax.ShapeDtypeStruct((B, tp, n), BF16), [pltpu.VMEM((tp, POOL_GROUP), F32)], "proj_pool", side)


def _proj_small(x, meta, wt, cols, kv_g, tp, side):
    B, d = x.shape[0], x.shape[2]
    w_specs = [_w_rows(d, cols[name][1], lambda j, name=name: cols[name][0]) for name in ("c_kv", "k_idx", "w_idx")]
    return _proj_call(_proj_small_kernel, x, meta, wt, w_specs, 1, [kv_g],
                      [pl.BlockSpec((1, KV_RANK), lambda b, j: (0, 0))],
                      [_batch_block(tp, KV_RANK), _batch_block(tp, LANES), _batch_block(tp, 128)],
                      [jax.ShapeDtypeStruct((B, tp, KV_RANK), BF16), jax.ShapeDtypeStruct((B, tp, LANES), BF16),
                       jax.ShapeDtypeStruct((B, tp, 128), F32)], [], "proj_small", side)


def _attn_kernel(*refs, n_tiles, aliased, n_sides, **static):
    n_in = 6
    sides_in = refs[n_in + int(aliased):n_in + int(aliased) + n_sides]
    refs = refs[:n_in] + refs[n_in + int(aliased) + n_sides:]
    o_ref = refs[n_in]
    sides_out = refs[n_in + 1:n_in + 1 + n_sides]
    refs = refs[:n_in + 1] + refs[n_in + 1 + n_sides:]

    @pl.when(pl.program_id(1) < n_tiles)
    def _():
        for side_in, side_out in zip(sides_in, sides_out):
            side_out[...] = side_in[...].astype(side_out.dtype)
        _attn_tile(*refs, **static)

    @pl.when(pl.program_id(1) >= n_tiles)
    def _():
        o_ref[...] = jnp.zeros(o_ref.shape, o_ref.dtype)


def _attn_tile(qq_ref, wi_ref, kk_ref, c_ref, wuk_ref, diag_ref, o_ref, nb_scr, sc_scr, sct_scr, scf_scr, mb_scr,
               wt_scr, qa_scr, s_scr, m_scr, l_scr, acc_scr, *, top_k, qt, first_tile, single_tile):
    i = first_tile if single_tile else first_tile + pl.program_id(1)
    n_chunks = ((i + 1) * qt - 1) // K_CHUNK + 1

    def chunk_loop(lo, body, init):
        return lax.fori_loop(lo, n_chunks, body, init, unroll=True if single_tile else 1)
    attn_w = N_HEADS * HEAD_DIM
    scale = HEAD_DIM ** -0.5
    n_pairs = IDX_HEADS // 2
    lanes_are_queries = qt % LANES == 0

    t_col = i * qt + lax.broadcasted_iota(jnp.int32, (qt, 1), 0)
    t_row = i * qt + lax.broadcasted_iota(jnp.int32, (1, qt), 1)
    s_row = lax.broadcasted_iota(jnp.int32, (1, K_CHUNK), 1)
    s_col = lax.broadcasted_iota(jnp.int32, (K_CHUNK, 1), 0)
    lane_half = lax.broadcasted_iota(jnp.int32, (K_CHUNK, LANES), 1) // IDX_DIM

    n_near = nb_scr.shape[0]

    @pl.when((pl.program_id(0) == 0) & (pl.program_id(1) == 0))
    def _():
        for k in range(n_near - 1):
            for h in range(N_HEADS):
                v = jnp.broadcast_to(diag_ref[k, h:h + 1, :], (qt, 2 * K_CHUNK))
                t = pltpu.roll(v, 0, 1, stride=1, stride_axis=0)
                nb_scr[k, h * qt:(h + 1) * qt, :] = t[:, :K_CHUNK]
        nb_scr[n_near - 1] = jnp.zeros(nb_scr.shape[1:], F32)

    for h in range(N_HEADS):
        qa_scr[h * qt:(h + 1) * qt, :] = _dot(
            qq_ref[:, h * HEAD_DIM:(h + 1) * HEAD_DIM], wuk_ref[h]).astype(BF16)
    if lanes_are_queries:
        wt_scr[...] = wi_ref[...].T

    def bf16_floor(v):
        near = v.astype(BF16)
        bits = lax.bitcast_convert_type(near, jnp.int16)
        below = lax.bitcast_convert_type(bits + jnp.where(bits < 0, jnp.int16(1), jnp.int16(-1)), BF16)
        return jnp.where(near.astype(F32) > v, below, near)

    def idx_chunk(j, carry):
        ks = kk_ref[pl.ds(pl.multiple_of(j * K_CHUNK, K_CHUNK), K_CHUNK), :]
        zero = jnp.zeros(ks.shape, ks.dtype)
        k_even = jnp.where(lane_half == 0, ks, zero)
        k_odd = jnp.where(lane_half == 1, ks, zero)
        score = jnp.zeros((K_CHUNK, qt) if lanes_are_queries else (qt, K_CHUNK), F32)
        if lanes_are_queries:
            for p in range(n_pairs):
                q_pair = qq_ref[:, attn_w + p * LANES:attn_w + (p + 1) * LANES]
                for hh, k_half in ((2 * p, k_even), (2 * p + 1, k_odd)):
                    score = score + jnp.maximum(_dot_nt(k_half, q_pair), 0.0) * wt_scr[hh:hh + 1, :]
        else:
            q_pairs = jnp.concatenate(
                [qq_ref[:, attn_w + p * LANES:attn_w + (p + 1) * LANES] for p in range(n_pairs)], axis=0)
            for half, k_half in enumerate((k_even, k_odd)):
                dots = jnp.maximum(_dot_nt(q_pairs, k_half), 0.0)
                for p in range(n_pairs):
                    hh = 2 * p + half
                    score = score + dots[p * qt:(p + 1) * qt] * wi_ref[:, hh:hh + 1]
        s_pos = j * K_CHUNK + (s_col if lanes_are_queries else s_row)
        score = jnp.where(s_pos <= (t_row if lanes_are_queries else t_col), score, NEG_INF)
        if lanes_are_queries:
            sct_scr[j] = score
            sc_scr[j] = score.T
            scf_scr[j] = bf16_floor(score)
        else:
            sc_scr[j] = score
        return carry

    chunk_loop(0, idx_chunk, 0)

    k_f = float(top_k)
    idx_bits = int(math.ceil(math.log2(sc_scr.shape[0] * K_CHUNK)))
    per_query = (1, qt) if lanes_are_queries else (qt, 1)
    key_axis = 0 if lanes_are_queries else 1
    s_idx = s_col if lanes_are_queries else s_row
    search_scr = sct_scr if lanes_are_queries else sc_scr

    def fold_chunks(chunk_fn, combine, init):
        if lanes_are_queries:
            acc_rows = 4 * SUBLANES

            def body(j, acc):
                v = chunk_fn(j, search_scr[j]).reshape(K_CHUNK // acc_rows, acc_rows, qt)
                return combine(acc, functools.reduce(combine, [v[t] for t in range(K_CHUNK // acc_rows)]))

            acc = chunk_loop(0, body, jnp.full((acc_rows, qt), init, F32))
        else:
            acc = chunk_loop(0, lambda j, acc: combine(acc, chunk_fn(j, search_scr[j])),
                                jnp.full((qt, K_CHUNK), init, F32))
        reduce = jnp.sum if combine is jnp.add else jnp.min
        return reduce(acc, axis=key_axis, keepdims=True)

    def count(pred):
        return fold_chunks(lambda j, sc: jnp.where(pred(j, sc), 1.0, 0.0), jnp.add, 0.0)

    def key_to_float(key):
        return lax.bitcast_convert_type(jnp.where(key < 0, key ^ jnp.int32(0x7FFFFFFF), key), F32)

    def count_coarse(cand_f):
        acc_rows = 4 * PACKED_SUBLANES
        cand_b = jnp.broadcast_to(cand_f, (acc_rows, qt)).astype(BF16)

        def body(j, acc):
            hit = jnp.where(scf_scr[j].reshape(K_CHUNK // acc_rows, acc_rows, qt) >= cand_b[None],
                            jnp.ones((), BF16), jnp.zeros((), BF16))
            return acc + functools.reduce(jnp.add, [hit[t] for t in range(K_CHUNK // acc_rows)])

        acc = chunk_loop(0, body, jnp.zeros((acc_rows, qt), BF16))
        return jnp.sum(acc.astype(F32), axis=0, keepdims=True)

    def bit_step(base, carry):
        key, bit = carry
        cand = key + bit
        cand_f = key_to_float(cand)
        n = count(lambda j, sc: sc >= cand_f) if base is None else count_coarse(cand_f - base)
        return jnp.where(n >= k_f, cand, key), lax.shift_right_logical(bit, jnp.int32(1))

    state = (jnp.full(per_query, INT_MIN, jnp.int32), jnp.int32(INT_MIN))
    if lanes_are_queries:
        state = lax.fori_loop(0, 16, lambda _, c: bit_step(0.0, c), state)
        for n_bits in (8, 8):
            base = key_to_float(state[0])

            def rebase(j, carry, base=base):
                scf_scr[j] = bf16_floor(sct_scr[j] - base)
                return carry

            chunk_loop(0, rebase, 0)
            state = lax.fori_loop(0, n_bits, lambda _, c, base=base: bit_step(base, c), state)
    else:
        state = lax.fori_loop(0, 32, lambda _, c: bit_step(None, c), state)
    thr = key_to_float(state[0])

    n_ge = count(lambda j, sc: sc >= thr)
    has_tie = jnp.max(jnp.where((n_ge > k_f) & (thr > NEG_INF), 1.0, 0.0)) > 0.0

    def tie_break():
        def next_value(_, m):
            n_gt = count(lambda j, sc: sc > m)
            above = fold_chunks(lambda j, sc: jnp.where(sc > m, sc, jnp.inf), jnp.minimum, jnp.inf)
            return jnp.where(n_gt >= k_f, above, m)

        m = lax.fori_loop(0, TIE_ROUNDS, next_value, thr)
        need = k_f - count(lambda j, sc: sc > m)

        def step(_, carry):
            cut, bit = carry
            cand = cut + bit
            n_before = count(lambda j, sc: (sc == m) & ((j * K_CHUNK + s_idx) < cand))
            return jnp.where(n_before < need, cand, cut), lax.shift_right_logical(bit, jnp.int32(1))

        cut, _ = lax.fori_loop(0, idx_bits, step,
                               (jnp.zeros(per_query, jnp.int32), jnp.int32(2 ** (idx_bits - 1))))
        return m, cut

    def to_rows(v):
        if not lanes_are_queries:
            return jnp.broadcast_to(v, (qt, K_CHUNK))
        t = jnp.broadcast_to(v, (qt, qt)).T
        return jnp.concatenate([t] * (K_CHUNK // qt), axis=1)

    def write_masks(keep_fn):
        def mask_chunk(j, carry):
            s_pos = j * K_CHUNK + s_row
            mb_scr[j] = jnp.where(keep_fn(sc_scr[j], s_pos) & (s_pos <= t_col), 0.0, NEG_INF)
            return carry

        chunk_loop(0, mask_chunk, 0)

    def masks_with_ties():
        m, cut = tie_break()
        m_b, cut_b = to_rows(m), to_rows(cut)
        write_masks(lambda sc, s_pos: (sc > m_b) | ((sc == m_b) & (s_pos <= cut_b)))

    def masks_plain():
        thr_b = to_rows(thr)
        write_masks(lambda sc, s_pos: sc >= thr_b)

    lax.cond(has_tie, masks_with_ties, masks_plain)

    rows_h = N_HEADS * qt
    lane_fold = lambda v, op: functools.reduce(op, [v[:, k * LANES:(k + 1) * LANES] for k in range(K_CHUNK // LANES)])

    def key_rows(j):
        start = j * K_CHUNK
        return pl.ds(start if isinstance(j, int) else pl.multiple_of(start, K_CHUNK), K_CHUNK)

    def over_chunks(chunk_fn):
        chunk_fn(0, True)
        chunk_loop(1, lambda j, carry: (chunk_fn(j, False), carry)[1], 0)

    def logit_chunk(j, first):
        near = jnp.minimum((i * qt) // K_CHUNK - j, n_near - 1)
        s = _dot_nt(qa_scr[...], c_ref[key_rows(j), :]) * scale + nb_scr[near]
        s = s + jnp.concatenate([mb_scr[j]] * N_HEADS, axis=0)
        s_scr[j] = s
        fold = lane_fold(s, jnp.maximum)
        m_scr[...] = fold if first else jnp.maximum(m_scr[...], fold)

    over_chunks(logit_chunk)
    m_b = jnp.broadcast_to(jnp.max(m_scr[...], axis=-1, keepdims=True), (rows_h, LANES))
    m_scr[...] = m_b

    def value_chunk(j, first):
        p = jnp.exp(s_scr[j] - jnp.concatenate([m_scr[...]] * (K_CHUNK // LANES), axis=1))
        pv = _dot(p.astype(BF16), c_ref[key_rows(j), :])
        l_scr[...] = lane_fold(p, jnp.add) if first else l_scr[...] + lane_fold(p, jnp.add)
        acc_scr[...] = pv if first else acc_scr[...] + pv

    over_chunks(value_chunk)
    out = acc_scr[...] * (1.0 / jnp.sum(l_scr[...], axis=-1, keepdims=True))
    for h in range(N_HEADS):
        o_ref[:, h * KV_RANK:(h + 1) * KV_RANK] = out[h * qt:(h + 1) * qt].astype(o_ref.dtype)


def _dsa_attention(qq, wi, kk, c_kv, wuk, rel_bias, top_k, seq, sides):
    B, tp, _ = qq.shape
    n_main = (seq + N_META) // Q_TILE
    assert (seq + N_META) - n_main * Q_TILE <= TAIL_TILE and Q_TILE % K_CHUNK == 0 and K_CHUNK % TAIL_TILE == 0
    n_chunks_max = tp // K_CHUNK
    width = N_HEADS * KV_RANK

    def call(qt, first_tile, n_tiles, n_fill, prev, sides=()):
        rows_h = N_HEADS * qt
        diag = _near_bias_diagonals(rel_bias, qt)
        tile = lambda i: first_tile + jnp.minimum(i, n_tiles - 1)
        in_specs = [
            pl.BlockSpec((None, qt, qq.shape[2]), lambda b, i: (b, tile(i), 0)),
            pl.BlockSpec((None, qt, 128), lambda b, i: (b, tile(i), 0)),
            pl.BlockSpec((None, tp, LANES), lambda b, i: (b, 0, 0)),
            pl.BlockSpec((None, tp, KV_RANK), lambda b, i: (b, 0, 0)),
            pl.BlockSpec(wuk.shape, lambda b, i: (0, 0, 0)),
            pl.BlockSpec(diag.shape, lambda b, i: (0, 0, 0)),
        ]
        args = [qq, wi, kk, c_kv, wuk, diag]
        out_specs = [pl.BlockSpec((None, qt, width), lambda b, i: (b, first_tile + i, 0))]
        out_shape = [jax.ShapeDtypeStruct((B, tp, width), BF16)]
        aliases = {}
        if prev is not None:
            in_specs.append(pl.BlockSpec(memory_space=pl.ANY))
            args.append(prev)
            aliases = {len(args) - 1: 0}
        for side in sides:
            side_block = _slab_block(side, B * n_tiles, lambda b, i: b * n_tiles + jnp.minimum(i, n_tiles - 1))
            in_specs.append(side_block)
            args.append(side)
            out_specs.append(side_block)
            out_shape.append(jax.ShapeDtypeStruct(side.shape, BF16))
        return pl.pallas_call(
            functools.partial(_attn_kernel, n_tiles=n_tiles, aliased=prev is not None, n_sides=len(sides),
                              top_k=top_k, qt=qt, first_tile=first_tile, single_tile=n_tiles == 1),
            grid=(B, n_tiles + n_fill),
            in_specs=in_specs,
            out_specs=out_specs,
            out_shape=out_shape,
            input_output_aliases=aliases,
            scratch_shapes=[pltpu.VMEM((diag.shape[0] + 1, rows_h, K_CHUNK), F32),
                            pltpu.VMEM((n_chunks_max, qt, K_CHUNK), F32),
                            pltpu.VMEM((n_chunks_max, K_CHUNK, qt), F32),
                            pltpu.VMEM((n_chunks_max, K_CHUNK, qt), BF16),
                            pltpu.VMEM((n_chunks_max, qt, K_CHUNK), F32),
                            pltpu.VMEM((LANES, qt), F32),
                            pltpu.VMEM((rows_h, KV_RANK), BF16),
                            pltpu.VMEM((n_chunks_max, rows_h, K_CHUNK), F32),
                            pltpu.VMEM((rows_h, LANES), F32),
                            pltpu.VMEM((rows_h, LANES), F32),
                            pltpu.VMEM((rows_h, KV_RANK), F32)],
            compiler_params=_compiler_params(("arbitrary", "arbitrary"), BIG_VMEM_LIMIT_BYTES),
            name="dsa_attention" if prev is None else "dsa_attention_tail",
        )(*args)

    o_lat, *sides_b = call(Q_TILE, 0, n_main, tp // Q_TILE - n_main, None, sides)
    return (call(TAIL_TILE, n_main * Q_TILE // TAIL_TILE, 1, 0, o_lat)[0], *sides_b)


def _mix_window_start(r, seq):
    return min(max(r * MIX_ROWS - N_META, 0), seq - MIX_ROWS)


def _residual_rows(x_ref, meta_ref, h_scr, seq):
    r = pl.program_id(1)
    n_real = -(-(seq + N_META) // MIX_ROWS)
    for rv in range(n_real):

        @pl.when(r == rv)
        def _(rv=rv):
            skip = rv * MIX_ROWS - N_META - _mix_window_start(rv, seq)
            if rv == 0:
                h_scr[:N_META, :] = meta_ref[...]
                h_scr[N_META:, :] = x_ref[:MIX_ROWS - N_META, :]
            elif skip == 0:
                h_scr[...] = x_ref[...]
            else:
                h_scr[:MIX_ROWS - skip, :] = x_ref[skip:, :]
                h_scr[MIX_ROWS - skip:, :] = jnp.zeros((skip, h_scr.shape[1]), F32)

    @pl.when(r >= n_real)
    def _():
        h_scr[...] = jnp.zeros(h_scr.shape, F32)


def _mix_kernel(ol_ref, pd_ref, x_ref, meta_ref, wuv_ref, wp_ref, ps_ref, wo_ref, g_ref, b_ref, h1_ref, h1b_ref,
                h_scr, *, seq):
    _residual_rows(x_ref, meta_ref, h_scr, seq)
    attn = [_dot(ol_ref[:, h * KV_RANK:(h + 1) * KV_RANK], wuv_ref[h]) for h in range(N_HEADS)]
    pool = [_dot(pd_ref[:, g * POOL_GROUP:(g + 1) * POOL_GROUP], wp_ref[g]) for g in range(len(POOL_WINDOWS))]
    pool = jnp.concatenate(pool, axis=-1) * ps_ref[...]
    cat = jnp.concatenate(attn + [pool], axis=-1).astype(BF16)
    y = ALPHA * h_scr[...] + _dot(cat, wo_ref[...])
    h1 = _layer_norm(y, g_ref[...], b_ref[...])
    h1_ref[...] = h1
    h1b_ref[...] = h1.astype(h1b_ref.dtype)


def _mix_ln1(o_lat, pool_diff, x, meta, wuv, wpool, pool_scale, wo, g, b):
    B, seq, d = x.shape
    n_rows = seq + N_META
    assert n_rows % MIX_ROWS == 0
    row = lambda n: pl.BlockSpec((None, MIX_ROWS, n), lambda bi, r: (bi, r, 0))
    full = lambda a: pl.BlockSpec(a.shape, lambda bi, r: (0,) * a.ndim, pipeline_mode=pl.Buffered(1))
    window = pl.BlockSpec(
        (None, pl.Element(MIX_ROWS), pl.Element(d)),
        lambda bi, r: (bi, pl.multiple_of(jnp.clip(r * MIX_ROWS - N_META, 0, seq - MIX_ROWS), N_META), 0))
    return pl.pallas_call(
        functools.partial(_mix_kernel, seq=seq),
        grid=(B, n_rows // MIX_ROWS),
        in_specs=[row(o_lat.shape[2]), row(pool_diff.shape[2]), window, full(meta),
                  full(wuv), full(wpool), full(pool_scale), full(wo), full(g), full(b)],
        out_specs=[row(d), row(d)],
        out_shape=[jax.ShapeDtypeStruct((B, n_rows, d), F32), jax.ShapeDtypeStruct((B, n_rows, d), BF16)],
        scratch_shapes=[pltpu.VMEM((MIX_ROWS, d), F32)],
        compiler_params=_compiler_params(("parallel", "arbitrary"), BIG_VMEM_LIMIT_BYTES),
        name="mix_ln1",
    )(o_lat, pool_diff, x, meta, wuv, wpool, pool_scale, wo, g, b)


def _gelu_tanh(x):
    return 0.5 * x * (1.0 + jnp.tanh(math.sqrt(2.0 / math.pi) * (x + 0.044715 * (x * x * x))))


def _ffn_kernel(hw_ref, hres_ref, wa_ref, wg_ref, cwa_ref, cwg_ref, cba_ref, cbg_ref, wd_ref, g_ref, b_ref, o_ref,
                za_scr, zg_scr):
    c = pl.program_id(2)
    n_c = pl.num_programs(2) - 1

    def up(slot):
        x = hw_ref[...]
        za_scr[slot] = _dot(x, wa_ref[...])
        zg_scr[slot] = _dot(x, wg_ref[...])

    def conv(z, cw_ref, cb_ref):
        cw = cw_ref[...]
        n = z.shape[0]
        out = z[HALO - 2:n - 2] * cw[0:1] + z[HALO - 1:n - 1] * cw[1:2] + z[HALO:] * cw[2:3]
        return out + cb_ref[...]

    def down(slot):
        a = conv(za_scr[slot], cwa_ref, cba_ref)
        gate = conv(zg_scr[slot], cwg_ref, cbg_ref)
        act = (_gelu_tanh(a) * gate).astype(BF16)
        o_ref[...] += _dot(act, wd_ref[...])

    @pl.when(c == 0)
    def _():
        o_ref[...] = jnp.zeros(o_ref.shape, o_ref.dtype)
        up(0)

    @pl.when((c > 0) & (c < n_c))
    def _():
        up(c % 2)
        down((c - 1) % 2)

    @pl.when(c == n_c)
    def _():
        down((c - 1) % 2)
        o_ref[...] = _layer_norm(ALPHA * hres_ref[...] + o_ref[...], g_ref[...], b_ref[...])


def _ffn_ln2(h1, h1b, w_up, conv_w, conv_b, w_down, g, b, seq):
    B, tp, d = h1.shape
    d_ff = w_down.shape[0]
    n_c = d_ff // FFN_COLS
    up_c = lambda c: jnp.minimum(c, n_c - 1)
    dn_c = lambda c: jnp.maximum(c - 1, 0)
    vec = lambda off: pl.BlockSpec((1, FFN_COLS), lambda bi, r, c: (0, dn_c(c) + off))
    return pl.pallas_call(
        _ffn_kernel,
        grid=(B, seq // FFN_ROWS, n_c + 1),
        in_specs=[
            pl.BlockSpec((None, pl.Element(FFN_ROWS + HALO), pl.Element(d)),
                         lambda bi, r, c: (bi, r * FFN_ROWS + N_META - HALO, 0)),
            pl.BlockSpec((None, pl.Element(FFN_ROWS), pl.Element(d)),
                         lambda bi, r, c: (bi, pl.multiple_of(r * FFN_ROWS + N_META, N_META), 0)),
            pl.BlockSpec((d, FFN_COLS), lambda bi, r, c: (0, up_c(c))),
            pl.BlockSpec((d, FFN_COLS), lambda bi, r, c: (0, up_c(c) + n_c)),
            pl.BlockSpec((CONV_WIDTH, FFN_COLS), lambda bi, r, c: (0, dn_c(c))),
            pl.BlockSpec((CONV_WIDTH, FFN_COLS), lambda bi, r, c: (0, dn_c(c) + n_c)),
            vec(0), vec(n_c),
            pl.BlockSpec((FFN_COLS, d), lambda bi, r, c: (dn_c(c), 0)),
            pl.BlockSpec((1, d), lambda bi, r, c: (0, 0)),
            pl.BlockSpec((1, d), lambda bi, r, c: (0, 0)),
        ],
        out_specs=pl.BlockSpec((None, FFN_ROWS, d), lambda bi, r, c: (bi, r, 0), pipeline_mode=pl.Buffered(1)),
        out_shape=jax.ShapeDtypeStruct((B, seq, d), F32),
        scratch_shapes=[pltpu.VMEM((2, FFN_ROWS + HALO, FFN_COLS), F32),
                        pltpu.VMEM((2, FFN_ROWS + HALO, FFN_COLS), F32)],
        compiler_params=_compiler_params(("parallel", "parallel", "arbitrary"), BIG_VMEM_LIMIT_BYTES),
        name="ffn_ln2",
    )(h1b, h1, w_up, w_up, conv_w, conv_w, conv_b, conv_b, w_down, g, b)


def _t5_bucket_table(n):
    dist = np.arange(n, dtype=np.int32)
    max_exact = REL_BUCKETS // 2
    d_f = np.maximum(dist, 1).astype(np.float32)
    large = max_exact + (np.log(d_f / np.float32(max_exact)) / np.float32(math.log(REL_MAX_DIST / max_exact))
                         * np.float32(REL_BUCKETS - max_exact)).astype(np.int32)
    return np.where(dist < max_exact, dist, np.minimum(large, REL_BUCKETS - 1))


def _near_bias_diagonals(rel_bias, qt):
    assert qt <= K_CHUNK
    probe = _t5_bucket_table(4 * REL_MAX_DIST)
    first_far = int(np.argmax(probe == REL_BUCKETS - 1))
    assert np.all(probe[first_far:] == REL_BUCKETS - 1)
    n_real = -(-(first_far + K_CHUNK - 1) // K_CHUNK)
    buckets = _t5_bucket_table((n_real + 1) * K_CHUNK)
    period = 2 * K_CHUNK
    u = np.arange(period)
    k = np.arange(n_real)[:, None]
    dist = np.where(u < K_CHUNK, k * K_CHUNK - u, k * K_CHUNK + period - u)
    idx = buckets[np.clip(dist, 0, len(buckets) - 1)]
    rel = rel_bias.astype(F32) - rel_bias[REL_BUCKETS - 1:].astype(F32)
    return jnp.transpose(rel[idx], (0, 2, 1))


def kernel(x, meta, rel_bias, w_in, kv_norm_g, w_uk, w_uv, w_pool, pool_scale, w_o, ln1_g, ln1_b, w_up, conv_w,
           conv_b, w_down, ln2_g, ln2_b):
    B, S, D = x.shape
    assert w_in.shape[0] == DEPTH and S % FFN_ROWS == 0
    T = S + N_META
    tp = -(-T // ROW_ALIGN) * ROW_ALIGN
    assert tp - T >= max(POOL_WINDOWS)
    top_k = min(TOPK_MAX, S // 4)
    assert top_k <= K_CHUNK

    wt = jnp.transpose(w_in[0])
    sizes = (("q", N_HEADS * HEAD_DIM), ("c_kv", KV_RANK), ("q_idx", IDX_HEADS * IDX_DIM), ("k_idx", IDX_DIM),
             ("w_idx", IDX_HEADS), ("u", len(POOL_WINDOWS) * POOL_GROUP))
    cols, start = {}, 0
    for name, size in sizes:
        cols[name] = (start, size)
        start += size
    assert start == wt.shape[0]

    qq, = _proj_qq(x, meta, wt, cols, tp, None)
    pool_diff, w_o_b = _proj_pool(x, meta, wt, cols, tp, w_o[0])
    c_kv, kk, wi = _proj_small(x, meta, wt, cols, kv_norm_g[0].reshape(1, KV_RANK), tp, None)

    wuk = jnp.transpose(w_uk[0], (1, 2, 0)).astype(BF16)
    o_lat, w_up_b, w_down_b = _dsa_attention(qq, wi, kk, c_kv, wuk, rel_bias, top_k, S, (w_up[0], w_down[0]))

    wuv = jnp.transpose(w_uv[0], (1, 0, 2)).astype(BF16)
    h1, h1b = _mix_ln1(o_lat, pool_diff, x, meta, wuv, w_pool[0].astype(BF16), pool_scale[0].reshape(1, -1),
                       w_o_b, ln1_g[0].reshape(1, D), ln1_b[0].reshape(1, D))

    return _ffn_ln2(h1, h1b, w_up_b, conv_w[0], conv_b[0].reshape(1, -1), w_down_b,
                    ln2_g[0].reshape(1, D), ln2_b[0].reshape(1, D), S)
```

```python
import functools
import math

import numpy as np
import jax
import jax.numpy as jnp
from jax import lax
from jax.experimental import pallas as pl
from jax.experimental.pallas import tpu as pltpu

F32 = jnp.float32
BF16 = jnp.bfloat16

N_META = 16
N_HEADS = 8
HEAD_DIM = 128
KV_RANK = 256
IDX_HEADS = 16
IDX_DIM = 64
TOPK_MAX = 256
POOL_WINDOWS = (2, 4, 8, 16)
POOL_GROUP = 256
CONV_WIDTH = 3
REL_BUCKETS = 32
REL_MAX_DIST = 128
DEPTH = 1
ALPHA = (2.0 * DEPTH) ** 0.25
LN_EPS = 1e-5
NEG_INF = -1e30

VMEM_LIMIT_BYTES = 60 * 1024 * 1024
SUBLANES = 8
LANES = 128
ROW_ALIGN = 256
PROJ_ROWS = 1024
Q_TILE = 256
TAIL_TILE = 16
K_CHUNK = 256
MIX_ROWS = 688
FFN_ROWS = 1024
FFN_COLS = 512
HALO = 16
INT_MIN = -(2 ** 31)
PACKED_SUBLANES = 16
TIE_ROUNDS = 8


def _dot(a, b):
    return jnp.dot(a, b, preferred_element_type=F32)


def _dot_nt(a, b):
    return lax.dot_general(a, b, (((1,), (1,)), ((), ())), preferred_element_type=F32)


def _layer_norm(y, g, b):
    mu = jnp.mean(y, axis=-1, keepdims=True)
    yc = y - mu
    var = jnp.mean(yc * yc, axis=-1, keepdims=True)
    return yc * lax.rsqrt(var + LN_EPS) * g + b


def _proj_rows(x_ref, meta_ref, w, tp, store):
    seq = x_ref.shape[0]
    store(pl.ds(0, N_META), _dot_nt(meta_ref[...].astype(BF16), w))
    for r in range(seq // PROJ_ROWS):
        acc = _dot_nt(x_ref[r * PROJ_ROWS:(r + 1) * PROJ_ROWS, :].astype(BF16), w)
        store(pl.ds(N_META + r * PROJ_ROWS, PROJ_ROWS), acc)
    n_pad = tp - seq - N_META
    store(pl.ds(seq + N_META, n_pad), jnp.zeros((n_pad, w.shape[0]), F32))


def _proj_cast_kernel(x_ref, meta_ref, w_ref, o_ref, *, scale_from):
    def store(rows, acc):
        o_ref[rows, :] = acc.astype(o_ref.dtype)

    scale = jnp.where(pl.program_id(1) >= scale_from, IDX_DIM ** -0.5, 1.0)
    _proj_rows(x_ref, meta_ref, (w_ref[...] * scale).astype(BF16), o_ref.shape[0], store)


def _proj_pool_kernel(x_ref, meta_ref, w_ref, o_ref, u_scr):
    def store(rows, acc):
        u_scr[rows, :] = acc

    _proj_rows(x_ref, meta_ref, w_ref[...].astype(BF16), u_scr.shape[0], store)
    group = pl.program_id(1)
    tp = u_scr.shape[0]
    pos = lax.broadcasted_iota(jnp.int32, (tp, 1), 0)
    for g, window in enumerate(POOL_WINDOWS):

        @pl.when(group == g)
        def _():
            u = u_scr[...]
            s = u
            shift = 1
            while shift < window:
                s = s + pltpu.roll(s, shift, axis=0)
                shift *= 2
            count = jnp.minimum(pos + 1, window).astype(F32)
            o_ref[...] = (s / count - u).astype(o_ref.dtype)


def _proj_small_kernel(x_ref, meta_ref, wc_ref, wk_ref, ww_ref, g_ref, c_ref, kk_ref, wi_ref):
    w = jnp.concatenate([wc_ref[...], wk_ref[...], wk_ref[...], ww_ref[...] * (IDX_HEADS ** -0.5),
                         jnp.zeros((LANES - IDX_HEADS, wc_ref.shape[1]), F32)], axis=0).astype(BF16)

    def store(rows, acc):
        c = acc[:, :KV_RANK]
        ms = jnp.mean(c * c, axis=-1, keepdims=True)
        c_ref[rows, :] = (c * lax.rsqrt(ms + LN_EPS) * g_ref[...]).astype(c_ref.dtype)
        kk_ref[rows, :] = acc[:, KV_RANK:KV_RANK + LANES].astype(kk_ref.dtype)
        wi_ref[rows, :] = acc[:, KV_RANK + LANES:]

    _proj_rows(x_ref, meta_ref, w, c_ref.shape[0], store)


def _batch_block(tp, n):
    return pl.BlockSpec((None, tp, n), lambda b, j: (b, 0, 0))


def _compiler_params(semantics):
    return pltpu.CompilerParams(dimension_semantics=semantics, vmem_limit_bytes=VMEM_LIMIT_BYTES)


def _with_side_cast(body, n_in, n_out):
    def kernel(*refs):
        side_in, side_out = refs[n_in], refs[n_in + 1 + n_out]
        side_out[...] = side_in[...].astype(side_out.dtype)
        body(*refs[:n_in], *refs[n_in + 1:n_in + 1 + n_out], *refs[n_in + 2 + n_out:])

    return kernel


def _slab_block(side, n_slabs, slab_of_step):
    slab = side.shape[0] // n_slabs
    assert slab * n_slabs == side.shape[0] and slab % PACKED_SUBLANES == 0
    return pl.BlockSpec((slab, side.shape[1]), lambda *step: (slab_of_step(*step), 0))


def _w_rows(d, n_rows, first_row):
    return pl.BlockSpec((pl.Element(n_rows), pl.Element(d)), lambda b, j: (first_row(j), 0))


def _proj_call(body, x, meta, wt, w_specs, n_j, extra_in, extra_specs, out_blocks, out_shapes, scratch, name, side):
    B, seq, d = x.shape
    out_blocks = list(out_blocks) if isinstance(out_blocks, (list, tuple)) else [out_blocks]
    out_shapes = list(out_shapes) if isinstance(out_shapes, (list, tuple)) else [out_shapes]
    in_specs = [_batch_block(seq, d), pl.BlockSpec(meta.shape, lambda b, j: (0, 0))] + w_specs + extra_specs
    args = [x, meta] + [wt] * len(w_specs) + list(extra_in)
    if side is not None:
        side_block = _slab_block(side, B * n_j, lambda b, j: b * n_j + j)
        body = _with_side_cast(body, len(args), len(out_blocks))
        in_specs, args = in_specs + [side_block], args + [side]
        out_blocks, out_shapes = out_blocks + [side_block], out_shapes + [jax.ShapeDtypeStruct(side.shape, BF16)]
    return pl.pallas_call(
        body,
        grid=(B, n_j),
        in_specs=in_specs,
        out_specs=out_blocks,
        out_shape=out_shapes,
        scratch_shapes=scratch,
        compiler_params=_compiler_params(("parallel", "arbitrary")),
        name=name,
    )(*args)


def _proj_qq(x, meta, wt, cols, tp, side):
    B, d, tn = x.shape[0], x.shape[2], 512
    n_q, n = cols["q"][1] // tn, cols["q"][1] + cols["q_idx"][1]
    first_row = lambda j: pl.multiple_of(
        jnp.where(j < n_q, cols["q"][0] + j * tn, cols["q_idx"][0] + (j - n_q) * tn), PACKED_SUBLANES)
    return _proj_call(functools.partial(_proj_cast_kernel, scale_from=n_q), x, meta, wt,
                      [_w_rows(d, tn, first_row)], n // tn, [], [],
                      pl.BlockSpec((None, tp, tn), lambda b, j: (b, 0, j)),
                      jax.ShapeDtypeStruct((B, tp, n), BF16), [], "proj_qq", side)


def _proj_pool(x, meta, wt, cols, tp, side):
    B, d, n = x.shape[0], x.shape[2], cols["u"][1]
    first_row = lambda j: pl.multiple_of(cols["u"][0] + j * POOL_GROUP, PACKED_SUBLANES)
    return _proj_call(_proj_pool_kernel, x, meta, wt, [_w_rows(d, POOL_GROUP, first_row)], n // POOL_GROUP, [], [],
                      pl.BlockSpec((None, tp, POOL_GROUP), lambda b, j: (b, 0, j)),
                      jax.ShapeDtypeStruct((B, tp, n), BF16), [pltpu.VMEM((tp, POOL_GROUP), F32)], "proj_pool", side)


def _proj_small(x, meta, wt, cols, kv_g, tp, side):
    B, d = x.shape[0], x.shape[2]
    w_specs = [_w_rows(d, cols[name][1], lambda j, name=name: cols[name][0]) for name in ("c_kv", "k_idx", "w_idx")]
    return _proj_call(_proj_small_kernel, x, meta, wt, w_specs, 1, [kv_g],
                      [pl.BlockSpec((1, KV_RANK), lambda b, j: (0, 0))],
                      [_batch_block(tp, KV_RANK), _batch_block(tp, LANES), _batch_block(tp, 128)],
                      [jax.ShapeDtypeStruct((B, tp, KV_RANK), BF16), jax.ShapeDtypeStruct((B, tp, LANES), BF16),
                       jax.ShapeDtypeStruct((B, tp, 128), F32)], [], "proj_small", side)


def _attn_kernel(*refs, n_tiles, aliased, n_sides, **static):
    n_in = 6
    sides_in = refs[n_in + int(aliased):n_in + int(aliased) + n_sides]
    refs = refs[:n_in] + refs[n_in + int(aliased) + n_sides:]
    o_ref = refs[n_in]
    sides_out = refs[n_in + 1:n_in + 1 + n_sides]
    refs = refs[:n_in + 1] + refs[n_in + 1 + n_sides:]

    @pl.when(pl.program_id(1) < n_tiles)
    def _():
        for side_in, side_out in zip(sides_in, sides_out):
            side_out[...] = side_in[...].astype(side_out.dtype)
        _attn_tile(*refs, **static)

    @pl.when(pl.program_id(1) >= n_tiles)
    def _():
        o_ref[...] = jnp.zeros(o_ref.shape, o_ref.dtype)


def _attn_tile(qq_ref, wi_ref, kk_ref, c_ref, wuk_ref, diag_ref, o_ref, nb_scr, sc_scr, sct_scr, scf_scr, mb_scr,
               wt_scr, qa_scr, s_scr, m_scr, l_scr, acc_scr, *, top_k, qt, first_tile, single_tile):
    i = first_tile if single_tile else first_tile + pl.program_id(1)
    n_chunks = ((i + 1) * qt - 1) // K_CHUNK + 1

    def chunk_loop(lo, body, init):
        return lax.fori_loop(lo, n_chunks, body, init, unroll=True if single_tile else 1)
    attn_w = N_HEADS * HEAD_DIM
    scale = HEAD_DIM ** -0.5
    n_pairs = IDX_HEADS // 2
    lanes_are_queries = qt % LANES == 0

    t_col = i * qt + lax.broadcasted_iota(jnp.int32, (qt, 1), 0)
    t_row = i * qt + lax.broadcasted_iota(jnp.int32, (1, qt), 1)
    s_row = lax.broadcasted_iota(jnp.int32, (1, K_CHUNK), 1)
    s_col = lax.broadcasted_iota(jnp.int32, (K_CHUNK, 1), 0)
    lane_half = lax.broadcasted_iota(jnp.int32, (K_CHUNK, LANES), 1) // IDX_DIM

    n_near = nb_scr.shape[0]

    @pl.when((pl.program_id(0) == 0) & (pl.program_id(1) == 0))
    def _():
        for k in range(n_near - 1):
            for h in range(N_HEADS):
                v = jnp.broadcast_to(diag_ref[k, h:h + 1, :], (qt, 2 * K_CHUNK))
                t = pltpu.roll(v, 0, 1, stride=1, stride_axis=0)
                nb_scr[k, h * qt:(h + 1) * qt, :] = t[:, :K_CHUNK]
        nb_scr[n_near - 1] = jnp.zeros(nb_scr.shape[1:], F32)

    for h in range(N_HEADS):
        qa_scr[h * qt:(h + 1) * qt, :] = _dot(
            qq_ref[:, h * HEAD_DIM:(h + 1) * HEAD_DIM], wuk_ref[h]).astype(BF16)
    if lanes_are_queries:
        wt_scr[...] = wi_ref[...].T

    def bf16_floor(v):
        near = v.astype(BF16)
        bits = lax.bitcast_convert_type(near, jnp.int16)
        below = lax.bitcast_convert_type(bits + jnp.where(bits < 0, jnp.int16(1), jnp.int16(-1)), BF16)
        return jnp.where(near.astype(F32) > v, below, near)

    def idx_chunk(j, carry):
        ks = kk_ref[pl.ds(pl.multiple_of(j * K_CHUNK, K_CHUNK), K_CHUNK), :]
        zero = jnp.zeros(ks.shape, ks.dtype)
        k_even = jnp.where(lane_half == 0, ks, zero)
        k_odd = jnp.where(lane_half == 1, ks, zero)
        score = jnp.zeros((K_CHUNK, qt) if lanes_are_queries else (qt, K_CHUNK), F32)
        if lanes_are_queries:
            for p in range(n_pairs):
                q_pair = qq_ref[:, attn_w + p * LANES:attn_w + (p + 1) * LANES]
                for hh, k_half in ((2 * p, k_even), (2 * p + 1, k_odd)):
                    score = score + jnp.maximum(_dot_nt(k_half, q_pair), 0.0) * wt_scr[hh:hh + 1, :]
        else:
            q_pairs = jnp.concatenate(
                [qq_ref[:, attn_w + p * LANES:attn_w + (p + 1) * LANES] for p in range(n_pairs)], axis=0)
            for half, k_half in enumerate((k_even, k_odd)):
                dots = jnp.maximum(_dot_nt(q_pairs, k_half), 0.0)
                for p in range(n_pairs):
                    hh = 2 * p + half
                    score = score + dots[p * qt:(p + 1) * qt] * wi_ref[:, hh:hh + 1]
        s_pos = j * K_CHUNK + (s_col if lanes_are_queries else s_row)
        score = jnp.where(s_pos <= (t_row if lanes_are_queries else t_col), score, NEG_INF)
        if lanes_are_queries:
            sct_scr[j] = score
            sc_scr[j] = score.T
            scf_scr[j] = bf16_floor(score)
        else:
            sc_scr[j] = score
        return carry

    chunk_loop(0, idx_chunk, 0)

    k_f = float(top_k)
    idx_bits = int(math.ceil(math.log2(sc_scr.shape[0] * K_CHUNK)))
    per_query = (1, qt) if lanes_are_queries else (qt, 1)
    key_axis = 0 if lanes_are_queries else 1
    s_idx = s_col if lanes_are_queries else s_row
    search_scr = sct_scr if lanes_are_queries else sc_scr

    def fold_chunks(chunk_fn, combine, init):
        if lanes_are_queries:
            acc_rows = 4 * SUBLANES

            def body(j, acc):
                v = chunk_fn(j, search_scr[j]).reshape(K_CHUNK // acc_rows, acc_rows, qt)
                return combine(acc, functools.reduce(combine, [v[t] for t in range(K_CHUNK // acc_rows)]))

            acc = chunk_loop(0, body, jnp.full((acc_rows, qt), init, F32))
        else:
            acc = chunk_loop(0, lambda j, acc: combine(acc, chunk_fn(j, search_scr[j])),
                                jnp.full((qt, K_CHUNK), init, F32))
        reduce = jnp.sum if combine is jnp.add else jnp.min
        return reduce(acc, axis=key_axis, keepdims=True)

    def count(pred):
        return fold_chunks(lambda j, sc: jnp.where(pred(j, sc), 1.0, 0.0), jnp.add, 0.0)

    def key_to_float(key):
        return lax.bitcast_convert_type(jnp.where(key < 0, key ^ jnp.int32(0x7FFFFFFF), key), F32)

    def count_coarse(cand_f):
        acc_rows = 4 * PACKED_SUBLANES
        cand_b = jnp.broadcast_to(cand_f, (acc_rows, qt)).astype(BF16)

        def body(j, acc):
            hit = jnp.where(scf_scr[j].reshape(K_CHUNK // acc_rows, acc_rows, qt) >= cand_b[None],
                            jnp.ones((), BF16), jnp.zeros((), BF16))
            return acc + functools.reduce(jnp.add, [hit[t] for t in range(K_CHUNK // acc_rows)])

        acc = chunk_loop(0, body, jnp.zeros((acc_rows, qt), BF16))
        return jnp.sum(acc.astype(F32), axis=0, keepdims=True)

    def bit_step(base, carry):
        key, bit = carry
        cand = key + bit
        cand_f = key_to_float(cand)
        n = count(lambda j, sc: sc >= cand_f) if base is None else count_coarse(cand_f - base)
        return jnp.where(n >= k_f, cand, key), lax.shift_right_logical(bit, jnp.int32(1))

    state = (jnp.full(per_query, INT_MIN, jnp.int32), jnp.int32(INT_MIN))
    if lanes_are_queries:
        state = lax.fori_loop(0, 16, lambda _, c: bit_step(0.0, c), state)
        for n_bits in (8, 8):
            base = key_to_float(state[0])

            def rebase(j, carry, base=base):
                scf_scr[j] = bf16_floor(sct_scr[j] - base)
                return carry

            chunk_loop(0, rebase, 0)
            state = lax.fori_loop(0, n_bits, lambda _, c, base=base: bit_step(base, c), state)
    else:
        state = lax.fori_loop(0, 32, lambda _, c: bit_step(None, c), state)
    thr = key_to_float(state[0])

    n_ge = count(lambda j, sc: sc >= thr)
    has_tie = jnp.max(jnp.where((n_ge > k_f) & (thr > NEG_INF), 1.0, 0.0)) > 0.0

    def tie_break():
        def next_value(_, m):
            n_gt = count(lambda j, sc: sc > m)
            above = fold_chunks(lambda j, sc: jnp.where(sc > m, sc, jnp.inf), jnp.minimum, jnp.inf)
            return jnp.where(n_gt >= k_f, above, m)

        m = lax.fori_loop(0, TIE_ROUNDS, next_value, thr)
        need = k_f - count(lambda j, sc: sc > m)

        def step(_, carry):
            cut, bit = carry
            cand = cut + bit
            n_before = count(lambda j, sc: (sc == m) & ((j * K_CHUNK + s_idx) < cand))
            return jnp.where(n_before < need, cand, cut), lax.shift_right_logical(bit, jnp.int32(1))

        cut, _ = lax.fori_loop(0, idx_bits, step,
                               (jnp.zeros(per_query, jnp.int32), jnp.int32(2 ** (idx_bits - 1))))
        return m, cut

    def to_rows(v):
        if not lanes_are_queries:
            return jnp.broadcast_to(v, (qt, K_CHUNK))
        t = jnp.broadcast_to(v, (qt, qt)).T
        return jnp.concatenate([t] * (K_CHUNK // qt), axis=1)

    def write_masks(keep_fn):
        def mask_chunk(j, carry):
            s_pos = j * K_CHUNK + s_row
            mb_scr[j] = jnp.where(keep_fn(sc_scr[j], s_pos) & (s_pos <= t_col), 0.0, NEG_INF)
            return carry

        chunk_loop(0, mask_chunk, 0)

    def masks_with_ties():
        m, cut = tie_break()
        m_b, cut_b = to_rows(m), to_rows(cut)
        write_masks(lambda sc, s_pos: (sc > m_b) | ((sc == m_b) & (s_pos <= cut_b)))

    def masks_plain():
        thr_b = to_rows(thr)
        write_masks(lambda sc, s_pos: sc >= thr_b)

    lax.cond(has_tie, masks_with_ties, masks_plain)

    rows_h = N_HEADS * qt
    lane_fold = lambda v, op: functools.reduce(op, [v[:, k * LANES:(k + 1) * LANES] for k in range(K_CHUNK // LANES)])

    def key_rows(j):
        start = j * K_CHUNK
        return pl.ds(start if isinstance(j, int) else pl.multiple_of(start, K_CHUNK), K_CHUNK)

    def over_chunks(chunk_fn):
        chunk_fn(0, True)
        chunk_loop(1, lambda j, carry: (chunk_fn(j, False), carry)[1], 0)

    def logit_chunk(j, first):
        near = jnp.minimum((i * qt) // K_CHUNK - j, n_near - 1)
        s = _dot_nt(qa_scr[...], c_ref[key_rows(j), :]) * scale + nb_scr[near]
        s = s + jnp.concatenate([mb_scr[j]] * N_HEADS, axis=0)
        s_scr[j] = s
        fold = lane_fold(s, jnp.maximum)
        m_scr[...] = fold if first else jnp.maximum(m_scr[...], fold)

    over_chunks(logit_chunk)
    m_b = jnp.broadcast_to(jnp.max(m_scr[...], axis=-1, keepdims=True), (rows_h, LANES))
    m_scr[...] = m_b

    def value_chunk(j, first):
        p = jnp.exp(s_scr[j] - jnp.concatenate([m_scr[...]] * (K_CHUNK // LANES), axis=1))
        pv = _dot(p.astype(BF16), c_ref[key_rows(j), :])
        l_scr[...] = lane_fold(p, jnp.add) if first else l_scr[...] + lane_fold(p, jnp.add)
        acc_scr[...] = pv if first else acc_scr[...] + pv

    over_chunks(value_chunk)
    out = acc_scr[...] * (1.0 / jnp.sum(l_scr[...], axis=-1, keepdims=True))
    for h in range(N_HEADS):
        o_ref[:, h * KV_RANK:(h + 1) * KV_RANK] = out[h * qt:(h + 1) * qt].astype(o_ref.dtype)


def _dsa_attention(qq, wi, kk, c_kv, wuk, rel_bias, top_k, seq, sides):
    B, tp, _ = qq.shape
    n_main = (seq + N_META) // Q_TILE
    assert (seq + N_META) - n_main * Q_TILE <= TAIL_TILE and Q_TILE % K_CHUNK == 0 and K_CHUNK % TAIL_TILE == 0
    n_chunks_max = tp // K_CHUNK
    width = N_HEADS * KV_RANK

    def call(qt, first_tile, n_tiles, n_fill, prev, sides=()):
        rows_h = N_HEADS * qt
        diag = _near_bias_diagonals(rel_bias, qt)
        tile = lambda i: first_tile + jnp.minimum(i, n_tiles - 1)
        in_specs = [
            pl.BlockSpec((None, qt, qq.shape[2]), lambda b, i: (b, tile(i), 0)),
            pl.BlockSpec((None, qt, 128), lambda b, i: (b, tile(i), 0)),
            pl.BlockSpec((None, tp, LANES), lambda b, i: (b, 0, 0)),
            pl.BlockSpec((None, tp, KV_RANK), lambda b, i: (b, 0, 0)),
            pl.BlockSpec(wuk.shape, lambda b, i: (0, 0, 0)),
            pl.BlockSpec(diag.shape, lambda b, i: (0, 0, 0)),
        ]
        args = [qq, wi, kk, c_kv, wuk, diag]
        out_specs = [pl.BlockSpec((None, qt, width), lambda b, i: (b, first_tile + i, 0))]
        out_shape = [jax.ShapeDtypeStruct((B, tp, width), BF16)]
        aliases = {}
        if prev is not None:
            in_specs.append(pl.BlockSpec(memory_space=pl.ANY))
            args.append(prev)
            aliases = {len(args) - 1: 0}
        for side in sides:
            side_block = _slab_block(side, B * n_tiles, lambda b, i: b * n_tiles + jnp.minimum(i, n_tiles - 1))
            in_specs.append(side_block)
            args.append(side)
            out_specs.append(side_block)
            out_shape.append(jax.ShapeDtypeStruct(side.shape, BF16))
        return pl.pallas_call(
            functools.partial(_attn_kernel, n_tiles=n_tiles, aliased=prev is not None, n_sides=len(sides),
                              top_k=top_k, qt=qt, first_tile=first_tile, single_tile=n_tiles == 1),
            grid=(B, n_tiles + n_fill),
            in_specs=in_specs,
            out_specs=out_specs,
            out_shape=out_shape,
            input_output_aliases=aliases,
            scratch_shapes=[pltpu.VMEM((diag.shape[0] + 1, rows_h, K_CHUNK), F32),
                            pltpu.VMEM((n_chunks_max, qt, K_CHUNK), F32),
                            pltpu.VMEM((n_chunks_max, K_CHUNK, qt), F32),
                            pltpu.VMEM((n_chunks_max, K_CHUNK, qt), BF16),
                            pltpu.VMEM((n_chunks_max, qt, K_CHUNK), F32),
                            pltpu.VMEM((LANES, qt), F32),
                            pltpu.VMEM((rows_h, KV_RANK), BF16),
                            pltpu.VMEM((n_chunks_max, rows_h, K_CHUNK), F32),
                            pltpu.VMEM((rows_h, LANES), F32),
                            pltpu.VMEM((rows_h, LANES), F32),
                            pltpu.VMEM((rows_h, KV_RANK), F32)],
            compiler_params=_compiler_params(("arbitrary", "arbitrary")),
            name="dsa_attention" if prev is None else "dsa_attention_tail",
        )(*args)

    o_lat, *sides_b = call(Q_TILE, 0, n_main, tp // Q_TILE - n_main, None, sides)
    return (call(TAIL_TILE, n_main * Q_TILE // TAIL_TILE, 1, 0, o_lat)[0], *sides_b)


def _mix_window_start(r, seq):
    return min(max(r * MIX_ROWS - N_META, 0), seq - MIX_ROWS)


def _residual_rows(x_ref, meta_ref, h_scr, seq):
    r = pl.program_id(1)
    n_real = -(-(seq + N_META) // MIX_ROWS)
    for rv in range(n_real):

        @pl.when(r == rv)
        def _(rv=rv):
            skip = rv * MIX_ROWS - N_META - _mix_window_start(rv, seq)
            if rv == 0:
                h_scr[:N_META, :] = meta_ref[...]
                h_scr[N_META:, :] = x_ref[:MIX_ROWS - N_META, :]
            elif skip == 0:
                h_scr[...] = x_ref[...]
            else:
                h_scr[:MIX_ROWS - skip, :] = x_ref[skip:, :]
                h_scr[MIX_ROWS - skip:, :] = jnp.zeros((skip, h_scr.shape[1]), F32)

    @pl.when(r >= n_real)
    def _():
        h_scr[...] = jnp.zeros(h_scr.shape, F32)


def _mix_kernel(ol_ref, pd_ref, x_ref, meta_ref, wuv_ref, wp_ref, ps_ref, wo_ref, g_ref, b_ref, h1_ref, h1b_ref,
                h_scr, *, seq):
    _residual_rows(x_ref, meta_ref, h_scr, seq)
    attn = [_dot(ol_ref[:, h * KV_RANK:(h + 1) * KV_RANK], wuv_ref[h]) for h in range(N_HEADS)]
    pool = [_dot(pd_ref[:, g * POOL_GROUP:(g + 1) * POOL_GROUP], wp_ref[g]) for g in range(len(POOL_WINDOWS))]
    pool = jnp.concatenate(pool, axis=-1) * ps_ref[...]
    cat = jnp.concatenate(attn + [pool], axis=-1).astype(BF16)
    y = ALPHA * h_scr[...] + _dot(cat, wo_ref[...])
    h1 = _layer_norm(y, g_ref[...], b_ref[...])
    h1_ref[...] = h1
    h1b_ref[...] = h1.astype(h1b_ref.dtype)


def _mix_ln1(o_lat, pool_diff, x, meta, wuv, wpool, pool_scale, wo, g, b):
    B, seq, d = x.shape
    n_rows = seq + N_META
    assert n_rows % MIX_ROWS == 0
    row = lambda n: pl.BlockSpec((None, MIX_ROWS, n), lambda bi, r: (bi, r, 0))
    full = lambda a: pl.BlockSpec(a.shape, lambda bi, r: (0,) * a.ndim, pipeline_mode=pl.Buffered(1))
    window = pl.BlockSpec(
        (None, pl.Element(MIX_ROWS), pl.Element(d)),
        lambda bi, r: (bi, pl.multiple_of(jnp.clip(r * MIX_ROWS - N_META, 0, seq - MIX_ROWS), N_META), 0))
    return pl.pallas_call(
        functools.partial(_mix_kernel, seq=seq),
        grid=(B, n_rows // MIX_ROWS),
        in_specs=[row(o_lat.shape[2]), row(pool_diff.shape[2]), window, full(meta),
                  full(wuv), full(wpool), full(pool_scale), full(wo), full(g), full(b)],
        out_specs=[row(d), row(d)],
        out_shape=[jax.ShapeDtypeStruct((B, n_rows, d), F32), jax.ShapeDtypeStruct((B, n_rows, d), BF16)],
        scratch_shapes=[pltpu.VMEM((MIX_ROWS, d), F32)],
        compiler_params=_compiler_params(("parallel", "arbitrary")),
        name="mix_ln1",
    )(o_lat, pool_diff, x, meta, wuv, wpool, pool_scale, wo, g, b)


def _gelu_tanh(x):
    return 0.5 * x * (1.0 + jnp.tanh(math.sqrt(2.0 / math.pi) * (x + 0.044715 * (x * x * x))))


def _ffn_kernel(hw_ref, hres_ref, wa_ref, wg_ref, cwa_ref, cwg_ref, cba_ref, cbg_ref, wd_ref, g_ref, b_ref, o_ref,
                za_scr, zg_scr):
    c = pl.program_id(2)
    n_c = pl.num_programs(2) - 1

    def up(slot):
        x = hw_ref[...]
        za_scr[slot] = _dot(x, wa_ref[...])
        zg_scr[slot] = _dot(x, wg_ref[...])

    def conv(z, cw_ref, cb_ref):
        cw = cw_ref[...]
        n = z.shape[0]
        out = z[HALO - 2:n - 2] * cw[0:1] + z[HALO - 1:n - 1] * cw[1:2] + z[HALO:] * cw[2:3]
        return out + cb_ref[...]

    def down(slot):
        a = conv(za_scr[slot], cwa_ref, cba_ref)
        gate = conv(zg_scr[slot], cwg_ref, cbg_ref)
        act = (_gelu_tanh(a) * gate).astype(BF16)
        o_ref[...] += _dot(act, wd_ref[...])

    @pl.when(c == 0)
    def _():
        o_ref[...] = jnp.zeros(o_ref.shape, o_ref.dtype)
        up(0)

    @pl.when((c > 0) & (c < n_c))
    def _():
        up(c % 2)
        down((c - 1) % 2)

    @pl.when(c == n_c)
    def _():
        down((c - 1) % 2)
        o_ref[...] = _layer_norm(ALPHA * hres_ref[...] + o_ref[...], g_ref[...], b_ref[...])


def _ffn_ln2(h1, h1b, w_up, conv_w, conv_b, w_down, g, b, seq):
    B, tp, d = h1.shape
    d_ff = w_down.shape[0]
    n_c = d_ff // FFN_COLS
    up_c = lambda c: jnp.minimum(c, n_c - 1)
    dn_c = lambda c: jnp.maximum(c - 1, 0)
    vec = lambda off: pl.BlockSpec((1, FFN_COLS), lambda bi, r, c: (0, dn_c(c) + off))
    return pl.pallas_call(
        _ffn_kernel,
        grid=(B, seq // FFN_ROWS, n_c + 1),
        in_specs=[
            pl.BlockSpec((None, pl.Element(FFN_ROWS + HALO), pl.Element(d)),
                         lambda bi, r, c: (bi, r * FFN_ROWS + N_META - HALO, 0)),
            pl.BlockSpec((None, pl.Element(FFN_ROWS), pl.Element(d)),
                         lambda bi, r, c: (bi, pl.multiple_of(r * FFN_ROWS + N_META, N_META), 0)),
            pl.BlockSpec((d, FFN_COLS), lambda bi, r, c: (0, up_c(c))),
            pl.BlockSpec((d, FFN_COLS), lambda bi, r, c: (0, up_c(c) + n_c)),
            pl.BlockSpec((CONV_WIDTH, FFN_COLS), lambda bi, r, c: (0, dn_c(c))),
            pl.BlockSpec((CONV_WIDTH, FFN_COLS), lambda bi, r, c: (0, dn_c(c) + n_c)),
            vec(0), vec(n_c),
            pl.BlockSpec((FFN_COLS, d), lambda bi, r, c: (dn_c(c), 0)),
            pl.BlockSpec((1, d), lambda bi, r, c: (0, 0)),
            pl.BlockSpec((1, d), lambda bi, r, c: (0, 0)),
        ],
        out_specs=pl.BlockSpec((None, FFN_ROWS, d), lambda bi, r, c: (bi, r, 0), pipeline_mode=pl.Buffered(1)),
        out_shape=jax.ShapeDtypeStruct((B, seq, d), F32),
        scratch_shapes=[pltpu.VMEM((2, FFN_ROWS + HALO, FFN_COLS), F32),
                        pltpu.VMEM((2, FFN_ROWS + HALO, FFN_COLS), F32)],
        compiler_params=_compiler_params(("parallel", "parallel", "arbitrary")),
        name="ffn_ln2",
    )(h1b, h1, w_up, w_up, conv_w, conv_w, conv_b, conv_b, w_down, g, b)


def _t5_bucket_table(n):
    dist = np.arange(n, dtype=np.int32)
    max_exact = REL_BUCKETS // 2
    d_f = np.maximum(dist, 1).astype(np.float32)
    large = max_exact + (np.log(d_f / np.float32(max_exact)) / np.float32(math.log(REL_MAX_DIST / max_exact))
                         * np.float32(REL_BUCKETS - max_exact)).astype(np.int32)
    return np.where(dist < max_exact, dist, np.minimum(large, REL_BUCKETS - 1))


def _near_bias_diagonals(rel_bias, qt):
    assert qt <= K_CHUNK
    probe = _t5_bucket_table(4 * REL_MAX_DIST)
    first_far = int(np.argmax(probe == REL_BUCKETS - 1))
    assert np.all(probe[first_far:] == REL_BUCKETS - 1)
    n_real = -(-(first_far + K_CHUNK - 1) // K_CHUNK)
    buckets = _t5_bucket_table((n_real + 1) * K_CHUNK)
    period = 2 * K_CHUNK
    u = np.arange(period)
    k = np.arange(n_real)[:, None]
    dist = np.where(u < K_CHUNK, k * K_CHUNK - u, k * K_CHUNK + period - u)
    idx = buckets[np.clip(dist, 0, len(buckets) - 1)]
    rel = rel_bias.astype(F32) - rel_bias[REL_BUCKETS - 1:].astype(F32)
    return jnp.transpose(rel[idx], (0, 2, 1))


def kernel(x, meta, rel_bias, w_in, kv_norm_g, w_uk, w_uv, w_pool, pool_scale, w_o, ln1_g, ln1_b, w_up, conv_w,
           conv_b, w_down, ln2_g, ln2_b):
    B, S, D = x.shape
    assert w_in.shape[0] == DEPTH and S % FFN_ROWS == 0
    T = S + N_META
    tp = -(-T // ROW_ALIGN) * ROW_ALIGN
    assert tp - T >= max(POOL_WINDOWS)
    top_k = min(TOPK_MAX, S // 4)
    assert top_k <= K_CHUNK

    wt = jnp.transpose(w_in[0])
    sizes = (("q", N_HEADS * HEAD_DIM), ("c_kv", KV_RANK), ("q_idx", IDX_HEADS * IDX_DIM), ("k_idx", IDX_DIM),
             ("w_idx", IDX_HEADS), ("u", len(POOL_WINDOWS) * POOL_GROUP))
    cols, start = {}, 0
    for name, size in sizes:
        cols[name] = (start, size)
        start += size
    assert start == wt.shape[0]

    qq, = _proj_qq(x, meta, wt, cols, tp, None)
    pool_diff, w_o_b = _proj_pool(x, meta, wt, cols, tp, w_o[0])
    c_kv, kk, wi = _proj_small(x, meta, wt, cols, kv_norm_g[0].reshape(1, KV_RANK), tp, None)

    wuk = jnp.transpose(w_uk[0], (1, 2, 0)).astype(BF16)
    o_lat, w_up_b, w_down_b = _dsa_attention(qq, wi, kk, c_kv, wuk, rel_bias, top_k, S, (w_up[0], w_down[0]))

    wuv = jnp.transpose(w_uv[0], (1, 0, 2)).astype(BF16)
    h1, h1b = _mix_ln1(o_lat, pool_diff, x, meta, wuv, w_pool[0].astype(BF16), pool_scale[0].reshape(1, -1),
                       w_o_b, ln1_g[0].reshape(1, D), ln1_b[0].reshape(1, D))

    return _ffn_ln2(h1, h1b, w_up_b, conv_w[0], conv_b[0].reshape(1, -1), w_down_b,
                    ln2_g[0].reshape(1, D), ln2_b[0].reshape(1, D), S)
```

```python
import functools
import math

import numpy as np
import jax
import jax.numpy as jnp
from jax import lax
from jax.experimental import pallas as pl
from jax.experimental.pallas import tpu as pltpu

F32 = jnp.float32
BF16 = jnp.bfloat16

N_META = 16
N_HEADS = 8
HEAD_DIM = 128
KV_RANK = 256
IDX_HEADS = 16
IDX_DIM = 64
TOPK_MAX = 256
POOL_WINDOWS = (2, 4, 8, 16)
POOL_GROUP = 256
CONV_WIDTH = 3
REL_BUCKETS = 32
REL_MAX_DIST = 128
DEPTH = 1
ALPHA = (2.0 * DEPTH) ** 0.25
LN_EPS = 1e-5
NEG_INF = -1e30

VMEM_LIMIT_BYTES = 60 * 1024 * 1024
SUBLANES = 8
LANES = 128
ROW_ALIGN = 256
PROJ_ROWS = 1024
Q_TILE = 256
TAIL_TILE = 16
K_CHUNK = 256
MIX_ROWS = 688
FFN_ROWS = 1024
FFN_COLS = 512
HALO = 16
INT_MIN = -(2 ** 31)
PACKED_SUBLANES = 16
TIE_ROUNDS = 8


def _dot(a, b):
    return jnp.dot(a, b, preferred_element_type=F32)


def _dot_nt(a, b):
    return lax.dot_general(a, b, (((1,), (1,)), ((), ())), preferred_element_type=F32)


def _layer_norm(y, g, b):
    mu = jnp.mean(y, axis=-1, keepdims=True)
    yc = y - mu
    var = jnp.mean(yc * yc, axis=-1, keepdims=True)
    return yc * lax.rsqrt(var + LN_EPS) * g + b


def _proj_rows(x_ref, meta_ref, w, tp, store):
    seq = x_ref.shape[0]
    store(pl.ds(0, N_META), _dot_nt(meta_ref[...].astype(BF16), w))
    for r in range(seq // PROJ_ROWS):
        acc = _dot_nt(x_ref[r * PROJ_ROWS:(r + 1) * PROJ_ROWS, :].astype(BF16), w)
        store(pl.ds(N_META + r * PROJ_ROWS, PROJ_ROWS), acc)
    n_pad = tp - seq - N_META
    store(pl.ds(seq + N_META, n_pad), jnp.zeros((n_pad, w.shape[0]), F32))


def _proj_cast_kernel(x_ref, meta_ref, w_ref, o_ref, *, scale_from):
    def store(rows, acc):
        o_ref[rows, :] = acc.astype(o_ref.dtype)

    scale = jnp.where(pl.program_id(1) >= scale_from, IDX_DIM ** -0.5, 1.0)
    _proj_rows(x_ref, meta_ref, (w_ref[...] * scale).astype(BF16), o_ref.shape[0], store)


def _proj_pool_kernel(x_ref, meta_ref, w_ref, o_ref, u_scr):
    def store(rows, acc):
        u_scr[rows, :] = acc

    _proj_rows(x_ref, meta_ref, w_ref[...].astype(BF16), u_scr.shape[0], store)
    group = pl.program_id(1)
    tp = u_scr.shape[0]
    pos = lax.broadcasted_iota(jnp.int32, (tp, 1), 0)
    for g, window in enumerate(POOL_WINDOWS):

        @pl.when(group == g)
        def _():
            u = u_scr[...]
            s = u
            shift = 1
            while shift < window:
                s = s + pltpu.roll(s, shift, axis=0)
                shift *= 2
            count = jnp.minimum(pos + 1, window).astype(F32)
            o_ref[...] = (s / count - u).astype(o_ref.dtype)


def _proj_small_kernel(x_ref, meta_ref, wc_ref, wk_ref, ww_ref, g_ref, c_ref, kk_ref, wi_ref):
    w = jnp.concatenate([wc_ref[...], wk_ref[...], wk_ref[...], ww_ref[...] * (IDX_HEADS ** -0.5),
                         jnp.zeros((LANES - IDX_HEADS, wc_ref.shape[1]), F32)], axis=0).astype(BF16)

    def store(rows, acc):
        c = acc[:, :KV_RANK]
        ms = jnp.mean(c * c, axis=-1, keepdims=True)
        c_ref[rows, :] = (c * lax.rsqrt(ms + LN_EPS) * g_ref[...]).astype(c_ref.dtype)
        kk_ref[rows, :] = acc[:, KV_RANK:KV_RANK + LANES].astype(kk_ref.dtype)
        wi_ref[rows, :] = acc[:, KV_RANK + LANES:]

    _proj_rows(x_ref, meta_ref, w, c_ref.shape[0], store)


def _batch_block(tp, n):
    return pl.BlockSpec((None, tp, n), lambda b, j: (b, 0, 0))


def _compiler_params(semantics):
    return pltpu.CompilerParams(dimension_semantics=semantics, vmem_limit_bytes=VMEM_LIMIT_BYTES)


def _with_side_cast(body, n_in, n_out):
    def kernel(*refs):
        side_in, side_out = refs[n_in], refs[n_in + 1 + n_out]
        side_out[...] = side_in[...].astype(side_out.dtype)
        body(*refs[:n_in], *refs[n_in + 1:n_in + 1 + n_out], *refs[n_in + 2 + n_out:])

    return kernel


def _slab_block(side, n_slabs, slab_of_step):
    slab = side.shape[0] // n_slabs
    assert slab * n_slabs == side.shape[0] and slab % PACKED_SUBLANES == 0
    return pl.BlockSpec((slab, side.shape[1]), lambda *step: (slab_of_step(*step), 0))


def _w_rows(d, n_rows, first_row):
    return pl.BlockSpec((pl.Element(n_rows), pl.Element(d)), lambda b, j: (first_row(j), 0))


def _proj_call(body, x, meta, wt, w_specs, n_j, extra_in, extra_specs, out_blocks, out_shapes, scratch, name, side):
    B, seq, d = x.shape
    out_blocks = list(out_blocks) if isinstance(out_blocks, (list, tuple)) else [out_blocks]
    out_shapes = list(out_shapes) if isinstance(out_shapes, (list, tuple)) else [out_shapes]
    in_specs = [_batch_block(seq, d), pl.BlockSpec(meta.shape, lambda b, j: (0, 0))] + w_specs + extra_specs
    args = [x, meta] + [wt] * len(w_specs) + list(extra_in)
    if side is not None:
        side_block = _slab_block(side, B * n_j, lambda b, j: b * n_j + j)
        body = _with_side_cast(body, len(args), len(out_blocks))
        in_specs, args = in_specs + [side_block], args + [side]
        out_blocks, out_shapes = out_blocks + [side_block], out_shapes + [jax.ShapeDtypeStruct(side.shape, BF16)]
    return pl.pallas_call(
        body,
        grid=(B, n_j),
        in_specs=in_specs,
        out_specs=out_blocks,
        out_shape=out_shapes,
        scratch_shapes=scratch,
        compiler_params=_compiler_params(("parallel", "arbitrary")),
        name=name,
    )(*args)


def _proj_qq(x, meta, wt, cols, tp, side):
    B, d, tn = x.shape[0], x.shape[2], 512
    n_q, n = cols["q"][1] // tn, cols["q"][1] + cols["q_idx"][1]
    first_row = lambda j: pl.multiple_of(
        jnp.where(j < n_q, cols["q"][0] + j * tn, cols["q_idx"][0] + (j - n_q) * tn), PACKED_SUBLANES)
    return _proj_call(functools.partial(_proj_cast_kernel, scale_from=n_q), x, meta, wt,
                      [_w_rows(d, tn, first_row)], n // tn, [], [],
                      pl.BlockSpec((None, tp, tn), lambda b, j: (b, 0, j)),
                      jax.ShapeDtypeStruct((B, tp, n), BF16), [], "proj_qq", side)


def _proj_pool(x, meta, wt, cols, tp, side):
    B, d, n = x.shape[0], x.shape[2], cols["u"][1]
    first_row = lambda j: pl.multiple_of(cols["u"][0] + j * POOL_GROUP, PACKED_SUBLANES)
    return _proj_call(_proj_pool_kernel, x, meta, wt, [_w_rows(d, POOL_GROUP, first_row)], n // POOL_GROUP, [], [],
                      pl.BlockSpec((None, tp, POOL_GROUP), lambda b, j: (b, 0, j)),
                      jax.ShapeDtypeStruct((B, tp, n), BF16), [pltpu.VMEM((tp, POOL_GROUP), F32)], "proj_pool", side)


def _proj_small(x, meta, wt, cols, kv_g, tp, side):
    B, d = x.shape[0], x.shape[2]
    w_specs = [_w_rows(d, cols[name][1], lambda j, name=name: cols[name][0]) for name in ("c_kv", "k_idx", "w_idx")]
    return _proj_call(_proj_small_kernel, x, meta, wt, w_specs, 1, [kv_g],
                      [pl.BlockSpec((1, KV_RANK), lambda b, j: (0, 0))],
                      [_batch_block(tp, KV_RANK), _batch_block(tp, LANES), _batch_block(tp, 128)],
                      [jax.ShapeDtypeStruct((B, tp, KV_RANK), BF16), jax.ShapeDtypeStruct((B, tp, LANES), BF16),
                       jax.ShapeDtypeStruct((B, tp, 128), F32)], [], "proj_small", side)


def _attn_kernel(*refs, n_tiles, aliased, n_sides, **static):
    n_in = 6
    sides_in = refs[n_in + int(aliased):n_in + int(aliased) + n_sides]
    refs = refs[:n_in] + refs[n_in + int(aliased) + n_sides:]
    o_ref = refs[n_in]
    sides_out = refs[n_in + 1:n_in + 1 + n_sides]
    refs = refs[:n_in + 1] + refs[n_in + 1 + n_sides:]

    @pl.when(pl.program_id(1) < n_tiles)
    def _():
        for side_in, side_out in zip(sides_in, sides_out):
            side_out[...] = side_in[...].astype(side_out.dtype)
        _attn_tile(*refs, **static)

    @pl.when(pl.program_id(1) >= n_tiles)
    def _():
        o_ref[...] = jnp.zeros(o_ref.shape, o_ref.dtype)


def _attn_tile(qq_ref, wi_ref, kk_ref, c_ref, wuk_ref, diag_ref, o_ref, nb_scr, sc_scr, sct_scr, scf_scr, mb_scr,
               wt_scr, qa_scr, s_scr, m_scr, l_scr, acc_scr, *, top_k, qt, first_tile, single_tile):
    i = first_tile if single_tile else first_tile + pl.program_id(1)
    n_chunks = ((i + 1) * qt - 1) // K_CHUNK + 1

    def chunk_loop(lo, body, init):
        return lax.fori_loop(lo, n_chunks, body, init, unroll=True if single_tile else 1)
    attn_w = N_HEADS * HEAD_DIM
    scale = HEAD_DIM ** -0.5
    n_pairs = IDX_HEADS // 2
    lanes_are_queries = qt % LANES == 0

    t_col = i * qt + lax.broadcasted_iota(jnp.int32, (qt, 1), 0)
    t_row = i * qt + lax.broadcasted_iota(jnp.int32, (1, qt), 1)
    s_row = lax.broadcasted_iota(jnp.int32, (1, K_CHUNK), 1)
    s_col = lax.broadcasted_iota(jnp.int32, (K_CHUNK, 1), 0)
    lane_half = lax.broadcasted_iota(jnp.int32, (K_CHUNK, LANES), 1) // IDX_DIM

    n_near = nb_scr.shape[0]

    @pl.when((pl.program_id(0) == 0) & (pl.program_id(1) == 0))
    def _():
        for k in range(n_near - 1):
            for h in range(N_HEADS):
                v = jnp.broadcast_to(diag_ref[k, h:h + 1, :], (qt, 2 * K_CHUNK))
                t = pltpu.roll(v, 0, 1, stride=1, stride_axis=0)
                nb_scr[k, h * qt:(h + 1) * qt, :] = t[:, :K_CHUNK]
        nb_scr[n_near - 1] = jnp.zeros(nb_scr.shape[1:], F32)

    for h in range(N_HEADS):
        qa_scr[h * qt:(h + 1) * qt, :] = _dot(
            qq_ref[:, h * HEAD_DIM:(h + 1) * HEAD_DIM], wuk_ref[h]).astype(BF16)
    if lanes_are_queries:
        wt_scr[...] = wi_ref[...].T

    def bf16_floor(v):
        near = v.astype(BF16)
        bits = lax.bitcast_convert_type(near, jnp.int16)
        below = lax.bitcast_convert_type(bits + jnp.where(bits < 0, jnp.int16(1), jnp.int16(-1)), BF16)
        return jnp.where(near.astype(F32) > v, below, near)

    def idx_chunk(j, carry):
        ks = kk_ref[pl.ds(pl.multiple_of(j * K_CHUNK, K_CHUNK), K_CHUNK), :]
        zero = jnp.zeros(ks.shape, ks.dtype)
        k_even = jnp.where(lane_half == 0, ks, zero)
        k_odd = jnp.where(lane_half == 1, ks, zero)
        score = jnp.zeros((K_CHUNK, qt) if lanes_are_queries else (qt, K_CHUNK), F32)
        if lanes_are_queries:
            for p in range(n_pairs):
                q_pair = qq_ref[:, attn_w + p * LANES:attn_w + (p + 1) * LANES]
                for hh, k_half in ((2 * p, k_even), (2 * p + 1, k_odd)):
                    score = score + jnp.maximum(_dot_nt(k_half, q_pair), 0.0) * wt_scr[hh:hh + 1, :]
        else:
            q_pairs = jnp.concatenate(
                [qq_ref[:, attn_w + p * LANES:attn_w + (p + 1) * LANES] for p in range(n_pairs)], axis=0)
            for half, k_half in enumerate((k_even, k_odd)):
                dots = jnp.maximum(_dot_nt(q_pairs, k_half), 0.0)
                for p in range(n_pairs):
                    hh = 2 * p + half
                    score = score + dots[p * qt:(p + 1) * qt] * wi_ref[:, hh:hh + 1]
        s_pos = j * K_CHUNK + (s_col if lanes_are_queries else s_row)
        score = jnp.where(s_pos <= (t_row if lanes_are_queries else t_col), score, NEG_INF)
        if lanes_are_queries:
            sct_scr[j] = score
            sc_scr[j] = score.T
            scf_scr[j] = bf16_floor(score)
        else:
            sc_scr[j] = score
        return carry

    chunk_loop(0, idx_chunk, 0)

    k_f = float(top_k)
    idx_bits = int(math.ceil(math.log2(sc_scr.shape[0] * K_CHUNK)))
    per_query = (1, qt) if lanes_are_queries else (qt, 1)
    key_axis = 0 if lanes_are_queries else 1
    s_idx = s_col if lanes_are_queries else s_row
    search_scr = sct_scr if lanes_are_queries else sc_scr

    def fold_chunks(chunk_fn, combine, init):
        if lanes_are_queries:
            acc_rows = 4 * SUBLANES

            def body(j, acc):
                v = chunk_fn(j, search_scr[j]).reshape(K_CHUNK // acc_rows, acc_rows, qt)
                return combine(acc, functools.reduce(combine, [v[t] for t in range(K_CHUNK // acc_rows)]))

            acc = chunk_loop(0, body, jnp.full((acc_rows, qt), init, F32))
        else:
            acc = chunk_loop(0, lambda j, acc: combine(acc, chunk_fn(j, search_scr[j])),
                                jnp.full((qt, K_CHUNK), init, F32))
        reduce = jnp.sum if combine is jnp.add else jnp.min
        return reduce(acc, axis=key_axis, keepdims=True)

    def count(pred):
        return fold_chunks(lambda j, sc: jnp.where(pred(j, sc), 1.0, 0.0), jnp.add, 0.0)

    def key_to_float(key):
        return lax.bitcast_convert_type(jnp.where(key < 0, key ^ jnp.int32(0x7FFFFFFF), key), F32)

    def count_coarse(cand_f):
        acc_rows = 4 * PACKED_SUBLANES
        cand_b = jnp.broadcast_to(cand_f, (acc_rows, qt)).astype(BF16)

        def body(j, acc):
            hit = jnp.where(scf_scr[j].reshape(K_CHUNK // acc_rows, acc_rows, qt) >= cand_b[None],
                            jnp.ones((), BF16), jnp.zeros((), BF16))
            return acc + functools.reduce(jnp.add, [hit[t] for t in range(K_CHUNK // acc_rows)])

        acc = chunk_loop(0, body, jnp.zeros((acc_rows, qt), BF16))
        return jnp.sum(acc.astype(F32), axis=0, keepdims=True)

    def bit_step(base, carry):
        key, bit = carry
        cand = key + bit
        cand_f = key_to_float(cand)
        n = count(lambda j, sc: sc >= cand_f) if base is None else count_coarse(cand_f - base)
        return jnp.where(n >= k_f, cand, key), lax.shift_right_logical(bit, jnp.int32(1))

    state = (jnp.full(per_query, INT_MIN, jnp.int32), jnp.int32(INT_MIN))
    if lanes_are_queries:
        state = lax.fori_loop(0, 16, lambda _, c: bit_step(0.0, c), state)
        for n_bits in (8, 8):
            base = key_to_float(state[0])

            def rebase(j, carry, base=base):
                scf_scr[j] = bf16_floor(sct_scr[j] - base)
                return carry

            chunk_loop(0, rebase, 0)
            state = lax.fori_loop(0, n_bits, lambda _, c, base=base: bit_step(base, c), state)
    else:
        state = lax.fori_loop(0, 32, lambda _, c: bit_step(None, c), state)
    thr = key_to_float(state[0])

    n_ge = count(lambda j, sc: sc >= thr)
    has_tie = jnp.max(jnp.where((n_ge > k_f) & (thr > NEG_INF), 1.0, 0.0)) > 0.0

    def tie_break():
        def next_value(_, m):
            n_gt = count(lambda j, sc: sc > m)
            above = fold_chunks(lambda j, sc: jnp.where(sc > m, sc, jnp.inf), jnp.minimum, jnp.inf)
            return jnp.where(n_gt >= k_f, above, m)

        m = lax.fori_loop(0, TIE_ROUNDS, next_value, thr)
        need = k_f - count(lambda j, sc: sc > m)

        def step(_, carry):
            cut, bit = carry
            cand = cut + bit
            n_before = count(lambda j, sc: (sc == m) & ((j * K_CHUNK + s_idx) < cand))
            return jnp.where(n_before < need, cand, cut), lax.shift_right_logical(bit, jnp.int32(1))

        cut, _ = lax.fori_loop(0, idx_bits, step,
                               (jnp.zeros(per_query, jnp.int32), jnp.int32(2 ** (idx_bits - 1))))
        return m, cut

    def to_rows(v):
        if not lanes_are_queries:
            return jnp.broadcast_to(v, (qt, K_CHUNK))
        t = jnp.broadcast_to(v, (qt, qt)).T
        return jnp.concatenate([t] * (K_CHUNK // qt), axis=1)

    def write_masks(keep_fn):
        def mask_chunk(j, carry):
            s_pos = j * K_CHUNK + s_row
            mb_scr[j] = jnp.where(keep_fn(sc_scr[j], s_pos) & (s_pos <= t_col), 0.0, NEG_INF)
            return carry

        chunk_loop(0, mask_chunk, 0)

    def masks_with_ties():
        m, cut = tie_break()
        m_b, cut_b = to_rows(m), to_rows(cut)
        write_masks(lambda sc, s_pos: (sc > m_b) | ((sc == m_b) & (s_pos <= cut_b)))

    def masks_plain():
        thr_b = to_rows(thr)
        write_masks(lambda sc, s_pos: sc >= thr_b)

    lax.cond(has_tie, masks_with_ties, masks_plain)

    rows_h = N_HEADS * qt
    lane_fold = lambda v, op: functools.reduce(op, [v[:, k * LANES:(k + 1) * LANES] for k in range(K_CHUNK // LANES)])

    def key_rows(j):
        start = j * K_CHUNK
        return pl.ds(start if isinstance(j, int) else pl.multiple_of(start, K_CHUNK), K_CHUNK)

    def over_chunks(chunk_fn):
        chunk_fn(0, True)
        if single_tile:
            chunk_loop(1, lambda j, carry: (chunk_fn(j, False), carry)[1], 0)
            return
        n_rest = n_chunks - 1

        def pair(p, carry):
            chunk_fn(2 * p + 1, False)
            chunk_fn(2 * p + 2, False)
            return carry

        lax.fori_loop(0, n_rest // 2, pair, 0)

        @pl.when(n_rest % 2 == 1)
        def _():
            chunk_fn(n_chunks - 1, False)

    def logit_chunk(j, first):
        near = jnp.minimum((i * qt) // K_CHUNK - j, n_near - 1)
        s = _dot_nt(qa_scr[...], c_ref[key_rows(j), :]) * scale + nb_scr[near]
        s = s + jnp.concatenate([mb_scr[j]] * N_HEADS, axis=0)
        s_scr[j] = s
        fold = lane_fold(s, jnp.maximum)
        m_scr[...] = fold if first else jnp.maximum(m_scr[...], fold)

    over_chunks(logit_chunk)
    m_b = jnp.broadcast_to(jnp.max(m_scr[...], axis=-1, keepdims=True), (rows_h, LANES))
    m_scr[...] = m_b

    def value_chunk(j, first):
        p = jnp.exp(s_scr[j] - jnp.concatenate([m_scr[...]] * (K_CHUNK // LANES), axis=1))
        pv = _dot(p.astype(BF16), c_ref[key_rows(j), :])
        l_scr[...] = lane_fold(p, jnp.add) if first else l_scr[...] + lane_fold(p, jnp.add)
        acc_scr[...] = pv if first else acc_scr[...] + pv

    over_chunks(value_chunk)
    out = acc_scr[...] * (1.0 / jnp.sum(l_scr[...], axis=-1, keepdims=True))
    for h in range(N_HEADS):
        o_ref[:, h * KV_RANK:(h + 1) * KV_RANK] = out[h * qt:(h + 1) * qt].astype(o_ref.dtype)


def _dsa_attention(qq, wi, kk, c_kv, wuk, rel_bias, top_k, seq, sides):
    B, tp, _ = qq.shape
    n_main = (seq + N_META) // Q_TILE
    assert (seq + N_META) - n_main * Q_TILE <= TAIL_TILE and Q_TILE % K_CHUNK == 0 and K_CHUNK % TAIL_TILE == 0
    n_chunks_max = tp // K_CHUNK
    width = N_HEADS * KV_RANK

    def call(qt, first_tile, n_tiles, n_fill, prev, sides=()):
        rows_h = N_HEADS * qt
        diag = _near_bias_diagonals(rel_bias, qt)
        tile = lambda i: first_tile + jnp.minimum(i, n_tiles - 1)
        in_specs = [
            pl.BlockSpec((None, qt, qq.shape[2]), lambda b, i: (b, tile(i), 0)),
            pl.BlockSpec((None, qt, 128), lambda b, i: (b, tile(i), 0)),
            pl.BlockSpec((None, tp, LANES), lambda b, i: (b, 0, 0)),
            pl.BlockSpec((None, tp, KV_RANK), lambda b, i: (b, 0, 0)),
            pl.BlockSpec(wuk.shape, lambda b, i: (0, 0, 0)),
            pl.BlockSpec(diag.shape, lambda b, i: (0, 0, 0)),
        ]
        args = [qq, wi, kk, c_kv, wuk, diag]
        out_specs = [pl.BlockSpec((None, qt, width), lambda b, i: (b, first_tile + i, 0))]
        out_shape = [jax.ShapeDtypeStruct((B, tp, width), BF16)]
        aliases = {}
        if prev is not None:
            in_specs.append(pl.BlockSpec(memory_space=pl.ANY))
            args.append(prev)
            aliases = {len(args) - 1: 0}
        for side in sides:
            side_block = _slab_block(side, B * n_tiles, lambda b, i: b * n_tiles + jnp.minimum(i, n_tiles - 1))
            in_specs.append(side_block)
            args.append(side)
            out_specs.append(side_block)
            out_shape.append(jax.ShapeDtypeStruct(side.shape, BF16))
        return pl.pallas_call(
            functools.partial(_attn_kernel, n_tiles=n_tiles, aliased=prev is not None, n_sides=len(sides),
                              top_k=top_k, qt=qt, first_tile=first_tile, single_tile=n_tiles == 1),
            grid=(B, n_tiles + n_fill),
            in_specs=in_specs,
            out_specs=out_specs,
            out_shape=out_shape,
            input_output_aliases=aliases,
            scratch_shapes=[pltpu.VMEM((diag.shape[0] + 1, rows_h, K_CHUNK), F32),
                            pltpu.VMEM((n_chunks_max, qt, K_CHUNK), F32),
                            pltpu.VMEM((n_chunks_max, K_CHUNK, qt), F32),
                            pltpu.VMEM((n_chunks_max, K_CHUNK, qt), BF16),
                            pltpu.VMEM((n_chunks_max, qt, K_CHUNK), F32),
                            pltpu.VMEM((LANES, qt), F32),
                            pltpu.VMEM((rows_h, KV_RANK), BF16),
                            pltpu.VMEM((n_chunks_max, rows_h, K_CHUNK), F32),
                            pltpu.VMEM((rows_h, LANES), F32),
                            pltpu.VMEM((rows_h, LANES), F32),
                            pltpu.VMEM((rows_h, KV_RANK), F32)],
            compiler_params=_compiler_params(("arbitrary", "arbitrary")),
            name="dsa_attention" if prev is None else "dsa_attention_tail",
        )(*args)

    o_lat, *sides_b = call(Q_TILE, 0, n_main, tp // Q_TILE - n_main, None, sides)
    return (call(TAIL_TILE, n_main * Q_TILE // TAIL_TILE, 1, 0, o_lat)[0], *sides_b)


def _mix_window_start(r, seq):
    return min(max(r * MIX_ROWS - N_META, 0), seq - MIX_ROWS)


def _residual_rows(x_ref, meta_ref, h_scr, seq):
    r = pl.program_id(1)
    n_real = -(-(seq + N_META) // MIX_ROWS)
    for rv in range(n_real):

        @pl.when(r == rv)
        def _(rv=rv):
            skip = rv * MIX_ROWS - N_META - _mix_window_start(rv, seq)
            if rv == 0:
                h_scr[:N_META, :] = meta_ref[...]
                h_scr[N_META:, :] = x_ref[:MIX_ROWS - N_META, :]
            elif skip == 0:
                h_scr[...] = x_ref[...]
            else:
                h_scr[:MIX_ROWS - skip, :] = x_ref[skip:, :]
                h_scr[MIX_ROWS - skip:, :] = jnp.zeros((skip, h_scr.shape[1]), F32)

    @pl.when(r >= n_real)
    def _():
        h_scr[...] = jnp.zeros(h_scr.shape, F32)


def _mix_kernel(ol_ref, pd_ref, x_ref, meta_ref, wuv_ref, wp_ref, ps_ref, wo_ref, g_ref, b_ref, h1_ref, h1b_ref,
                h_scr, *, seq):
    _residual_rows(x_ref, meta_ref, h_scr, seq)
    attn = [_dot(ol_ref[:, h * KV_RANK:(h + 1) * KV_RANK], wuv_ref[h]) for h in range(N_HEADS)]
    pool = [_dot(pd_ref[:, g * POOL_GROUP:(g + 1) * POOL_GROUP], wp_ref[g]) for g in range(len(POOL_WINDOWS))]
    pool = jnp.concatenate(pool, axis=-1) * ps_ref[...]
    cat = jnp.concatenate(attn + [pool], axis=-1).astype(BF16)
    y = ALPHA * h_scr[...] + _dot(cat, wo_ref[...])
    h1 = _layer_norm(y, g_ref[...], b_ref[...])
    h1_ref[...] = h1
    h1b_ref[...] = h1.astype(h1b_ref.dtype)


def _mix_ln1(o_lat, pool_diff, x, meta, wuv, wpool, pool_scale, wo, g, b):
    B, seq, d = x.shape
    n_rows = seq + N_META
    assert n_rows % MIX_ROWS == 0
    row = lambda n: pl.BlockSpec((None, MIX_ROWS, n), lambda bi, r: (bi, r, 0))
    full = lambda a: pl.BlockSpec(a.shape, lambda bi, r: (0,) * a.ndim, pipeline_mode=pl.Buffered(1))
    window = pl.BlockSpec(
        (None, pl.Element(MIX_ROWS), pl.Element(d)),
        lambda bi, r: (bi, pl.multiple_of(jnp.clip(r * MIX_ROWS - N_META, 0, seq - MIX_ROWS), N_META), 0))
    return pl.pallas_call(
        functools.partial(_mix_kernel, seq=seq),
        grid=(B, n_rows // MIX_ROWS),
        in_specs=[row(o_lat.shape[2]), row(pool_diff.shape[2]), window, full(meta),
                  full(wuv), full(wpool), full(pool_scale), full(wo), full(g), full(b)],
        out_specs=[row(d), row(d)],
        out_shape=[jax.ShapeDtypeStruct((B, n_rows, d), F32), jax.ShapeDtypeStruct((B, n_rows, d), BF16)],
        scratch_shapes=[pltpu.VMEM((MIX_ROWS, d), F32)],
        compiler_params=_compiler_params(("parallel", "arbitrary")),
        name="mix_ln1",
    )(o_lat, pool_diff, x, meta, wuv, wpool, pool_scale, wo, g, b)


def _gelu_tanh(x):
    return 0.5 * x * (1.0 + jnp.tanh(math.sqrt(2.0 / math.pi) * (x + 0.044715 * (x * x * x))))


def _ffn_kernel(hw_ref, hres_ref, wa_ref, wg_ref, cwa_ref, cwg_ref, cba_ref, cbg_ref, wd_ref, g_ref, b_ref, o_ref,
                za_scr, zg_scr):
    c = pl.program_id(2)
    n_c = pl.num_programs(2) - 1

    def up(slot):
        x = hw_ref[...]
        za_scr[slot] = _dot(x, wa_ref[...])
        zg_scr[slot] = _dot(x, wg_ref[...])

    def conv(z, cw_ref, cb_ref):
        cw = cw_ref[...]
        n = z.shape[0]
        out = z[HALO - 2:n - 2] * cw[0:1] + z[HALO - 1:n - 1] * cw[1:2] + z[HALO:] * cw[2:3]
        return out + cb_ref[...]

    def down(slot):
        a = conv(za_scr[slot], cwa_ref, cba_ref)
        gate = conv(zg_scr[slot], cwg_ref, cbg_ref)
        act = (_gelu_tanh(a) * gate).astype(BF16)
        o_ref[...] += _dot(act, wd_ref[...])

    @pl.when(c == 0)
    def _():
        o_ref[...] = jnp.zeros(o_ref.shape, o_ref.dtype)
        up(0)

    @pl.when((c > 0) & (c < n_c))
    def _():
        up(c % 2)
        down((c - 1) % 2)

    @pl.when(c == n_c)
    def _():
        down((c - 1) % 2)
        o_ref[...] = _layer_norm(ALPHA * hres_ref[...] + o_ref[...], g_ref[...], b_ref[...])


def _ffn_ln2(h1, h1b, w_up, conv_w, conv_b, w_down, g, b, seq):
    B, tp, d = h1.shape
    d_ff = w_down.shape[0]
    n_c = d_ff // FFN_COLS
    up_c = lambda c: jnp.minimum(c, n_c - 1)
    dn_c = lambda c: jnp.maximum(c - 1, 0)
    vec = lambda off: pl.BlockSpec((1, FFN_COLS), lambda bi, r, c: (0, dn_c(c) + off))
    return pl.pallas_call(
        _ffn_kernel,
        grid=(B, seq // FFN_ROWS, n_c + 1),
        in_specs=[
            pl.BlockSpec((None, pl.Element(FFN_ROWS + HALO), pl.Element(d)),
                         lambda bi, r, c: (bi, r * FFN_ROWS + N_META - HALO, 0)),
            pl.BlockSpec((None, pl.Element(FFN_ROWS), pl.Element(d)),
                         lambda bi, r, c: (bi, pl.multiple_of(r * FFN_ROWS + N_META, N_META), 0)),
            pl.BlockSpec((d, FFN_COLS), lambda bi, r, c: (0, up_c(c))),
            pl.BlockSpec((d, FFN_COLS), lambda bi, r, c: (0, up_c(c) + n_c)),
            pl.BlockSpec((CONV_WIDTH, FFN_COLS), lambda bi, r, c: (0, dn_c(c))),
            pl.BlockSpec((CONV_WIDTH, FFN_COLS), lambda bi, r, c: (0, dn_c(c) + n_c)),
            vec(0), vec(n_c),
            pl.BlockSpec((FFN_COLS, d), lambda bi, r, c: (dn_c(c), 0)),
            pl.BlockSpec((1, d), lambda bi, r, c: (0, 0)),
            pl.BlockSpec((1, d), lambda bi, r, c: (0, 0)),
        ],
        out_specs=pl.BlockSpec((None, FFN_ROWS, d), lambda bi, r, c: (bi, r, 0), pipeline_mode=pl.Buffered(1)),
        out_shape=jax.ShapeDtypeStruct((B, seq, d), F32),
        scratch_shapes=[pltpu.VMEM((2, FFN_ROWS + HALO, FFN_COLS), F32),
                        pltpu.VMEM((2, FFN_ROWS + HALO, FFN_COLS), F32)],
        compiler_params=_compiler_params(("parallel", "parallel", "arbitrary")),
        name="ffn_ln2",
    )(h1b, h1, w_up, w_up, conv_w, conv_w, conv_b, conv_b, w_down, g, b)


def _t5_bucket_table(n):
    dist = np.arange(n, dtype=np.int32)
    max_exact = REL_BUCKETS // 2
    d_f = np.maximum(dist, 1).astype(np.float32)
    large = max_exact + (np.log(d_f / np.float32(max_exact)) / np.float32(math.log(REL_MAX_DIST / max_exact))
                         * np.float32(REL_BUCKETS - max_exact)).astype(np.int32)
    return np.where(dist < max_exact, dist, np.minimum(large, REL_BUCKETS - 1))


def _near_bias_diagonals(rel_bias, qt):
    assert qt <= K_CHUNK
    probe = _t5_bucket_table(4 * REL_MAX_DIST)
    first_far = int(np.argmax(probe == REL_BUCKETS - 1))
    assert np.all(probe[first_far:] == REL_BUCKETS - 1)
    n_real = -(-(first_far + K_CHUNK - 1) // K_CHUNK)
    buckets = _t5_bucket_table((n_real + 1) * K_CHUNK)
    period = 2 * K_CHUNK
    u = np.arange(period)
    k = np.arange(n_real)[:, None]
    dist = np.where(u < K_CHUNK, k * K_CHUNK - u, k * K_CHUNK + period - u)
    idx = buckets[np.clip(dist, 0, len(buckets) - 1)]
    rel = rel_bias.astype(F32) - rel_bias[REL_BUCKETS - 1:].astype(F32)
    return jnp.transpose(rel[idx], (0, 2, 1))


def kernel(x, meta, rel_bias, w_in, kv_norm_g, w_uk, w_uv, w_pool, pool_scale, w_o, ln1_g, ln1_b, w_up, conv_w,
           conv_b, w_down, ln2_g, ln2_b):
    B, S, D = x.shape
    assert w_in.shape[0] == DEPTH and S % FFN_ROWS == 0
    T = S + N_META
    tp = -(-T // ROW_ALIGN) * ROW_ALIGN
    assert tp - T >= max(POOL_WINDOWS)
    top_k = min(TOPK_MAX, S // 4)
    assert top_k <= K_CHUNK

    wt = jnp.transpose(w_in[0])
    sizes = (("q", N_HEADS * HEAD_DIM), ("c_kv", KV_RANK), ("q_idx", IDX_HEADS * IDX_DIM), ("k_idx", IDX_DIM),
             ("w_idx", IDX_HEADS), ("u", len(POOL_WINDOWS) * POOL_GROUP))
    cols, start = {}, 0
    for name, size in sizes:
        cols[name] = (start, size)
        start += size
    assert start == wt.shape[0]

    qq, = _proj_qq(x, meta, wt, cols, tp, None)
    pool_diff, w_o_b = _proj_pool(x, meta, wt, cols, tp, w_o[0])
    c_kv, kk, wi = _proj_small(x, meta, wt, cols, kv_norm_g[0].reshape(1, KV_RANK), tp, None)

    wuk = jnp.transpose(w_uk[0], (1, 2, 0)).astype(BF16)
    o_lat, w_up_b, w_down_b = _dsa_attention(qq, wi, kk, c_kv, wuk, rel_bias, top_k, S, (w_up[0], w_down[0]))

    wuv = jnp.transpose(w_uv[0], (1, 0, 2)).astype(BF16)
    h1, h1b = _mix_ln1(o_lat, pool_diff, x, meta, wuv, w_pool[0].astype(BF16), pool_scale[0].reshape(1, -1),
                       w_o_b, ln1_g[0].reshape(1, D), ln1_b[0].reshape(1, D))

    return _ffn_ln2(h1, h1b, w_up_b, conv_w[0], conv_b[0].reshape(1, -1), w_down_b,
                    ln2_g[0].reshape(1, D), ln2_b[0].reshape(1, D), S)
```

```python
import functools
import math

import numpy as np
import jax
import jax.numpy as jnp
from jax import lax
from jax.experimental import pallas as pl
from jax.experimental.pallas import tpu as pltpu

F32 = jnp.float32
BF16 = jnp.bfloat16

N_META = 16
N_HEADS = 8
HEAD_DIM = 128
KV_RANK = 256
IDX_HEADS = 16
IDX_DIM = 64
TOPK_MAX = 256
POOL_WINDOWS = (2, 4, 8, 16)
POOL_GROUP = 256
CONV_WIDTH = 3
REL_BUCKETS = 32
REL_MAX_DIST = 128
DEPTH = 1
ALPHA = (2.0 * DEPTH) ** 0.25
LN_EPS = 1e-5
NEG_INF = -1e30

VMEM_LIMIT_BYTES = 60 * 1024 * 1024
SUBLANES = 8
LANES = 128
ROW_ALIGN = 256
PROJ_ROWS = 1024
Q_TILE = 256
TAIL_TILE = 16
K_CHUNK = 256
MIX_ROWS = 688
FFN_ROWS = 1024
FFN_COLS = 512
HALO = 16
INT_MIN = -(2 ** 31)
PACKED_SUBLANES = 16
CHUNKS_PER_TRIP = 2
TIE_ROUNDS = 8


def _dot(a, b):
    return jnp.dot(a, b, preferred_element_type=F32)


def _dot_nt(a, b):
    return lax.dot_general(a, b, (((1,), (1,)), ((), ())), preferred_element_type=F32)


def _layer_norm(y, g, b):
    mu = jnp.mean(y, axis=-1, keepdims=True)
    yc = y - mu
    var = jnp.mean(yc * yc, axis=-1, keepdims=True)
    return yc * lax.rsqrt(var + LN_EPS) * g + b


def _proj_rows(x_ref, meta_ref, w, tp, store):
    seq = x_ref.shape[0]
    store(pl.ds(0, N_META), _dot_nt(meta_ref[...].astype(BF16), w))
    for r in range(seq // PROJ_ROWS):
        acc = _dot_nt(x_ref[r * PROJ_ROWS:(r + 1) * PROJ_ROWS, :].astype(BF16), w)
        store(pl.ds(N_META + r * PROJ_ROWS, PROJ_ROWS), acc)
    n_pad = tp - seq - N_META
    store(pl.ds(seq + N_META, n_pad), jnp.zeros((n_pad, w.shape[0]), F32))


def _proj_cast_kernel(x_ref, meta_ref, w_ref, o_ref, *, scale_from):
    def store(rows, acc):
        o_ref[rows, :] = acc.astype(o_ref.dtype)

    scale = jnp.where(pl.program_id(1) >= scale_from, IDX_DIM ** -0.5, 1.0)
    _proj_rows(x_ref, meta_ref, (w_ref[...] * scale).astype(BF16), o_ref.shape[0], store)


def _proj_pool_kernel(x_ref, meta_ref, w_ref, o_ref, u_scr):
    def store(rows, acc):
        u_scr[rows, :] = acc

    _proj_rows(x_ref, meta_ref, w_ref[...].astype(BF16), u_scr.shape[0], store)
    group = pl.program_id(1)
    tp = u_scr.shape[0]
    pos = lax.broadcasted_iota(jnp.int32, (tp, 1), 0)
    for g, window in enumerate(POOL_WINDOWS):

        @pl.when(group == g)
        def _():
            u = u_scr[...]
            s = u
            shift = 1
            while shift < window:
                s = s + pltpu.roll(s, shift, axis=0)
                shift *= 2
            count = jnp.minimum(pos + 1, window).astype(F32)
            o_ref[...] = (s / count - u).astype(o_ref.dtype)


def _proj_small_kernel(x_ref, meta_ref, wc_ref, wk_ref, ww_ref, g_ref, c_ref, kk_ref, wi_ref):
    w = jnp.concatenate([wc_ref[...], wk_ref[...], wk_ref[...], ww_ref[...] * (IDX_HEADS ** -0.5),
                         jnp.zeros((LANES - IDX_HEADS, wc_ref.shape[1]), F32)], axis=0).astype(BF16)

    def store(rows, acc):
        c = acc[:, :KV_RANK]
        ms = jnp.mean(c * c, axis=-1, keepdims=True)
        c_ref[rows, :] = (c * lax.rsqrt(ms + LN_EPS) * g_ref[...]).astype(c_ref.dtype)
        kk_ref[rows, :] = acc[:, KV_RANK:KV_RANK + LANES].astype(kk_ref.dtype)
        wi_ref[rows, :] = acc[:, KV_RANK + LANES:]

    _proj_rows(x_ref, meta_ref, w, c_ref.shape[0], store)


def _batch_block(tp, n):
    return pl.BlockSpec((None, tp, n), lambda b, j: (b, 0, 0))


def _compiler_params(semantics):
    return pltpu.CompilerParams(dimension_semantics=semantics, vmem_limit_bytes=VMEM_LIMIT_BYTES)


def _with_side_cast(body, n_in, n_out):
    def kernel(*refs):
        side_in, side_out = refs[n_in], refs[n_in + 1 + n_out]
        side_out[...] = side_in[...].astype(side_out.dtype)
        body(*refs[:n_in], *refs[n_in + 1:n_in + 1 + n_out], *refs[n_in + 2 + n_out:])

    return kernel


def _slab_block(side, n_slabs, slab_of_step):
    slab = side.shape[0] // n_slabs
    assert slab * n_slabs == side.shape[0] and slab % PACKED_SUBLANES == 0
    return pl.BlockSpec((slab, side.shape[1]), lambda *step: (slab_of_step(*step), 0))


def _w_rows(d, n_rows, first_row):
    return pl.BlockSpec((pl.Element(n_rows), pl.Element(d)), lambda b, j: (first_row(j), 0))


def _proj_call(body, x, meta, wt, w_specs, n_j, extra_in, extra_specs, out_blocks, out_shapes, scratch, name, side):
    B, seq, d = x.shape
    out_blocks = list(out_blocks) if isinstance(out_blocks, (list, tuple)) else [out_blocks]
    out_shapes = list(out_shapes) if isinstance(out_shapes, (list, tuple)) else [out_shapes]
    in_specs = [_batch_block(seq, d), pl.BlockSpec(meta.shape, lambda b, j: (0, 0))] + w_specs + extra_specs
    args = [x, meta] + [wt] * len(w_specs) + list(extra_in)
    if side is not None:
        side_block = _slab_block(side, B * n_j, lambda b, j: b * n_j + j)
        body = _with_side_cast(body, len(args), len(out_blocks))
        in_specs, args = in_specs + [side_block], args + [side]
        out_blocks, out_shapes = out_blocks + [side_block], out_shapes + [jax.ShapeDtypeStruct(side.shape, BF16)]
    return pl.pallas_call(
        body,
        grid=(B, n_j),
        in_specs=in_specs,
        out_specs=out_blocks,
        out_shape=out_shapes,
        scratch_shapes=scratch,
        compiler_params=_compiler_params(("parallel", "arbitrary")),
        name=name,
    )(*args)


def _proj_qq(x, meta, wt, cols, tp, side):
    B, d, tn = x.shape[0], x.shape[2], 512
    n_q, n = cols["q"][1] // tn, cols["q"][1] + cols["q_idx"][1]
    first_row = lambda j: pl.multiple_of(
        jnp.where(j < n_q, cols["q"][0] + j * tn, cols["q_idx"][0] + (j - n_q) * tn), PACKED_SUBLANES)
    return _proj_call(functools.partial(_proj_cast_kernel, scale_from=n_q), x, meta, wt,
                      [_w_rows(d, tn, first_row)], n // tn, [], [],
                      pl.BlockSpec((None, tp, tn), lambda b, j: (b, 0, j)),
                      jax.ShapeDtypeStruct((B, tp, n), BF16), [], "proj_qq", side)


def _proj_pool(x, meta, wt, cols, tp, side):
    B, d, n = x.shape[0], x.shape[2], cols["u"][1]
    first_row = lambda j: pl.multiple_of(cols["u"][0] + j * POOL_GROUP, PACKED_SUBLANES)
    return _proj_call(_proj_pool_kernel, x, meta, wt, [_w_rows(d, POOL_GROUP, first_row)], n // POOL_GROUP, [], [],
                      pl.BlockSpec((None, tp, POOL_GROUP), lambda b, j: (b, 0, j)),
                      jax.ShapeDtypeStruct((B, tp, n), BF16), [pltpu.VMEM((tp, POOL_GROUP), F32)], "proj_pool", side)


def _proj_small(x, meta, wt, cols, kv_g, tp, side):
    B, d = x.shape[0], x.shape[2]
    w_specs = [_w_rows(d, cols[name][1], lambda j, name=name: cols[name][0]) for name in ("c_kv", "k_idx", "w_idx")]
    return _proj_call(_proj_small_kernel, x, meta, wt, w_specs, 1, [kv_g],
                      [pl.BlockSpec((1, KV_RANK), lambda b, j: (0, 0))],
                      [_batch_block(tp, KV_RANK), _batch_block(tp, LANES), _batch_block(tp, 128)],
                      [jax.ShapeDtypeStruct((B, tp, KV_RANK), BF16), jax.ShapeDtypeStruct((B, tp, LANES), BF16),
                       jax.ShapeDtypeStruct((B, tp, 128), F32)], [], "proj_small", side)


def _attn_kernel(*refs, n_tiles, aliased, n_sides, **static):
    n_in = 6
    sides_in = refs[n_in + int(aliased):n_in + int(aliased) + n_sides]
    refs = refs[:n_in] + refs[n_in + int(aliased) + n_sides:]
    o_ref = refs[n_in]
    sides_out = refs[n_in + 1:n_in + 1 + n_sides]
    refs = refs[:n_in + 1] + refs[n_in + 1 + n_sides:]

    @pl.when(pl.program_id(1) < n_tiles)
    def _():
        for side_in, side_out in zip(sides_in, sides_out):
            side_out[...] = side_in[...].astype(side_out.dtype)
        _attn_tile(*refs, **static)

    @pl.when(pl.program_id(1) >= n_tiles)
    def _():
        o_ref[...] = jnp.zeros(o_ref.shape, o_ref.dtype)


def _attn_tile(qq_ref, wi_ref, kk_ref, c_ref, wuk_ref, diag_ref, o_ref, nb_scr, sc_scr, sct_scr, scf_scr, mb_scr,
               wt_scr, qa_scr, s_scr, m_scr, l_scr, acc_scr, *, top_k, qt, first_tile, single_tile):
    i = first_tile if single_tile else first_tile + pl.program_id(1)
    n_chunks = ((i + 1) * qt - 1) // K_CHUNK + 1

    def chunk_loop(lo, body, init):
        if single_tile:
            return lax.fori_loop(lo, n_chunks, body, init, unroll=True)

        def block(k, first, c):
            for t in range(k):
                c = body(first + t, c)
            return c

        n, k = n_chunks - lo, CHUNKS_PER_TRIP
        carry = lax.fori_loop(0, n // k, lambda p, c: block(k, lo + k * p, c), init)
        done = lo + n // k * k
        while k > 1:
            k //= 2
            take = ((n_chunks - done) // k) % 2 == 1
            carry = lax.cond(take, lambda c, k=k, done=done: block(k, done, c), lambda c: c, carry)
            done = done + jnp.where(take, k, 0)
        return carry
    attn_w = N_HEADS * HEAD_DIM
    scale = HEAD_DIM ** -0.5
    n_pairs = IDX_HEADS // 2
    lanes_are_queries = qt % LANES == 0

    t_col = i * qt + lax.broadcasted_iota(jnp.int32, (qt, 1), 0)
    t_row = i * qt + lax.broadcasted_iota(jnp.int32, (1, qt), 1)
    s_row = lax.broadcasted_iota(jnp.int32, (1, K_CHUNK), 1)
    s_col = lax.broadcasted_iota(jnp.int32, (K_CHUNK, 1), 0)
    lane_half = lax.broadcasted_iota(jnp.int32, (K_CHUNK, LANES), 1) // IDX_DIM

    n_near = nb_scr.shape[0]

    @pl.when((pl.program_id(0) == 0) & (pl.program_id(1) == 0))
    def _():
        for k in range(n_near - 1):
            for h in range(N_HEADS):
                v = jnp.broadcast_to(diag_ref[k, h:h + 1, :], (qt, 2 * K_CHUNK))
                t = pltpu.roll(v, 0, 1, stride=1, stride_axis=0)
                nb_scr[k, h * qt:(h + 1) * qt, :] = t[:, :K_CHUNK]
        nb_scr[n_near - 1] = jnp.zeros(nb_scr.shape[1:], F32)

    for h in range(N_HEADS):
        qa_scr[h * qt:(h + 1) * qt, :] = _dot(
            qq_ref[:, h * HEAD_DIM:(h + 1) * HEAD_DIM], wuk_ref[h]).astype(BF16)
    if lanes_are_queries:
        wt_scr[...] = wi_ref[...].T

    def bf16_floor(v):
        near = v.astype(BF16)
        bits = lax.bitcast_convert_type(near, jnp.int16)
        below = lax.bitcast_convert_type(bits + jnp.where(bits < 0, jnp.int16(1), jnp.int16(-1)), BF16)
        return jnp.where(near.astype(F32) > v, below, near)

    def idx_chunk(j, carry):
        ks = kk_ref[pl.ds(pl.multiple_of(j * K_CHUNK, K_CHUNK), K_CHUNK), :]
        zero = jnp.zeros(ks.shape, ks.dtype)
        k_even = jnp.where(lane_half == 0, ks, zero)
        k_odd = jnp.where(lane_half == 1, ks, zero)
        score = jnp.zeros((K_CHUNK, qt) if lanes_are_queries else (qt, K_CHUNK), F32)
        if lanes_are_queries:
            for p in range(n_pairs):
                q_pair = qq_ref[:, attn_w + p * LANES:attn_w + (p + 1) * LANES]
                for hh, k_half in ((2 * p, k_even), (2 * p + 1, k_odd)):
                    score = score + jnp.maximum(_dot_nt(k_half, q_pair), 0.0) * wt_scr[hh:hh + 1, :]
        else:
            q_pairs = jnp.concatenate(
                [qq_ref[:, attn_w + p * LANES:attn_w + (p + 1) * LANES] for p in range(n_pairs)], axis=0)
            for half, k_half in enumerate((k_even, k_odd)):
                dots = jnp.maximum(_dot_nt(q_pairs, k_half), 0.0)
                for p in range(n_pairs):
                    hh = 2 * p + half
                    score = score + dots[p * qt:(p + 1) * qt] * wi_ref[:, hh:hh + 1]
        s_pos = j * K_CHUNK + (s_col if lanes_are_queries else s_row)
        score = jnp.where(s_pos <= (t_row if lanes_are_queries else t_col), score, NEG_INF)
        if lanes_are_queries:
            sct_scr[j] = score
            sc_scr[j] = score.T
            scf_scr[j] = bf16_floor(score)
        else:
            sc_scr[j] = score
        return carry

    chunk_loop(0, idx_chunk, 0)

    k_f = float(top_k)
    idx_bits = int(math.ceil(math.log2(sc_scr.shape[0] * K_CHUNK)))
    per_query = (1, qt) if lanes_are_queries else (qt, 1)
    key_axis = 0 if lanes_are_queries else 1
    s_idx = s_col if lanes_are_queries else s_row
    search_scr = sct_scr if lanes_are_queries else sc_scr

    def fold_chunks(chunk_fn, combine, init):
        if lanes_are_queries:
            acc_rows = 4 * SUBLANES

            def body(j, acc):
                v = chunk_fn(j, search_scr[j]).reshape(K_CHUNK // acc_rows, acc_rows, qt)
                return combine(acc, functools.reduce(combine, [v[t] for t in range(K_CHUNK // acc_rows)]))

            acc = chunk_loop(0, body, jnp.full((acc_rows, qt), init, F32))
        else:
            acc = chunk_loop(0, lambda j, acc: combine(acc, chunk_fn(j, search_scr[j])),
                                jnp.full((qt, K_CHUNK), init, F32))
        reduce = jnp.sum if combine is jnp.add else jnp.min
        return reduce(acc, axis=key_axis, keepdims=True)

    def count(pred):
        return fold_chunks(lambda j, sc: jnp.where(pred(j, sc), 1.0, 0.0), jnp.add, 0.0)

    def key_to_float(key):
        return lax.bitcast_convert_type(jnp.where(key < 0, key ^ jnp.int32(0x7FFFFFFF), key), F32)

    def count_coarse(cand_f):
        acc_rows = 4 * PACKED_SUBLANES
        cand_b = jnp.broadcast_to(cand_f, (acc_rows, qt)).astype(BF16)

        def body(j, acc):
            hit = jnp.where(scf_scr[j].reshape(K_CHUNK // acc_rows, acc_rows, qt) >= cand_b[None],
                            jnp.ones((), BF16), jnp.zeros((), BF16))
            return acc + functools.reduce(jnp.add, [hit[t] for t in range(K_CHUNK // acc_rows)])

        acc = chunk_loop(0, body, jnp.zeros((acc_rows, qt), BF16))
        return jnp.sum(acc.astype(F32), axis=0, keepdims=True)

    def bit_step(base, carry):
        key, bit = carry
        cand = key + bit
        cand_f = key_to_float(cand)
        n = count(lambda j, sc: sc >= cand_f) if base is None else count_coarse(cand_f - base)
        return jnp.where(n >= k_f, cand, key), lax.shift_right_logical(bit, jnp.int32(1))

    state = (jnp.full(per_query, INT_MIN, jnp.int32), jnp.int32(INT_MIN))
    if lanes_are_queries:
        state = lax.fori_loop(0, 16, lambda _, c: bit_step(0.0, c), state)
        for n_bits in (8, 8):
            base = key_to_float(state[0])

            def rebase(j, carry, base=base):
                scf_scr[j] = bf16_floor(sct_scr[j] - base)
                return carry

            chunk_loop(0, rebase, 0)
            state = lax.fori_loop(0, n_bits, lambda _, c, base=base: bit_step(base, c), state)
    else:
        state = lax.fori_loop(0, 32, lambda _, c: bit_step(None, c), state)
    thr = key_to_float(state[0])

    n_ge = count(lambda j, sc: sc >= thr)
    has_tie = jnp.max(jnp.where((n_ge > k_f) & (thr > NEG_INF), 1.0, 0.0)) > 0.0

    def tie_break():
        def next_value(_, m):
            n_gt = count(lambda j, sc: sc > m)
            above = fold_chunks(lambda j, sc: jnp.where(sc > m, sc, jnp.inf), jnp.minimum, jnp.inf)
            return jnp.where(n_gt >= k_f, above, m)

        m = lax.fori_loop(0, TIE_ROUNDS, next_value, thr)
        need = k_f - count(lambda j, sc: sc > m)

        def step(_, carry):
            cut, bit = carry
            cand = cut + bit
            n_before = count(lambda j, sc: (sc == m) & ((j * K_CHUNK + s_idx) < cand))
            return jnp.where(n_before < need, cand, cut), lax.shift_right_logical(bit, jnp.int32(1))

        cut, _ = lax.fori_loop(0, idx_bits, step,
                               (jnp.zeros(per_query, jnp.int32), jnp.int32(2 ** (idx_bits - 1))))
        return m, cut

    def to_rows(v):
        if not lanes_are_queries:
            return jnp.broadcast_to(v, (qt, K_CHUNK))
        t = jnp.broadcast_to(v, (qt, qt)).T
        return jnp.concatenate([t] * (K_CHUNK // qt), axis=1)

    def write_masks(keep_fn):
        def mask_chunk(j, carry):
            s_pos = j * K_CHUNK + s_row
            mb_scr[j] = jnp.where(keep_fn(sc_scr[j], s_pos) & (s_pos <= t_col), 0.0, NEG_INF)
            return carry

        chunk_loop(0, mask_chunk, 0)

    def masks_with_ties():
        m, cut = tie_break()
        m_b, cut_b = to_rows(m), to_rows(cut)
        write_masks(lambda sc, s_pos: (sc > m_b) | ((sc == m_b) & (s_pos <= cut_b)))

    def masks_plain():
        thr_b = to_rows(thr)
        write_masks(lambda sc, s_pos: sc >= thr_b)

    lax.cond(has_tie, masks_with_ties, masks_plain)

    rows_h = N_HEADS * qt
    lane_fold = lambda v, op: functools.reduce(op, [v[:, k * LANES:(k + 1) * LANES] for k in range(K_CHUNK // LANES)])

    def key_rows(j):
        start = j * K_CHUNK
        return pl.ds(start if isinstance(j, int) else pl.multiple_of(start, K_CHUNK), K_CHUNK)

    def over_chunks(chunk_fn):
        chunk_fn(0, True)
        chunk_loop(1, lambda j, carry: (chunk_fn(j, False), carry)[1], 0)

    def logit_chunk(j, first):
        near = jnp.minimum((i * qt) // K_CHUNK - j, n_near - 1)
        s = _dot_nt(qa_scr[...], c_ref[key_rows(j), :]) * scale + nb_scr[near]
        s = s + jnp.concatenate([mb_scr[j]] * N_HEADS, axis=0)
        s_scr[j] = s
        fold = lane_fold(s, jnp.maximum)
        m_scr[...] = fold if first else jnp.maximum(m_scr[...], fold)

    over_chunks(logit_chunk)
    m_b = jnp.broadcast_to(jnp.max(m_scr[...], axis=-1, keepdims=True), (rows_h, LANES))
    m_scr[...] = m_b

    def value_chunk(j, first):
        p = jnp.exp(s_scr[j] - jnp.concatenate([m_scr[...]] * (K_CHUNK // LANES), axis=1))
        pv = _dot(p.astype(BF16), c_ref[key_rows(j), :])
        l_scr[...] = lane_fold(p, jnp.add) if first else l_scr[...] + lane_fold(p, jnp.add)
        acc_scr[...] = pv if first else acc_scr[...] + pv

    over_chunks(value_chunk)
    out = acc_scr[...] * (1.0 / jnp.sum(l_scr[...], axis=-1, keepdims=True))
    for h in range(N_HEADS):
        o_ref[:, h * KV_RANK:(h + 1) * KV_RANK] = out[h * qt:(h + 1) * qt].astype(o_ref.dtype)


def _dsa_attention(qq, wi, kk, c_kv, wuk, rel_bias, top_k, seq, sides):
    B, tp, _ = qq.shape
    n_main = (seq + N_META) // Q_TILE
    assert (seq + N_META) - n_main * Q_TILE <= TAIL_TILE and Q_TILE % K_CHUNK == 0 and K_CHUNK % TAIL_TILE == 0
    n_chunks_max = tp // K_CHUNK
    width = N_HEADS * KV_RANK

    def call(qt, first_tile, n_tiles, n_fill, prev, sides=()):
        rows_h = N_HEADS * qt
        diag = _near_bias_diagonals(rel_bias, qt)
        tile = lambda i: first_tile + jnp.minimum(i, n_tiles - 1)
        in_specs = [
            pl.BlockSpec((None, qt, qq.shape[2]), lambda b, i: (b, tile(i), 0)),
            pl.BlockSpec((None, qt, 128), lambda b, i: (b, tile(i), 0)),
            pl.BlockSpec((None, tp, LANES), lambda b, i: (b, 0, 0)),
            pl.BlockSpec((None, tp, KV_RANK), lambda b, i: (b, 0, 0)),
            pl.BlockSpec(wuk.shape, lambda b, i: (0, 0, 0)),
            pl.BlockSpec(diag.shape, lambda b, i: (0, 0, 0)),
        ]
        args = [qq, wi, kk, c_kv, wuk, diag]
        out_specs = [pl.BlockSpec((None, qt, width), lambda b, i: (b, first_tile + i, 0))]
        out_shape = [jax.ShapeDtypeStruct((B, tp, width), BF16)]
        aliases = {}
        if prev is not None:
            in_specs.append(pl.BlockSpec(memory_space=pl.ANY))
            args.append(prev)
            aliases = {len(args) - 1: 0}
        for side in sides:
            side_block = _slab_block(side, B * n_tiles, lambda b, i: b * n_tiles + jnp.minimum(i, n_tiles - 1))
            in_specs.append(side_block)
            args.append(side)
            out_specs.append(side_block)
            out_shape.append(jax.ShapeDtypeStruct(side.shape, BF16))
        return pl.pallas_call(
            functools.partial(_attn_kernel, n_tiles=n_tiles, aliased=prev is not None, n_sides=len(sides),
                              top_k=top_k, qt=qt, first_tile=first_tile, single_tile=n_tiles == 1),
            grid=(B, n_tiles + n_fill),
            in_specs=in_specs,
            out_specs=out_specs,
            out_shape=out_shape,
            input_output_aliases=aliases,
            scratch_shapes=[pltpu.VMEM((diag.shape[0] + 1, rows_h, K_CHUNK), F32),
                            pltpu.VMEM((n_chunks_max, qt, K_CHUNK), F32),
                            pltpu.VMEM((n_chunks_max, K_CHUNK, qt), F32),
                            pltpu.VMEM((n_chunks_max, K_CHUNK, qt), BF16),
                            pltpu.VMEM((n_chunks_max, qt, K_CHUNK), F32),
                            pltpu.VMEM((LANES, qt), F32),
                            pltpu.VMEM((rows_h, KV_RANK), BF16),
                            pltpu.VMEM((n_chunks_max, rows_h, K_CHUNK), F32),
                            pltpu.VMEM((rows_h, LANES), F32),
                            pltpu.VMEM((rows_h, LANES), F32),
                            pltpu.VMEM((rows_h, KV_RANK), F32)],
            compiler_params=_compiler_params(("arbitrary", "arbitrary")),
            name="dsa_attention" if prev is None else "dsa_attention_tail",
        )(*args)

    o_lat, *sides_b = call(Q_TILE, 0, n_main, tp // Q_TILE - n_main, None, sides)
    return (call(TAIL_TILE, n_main * Q_TILE // TAIL_TILE, 1, 0, o_lat)[0], *sides_b)


def _mix_window_start(r, seq):
    return min(max(r * MIX_ROWS - N_META, 0), seq - MIX_ROWS)


def _residual_rows(x_ref, meta_ref, h_scr, seq):
    r = pl.program_id(1)
    n_real = -(-(seq + N_META) // MIX_ROWS)
    for rv in range(n_real):

        @pl.when(r == rv)
        def _(rv=rv):
            skip = rv * MIX_ROWS - N_META - _mix_window_start(rv, seq)
            if rv == 0:
                h_scr[:N_META, :] = meta_ref[...]
                h_scr[N_META:, :] = x_ref[:MIX_ROWS - N_META, :]
            elif skip == 0:
                h_scr[...] = x_ref[...]
            else:
                h_scr[:MIX_ROWS - skip, :] = x_ref[skip:, :]
                h_scr[MIX_ROWS - skip:, :] = jnp.zeros((skip, h_scr.shape[1]), F32)

    @pl.when(r >= n_real)
    def _():
        h_scr[...] = jnp.zeros(h_scr.shape, F32)


def _mix_kernel(ol_ref, pd_ref, x_ref, meta_ref, wuv_ref, wp_ref, ps_ref, wo_ref, g_ref, b_ref, h1_ref, h1b_ref,
                h_scr, *, seq):
    _residual_rows(x_ref, meta_ref, h_scr, seq)
    attn = [_dot(ol_ref[:, h * KV_RANK:(h + 1) * KV_RANK], wuv_ref[h]) for h in range(N_HEADS)]
    pool = [_dot(pd_ref[:, g * POOL_GROUP:(g + 1) * POOL_GROUP], wp_ref[g]) for g in range(len(POOL_WINDOWS))]
    pool = jnp.concatenate(pool, axis=-1) * ps_ref[...]
    cat = jnp.concatenate(attn + [pool], axis=-1).astype(BF16)
    y = ALPHA * h_scr[...] + _dot(cat, wo_ref[...])
    h1 = _layer_norm(y, g_ref[...], b_ref[...])
    h1_ref[...] = h1
    h1b_ref[...] = h1.astype(h1b_ref.dtype)


def _mix_ln1(o_lat, pool_diff, x, meta, wuv, wpool, pool_scale, wo, g, b):
    B, seq, d = x.shape
    n_rows = seq + N_META
    assert n_rows % MIX_ROWS == 0
    row = lambda n: pl.BlockSpec((None, MIX_ROWS, n), lambda bi, r: (bi, r, 0))
    full = lambda a: pl.BlockSpec(a.shape, lambda bi, r: (0,) * a.ndim, pipeline_mode=pl.Buffered(1))
    window = pl.BlockSpec(
        (None, pl.Element(MIX_ROWS), pl.Element(d)),
        lambda bi, r: (bi, pl.multiple_of(jnp.clip(r * MIX_ROWS - N_META, 0, seq - MIX_ROWS), N_META), 0))
    return pl.pallas_call(
        functools.partial(_mix_kernel, seq=seq),
        grid=(B, n_rows // MIX_ROWS),
        in_specs=[row(o_lat.shape[2]), row(pool_diff.shape[2]), window, full(meta),
                  full(wuv), full(wpool), full(pool_scale), full(wo), full(g), full(b)],
        out_specs=[row(d), row(d)],
        out_shape=[jax.ShapeDtypeStruct((B, n_rows, d), F32), jax.ShapeDtypeStruct((B, n_rows, d), BF16)],
        scratch_shapes=[pltpu.VMEM((MIX_ROWS, d), F32)],
        compiler_params=_compiler_params(("parallel", "arbitrary")),
        name="mix_ln1",
    )(o_lat, pool_diff, x, meta, wuv, wpool, pool_scale, wo, g, b)


def _gelu_tanh(x):
    return 0.5 * x * (1.0 + jnp.tanh(math.sqrt(2.0 / math.pi) * (x + 0.044715 * (x * x * x))))


def _ffn_kernel(hw_ref, hres_ref, wa_ref, wg_ref, cwa_ref, cwg_ref, cba_ref, cbg_ref, wd_ref, g_ref, b_ref, o_ref,
                za_scr, zg_scr):
    c = pl.program_id(2)
    n_c = pl.num_programs(2) - 1

    def up(slot):
        x = hw_ref[...]
        za_scr[slot] = _dot(x, wa_ref[...])
        zg_scr[slot] = _dot(x, wg_ref[...])

    def conv(z, cw_ref, cb_ref):
        cw = cw_ref[...]
        n = z.shape[0]
        out = z[HALO - 2:n - 2] * cw[0:1] + z[HALO - 1:n - 1] * cw[1:2] + z[HALO:] * cw[2:3]
        return out + cb_ref[...]

    def down(slot):
        a = conv(za_scr[slot], cwa_ref, cba_ref)
        gate = conv(zg_scr[slot], cwg_ref, cbg_ref)
        act = (_gelu_tanh(a) * gate).astype(BF16)
        o_ref[...] += _dot(act, wd_ref[...])

    @pl.when(c == 0)
    def _():
        o_ref[...] = jnp.zeros(o_ref.shape, o_ref.dtype)
        up(0)

    @pl.when((c > 0) & (c < n_c))
    def _():
        up(c % 2)
        down((c - 1) % 2)

    @pl.when(c == n_c)
    def _():
        down((c - 1) % 2)
        o_ref[...] = _layer_norm(ALPHA * hres_ref[...] + o_ref[...], g_ref[...], b_ref[...])


def _ffn_ln2(h1, h1b, w_up, conv_w, conv_b, w_down, g, b, seq):
    B, tp, d = h1.shape
    d_ff = w_down.shape[0]
    n_c = d_ff // FFN_COLS
    up_c = lambda c: jnp.minimum(c, n_c - 1)
    dn_c = lambda c: jnp.maximum(c - 1, 0)
    vec = lambda off: pl.BlockSpec((1, FFN_COLS), lambda bi, r, c: (0, dn_c(c) + off))
    return pl.pallas_call(
        _ffn_kernel,
        grid=(B, seq // FFN_ROWS, n_c + 1),
        in_specs=[
            pl.BlockSpec((None, pl.Element(FFN_ROWS + HALO), pl.Element(d)),
                         lambda bi, r, c: (bi, r * FFN_ROWS + N_META - HALO, 0)),
            pl.BlockSpec((None, pl.Element(FFN_ROWS), pl.Element(d)),
                         lambda bi, r, c: (bi, pl.multiple_of(r * FFN_ROWS + N_META, N_META), 0)),
            pl.BlockSpec((d, FFN_COLS), lambda bi, r, c: (0, up_c(c))),
            pl.BlockSpec((d, FFN_COLS), lambda bi, r, c: (0, up_c(c) + n_c)),
            pl.BlockSpec((CONV_WIDTH, FFN_COLS), lambda bi, r, c: (0, dn_c(c))),
            pl.BlockSpec((CONV_WIDTH, FFN_COLS), lambda bi, r, c: (0, dn_c(c) + n_c)),
            vec(0), vec(n_c),
            pl.BlockSpec((FFN_COLS, d), lambda bi, r, c: (dn_c(c), 0)),
            pl.BlockSpec((1, d), lambda bi, r, c: (0, 0)),
            pl.BlockSpec((1, d), lambda bi, r, c: (0, 0)),
        ],
        out_specs=pl.BlockSpec((None, FFN_ROWS, d), lambda bi, r, c: (bi, r, 0), pipeline_mode=pl.Buffered(1)),
        out_shape=jax.ShapeDtypeStruct((B, seq, d), F32),
        scratch_shapes=[pltpu.VMEM((2, FFN_ROWS + HALO, FFN_COLS), F32),
                        pltpu.VMEM((2, FFN_ROWS + HALO, FFN_COLS), F32)],
        compiler_params=_compiler_params(("parallel", "parallel", "arbitrary")),
        name="ffn_ln2",
    )(h1b, h1, w_up, w_up, conv_w, conv_w, conv_b, conv_b, w_down, g, b)


def _t5_bucket_table(n):
    dist = np.arange(n, dtype=np.int32)
    max_exact = REL_BUCKETS // 2
    d_f = np.maximum(dist, 1).astype(np.float32)
    large = max_exact + (np.log(d_f / np.float32(max_exact)) / np.float32(math.log(REL_MAX_DIST / max_exact))
                         * np.float32(REL_BUCKETS - max_exact)).astype(np.int32)
    return np.where(dist < max_exact, dist, np.minimum(large, REL_BUCKETS - 1))


def _near_bias_diagonals(rel_bias, qt):
    assert qt <= K_CHUNK
    probe = _t5_bucket_table(4 * REL_MAX_DIST)
    first_far = int(np.argmax(probe == REL_BUCKETS - 1))
    assert np.all(probe[first_far:] == REL_BUCKETS - 1)
    n_real = -(-(first_far + K_CHUNK - 1) // K_CHUNK)
    buckets = _t5_bucket_table((n_real + 1) * K_CHUNK)
    period = 2 * K_CHUNK
    u = np.arange(period)
    k = np.arange(n_real)[:, None]
    dist = np.where(u < K_CHUNK, k * K_CHUNK - u, k * K_CHUNK + period - u)
    idx = buckets[np.clip(dist, 0, len(buckets) - 1)]
    rel = rel_bias.astype(F32) - rel_bias[REL_BUCKETS - 1:].astype(F32)
    return jnp.transpose(rel[idx], (0, 2, 1))


def kernel(x, meta, rel_bias, w_in, kv_norm_g, w_uk, w_uv, w_pool, pool_scale, w_o, ln1_g, ln1_b, w_up, conv_w,
           conv_b, w_down, ln2_g, ln2_b):
    B, S, D = x.shape
    assert w_in.shape[0] == DEPTH and S % FFN_ROWS == 0
    T = S + N_META
    tp = -(-T // ROW_ALIGN) * ROW_ALIGN
    assert tp - T >= max(POOL_WINDOWS)
    top_k = min(TOPK_MAX, S // 4)
    assert top_k <= K_CHUNK

    wt = jnp.transpose(w_in[0])
    sizes = (("q", N_HEADS * HEAD_DIM), ("c_kv", KV_RANK), ("q_idx", IDX_HEADS * IDX_DIM), ("k_idx", IDX_DIM),
             ("w_idx", IDX_HEADS), ("u", len(POOL_WINDOWS) * POOL_GROUP))
    cols, start = {}, 0
    for name, size in sizes:
        cols[name] = (start, size)
        start += size
    assert start == wt.shape[0]

    qq, = _proj_qq(x, meta, wt, cols, tp, None)
    pool_diff, w_o_b = _proj_pool(x, meta, wt, cols, tp, w_o[0])
    c_kv, kk, wi = _proj_small(x, meta, wt, cols, kv_norm_g[0].reshape(1, KV_RANK), tp, None)

    wuk = jnp.transpose(w_uk[0], (1, 2, 0)).astype(BF16)
    o_lat, w_up_b, w_down_b = _dsa_attention(qq, wi, kk, c_kv, wuk, rel_bias, top_k, S, (w_up[0], w_down[0]))

    wuv = jnp.transpose(w_uv[0], (1, 0, 2)).astype(BF16)
    h1, h1b = _mix_ln1(o_lat, pool_diff, x, meta, wuv, w_pool[0].astype(BF16), pool_scale[0].reshape(1, -1),
                       w_o_b, ln1_g[0].reshape(1, D), ln1_b[0].reshape(1, D))

    return _ffn_ln2(h1, h1b, w_up_b, conv_w[0], conv_b[0].reshape(1, -1), w_down_b,
                    ln2_g[0].reshape(1, D), ln2_b[0].reshape(1, D), S)
```

```python
import functools
import math

import numpy as np
import jax
import jax.numpy as jnp
from jax import lax
from jax.experimental import pallas as pl
from jax.experimental.pallas import tpu as pltpu

F32 = jnp.float32
BF16 = jnp.bfloat16

N_META = 16
N_HEADS = 8
HEAD_DIM = 128
KV_RANK = 256
IDX_HEADS = 16
IDX_DIM = 64
TOPK_MAX = 256
POOL_WINDOWS = (2, 4, 8, 16)
POOL_GROUP = 256
CONV_WIDTH = 3
REL_BUCKETS = 32
REL_MAX_DIST = 128
DEPTH = 1
ALPHA = (2.0 * DEPTH) ** 0.25
LN_EPS = 1e-5
NEG_INF = -1e30

VMEM_LIMIT_BYTES = 60 * 1024 * 1024
SUBLANES = 8
LANES = 128
ROW_ALIGN = 256
PROJ_ROWS = 1024
Q_TILE = 256
TAIL_TILE = 16
K_CHUNK = 256
MIX_ROWS = 688
FFN_ROWS = 1024
FFN_COLS = 512
HALO = 16
INT_MIN = -(2 ** 31)
PACKED_SUBLANES = 16
CHUNKS_PER_TRIP = 2
TIE_ROUNDS = 8


def _dot(a, b):
    return jnp.dot(a, b, preferred_element_type=F32)


def _dot_nt(a, b):
    return lax.dot_general(a, b, (((1,), (1,)), ((), ())), preferred_element_type=F32)


def _layer_norm(y, g, b):
    mu = jnp.mean(y, axis=-1, keepdims=True)
    yc = y - mu
    var = jnp.mean(yc * yc, axis=-1, keepdims=True)
    return yc * lax.rsqrt(var + LN_EPS) * g + b


def _proj_rows(x_ref, meta_ref, w, tp, store):
    seq = x_ref.shape[0]
    store(pl.ds(0, N_META), _dot_nt(meta_ref[...].astype(BF16), w))
    for r in range(seq // PROJ_ROWS):
        acc = _dot_nt(x_ref[r * PROJ_ROWS:(r + 1) * PROJ_ROWS, :].astype(BF16), w)
        store(pl.ds(N_META + r * PROJ_ROWS, PROJ_ROWS), acc)
    n_pad = tp - seq - N_META
    store(pl.ds(seq + N_META, n_pad), jnp.zeros((n_pad, w.shape[0]), F32))


def _proj_cast_kernel(x_ref, meta_ref, w_ref, o_ref, *, scale_from):
    def store(rows, acc):
        o_ref[rows, :] = acc.astype(o_ref.dtype)

    scale = jnp.where(pl.program_id(1) >= scale_from, IDX_DIM ** -0.5, 1.0)
    _proj_rows(x_ref, meta_ref, (w_ref[...] * scale).astype(BF16), o_ref.shape[0], store)


def _proj_pool_kernel(x_ref, meta_ref, w_ref, o_ref, u_scr):
    def store(rows, acc):
        u_scr[rows, :] = acc

    _proj_rows(x_ref, meta_ref, w_ref[...].astype(BF16), u_scr.shape[0], store)
    group = pl.program_id(1)
    tp = u_scr.shape[0]
    pos = lax.broadcasted_iota(jnp.int32, (tp, 1), 0)
    for g, window in enumerate(POOL_WINDOWS):

        @pl.when(group == g)
        def _():
            u = u_scr[...]
            s = u
            shift = 1
            while shift < window:
                s = s + pltpu.roll(s, shift, axis=0)
                shift *= 2
            count = jnp.minimum(pos + 1, window).astype(F32)
            o_ref[...] = (s / count - u).astype(o_ref.dtype)


def _proj_small_kernel(x_ref, meta_ref, wc_ref, wk_ref, ww_ref, g_ref, c_ref, kk_ref, wi_ref):
    w = jnp.concatenate([wc_ref[...], wk_ref[...], wk_ref[...], ww_ref[...] * (IDX_HEADS ** -0.5),
                         jnp.zeros((LANES - IDX_HEADS, wc_ref.shape[1]), F32)], axis=0).astype(BF16)

    def store(rows, acc):
        c = acc[:, :KV_RANK]
        ms = jnp.mean(c * c, axis=-1, keepdims=True)
        c_ref[rows, :] = (c * lax.rsqrt(ms + LN_EPS) * g_ref[...]).astype(c_ref.dtype)
        kk_ref[rows, :] = acc[:, KV_RANK:KV_RANK + LANES].astype(kk_ref.dtype)
        wi_ref[rows, :] = acc[:, KV_RANK + LANES:]

    _proj_rows(x_ref, meta_ref, w, c_ref.shape[0], store)


def _batch_block(tp, n):
    return pl.BlockSpec((None, tp, n), lambda b, j: (b, 0, 0))


def _compiler_params(semantics):
    return pltpu.CompilerParams(dimension_semantics=semantics, vmem_limit_bytes=VMEM_LIMIT_BYTES)


def _with_side_cast(body, n_in, n_out):
    def kernel(*refs):
        side_in, side_out = refs[n_in], refs[n_in + 1 + n_out]
        side_out[...] = side_in[...].astype(side_out.dtype)
        body(*refs[:n_in], *refs[n_in + 1:n_in + 1 + n_out], *refs[n_in + 2 + n_out:])

    return kernel


def _slab_block(side, n_slabs, slab_of_step):
    slab = side.shape[0] // n_slabs
    assert slab * n_slabs == side.shape[0] and slab % PACKED_SUBLANES == 0
    return pl.BlockSpec((slab, side.shape[1]), lambda *step: (slab_of_step(*step), 0))


def _w_rows(d, n_rows, first_row):
    return pl.BlockSpec((pl.Element(n_rows), pl.Element(d)), lambda b, j: (first_row(j), 0))


def _proj_call(body, x, meta, wt, w_specs, n_j, extra_in, extra_specs, out_blocks, out_shapes, scratch, name, side):
    B, seq, d = x.shape
    out_blocks = list(out_blocks) if isinstance(out_blocks, (list, tuple)) else [out_blocks]
    out_shapes = list(out_shapes) if isinstance(out_shapes, (list, tuple)) else [out_shapes]
    in_specs = [_batch_block(seq, d), pl.BlockSpec(meta.shape, lambda b, j: (0, 0))] + w_specs + extra_specs
    args = [x, meta] + [wt] * len(w_specs) + list(extra_in)
    if side is not None:
        side_block = _slab_block(side, B * n_j, lambda b, j: b * n_j + j)
        body = _with_side_cast(body, len(args), len(out_blocks))
        in_specs, args = in_specs + [side_block], args + [side]
        out_blocks, out_shapes = out_blocks + [side_block], out_shapes + [jax.ShapeDtypeStruct(side.shape, BF16)]
    return pl.pallas_call(
        body,
        grid=(B, n_j),
        in_specs=in_specs,
        out_specs=out_blocks,
        out_shape=out_shapes,
        scratch_shapes=scratch,
        compiler_params=_compiler_params(("parallel", "arbitrary")),
        name=name,
    )(*args)


def _proj_qq(x, meta, wt, cols, tp, side):
    B, d, tn = x.shape[0], x.shape[2], 512
    n_q, n = cols["q"][1] // tn, cols["q"][1] + cols["q_idx"][1]
    first_row = lambda j: pl.multiple_of(
        jnp.where(j < n_q, cols["q"][0] + j * tn, cols["q_idx"][0] + (j - n_q) * tn), PACKED_SUBLANES)
    return _proj_call(functools.partial(_proj_cast_kernel, scale_from=n_q), x, meta, wt,
                      [_w_rows(d, tn, first_row)], n // tn, [], [],
                      pl.BlockSpec((None, tp, tn), lambda b, j: (b, 0, j)),
                      jax.ShapeDtypeStruct((B, tp, n), BF16), [], "proj_qq", side)


def _proj_pool(x, meta, wt, cols, tp, side):
    B, d, n = x.shape[0], x.shape[2], cols["u"][1]
    first_row = lambda j: pl.multiple_of(cols["u"][0] + j * POOL_GROUP, PACKED_SUBLANES)
    return _proj_call(_proj_pool_kernel, x, meta, wt, [_w_rows(d, POOL_GROUP, first_row)], n // POOL_GROUP, [], [],
                      pl.BlockSpec((None, tp, POOL_GROUP), lambda b, j: (b, 0, j)),
                      jax.ShapeDtypeStruct((B, tp, n), BF16), [pltpu.VMEM((tp, POOL_GROUP), F32)], "proj_pool", side)


def _proj_small(x, meta, wt, cols, kv_g, tp, side):
    B, d = x.shape[0], x.shape[2]
    w_specs = [_w_rows(d, cols[name][1], lambda j, name=name: cols[name][0]) for name in ("c_kv", "k_idx", "w_idx")]
    return _proj_call(_proj_small_kernel, x, meta, wt, w_specs, 1, [kv_g],
                      [pl.BlockSpec((1, KV_RANK), lambda b, j: (0, 0))],
                      [_batch_block(tp, KV_RANK), _batch_block(tp, LANES), _batch_block(tp, 128)],
                      [jax.ShapeDtypeStruct((B, tp, KV_RANK), BF16), jax.ShapeDtypeStruct((B, tp, LANES), BF16),
                       jax.ShapeDtypeStruct((B, tp, 128), F32)], [], "proj_small", side)


def _attn_kernel(*refs, n_tiles, aliased, n_sides, **static):
    n_in = 6
    sides_in = refs[n_in + int(aliased):n_in + int(aliased) + n_sides]
    refs = refs[:n_in] + refs[n_in + int(aliased) + n_sides:]
    o_ref = refs[n_in]
    sides_out = refs[n_in + 1:n_in + 1 + n_sides]
    refs = refs[:n_in + 1] + refs[n_in + 1 + n_sides:]

    @pl.when(pl.program_id(1) < n_tiles)
    def _():
        for side_in, side_out in zip(sides_in, sides_out):
            side_out[...] = side_in[...].astype(side_out.dtype)
        _attn_tile(*refs, **static)

    @pl.when(pl.program_id(1) >= n_tiles)
    def _():
        o_ref[...] = jnp.zeros(o_ref.shape, o_ref.dtype)


def _attn_tile(qq_ref, wi_ref, kk_ref, c_ref, wuk_ref, diag_ref, o_ref, nb_scr, sc_scr, sct_scr, scf_scr, mb_scr,
               wt_scr, qa_scr, s_scr, m_scr, l_scr, acc_scr, *, top_k, qt, first_tile, single_tile):
    i = first_tile if single_tile else first_tile + pl.program_id(1)
    n_chunks = ((i + 1) * qt - 1) // K_CHUNK + 1

    def chunk_loop(lo, body, init):
        if single_tile:
            return lax.fori_loop(lo, n_chunks, body, init, unroll=True)

        def block(k, first, c):
            for t in range(k):
                c = body(first + t, c)
            return c

        n, k = n_chunks - lo, CHUNKS_PER_TRIP
        carry = lax.fori_loop(0, n // k, lambda p, c: block(k, lo + k * p, c), init)
        done = lo + n // k * k
        while k > 1:
            k //= 2
            take = ((n_chunks - done) // k) % 2 == 1
            carry = lax.cond(take, lambda c, k=k, done=done: block(k, done, c), lambda c: c, carry)
            done = done + jnp.where(take, k, 0)
        return carry
    attn_w = N_HEADS * HEAD_DIM
    scale = HEAD_DIM ** -0.5
    n_pairs = IDX_HEADS // 2
    lanes_are_queries = qt % LANES == 0

    t_col = i * qt + lax.broadcasted_iota(jnp.int32, (qt, 1), 0)
    t_row = i * qt + lax.broadcasted_iota(jnp.int32, (1, qt), 1)
    s_row = lax.broadcasted_iota(jnp.int32, (1, K_CHUNK), 1)
    s_col = lax.broadcasted_iota(jnp.int32, (K_CHUNK, 1), 0)
    lane_half = lax.broadcasted_iota(jnp.int32, (K_CHUNK, LANES), 1) // IDX_DIM

    n_near = nb_scr.shape[0]

    @pl.when((pl.program_id(0) == 0) & (pl.program_id(1) == 0))
    def _():
        for k in range(n_near - 1):
            for h in range(N_HEADS):
                v = jnp.broadcast_to(diag_ref[k, h:h + 1, :], (qt, 2 * K_CHUNK))
                t = pltpu.roll(v, 0, 1, stride=1, stride_axis=0)
                nb_scr[k, h * qt:(h + 1) * qt, :] = t[:, :K_CHUNK]
        nb_scr[n_near - 1] = jnp.zeros(nb_scr.shape[1:], F32)

    for h in range(N_HEADS):
        qa_scr[h * qt:(h + 1) * qt, :] = _dot(
            qq_ref[:, h * HEAD_DIM:(h + 1) * HEAD_DIM], wuk_ref[h]).astype(BF16)
    if lanes_are_queries:
        wt_scr[...] = wi_ref[...].T

    def bf16_floor(v):
        near = v.astype(BF16)
        bits = lax.bitcast_convert_type(near, jnp.int16)
        below = lax.bitcast_convert_type(bits + jnp.where(bits < 0, jnp.int16(1), jnp.int16(-1)), BF16)
        return jnp.where(near.astype(F32) > v, below, near)

    def idx_chunk(j, carry):
        ks = kk_ref[pl.ds(pl.multiple_of(j * K_CHUNK, K_CHUNK), K_CHUNK), :]
        zero = jnp.zeros(ks.shape, ks.dtype)
        k_even = jnp.where(lane_half == 0, ks, zero)
        k_odd = jnp.where(lane_half == 1, ks, zero)
        score = jnp.zeros((K_CHUNK, qt) if lanes_are_queries else (qt, K_CHUNK), F32)
        if lanes_are_queries:
            for p in range(n_pairs):
                q_pair = qq_ref[:, attn_w + p * LANES:attn_w + (p + 1) * LANES]
                for hh, k_half in ((2 * p, k_even), (2 * p + 1, k_odd)):
                    score = score + jnp.maximum(_dot_nt(k_half, q_pair), 0.0) * wt_scr[hh:hh + 1, :]
        else:
            q_pairs = jnp.concatenate(
                [qq_ref[:, attn_w + p * LANES:attn_w + (p + 1) * LANES] for p in range(n_pairs)], axis=0)
            for half, k_half in enumerate((k_even, k_odd)):
                dots = jnp.maximum(_dot_nt(q_pairs, k_half), 0.0)
                for p in range(n_pairs):
                    hh = 2 * p + half
                    score = score + dots[p * qt:(p + 1) * qt] * wi_ref[:, hh:hh + 1]
        s_pos = j * K_CHUNK + (s_col if lanes_are_queries else s_row)
        score = jnp.where(s_pos <= (t_row if lanes_are_queries else t_col), score, NEG_INF)
        if lanes_are_queries:
            sct_scr[j] = score
            sc_scr[j] = score.T
            scf_scr[j] = bf16_floor(score)
        else:
            sc_scr[j] = score
        return carry

    chunk_loop(0, idx_chunk, 0)

    k_f = float(top_k)
    idx_bits = int(math.ceil(math.log2(sc_scr.shape[0] * K_CHUNK)))
    per_query = (1, qt) if lanes_are_queries else (qt, 1)
    key_axis = 0 if lanes_are_queries else 1
    s_idx = s_col if lanes_are_queries else s_row
    search_scr = sct_scr if lanes_are_queries else sc_scr

    def fold_chunks(chunk_fn, combine, init):
        if lanes_are_queries:
            acc_rows = 4 * SUBLANES

            def body(j, acc):
                v = chunk_fn(j, search_scr[j]).reshape(K_CHUNK // acc_rows, acc_rows, qt)
                return combine(acc, functools.reduce(combine, [v[t] for t in range(K_CHUNK // acc_rows)]))

            acc = chunk_loop(0, body, jnp.full((acc_rows, qt), init, F32))
        else:
            acc = chunk_loop(0, lambda j, acc: combine(acc, chunk_fn(j, search_scr[j])),
                                jnp.full((qt, K_CHUNK), init, F32))
        reduce = jnp.sum if combine is jnp.add else jnp.min
        return reduce(acc, axis=key_axis, keepdims=True)

    def count(pred):
        return fold_chunks(lambda j, sc: jnp.where(pred(j, sc), 1.0, 0.0), jnp.add, 0.0)

    def key_to_float(key):
        return lax.bitcast_convert_type(jnp.where(key < 0, key ^ jnp.int32(0x7FFFFFFF), key), F32)

    def count_coarse(cand_f):
        acc_rows = 4 * PACKED_SUBLANES
        cand_b = jnp.broadcast_to(cand_f, (acc_rows, qt)).astype(BF16)

        def body(j, acc):
            hit = jnp.where(scf_scr[j].reshape(K_CHUNK // acc_rows, acc_rows, qt) >= cand_b[None],
                            jnp.ones((), BF16), jnp.zeros((), BF16))
            return acc + functools.reduce(jnp.add, [hit[t] for t in range(K_CHUNK // acc_rows)])

        acc = chunk_loop(0, body, jnp.zeros((acc_rows, qt), BF16))
        return jnp.sum(acc.astype(F32), axis=0, keepdims=True)

    def bit_step(base, carry):
        key, bit = carry
        cand = key + bit
        cand_f = key_to_float(cand)
        n = count(lambda j, sc: sc >= cand_f) if base is None else count_coarse(cand_f - base)
        return jnp.where(n >= k_f, cand, key), lax.shift_right_logical(bit, jnp.int32(1))

    state = (jnp.full(per_query, INT_MIN, jnp.int32), jnp.int32(INT_MIN))
    if lanes_are_queries:
        state = lax.fori_loop(0, 16, lambda _, c: bit_step(0.0, c), state)
        for n_bits in (8, 8):
            base = key_to_float(state[0])

            def rebase(j, carry, base=base):
                scf_scr[j] = bf16_floor(sct_scr[j] - base)
                return carry

            chunk_loop(0, rebase, 0)
            state = lax.fori_loop(0, n_bits, lambda _, c, base=base: bit_step(base, c), state)
    else:
        state = lax.fori_loop(0, 32, lambda _, c: bit_step(None, c), state)
    thr = key_to_float(state[0])

    n_ge = count(lambda j, sc: sc >= thr)
    has_tie = jnp.max(jnp.where((n_ge > k_f) & (thr > NEG_INF), 1.0, 0.0)) > 0.0

    def tie_break():
        def next_value(_, m):
            n_gt = count(lambda j, sc: sc > m)
            above = fold_chunks(lambda j, sc: jnp.where(sc > m, sc, jnp.inf), jnp.minimum, jnp.inf)
            return jnp.where(n_gt >= k_f, above, m)

        m = lax.fori_loop(0, TIE_ROUNDS, next_value, thr)
        need = k_f - count(lambda j, sc: sc > m)

        def step(_, carry):
            cut, bit = carry
            cand = cut + bit
            n_before = count(lambda j, sc: (sc == m) & ((j * K_CHUNK + s_idx) < cand))
            return jnp.where(n_before < need, cand, cut), lax.shift_right_logical(bit, jnp.int32(1))

        cut, _ = lax.fori_loop(0, idx_bits, step,
                               (jnp.zeros(per_query, jnp.int32), jnp.int32(2 ** (idx_bits - 1))))
        return m, cut

    def to_rows(v):
        if not lanes_are_queries:
            return jnp.broadcast_to(v, (qt, K_CHUNK))
        t = jnp.broadcast_to(v, (qt, qt)).T
        return jnp.concatenate([t] * (K_CHUNK // qt), axis=1)

    def write_masks(keep_fn):
        def mask_chunk(j, carry):
            s_pos = j * K_CHUNK + s_row
            mb_scr[j] = jnp.where(keep_fn(sc_scr[j], s_pos) & (s_pos <= t_col), 0.0, NEG_INF)
            return carry

        chunk_loop(0, mask_chunk, 0)

    def masks_with_ties():
        m, cut = tie_break()
        m_b, cut_b = to_rows(m), to_rows(cut)
        write_masks(lambda sc, s_pos: (sc > m_b) | ((sc == m_b) & (s_pos <= cut_b)))

    def masks_plain():
        thr_b = to_rows(thr)
        write_masks(lambda sc, s_pos: sc >= thr_b)

    lax.cond(has_tie, masks_with_ties, masks_plain)

    rows_h = N_HEADS * qt
    lane_fold = lambda v, op: functools.reduce(op, [v[:, k * LANES:(k + 1) * LANES] for k in range(K_CHUNK // LANES)])

    def key_rows(j):
        start = j * K_CHUNK
        return pl.ds(start if isinstance(j, int) else pl.multiple_of(start, K_CHUNK), K_CHUNK)

    def over_chunks(chunk_fn):
        chunk_fn(0, True)
        chunk_loop(1, lambda j, carry: (chunk_fn(j, False), carry)[1], 0)

    def logit_chunk(j, first):
        near = jnp.minimum((i * qt) // K_CHUNK - j, n_near - 1)
        s = _dot_nt(qa_scr[...], c_ref[key_rows(j), :]) * scale + nb_scr[near]
        s = s + jnp.concatenate([mb_scr[j]] * N_HEADS, axis=0)
        s_scr[j] = s
        fold = lane_fold(s, jnp.maximum)
        m_scr[...] = fold if first else jnp.maximum(m_scr[...], fold)

    over_chunks(logit_chunk)
    m_b = jnp.broadcast_to(jnp.max(m_scr[...], axis=-1, keepdims=True), (rows_h, LANES))
    m_scr[...] = m_b

    def value_chunk(j, first):
        p = jnp.exp(s_scr[j] - jnp.concatenate([m_scr[...]] * (K_CHUNK // LANES), axis=1))
        pv = _dot(p.astype(BF16), c_ref[key_rows(j), :])
        l_scr[...] = lane_fold(p, jnp.add) if first else l_scr[...] + lane_fold(p, jnp.add)
        acc_scr[...] = pv if first else acc_scr[...] + pv

    over_chunks(value_chunk)
    out = acc_scr[...] * (1.0 / jnp.sum(l_scr[...], axis=-1, keepdims=True))
    for h in range(N_HEADS):
        o_ref[:, h * KV_RANK:(h + 1) * KV_RANK] = out[h * qt:(h + 1) * qt].astype(o_ref.dtype)


def _dsa_attention(qq, wi, kk, c_kv, wuk, rel_bias, top_k, seq, sides):
    B, tp, _ = qq.shape
    n_main = (seq + N_META) // Q_TILE
    assert (seq + N_META) - n_main * Q_TILE <= TAIL_TILE and Q_TILE % K_CHUNK == 0 and K_CHUNK % TAIL_TILE == 0
    n_chunks_max = tp // K_CHUNK
    width = N_HEADS * KV_RANK

    def call(qt, first_tile, n_tiles, n_fill, prev, sides=()):
        rows_h = N_HEADS * qt
        diag = _near_bias_diagonals(rel_bias, qt)
        tile = lambda i: first_tile + jnp.minimum(i, n_tiles - 1)
        in_specs = [
            pl.BlockSpec((None, qt, qq.shape[2]), lambda b, i: (b, tile(i), 0)),
            pl.BlockSpec((None, qt, 128), lambda b, i: (b, tile(i), 0)),
            pl.BlockSpec((None, tp, LANES), lambda b, i: (b, 0, 0)),
            pl.BlockSpec((None, tp, KV_RANK), lambda b, i: (b, 0, 0)),
            pl.BlockSpec(wuk.shape, lambda b, i: (0, 0, 0)),
            pl.BlockSpec(diag.shape, lambda b, i: (0, 0, 0)),
        ]
        args = [qq, wi, kk, c_kv, wuk, diag]
        out_specs = [pl.BlockSpec((None, qt, width), lambda b, i: (b, first_tile + i, 0))]
        out_shape = [jax.ShapeDtypeStruct((B, tp, width), BF16)]
        aliases = {}
        if prev is not None:
            in_specs.append(pl.BlockSpec(memory_space=pl.ANY))
            args.append(prev)
            aliases = {len(args) - 1: 0}
        for side in sides:
            side_block = _slab_block(side, B * n_tiles, lambda b, i: b * n_tiles + jnp.minimum(i, n_tiles - 1))
            in_specs.append(side_block)
            args.append(side)
            out_specs.append(side_block)
            out_shape.append(jax.ShapeDtypeStruct(side.shape, BF16))
        return pl.pallas_call(
            functools.partial(_attn_kernel, n_tiles=n_tiles, aliased=prev is not None, n_sides=len(sides),
                              top_k=top_k, qt=qt, first_tile=first_tile, single_tile=n_tiles == 1),
            grid=(B, n_tiles + n_fill),
            in_specs=in_specs,
            out_specs=out_specs,
            out_shape=out_shape,
            input_output_aliases=aliases,
            scratch_shapes=[pltpu.VMEM((diag.shape[0] + 1, rows_h, K_CHUNK), F32),
                            pltpu.VMEM((n_chunks_max, qt, K_CHUNK), F32),
                            pltpu.VMEM((n_chunks_max, K_CHUNK, qt), F32),
                            pltpu.VMEM((n_chunks_max, K_CHUNK, qt), BF16),
                            pltpu.VMEM((n_chunks_max, qt, K_CHUNK), F32),
                            pltpu.VMEM((LANES, qt), F32),
                            pltpu.VMEM((rows_h, KV_RANK), BF16),
                            pltpu.VMEM((n_chunks_max, rows_h, K_CHUNK), F32),
                            pltpu.VMEM((rows_h, LANES), F32),
                            pltpu.VMEM((rows_h, LANES), F32),
                            pltpu.VMEM((rows_h, KV_RANK), F32)],
            compiler_params=_compiler_params(("arbitrary", "arbitrary")),
            name="dsa_attention" if prev is None else "dsa_attention_tail",
        )(*args)

    o_lat, *sides_b = call(Q_TILE, 0, n_main, tp // Q_TILE - n_main, None, sides)
    return (call(TAIL_TILE, n_main * Q_TILE // TAIL_TILE, 1, 0, o_lat)[0], *sides_b)


def _mix_window_start(r, seq):
    return min(max(r * MIX_ROWS - N_META, 0), seq - MIX_ROWS)


def _residual_rows(x_ref, meta_ref, h_scr, seq):
    r = pl.program_id(1)
    n_real = -(-(seq + N_META) // MIX_ROWS)
    for rv in range(n_real):

        @pl.when(r == rv)
        def _(rv=rv):
            skip = rv * MIX_ROWS - N_META - _mix_window_start(rv, seq)
            if rv == 0:
                h_scr[:N_META, :] = meta_ref[...]
                h_scr[N_META:, :] = x_ref[:MIX_ROWS - N_META, :]
            elif skip == 0:
                h_scr[...] = x_ref[...]
            else:
                h_scr[:MIX_ROWS - skip, :] = x_ref[skip:, :]
                h_scr[MIX_ROWS - skip:, :] = jnp.zeros((skip, h_scr.shape[1]), F32)

    @pl.when(r >= n_real)
    def _():
        h_scr[...] = jnp.zeros(h_scr.shape, F32)


def _mix_kernel(ol_ref, pd_ref, x_ref, meta_ref, wuv_ref, wp_ref, ps_ref, wo_ref, g_ref, b_ref, h1_ref, h1b_ref,
                h_scr, *, seq):
    _residual_rows(x_ref, meta_ref, h_scr, seq)
    attn = [_dot(ol_ref[:, h * KV_RANK:(h + 1) * KV_RANK], wuv_ref[h]) for h in range(N_HEADS)]
    pool = [_dot(pd_ref[:, g * POOL_GROUP:(g + 1) * POOL_GROUP], wp_ref[g]) for g in range(len(POOL_WINDOWS))]
    pool = jnp.concatenate(pool, axis=-1) * ps_ref[...]
    cat = jnp.concatenate(attn + [pool], axis=-1).astype(BF16)
    y = ALPHA * h_scr[...] + _dot(cat, wo_ref[...])
    h1 = _layer_norm(y, g_ref[...], b_ref[...])
    h1_ref[...] = h1
    h1b_ref[...] = h1.astype(h1b_ref.dtype)


def _mix_ln1(o_lat, pool_diff, x, meta, wuv, wpool, pool_scale, wo, g, b):
    B, seq, d = x.shape
    n_rows = seq + N_META
    assert n_rows % MIX_ROWS == 0
    row = lambda n: pl.BlockSpec((None, MIX_ROWS, n), lambda bi, r: (bi, r, 0))
    full = lambda a: pl.BlockSpec(a.shape, lambda bi, r: (0,) * a.ndim, pipeline_mode=pl.Buffered(1))
    window = pl.BlockSpec(
        (None, pl.Element(MIX_ROWS), pl.Element(d)),
        lambda bi, r: (bi, pl.multiple_of(jnp.clip(r * MIX_ROWS - N_META, 0, seq - MIX_ROWS), N_META), 0))
    return pl.pallas_call(
        functools.partial(_mix_kernel, seq=seq),
        grid=(B, n_rows // MIX_ROWS),
        in_specs=[row(o_lat.shape[2]), row(pool_diff.shape[2]), window, full(meta),
                  full(wuv), full(wpool), full(pool_scale), full(wo), full(g), full(b)],
        out_specs=[row(d), row(d)],
        out_shape=[jax.ShapeDtypeStruct((B, n_rows, d), F32), jax.ShapeDtypeStruct((B, n_rows, d), BF16)],
        scratch_shapes=[pltpu.VMEM((MIX_ROWS, d), F32)],
        compiler_params=_compiler_params(("parallel", "arbitrary")),
        name="mix_ln1",
    )(o_lat, pool_diff, x, meta, wuv, wpool, pool_scale, wo, g, b)


def _gelu_tanh(x):
    return 0.5 * x * (1.0 + jnp.tanh(math.sqrt(2.0 / math.pi) * (x + 0.044715 * (x * x * x))))


def _ffn_kernel(hw_ref, h1_hbm, wa_ref, wg_ref, cwa_ref, cwg_ref, cba_ref, cbg_ref, wd_ref, g_ref, b_ref, o_ref,
                za_scr, zg_scr, hres_scr, hres_sem):
    c = pl.program_id(2)
    n_c = pl.num_programs(2) - 1

    def residual_copy():
        first_row = pl.multiple_of(pl.program_id(1) * FFN_ROWS + N_META, N_META)
        return pltpu.make_async_copy(h1_hbm.at[pl.program_id(0), pl.ds(first_row, FFN_ROWS), :], hres_scr, hres_sem)

    def up(slot):
        x = hw_ref[...]
        za_scr[slot] = _dot(x, wa_ref[...])
        zg_scr[slot] = _dot(x, wg_ref[...])

    def conv(z, cw_ref, cb_ref):
        cw = cw_ref[...]
        n = z.shape[0]
        out = z[HALO - 2:n - 2] * cw[0:1] + z[HALO - 1:n - 1] * cw[1:2] + z[HALO:] * cw[2:3]
        return out + cb_ref[...]

    def down(slot):
        a = conv(za_scr[slot], cwa_ref, cba_ref)
        gate = conv(zg_scr[slot], cwg_ref, cbg_ref)
        act = (_gelu_tanh(a) * gate).astype(BF16)
        o_ref[...] += _dot(act, wd_ref[...])

    @pl.when(c == 0)
    def _():
        residual_copy().start()
        o_ref[...] = jnp.zeros(o_ref.shape, o_ref.dtype)
        up(0)

    @pl.when((c > 0) & (c < n_c))
    def _():
        up(c % 2)
        down((c - 1) % 2)

    @pl.when(c == n_c)
    def _():
        down((c - 1) % 2)
        residual_copy().wait()
        o_ref[...] = _layer_norm(ALPHA * hres_scr[...] + o_ref[...], g_ref[...], b_ref[...])


def _ffn_ln2(h1, h1b, w_up, conv_w, conv_b, w_down, g, b, seq):
    B, tp, d = h1.shape
    d_ff = w_down.shape[0]
    n_c = d_ff // FFN_COLS
    up_c = lambda c: jnp.minimum(c, n_c - 1)
    dn_c = lambda c: jnp.maximum(c - 1, 0)
    vec = lambda off: pl.BlockSpec((1, FFN_COLS), lambda bi, r, c: (0, dn_c(c) + off))
    return pl.pallas_call(
        _ffn_kernel,
        grid=(B, seq // FFN_ROWS, n_c + 1),
        in_specs=[
            pl.BlockSpec((None, pl.Element(FFN_ROWS + HALO), pl.Element(d)),
                         lambda bi, r, c: (bi, r * FFN_ROWS + N_META - HALO, 0), pipeline_mode=pl.Buffered(1)),
            pl.BlockSpec(memory_space=pl.ANY),
            pl.BlockSpec((d, FFN_COLS), lambda bi, r, c: (0, up_c(c))),
            pl.BlockSpec((d, FFN_COLS), lambda bi, r, c: (0, up_c(c) + n_c)),
            pl.BlockSpec((CONV_WIDTH, FFN_COLS), lambda bi, r, c: (0, dn_c(c))),
            pl.BlockSpec((CONV_WIDTH, FFN_COLS), lambda bi, r, c: (0, dn_c(c) + n_c)),
            vec(0), vec(n_c),
            pl.BlockSpec((FFN_COLS, d), lambda bi, r, c: (dn_c(c), 0)),
            pl.BlockSpec((1, d), lambda bi, r, c: (0, 0)),
            pl.BlockSpec((1, d), lambda bi, r, c: (0, 0)),
        ],
        out_specs=pl.BlockSpec((None, FFN_ROWS, d), lambda bi, r, c: (bi, r, 0)),
        out_shape=jax.ShapeDtypeStruct((B, seq, d), F32),
        scratch_shapes=[pltpu.VMEM((2, FFN_ROWS + HALO, FFN_COLS), F32),
                        pltpu.VMEM((2, FFN_ROWS + HALO, FFN_COLS), F32),
                        pltpu.VMEM((FFN_ROWS, d), F32),
                        pltpu.SemaphoreType.DMA(())],
        compiler_params=_compiler_params(("arbitrary", "arbitrary", "arbitrary")),
        name="ffn_ln2",
    )(h1b, h1, w_up, w_up, conv_w, conv_w, conv_b, conv_b, w_down, g, b)


def _t5_bucket_table(n):
    dist = np.arange(n, dtype=np.int32)
    max_exact = REL_BUCKETS // 2
    d_f = np.maximum(dist, 1).astype(np.float32)
    large = max_exact + (np.log(d_f / np.float32(max_exact)) / np.float32(math.log(REL_MAX_DIST / max_exact))
                         * np.float32(REL_BUCKETS - max_exact)).astype(np.int32)
    return np.where(dist < max_exact, dist, np.minimum(large, REL_BUCKETS - 1))


def _near_bias_diagonals(rel_bias, qt):
    assert qt <= K_CHUNK
    probe = _t5_bucket_table(4 * REL_MAX_DIST)
    first_far = int(np.argmax(probe == REL_BUCKETS - 1))
    assert np.all(probe[first_far:] == REL_BUCKETS - 1)
    n_real = -(-(first_far + K_CHUNK - 1) // K_CHUNK)
    buckets = _t5_bucket_table((n_real + 1) * K_CHUNK)
    period = 2 * K_CHUNK
    u = np.arange(period)
    k = np.arange(n_real)[:, None]
    dist = np.where(u < K_CHUNK, k * K_CHUNK - u, k * K_CHUNK + period - u)
    idx = buckets[np.clip(dist, 0, len(buckets) - 1)]
    rel = rel_bias.astype(F32) - rel_bias[REL_BUCKETS - 1:].astype(F32)
    return jnp.transpose(rel[idx], (0, 2, 1))


def kernel(x, meta, rel_bias, w_in, kv_norm_g, w_uk, w_uv, w_pool, pool_scale, w_o, ln1_g, ln1_b, w_up, conv_w,
           conv_b, w_down, ln2_g, ln2_b):
    B, S, D = x.shape
    assert w_in.shape[0] == DEPTH and S % FFN_ROWS == 0
    T = S + N_META
    tp = -(-T // ROW_ALIGN) * ROW_ALIGN
    assert tp - T >= max(POOL_WINDOWS)
    top_k = min(TOPK_MAX, S // 4)
    assert top_k <= K_CHUNK

    wt = jnp.transpose(w_in[0])
    sizes = (("q", N_HEADS * HEAD_DIM), ("c_kv", KV_RANK), ("q_idx", IDX_HEADS * IDX_DIM), ("k_idx", IDX_DIM),
             ("w_idx", IDX_HEADS), ("u", len(POOL_WINDOWS) * POOL_GROUP))
    cols, start = {}, 0
    for name, size in sizes:
        cols[name] = (start, size)
        start += size
    assert start == wt.shape[0]

    qq, = _proj_qq(x, meta, wt, cols, tp, None)
    pool_diff, w_o_b = _proj_pool(x, meta, wt, cols, tp, w_o[0])
    c_kv, kk, wi = _proj_small(x, meta, wt, cols, kv_norm_g[0].reshape(1, KV_RANK), tp, None)

    wuk = jnp.transpose(w_uk[0], (1, 2, 0)).astype(BF16)
    o_lat, w_up_b, w_down_b = _dsa_attention(qq, wi, kk, c_kv, wuk, rel_bias, top_k, S, (w_up[0], w_down[0]))

    wuv = jnp.transpose(w_uv[0], (1, 0, 2)).astype(BF16)
    h1, h1b = _mix_ln1(o_lat, pool_diff, x, meta, wuv, w_pool[0].astype(BF16), pool_scale[0].reshape(1, -1),
                       w_o_b, ln1_g[0].reshape(1, D), ln1_b[0].reshape(1, D))

    return _ffn_ln2(h1, h1b, w_up_b, conv_w[0], conv_b[0].reshape(1, -1), w_down_b,
                    ln2_g[0].reshape(1, D), ln2_b[0].reshape(1, D), S)
```

```python
import functools
import math

import numpy as np
import jax
import jax.numpy as jnp
from jax import lax
from jax.experimental import pallas as pl
from jax.experimental.pallas import tpu as pltpu

F32 = jnp.float32
BF16 = jnp.bfloat16

N_META = 16
N_HEADS = 8
HEAD_DIM = 128
KV_RANK = 256
IDX_HEADS = 16
IDX_DIM = 64
TOPK_MAX = 256
POOL_WINDOWS = (2, 4, 8, 16)
POOL_GROUP = 256
CONV_WIDTH = 3
REL_BUCKETS = 32
REL_MAX_DIST = 128
DEPTH = 1
ALPHA = (2.0 * DEPTH) ** 0.25
LN_EPS = 1e-5
NEG_INF = -1e30

VMEM_LIMIT_BYTES = 60 * 1024 * 1024
FFN_VMEM_LIMIT_BYTES = 62 * 1024 * 1024
SUBLANES = 8
LANES = 128
ROW_ALIGN = 256
PROJ_ROWS = 1024
Q_TILE = 256
TAIL_TILE = 16
K_CHUNK = 256
MIX_ROWS = 688
FFN_ROWS = 1024
FFN_COLS = 512
HALO = 16
INT_MIN = -(2 ** 31)
PACKED_SUBLANES = 16
CHUNKS_PER_TRIP = 2
TIE_ROUNDS = 8


def _dot(a, b):
    return jnp.dot(a, b, preferred_element_type=F32)


def _dot_nt(a, b):
    return lax.dot_general(a, b, (((1,), (1,)), ((), ())), preferred_element_type=F32)


def _layer_norm(y, g, b):
    mu = jnp.mean(y, axis=-1, keepdims=True)
    yc = y - mu
    var = jnp.mean(yc * yc, axis=-1, keepdims=True)
    return yc * lax.rsqrt(var + LN_EPS) * g + b


def _proj_rows(x_ref, meta_ref, w, tp, store):
    seq = x_ref.shape[0]
    store(pl.ds(0, N_META), _dot_nt(meta_ref[...].astype(BF16), w))
    for r in range(seq // PROJ_ROWS):
        acc = _dot_nt(x_ref[r * PROJ_ROWS:(r + 1) * PROJ_ROWS, :].astype(BF16), w)
        store(pl.ds(N_META + r * PROJ_ROWS, PROJ_ROWS), acc)
    n_pad = tp - seq - N_META
    store(pl.ds(seq + N_META, n_pad), jnp.zeros((n_pad, w.shape[0]), F32))


def _proj_cast_kernel(x_ref, meta_ref, w_ref, o_ref, *, scale_from):
    def store(rows, acc):
        o_ref[rows, :] = acc.astype(o_ref.dtype)

    scale = jnp.where(pl.program_id(1) >= scale_from, IDX_DIM ** -0.5, 1.0)
    _proj_rows(x_ref, meta_ref, (w_ref[...] * scale).astype(BF16), o_ref.shape[0], store)


def _proj_pool_kernel(x_ref, meta_ref, w_ref, o_ref, u_scr):
    def store(rows, acc):
        u_scr[rows, :] = acc

    _proj_rows(x_ref, meta_ref, w_ref[...].astype(BF16), u_scr.shape[0], store)
    group = pl.program_id(1)
    tp = u_scr.shape[0]
    pos = lax.broadcasted_iota(jnp.int32, (tp, 1), 0)
    for g, window in enumerate(POOL_WINDOWS):

        @pl.when(group == g)
        def _():
            u = u_scr[...]
            s = u
            shift = 1
            while shift < window:
                s = s + pltpu.roll(s, shift, axis=0)
                shift *= 2
            count = jnp.minimum(pos + 1, window).astype(F32)
            o_ref[...] = (s / count - u).astype(o_ref.dtype)


def _proj_small_kernel(x_ref, meta_ref, wc_ref, wk_ref, ww_ref, g_ref, c_ref, kk_ref, wi_ref):
    w = jnp.concatenate([wc_ref[...], wk_ref[...], wk_ref[...], ww_ref[...] * (IDX_HEADS ** -0.5),
                         jnp.zeros((LANES - IDX_HEADS, wc_ref.shape[1]), F32)], axis=0).astype(BF16)

    def store(rows, acc):
        c = acc[:, :KV_RANK]
        ms = jnp.mean(c * c, axis=-1, keepdims=True)
        c_ref[rows, :] = (c * lax.rsqrt(ms + LN_EPS) * g_ref[...]).astype(c_ref.dtype)
        kk_ref[rows, :] = acc[:, KV_RANK:KV_RANK + LANES].astype(kk_ref.dtype)
        wi_ref[rows, :] = acc[:, KV_RANK + LANES:]

    _proj_rows(x_ref, meta_ref, w, c_ref.shape[0], store)


def _batch_block(tp, n):
    return pl.BlockSpec((None, tp, n), lambda b, j: (b, 0, 0))


def _compiler_params(semantics, vmem_limit_bytes=VMEM_LIMIT_BYTES):
    return pltpu.CompilerParams(dimension_semantics=semantics, vmem_limit_bytes=vmem_limit_bytes)


def _with_side_cast(body, n_in, n_out):
    def kernel(*refs):
        side_in, side_out = refs[n_in], refs[n_in + 1 + n_out]
        side_out[...] = side_in[...].astype(side_out.dtype)
        body(*refs[:n_in], *refs[n_in + 1:n_in + 1 + n_out], *refs[n_in + 2 + n_out:])

    return kernel


def _slab_block(side, n_slabs, slab_of_step):
    slab = side.shape[0] // n_slabs
    assert slab * n_slabs == side.shape[0] and slab % PACKED_SUBLANES == 0
    return pl.BlockSpec((slab, side.shape[1]), lambda *step: (slab_of_step(*step), 0))


def _w_rows(d, n_rows, first_row):
    return pl.BlockSpec((pl.Element(n_rows), pl.Element(d)), lambda b, j: (first_row(j), 0))


def _proj_call(body, x, meta, wt, w_specs, n_j, extra_in, extra_specs, out_blocks, out_shapes, scratch, name, side):
    B, seq, d = x.shape
    out_blocks = list(out_blocks) if isinstance(out_blocks, (list, tuple)) else [out_blocks]
    out_shapes = list(out_shapes) if isinstance(out_shapes, (list, tuple)) else [out_shapes]
    in_specs = [_batch_block(seq, d), pl.BlockSpec(meta.shape, lambda b, j: (0, 0))] + w_specs + extra_specs
    args = [x, meta] + [wt] * len(w_specs) + list(extra_in)
    if side is not None:
        side_block = _slab_block(side, B * n_j, lambda b, j: b * n_j + j)
        body = _with_side_cast(body, len(args), len(out_blocks))
        in_specs, args = in_specs + [side_block], args + [side]
        out_blocks, out_shapes = out_blocks + [side_block], out_shapes + [jax.ShapeDtypeStruct(side.shape, BF16)]
    return pl.pallas_call(
        body,
        grid=(B, n_j),
        in_specs=in_specs,
        out_specs=out_blocks,
        out_shape=out_shapes,
        scratch_shapes=scratch,
        compiler_params=_compiler_params(("parallel", "arbitrary")),
        name=name,
    )(*args)


def _proj_qq(x, meta, wt, cols, tp, side):
    B, d, tn = x.shape[0], x.shape[2], 512
    n_q, n = cols["q"][1] // tn, cols["q"][1] + cols["q_idx"][1]
    first_row = lambda j: pl.multiple_of(
        jnp.where(j < n_q, cols["q"][0] + j * tn, cols["q_idx"][0] + (j - n_q) * tn), PACKED_SUBLANES)
    return _proj_call(functools.partial(_proj_cast_kernel, scale_from=n_q), x, meta, wt,
                      [_w_rows(d, tn, first_row)], n // tn, [], [],
                      pl.BlockSpec((None, tp, tn), lambda b, j: (b, 0, j)),
                      jax.ShapeDtypeStruct((B, tp, n), BF16), [], "proj_qq", side)


def _proj_pool(x, meta, wt, cols, tp, side):
    B, d, n = x.shape[0], x.shape[2], cols["u"][1]
    first_row = lambda j: pl.multiple_of(cols["u"][0] + j * POOL_GROUP, PACKED_SUBLANES)
    return _proj_call(_proj_pool_kernel, x, meta, wt, [_w_rows(d, POOL_GROUP, first_row)], n // POOL_GROUP, [], [],
                      pl.BlockSpec((None, tp, POOL_GROUP), lambda b, j: (b, 0, j)),
                      jax.ShapeDtypeStruct((B, tp, n), BF16), [pltpu.VMEM((tp, POOL_GROUP), F32)], "proj_pool", side)


def _proj_small(x, meta, wt, cols, kv_g, tp, side):
    B, d = x.shape[0], x.shape[2]
    w_specs = [_w_rows(d, cols[name][1], lambda j, name=name: cols[name][0]) for name in ("c_kv", "k_idx", "w_idx")]
    return _proj_call(_proj_small_kernel, x, meta, wt, w_specs, 1, [kv_g],
                      [pl.BlockSpec((1, KV_RANK), lambda b, j: (0, 0))],
                      [_batch_block(tp, KV_RANK), _batch_block(tp, LANES), _batch_block(tp, 128)],
                      [jax.ShapeDtypeStruct((B, tp, KV_RANK), BF16), jax.ShapeDtypeStruct((B, tp, LANES), BF16),
                       jax.ShapeDtypeStruct((B, tp, 128), F32)], [], "proj_small", side)


def _attn_kernel(*refs, n_tiles, aliased, n_sides, **static):
    n_in = 6
    sides_in = refs[n_in + int(aliased):n_in + int(aliased) + n_sides]
    refs = refs[:n_in] + refs[n_in + int(aliased) + n_sides:]
    o_ref = refs[n_in]
    sides_out = refs[n_in + 1:n_in + 1 + n_sides]
    refs = refs[:n_in + 1] + refs[n_in + 1 + n_sides:]

    @pl.when(pl.program_id(1) < n_tiles)
    def _():
        for side_in, side_out in zip(sides_in, sides_out):
            side_out[...] = side_in[...].astype(side_out.dtype)
        _attn_tile(*refs, **static)

    @pl.when(pl.program_id(1) >= n_tiles)
    def _():
        o_ref[...] = jnp.zeros(o_ref.shape, o_ref.dtype)


def _attn_tile(qq_ref, wi_ref, kk_ref, c_ref, wuk_ref, diag_ref, o_ref, nb_scr, sc_scr, sct_scr, scf_scr, mb_scr,
               wt_scr, qa_scr, s_scr, m_scr, l_scr, acc_scr, *, top_k, qt, first_tile, single_tile):
    i = first_tile if single_tile else first_tile + pl.program_id(1)
    n_chunks = ((i + 1) * qt - 1) // K_CHUNK + 1

    def chunk_loop(lo, body, init):
        if single_tile:
            return lax.fori_loop(lo, n_chunks, body, init, unroll=True)

        def block(k, first, c):
            for t in range(k):
                c = body(first + t, c)
            return c

        n, k = n_chunks - lo, CHUNKS_PER_TRIP
        carry = lax.fori_loop(0, n // k, lambda p, c: block(k, lo + k * p, c), init)
        done = lo + n // k * k
        while k > 1:
            k //= 2
            take = ((n_chunks - done) // k) % 2 == 1
            carry = lax.cond(take, lambda c, k=k, done=done: block(k, done, c), lambda c: c, carry)
            done = done + jnp.where(take, k, 0)
        return carry
    attn_w = N_HEADS * HEAD_DIM
    scale = HEAD_DIM ** -0.5
    n_pairs = IDX_HEADS // 2
    lanes_are_queries = qt % LANES == 0

    t_col = i * qt + lax.broadcasted_iota(jnp.int32, (qt, 1), 0)
    t_row = i * qt + lax.broadcasted_iota(jnp.int32, (1, qt), 1)
    s_row = lax.broadcasted_iota(jnp.int32, (1, K_CHUNK), 1)
    s_col = lax.broadcasted_iota(jnp.int32, (K_CHUNK, 1), 0)
    lane_half = lax.broadcasted_iota(jnp.int32, (K_CHUNK, LANES), 1) // IDX_DIM

    n_near = nb_scr.shape[0]

    @pl.when((pl.program_id(0) == 0) & (pl.program_id(1) == 0))
    def _():
        for k in range(n_near - 1):
            for h in range(N_HEADS):
                v = jnp.broadcast_to(diag_ref[k, h:h + 1, :], (qt, 2 * K_CHUNK))
                t = pltpu.roll(v, 0, 1, stride=1, stride_axis=0)
                nb_scr[k, h * qt:(h + 1) * qt, :] = t[:, :K_CHUNK]
        nb_scr[n_near - 1] = jnp.zeros(nb_scr.shape[1:], F32)

    for h in range(N_HEADS):
        qa_scr[h * qt:(h + 1) * qt, :] = _dot(
            qq_ref[:, h * HEAD_DIM:(h + 1) * HEAD_DIM], wuk_ref[h]).astype(BF16)
    if lanes_are_queries:
        wt_scr[...] = wi_ref[...].T

    def bf16_floor(v):
        near = v.astype(BF16)
        bits = lax.bitcast_convert_type(near, jnp.int16)
        below = lax.bitcast_convert_type(bits + jnp.where(bits < 0, jnp.int16(1), jnp.int16(-1)), BF16)
        return jnp.where(near.astype(F32) > v, below, near)

    def idx_chunk(j, carry):
        ks = kk_ref[pl.ds(pl.multiple_of(j * K_CHUNK, K_CHUNK), K_CHUNK), :]
        zero = jnp.zeros(ks.shape, ks.dtype)
        k_even = jnp.where(lane_half == 0, ks, zero)
        k_odd = jnp.where(lane_half == 1, ks, zero)
        score = jnp.zeros((K_CHUNK, qt) if lanes_are_queries else (qt, K_CHUNK), F32)
        if lanes_are_queries:
            for p in range(n_pairs):
                q_pair = qq_ref[:, attn_w + p * LANES:attn_w + (p + 1) * LANES]
                for hh, k_half in ((2 * p, k_even), (2 * p + 1, k_odd)):
                    score = score + jnp.maximum(_dot_nt(k_half, q_pair), 0.0) * wt_scr[hh:hh + 1, :]
        else:
            q_pairs = jnp.concatenate(
                [qq_ref[:, attn_w + p * LANES:attn_w + (p + 1) * LANES] for p in range(n_pairs)], axis=0)
            for half, k_half in enumerate((k_even, k_odd)):
                dots = jnp.maximum(_dot_nt(q_pairs, k_half), 0.0)
                for p in range(n_pairs):
                    hh = 2 * p + half
                    score = score + dots[p * qt:(p + 1) * qt] * wi_ref[:, hh:hh + 1]
        s_pos = j * K_CHUNK + (s_col if lanes_are_queries else s_row)
        score = jnp.where(s_pos <= (t_row if lanes_are_queries else t_col), score, NEG_INF)
        if lanes_are_queries:
            sct_scr[j] = score
            sc_scr[j] = score.T
            scf_scr[j] = bf16_floor(score)
        else:
            sc_scr[j] = score
        return carry

    chunk_loop(0, idx_chunk, 0)

    k_f = float(top_k)
    idx_bits = int(math.ceil(math.log2(sc_scr.shape[0] * K_CHUNK)))
    per_query = (1, qt) if lanes_are_queries else (qt, 1)
    key_axis = 0 if lanes_are_queries else 1
    s_idx = s_col if lanes_are_queries else s_row
    search_scr = sct_scr if lanes_are_queries else sc_scr

    def fold_chunks(chunk_fn, combine, init):
        if lanes_are_queries:
            acc_rows = 4 * SUBLANES

            def body(j, acc):
                v = chunk_fn(j, search_scr[j]).reshape(K_CHUNK // acc_rows, acc_rows, qt)
                return combine(acc, functools.reduce(combine, [v[t] for t in range(K_CHUNK // acc_rows)]))

            acc = chunk_loop(0, body, jnp.full((acc_rows, qt), init, F32))
        else:
            acc = chunk_loop(0, lambda j, acc: combine(acc, chunk_fn(j, search_scr[j])),
                                jnp.full((qt, K_CHUNK), init, F32))
        reduce = jnp.sum if combine is jnp.add else jnp.min
        return reduce(acc, axis=key_axis, keepdims=True)

    def count(pred):
        return fold_chunks(lambda j, sc: jnp.where(pred(j, sc), 1.0, 0.0), jnp.add, 0.0)

    def key_to_float(key):
        return lax.bitcast_convert_type(jnp.where(key < 0, key ^ jnp.int32(0x7FFFFFFF), key), F32)

    def count_coarse(cand_f):
        acc_rows = 4 * PACKED_SUBLANES
        cand_b = jnp.broadcast_to(cand_f, (acc_rows, qt)).astype(BF16)

        def body(j, acc):
            hit = jnp.where(scf_scr[j].reshape(K_CHUNK // acc_rows, acc_rows, qt) >= cand_b[None],
                            jnp.ones((), BF16), jnp.zeros((), BF16))
            return acc + functools.reduce(jnp.add, [hit[t] for t in range(K_CHUNK // acc_rows)])

        acc = chunk_loop(0, body, jnp.zeros((acc_rows, qt), BF16))
        return jnp.sum(acc.astype(F32), axis=0, keepdims=True)

    def bit_step(base, carry):
        key, bit = carry
        cand = key + bit
        cand_f = key_to_float(cand)
        n = count(lambda j, sc: sc >= cand_f) if base is None else count_coarse(cand_f - base)
        return jnp.where(n >= k_f, cand, key), lax.shift_right_logical(bit, jnp.int32(1))

    state = (jnp.full(per_query, INT_MIN, jnp.int32), jnp.int32(INT_MIN))
    if lanes_are_queries:
        state = lax.fori_loop(0, 16, lambda _, c: bit_step(0.0, c), state)
        for n_bits in (8, 8):
            base = key_to_float(state[0])

            def rebase(j, carry, base=base):
                scf_scr[j] = bf16_floor(sct_scr[j] - base)
                return carry

            chunk_loop(0, rebase, 0)
            state = lax.fori_loop(0, n_bits, lambda _, c, base=base: bit_step(base, c), state)
    else:
        state = lax.fori_loop(0, 32, lambda _, c: bit_step(None, c), state)
    thr = key_to_float(state[0])

    n_ge = count(lambda j, sc: sc >= thr)
    has_tie = jnp.max(jnp.where((n_ge > k_f) & (thr > NEG_INF), 1.0, 0.0)) > 0.0

    def tie_break():
        def next_value(_, m):
            n_gt = count(lambda j, sc: sc > m)
            above = fold_chunks(lambda j, sc: jnp.where(sc > m, sc, jnp.inf), jnp.minimum, jnp.inf)
            return jnp.where(n_gt >= k_f, above, m)

        m = lax.fori_loop(0, TIE_ROUNDS, next_value, thr)
        need = k_f - count(lambda j, sc: sc > m)

        def step(_, carry):
            cut, bit = carry
            cand = cut + bit
            n_before = count(lambda j, sc: (sc == m) & ((j * K_CHUNK + s_idx) < cand))
            return jnp.where(n_before < need, cand, cut), lax.shift_right_logical(bit, jnp.int32(1))

        cut, _ = lax.fori_loop(0, idx_bits, step,
                               (jnp.zeros(per_query, jnp.int32), jnp.int32(2 ** (idx_bits - 1))))
        return m, cut

    def to_rows(v):
        if not lanes_are_queries:
            return jnp.broadcast_to(v, (qt, K_CHUNK))
        t = jnp.broadcast_to(v, (qt, qt)).T
        return jnp.concatenate([t] * (K_CHUNK // qt), axis=1)

    def write_masks(keep_fn):
        def mask_chunk(j, carry):
            s_pos = j * K_CHUNK + s_row
            mb_scr[j] = jnp.where(keep_fn(sc_scr[j], s_pos) & (s_pos <= t_col), 0.0, NEG_INF)
            return carry

        chunk_loop(0, mask_chunk, 0)

    def masks_with_ties():
        m, cut = tie_break()
        m_b, cut_b = to_rows(m), to_rows(cut)
        write_masks(lambda sc, s_pos: (sc > m_b) | ((sc == m_b) & (s_pos <= cut_b)))

    def masks_plain():
        thr_b = to_rows(thr)
        write_masks(lambda sc, s_pos: sc >= thr_b)

    lax.cond(has_tie, masks_with_ties, masks_plain)

    rows_h = N_HEADS * qt
    lane_fold = lambda v, op: functools.reduce(op, [v[:, k * LANES:(k + 1) * LANES] for k in range(K_CHUNK // LANES)])

    def key_rows(j):
        start = j * K_CHUNK
        return pl.ds(start if isinstance(j, int) else pl.multiple_of(start, K_CHUNK), K_CHUNK)

    def over_chunks(chunk_fn):
        chunk_fn(0, True)
        chunk_loop(1, lambda j, carry: (chunk_fn(j, False), carry)[1], 0)

    def logit_chunk(j, first):
        near = jnp.minimum((i * qt) // K_CHUNK - j, n_near - 1)
        s = _dot_nt(qa_scr[...], c_ref[key_rows(j), :]) * scale + nb_scr[near]
        s = s + jnp.concatenate([mb_scr[j]] * N_HEADS, axis=0)
        s_scr[j] = s
        fold = lane_fold(s, jnp.maximum)
        m_scr[...] = fold if first else jnp.maximum(m_scr[...], fold)

    over_chunks(logit_chunk)
    m_b = jnp.broadcast_to(jnp.max(m_scr[...], axis=-1, keepdims=True), (rows_h, LANES))
    m_scr[...] = m_b

    def value_chunk(j, first):
        p = jnp.exp(s_scr[j] - jnp.concatenate([m_scr[...]] * (K_CHUNK // LANES), axis=1))
        pv = _dot(p.astype(BF16), c_ref[key_rows(j), :])
        l_scr[...] = lane_fold(p, jnp.add) if first else l_scr[...] + lane_fold(p, jnp.add)
        acc_scr[...] = pv if first else acc_scr[...] + pv

    over_chunks(value_chunk)
    out = acc_scr[...] * (1.0 / jnp.sum(l_scr[...], axis=-1, keepdims=True))
    for h in range(N_HEADS):
        o_ref[:, h * KV_RANK:(h + 1) * KV_RANK] = out[h * qt:(h + 1) * qt].astype(o_ref.dtype)


def _dsa_attention(qq, wi, kk, c_kv, wuk, rel_bias, top_k, seq, sides):
    B, tp, _ = qq.shape
    n_main = (seq + N_META) // Q_TILE
    assert (seq + N_META) - n_main * Q_TILE <= TAIL_TILE and Q_TILE % K_CHUNK == 0 and K_CHUNK % TAIL_TILE == 0
    n_chunks_max = tp // K_CHUNK
    width = N_HEADS * KV_RANK

    def call(qt, first_tile, n_tiles, n_fill, prev, sides=()):
        rows_h = N_HEADS * qt
        diag = _near_bias_diagonals(rel_bias, qt)
        tile = lambda i: first_tile + jnp.minimum(i, n_tiles - 1)
        in_specs = [
            pl.BlockSpec((None, qt, qq.shape[2]), lambda b, i: (b, tile(i), 0)),
            pl.BlockSpec((None, qt, 128), lambda b, i: (b, tile(i), 0)),
            pl.BlockSpec((None, tp, LANES), lambda b, i: (b, 0, 0)),
            pl.BlockSpec((None, tp, KV_RANK), lambda b, i: (b, 0, 0)),
            pl.BlockSpec(wuk.shape, lambda b, i: (0, 0, 0)),
            pl.BlockSpec(diag.shape, lambda b, i: (0, 0, 0)),
        ]
        args = [qq, wi, kk, c_kv, wuk, diag]
        out_specs = [pl.BlockSpec((None, qt, width), lambda b, i: (b, first_tile + i, 0))]
        out_shape = [jax.ShapeDtypeStruct((B, tp, width), BF16)]
        aliases = {}
        if prev is not None:
            in_specs.append(pl.BlockSpec(memory_space=pl.ANY))
            args.append(prev)
            aliases = {len(args) - 1: 0}
        for side in sides:
            side_block = _slab_block(side, B * n_tiles, lambda b, i: b * n_tiles + jnp.minimum(i, n_tiles - 1))
            in_specs.append(side_block)
            args.append(side)
            out_specs.append(side_block)
            out_shape.append(jax.ShapeDtypeStruct(side.shape, BF16))
        return pl.pallas_call(
            functools.partial(_attn_kernel, n_tiles=n_tiles, aliased=prev is not None, n_sides=len(sides),
                              top_k=top_k, qt=qt, first_tile=first_tile, single_tile=n_tiles == 1),
            grid=(B, n_tiles + n_fill),
            in_specs=in_specs,
            out_specs=out_specs,
            out_shape=out_shape,
            input_output_aliases=aliases,
            scratch_shapes=[pltpu.VMEM((diag.shape[0] + 1, rows_h, K_CHUNK), F32),
                            pltpu.VMEM((n_chunks_max, qt, K_CHUNK), F32),
                            pltpu.VMEM((n_chunks_max, K_CHUNK, qt), F32),
                            pltpu.VMEM((n_chunks_max, K_CHUNK, qt), BF16),
                            pltpu.VMEM((n_chunks_max, qt, K_CHUNK), F32),
                            pltpu.VMEM((LANES, qt), F32),
                            pltpu.VMEM((rows_h, KV_RANK), BF16),
                            pltpu.VMEM((n_chunks_max, rows_h, K_CHUNK), F32),
                            pltpu.VMEM((rows_h, LANES), F32),
                            pltpu.VMEM((rows_h, LANES), F32),
                            pltpu.VMEM((rows_h, KV_RANK), F32)],
            compiler_params=_compiler_params(("arbitrary", "arbitrary")),
            name="dsa_attention" if prev is None else "dsa_attention_tail",
        )(*args)

    o_lat, *sides_b = call(Q_TILE, 0, n_main, tp // Q_TILE - n_main, None, sides)
    return (call(TAIL_TILE, n_main * Q_TILE // TAIL_TILE, 1, 0, o_lat)[0], *sides_b)


def _mix_window_start(r, seq):
    return min(max(r * MIX_ROWS - N_META, 0), seq - MIX_ROWS)


def _residual_rows(x_ref, meta_ref, h_scr, seq):
    r = pl.program_id(1)
    n_real = -(-(seq + N_META) // MIX_ROWS)
    for rv in range(n_real):

        @pl.when(r == rv)
        def _(rv=rv):
            skip = rv * MIX_ROWS - N_META - _mix_window_start(rv, seq)
            if rv == 0:
                h_scr[:N_META, :] = meta_ref[...]
                h_scr[N_META:, :] = x_ref[:MIX_ROWS - N_META, :]
            elif skip == 0:
                h_scr[...] = x_ref[...]
            else:
                h_scr[:MIX_ROWS - skip, :] = x_ref[skip:, :]
                h_scr[MIX_ROWS - skip:, :] = jnp.zeros((skip, h_scr.shape[1]), F32)

    @pl.when(r >= n_real)
    def _():
        h_scr[...] = jnp.zeros(h_scr.shape, F32)


def _mix_kernel(ol_ref, pd_ref, x_ref, meta_ref, wuv_ref, wp_ref, ps_ref, wo_ref, g_ref, b_ref, h1_ref, h1b_ref,
                h_scr, *, seq):
    _residual_rows(x_ref, meta_ref, h_scr, seq)
    attn = [_dot(ol_ref[:, h * KV_RANK:(h + 1) * KV_RANK], wuv_ref[h]) for h in range(N_HEADS)]
    pool = [_dot(pd_ref[:, g * POOL_GROUP:(g + 1) * POOL_GROUP], wp_ref[g]) for g in range(len(POOL_WINDOWS))]
    pool = jnp.concatenate(pool, axis=-1) * ps_ref[...]
    cat = jnp.concatenate(attn + [pool], axis=-1).astype(BF16)
    y = ALPHA * h_scr[...] + _dot(cat, wo_ref[...])
    h1 = _layer_norm(y, g_ref[...], b_ref[...])
    h1_ref[...] = h1
    h1b_ref[...] = h1.astype(h1b_ref.dtype)


def _mix_ln1(o_lat, pool_diff, x, meta, wuv, wpool, pool_scale, wo, g, b):
    B, seq, d = x.shape
    n_rows = seq + N_META
    assert n_rows % MIX_ROWS == 0
    row = lambda n: pl.BlockSpec((None, MIX_ROWS, n), lambda bi, r: (bi, r, 0))
    full = lambda a: pl.BlockSpec(a.shape, lambda bi, r: (0,) * a.ndim, pipeline_mode=pl.Buffered(1))
    window = pl.BlockSpec(
        (None, pl.Element(MIX_ROWS), pl.Element(d)),
        lambda bi, r: (bi, pl.multiple_of(jnp.clip(r * MIX_ROWS - N_META, 0, seq - MIX_ROWS), N_META), 0))
    return pl.pallas_call(
        functools.partial(_mix_kernel, seq=seq),
        grid=(B, n_rows // MIX_ROWS),
        in_specs=[row(o_lat.shape[2]), row(pool_diff.shape[2]), window, full(meta),
                  full(wuv), full(wpool), full(pool_scale), full(wo), full(g), full(b)],
        out_specs=[row(d), row(d)],
        out_shape=[jax.ShapeDtypeStruct((B, n_rows, d), F32), jax.ShapeDtypeStruct((B, n_rows, d), BF16)],
        scratch_shapes=[pltpu.VMEM((MIX_ROWS, d), F32)],
        compiler_params=_compiler_params(("parallel", "arbitrary")),
        name="mix_ln1",
    )(o_lat, pool_diff, x, meta, wuv, wpool, pool_scale, wo, g, b)


def _gelu_tanh(x):
    return 0.5 * x * (1.0 + jnp.tanh(math.sqrt(2.0 / math.pi) * (x + 0.044715 * (x * x * x))))


def _ffn_kernel(hw_ref, h1_hbm, wa_ref, wg_ref, cwa_ref, cwg_ref, cba_ref, cbg_ref, wd_ref, g_ref, b_ref, o_ref,
                za_scr, zg_scr, hres_scr, hres_sem):
    c = pl.program_id(2)
    n_c = pl.num_programs(2) - 1

    def residual_copy():
        first_row = pl.multiple_of(pl.program_id(1) * FFN_ROWS + N_META, N_META)
        return pltpu.make_async_copy(h1_hbm.at[pl.program_id(0), pl.ds(first_row, FFN_ROWS), :], hres_scr, hres_sem)

    def up(slot):
        x = hw_ref[...]
        za_scr[slot] = _dot(x, wa_ref[...])
        zg_scr[slot] = _dot(x, wg_ref[...])

    def conv(z, cw_ref, cb_ref):
        cw = cw_ref[...]
        n = z.shape[0]
        out = z[HALO - 2:n - 2] * cw[0:1] + z[HALO - 1:n - 1] * cw[1:2] + z[HALO:] * cw[2:3]
        return out + cb_ref[...]

    def down(slot):
        a = conv(za_scr[slot], cwa_ref, cba_ref)
        gate = conv(zg_scr[slot], cwg_ref, cbg_ref)
        act = (_gelu_tanh(a) * gate).astype(BF16)
        o_ref[...] += _dot(act, wd_ref[...])

    @pl.when(c == 0)
    def _():
        residual_copy().start()
        o_ref[...] = jnp.zeros(o_ref.shape, o_ref.dtype)
        up(0)

    @pl.when((c > 0) & (c < n_c))
    def _():
        up(c % 2)
        down((c - 1) % 2)

    @pl.when(c == n_c)
    def _():
        down((c - 1) % 2)
        residual_copy().wait()
        o_ref[...] = _layer_norm(ALPHA * hres_scr[...] + o_ref[...], g_ref[...], b_ref[...])


def _ffn_ln2(h1, h1b, w_up, conv_w, conv_b, w_down, g, b, seq):
    B, tp, d = h1.shape
    d_ff = w_down.shape[0]
    n_c = d_ff // FFN_COLS
    up_c = lambda c: jnp.minimum(c, n_c - 1)
    dn_c = lambda c: jnp.maximum(c - 1, 0)
    vec = lambda off: pl.BlockSpec((1, FFN_COLS), lambda bi, r, c: (0, dn_c(c) + off))
    return pl.pallas_call(
        _ffn_kernel,
        grid=(B, seq // FFN_ROWS, n_c + 1),
        in_specs=[
            pl.BlockSpec((None, pl.Element(FFN_ROWS + HALO), pl.Element(d)),
                         lambda bi, r, c: (bi, r * FFN_ROWS + N_META - HALO, 0)),
            pl.BlockSpec(memory_space=pl.ANY),
            pl.BlockSpec((d, FFN_COLS), lambda bi, r, c: (0, up_c(c))),
            pl.BlockSpec((d, FFN_COLS), lambda bi, r, c: (0, up_c(c) + n_c)),
            pl.BlockSpec((CONV_WIDTH, FFN_COLS), lambda bi, r, c: (0, dn_c(c))),
            pl.BlockSpec((CONV_WIDTH, FFN_COLS), lambda bi, r, c: (0, dn_c(c) + n_c)),
            vec(0), vec(n_c),
            pl.BlockSpec((FFN_COLS, d), lambda bi, r, c: (dn_c(c), 0)),
            pl.BlockSpec((1, d), lambda bi, r, c: (0, 0)),
            pl.BlockSpec((1, d), lambda bi, r, c: (0, 0)),
        ],
        out_specs=pl.BlockSpec((None, FFN_ROWS, d), lambda bi, r, c: (bi, r, 0)),
        out_shape=jax.ShapeDtypeStruct((B, seq, d), F32),
        scratch_shapes=[pltpu.VMEM((2, FFN_ROWS + HALO, FFN_COLS), F32),
                        pltpu.VMEM((2, FFN_ROWS + HALO, FFN_COLS), F32),
                        pltpu.VMEM((FFN_ROWS, d), F32),
                        pltpu.SemaphoreType.DMA(())],
        compiler_params=_compiler_params(("arbitrary", "arbitrary", "arbitrary"), FFN_VMEM_LIMIT_BYTES),
        name="ffn_ln2",
    )(h1b, h1, w_up, w_up, conv_w, conv_w, conv_b, conv_b, w_down, g, b)


def _t5_bucket_table(n):
    dist = np.arange(n, dtype=np.int32)
    max_exact = REL_BUCKETS // 2
    d_f = np.maximum(dist, 1).astype(np.float32)
    large = max_exact + (np.log(d_f / np.float32(max_exact)) / np.float32(math.log(REL_MAX_DIST / max_exact))
                         * np.float32(REL_BUCKETS - max_exact)).astype(np.int32)
    return np.where(dist < max_exact, dist, np.minimum(large, REL_BUCKETS - 1))


def _near_bias_diagonals(rel_bias, qt):
    assert qt <= K_CHUNK
    probe = _t5_bucket_table(4 * REL_MAX_DIST)
    first_far = int(np.argmax(probe == REL_BUCKETS - 1))
    assert np.all(probe[first_far:] == REL_BUCKETS - 1)
    n_real = -(-(first_far + K_CHUNK - 1) // K_CHUNK)
    buckets = _t5_bucket_table((n_real + 1) * K_CHUNK)
    period = 2 * K_CHUNK
    u = np.arange(period)
    k = np.arange(n_real)[:, None]
    dist = np.where(u < K_CHUNK, k * K_CHUNK - u, k * K_CHUNK + period - u)
    idx = buckets[np.clip(dist, 0, len(buckets) - 1)]
    rel = rel_bias.astype(F32) - rel_bias[REL_BUCKETS - 1:].astype(F32)
    return jnp.transpose(rel[idx], (0, 2, 1))


def kernel(x, meta, rel_bias, w_in, kv_norm_g, w_uk, w_uv, w_pool, pool_scale, w_o, ln1_g, ln1_b, w_up, conv_w,
           conv_b, w_down, ln2_g, ln2_b):
    B, S, D = x.shape
    assert w_in.shape[0] == DEPTH and S % FFN_ROWS == 0
    T = S + N_META
    tp = -(-T // ROW_ALIGN) * ROW_ALIGN
    assert tp - T >= max(POOL_WINDOWS)
    top_k = min(TOPK_MAX, S // 4)
    assert top_k <= K_CHUNK

    wt = jnp.transpose(w_in[0])
    sizes = (("q", N_HEADS * HEAD_DIM), ("c_kv", KV_RANK), ("q_idx", IDX_HEADS * IDX_DIM), ("k_idx", IDX_DIM),
             ("w_idx", IDX_HEADS), ("u", len(POOL_WINDOWS) * POOL_GROUP))
    cols, start = {}, 0
    for name, size in sizes:
        cols[name] = (start, size)
        start += size
    assert start == wt.shape[0]

    qq, = _proj_qq(x, meta, wt, cols, tp, None)
    pool_diff, w_o_b = _proj_pool(x, meta, wt, cols, tp, w_o[0])
    c_kv, kk, wi = _proj_small(x, meta, wt, cols, kv_norm_g[0].reshape(1, KV_RANK), tp, None)

    wuk = jnp.transpose(w_uk[0], (1, 2, 0)).astype(BF16)
    o_lat, w_up_b, w_down_b = _dsa_attention(qq, wi, kk, c_kv, wuk, rel_bias, top_k, S, (w_up[0], w_down[0]))

    wuv = jnp.transpose(w_uv[0], (1, 0, 2)).astype(BF16)
    h1, h1b = _mix_ln1(o_lat, pool_diff, x, meta, wuv, w_pool[0].astype(BF16), pool_scale[0].reshape(1, -1),
                       w_o_b, ln1_g[0].reshape(1, D), ln1_b[0].reshape(1, D))

    return _ffn_ln2(h1, h1b, w_up_b, conv_w[0], conv_b[0].reshape(1, -1), w_down_b,
                    ln2_g[0].reshape(1, D), ln2_b[0].reshape(1, D), S)
```
